```python
import math
import jax, jax.numpy as jnp
from jax import lax
import numpy as np

D_MODEL = 1024
BATCH = 16
SEQ = 256
DEPTH = 2
DEC_BATCH = 2
DEC_SEQ = 2048
PAST_LEN = 256

GRID_W = 64
N_AB = (DEPTH + 1) // 2
N_NA = DEPTH // 2
D_FF = 2816
N_ADA = 9
EPS = 1e-6
CHUNK = 64
QBLK = 128
H_A = 4
DQK_A = 64
DV_A = 128
H_B = 4
DK_B = 128
DV_B = 128
CONV_K = 5
H_C = 16
DH_C = 64
WIN_R = 8
WIN_C = 16

AB_SIZES = (H_A * DQK_A, H_A * DQK_A, H_A * DV_A, H_A * DV_A, 2 * H_A, 2 * H_A,
            H_B * DK_B, H_B * DK_B, H_B * DV_B, H_B * DV_B, 2 * H_B, 2 * H_B)
AB_IN = sum(AB_SIZES)
AB_OUT = H_A * DV_A + H_B * DV_B
QKV_B = 2 * H_B * DK_B + H_B * DV_B
NA_W = H_C * DH_C

kernel_name = 'hybrid_mlstm_deltanet_natten_prefix_dit_step'


def rmsnorm(x, g):
    xf = x.astype(jnp.float32)
    y = xf * lax.rsqrt(jnp.mean(xf * xf, axis=-1, keepdims=True) + EPS)
    return (y * g.astype(jnp.float32)).astype(x.dtype)


def head_rms(x):
    return x * lax.rsqrt(jnp.mean(x * x, axis=-1, keepdims=True) + EPS)


def l2norm(x):
    return x * lax.rsqrt(jnp.sum(x * x, axis=-1, keepdims=True) + EPS)


def modulate(x, g, shift, scale):
    return rmsnorm(x, g) * (1 + scale) + shift


def swiglu(h, wg, wu, wd):
    return (jax.nn.silu(h @ wg) * (h @ wu)) @ wd


def split_heads(a, n_heads):
    b, t, _ = a.shape
    return a.reshape(b, t, n_heads, -1).transpose(0, 2, 1, 3)


def merge_heads(a):
    b, h, t, d = a.shape
    return a.transpose(0, 2, 1, 3).reshape(b, t, h * d)


def dir_gates(a):
    b, t, _ = a.shape
    return a.reshape(b, t, 2, -1).transpose(0, 2, 3, 1).astype(jnp.float32)


def flip_t(a):
    return jnp.flip(a, axis=2)


def to_chunks(a):
    nc = a.shape[2] // CHUNK
    return jnp.moveaxis(a.reshape(a.shape[:2] + (nc, CHUNK) + a.shape[3:]), 2, 0)


def from_chunks(a):
    a = jnp.moveaxis(a, 0, 2)
    return a.reshape(a.shape[:2] + (a.shape[2] * a.shape[3],) + a.shape[4:])


def short_conv(x, w):
    ch = x.shape[-1]
    return lax.conv_general_dilated(x, w[:, None, :].astype(x.dtype), window_strides=(1,),
                                    padding=[(CONV_K // 2, CONV_K // 2)],
                                    dimension_numbers=('NWC', 'WIO', 'NWC'),
                                    feature_group_count=ch)


def mlstm_chunked(q, k, v, logi, logf, c0, n0, m0):
    tri = jnp.tril(jnp.ones((CHUNK, CHUNK), bool))

    def step(carry, inp):
        cm, nv, m = carry
        qc, kc, vc, ic, fc = inp
        b = jnp.cumsum(fc, axis=-1)
        dmat = jnp.where(tri, b[..., :, None] - b[..., None, :] + ic[..., None, :], -jnp.inf)
        m_inter = b + m[..., None]
        m_t = jnp.maximum(m_inter, jnp.max(dmat, axis=-1))
        a_inter = jnp.exp(m_inter - m_t)
        sw = jnp.einsum('bhtd,bhsd->bhts', qc, kc) * jnp.exp(dmat - m_t[..., None])
        num = (jnp.einsum('bhts,bhsv->bhtv', sw, vc)
               + a_inter[..., None] * jnp.einsum('bhtd,bhdv->bhtv', qc, cm))
        den = jnp.sum(sw, axis=-1) + a_inter * jnp.einsum('bhtd,bhd->bht', qc, nv)
        h = num / jnp.maximum(jnp.abs(den), jnp.exp(-m_t))[..., None]
        b_last = b[..., -1]
        gs = b_last[..., None] - b + ic
        m_new = jnp.maximum(b_last + m, jnp.max(gs, axis=-1))
        carry_decay = jnp.exp(b_last + m - m_new)
        ws = jnp.exp(gs - m_new[..., None])
        cm = carry_decay[..., None, None] * cm + jnp.einsum('bhs,bhsd,bhsv->bhdv', ws, kc, vc)
        nv = carry_decay[..., None] * nv + jnp.einsum('bhs,bhsd->bhd', ws, kc)
        return (cm, nv, m_new), h

    (cm, nv, m), h = lax.scan(step, (c0, n0, m0), tuple(to_chunks(a) for a in (q, k, v, logi, logf)))
    return from_chunks(h), (cm, nv, m)


def gated_delta_chunked(q, k, v, g, beta, s0):
    qc, kc, vc = to_chunks(q), to_chunks(k), to_chunks(v)
    gc = jnp.cumsum(to_chunks(g), axis=-1)
    bc = to_chunks(beta)
    tri = jnp.tril(jnp.ones((CHUNK, CHUNK), bool))
    strict = jnp.tril(jnp.ones((CHUNK, CHUNK), bool), -1)
    decay = jnp.exp(jnp.where(tri, gc[..., :, None] - gc[..., None, :], -jnp.inf))
    kb = kc * bc[..., None]
    mlow = jnp.where(strict, jnp.einsum('nbhtd,nbhsd->nbhts', kb, kc) * decay, 0.0)
    amat = mlow + jnp.eye(CHUNK, dtype=mlow.dtype)
    u = lax.linalg.triangular_solve(amat, vc * bc[..., None], left_side=True, lower=True,
                                    unit_diagonal=True)
    w = lax.linalg.triangular_solve(amat, kb * jnp.exp(gc)[..., None], left_side=True, lower=True,
                                    unit_diagonal=True)
    qk = jnp.where(tri, jnp.einsum('nbhtd,nbhsd->nbhts', qc, kc) * decay, 0.0)
    qg = qc * jnp.exp(gc)[..., None]
    kd = kc * jnp.exp(gc[..., -1:] - gc)[..., None]
    gl = jnp.exp(gc[..., -1])

    def step(s, inp):
        u_i, w_i, qk_i, qg_i, kd_i, gl_i = inp
        v_new = u_i - jnp.einsum('bhtd,bhdv->bhtv', w_i, s)
        o = jnp.einsum('bhtd,bhdv->bhtv', qg_i, s) + jnp.einsum('bhts,bhsv->bhtv', qk_i, v_new)
        s = s * gl_i[..., None, None] + jnp.einsum('bhsd,bhsv->bhdv', kd_i, v_new)
        return s, o

    s, o = lax.scan(step, s0, (u, w, qk, qg, kd, gl))
    return from_chunks(o), s


def ab_mixer(h, w_in, w_out, b_i, b_f, g_m, conv_w, a_log, dt_bias, g_d, c0, n0, m0, s0):
    f32 = jnp.float32
    idx = np.cumsum(AB_SIZES)[:-1].tolist()
    (q_m, k_m, v_m, o_m, i_pre, f_pre,
     q_d, k_d, v_d, z_d, b_pre, a_pre) = jnp.split(h @ w_in, idx, axis=-1)
    qm = split_heads(q_m, H_A).astype(f32) * DQK_A ** -0.5
    km = split_heads(k_m, H_A).astype(f32)
    vm = split_heads(v_m, H_A).astype(f32)
    logi = dir_gates(i_pre) + b_i[None, :, :, None]
    logf = jax.nn.log_sigmoid(dir_gates(f_pre) + b_f[None, :, :, None])
    c0, n0, m0, s0 = c0.astype(f32), n0.astype(f32), m0.astype(f32), s0.astype(f32)
    hf, (cf, nf, mf) = mlstm_chunked(qm, km, vm, logi[:, 0], logf[:, 0], c0[:, 0], n0[:, 0], m0[:, 0])
    hb, (cb, nb, mb) = mlstm_chunked(flip_t(qm), flip_t(km), flip_t(vm), flip_t(logi[:, 1]),
                                     flip_t(logf[:, 1]), c0[:, 1], n0[:, 1], m0[:, 1])
    hm = merge_heads(head_rms(hf + flip_t(hb))) * g_m * jax.nn.sigmoid(o_m.astype(f32))
    qkv = jax.nn.silu(short_conv(jnp.concatenate([q_d, k_d, v_d], axis=-1), conv_w))
    q_d, k_d, v_d = jnp.split(qkv, [H_B * DK_B, 2 * H_B * DK_B], axis=-1)
    qd = l2norm(split_heads(q_d, H_B).astype(f32)) * DK_B ** -0.5
    kd = l2norm(split_heads(k_d, H_B).astype(f32))
    vd = split_heads(v_d, H_B).astype(f32)
    beta = jax.nn.sigmoid(dir_gates(b_pre))
    g = -jnp.exp(a_log.astype(f32))[None, :, :, None] * jax.nn.softplus(dir_gates(a_pre) + dt_bias[None, :, :, None])
    of, sf = gated_delta_chunked(qd, kd, vd, g[:, 0], beta[:, 0], s0[:, 0])
    ob, sb = gated_delta_chunked(flip_t(qd), flip_t(kd), flip_t(vd), flip_t(g[:, 1]),
                                 flip_t(beta[:, 1]), s0[:, 1])
    od = merge_heads(head_rms(of + flip_t(ob)) * g_d) * jax.nn.silu(z_d.astype(f32))
    out = jnp.concatenate([hm, od], axis=-1).astype(h.dtype) @ w_out
    return out, (jnp.stack([cf, cb], axis=1), jnp.stack([nf, nb], axis=1),
                 jnp.stack([mf, mb], axis=1), jnp.stack([sf, sb], axis=1))


def na_project(h, w_in):
    q, k, v = jnp.split(h @ w_in, 3, axis=-1)
    return split_heads(q, H_C), split_heads(k, H_C), split_heads(v, H_C)


def context_attention(q, k, v):
    b, hh, t, d = q.shape
    qb = jnp.moveaxis(q.reshape(b, hh, t // QBLK, QBLK, d), 2, 0)

    def block(qi):
        s = jnp.einsum('bhqd,bhkd->bhqk', qi, k).astype(jnp.float32) * DH_C ** -0.5
        p = jax.nn.softmax(s, axis=-1).astype(v.dtype)
        return jnp.einsum('bhqk,bhkd->bhqd', p, v)

    o = lax.map(block, qb)
    return jnp.moveaxis(o, 0, 2).reshape(b, hh, t, d)


def neighbourhood_attention(q, k, v, kc, vc, rel_bias):
    b, hh, n, d = q.shape
    rows = n // GRID_W
    wr = min(WIN_R, rows)
    r = np.arange(rows)
    krow = np.clip(r - wr // 2, 0, rows - wr)[:, None] + np.arange(wr)[None, :]
    col = np.arange(GRID_W)
    cs = np.clip(col - WIN_C // 2, 0, GRID_W - WIN_C)
    valid = (col[None, :] >= cs[:, None]) & (col[None, :] < cs[:, None] + WIN_C)
    ridx = krow - r[:, None] + WIN_R - 1
    cidx = np.clip(col[None, :] - col[:, None], -(WIN_C - 1), WIN_C - 1) + WIN_C - 1
    bias = rel_bias[:, ridx[:, None, :, None], cidx[None, :, None, :]].astype(jnp.float32)
    bias = jnp.where(valid[None, None, :, None, :], bias, -jnp.inf)
    scale = DH_C ** -0.5
    qg = q.reshape(b, hh, rows, GRID_W, d)
    kband = k.reshape(b, hh, rows, GRID_W, d)[:, :, krow]
    vband = v.reshape(b, hh, rows, GRID_W, d)[:, :, krow]
    s_loc = jnp.einsum('bhrqd,bhrwkd->bhrqwk', qg, kband).astype(jnp.float32) * scale + bias[None]
    s_ctx = jnp.einsum('bhrqd,bhpd->bhrqp', qg, kc).astype(jnp.float32) * scale
    nloc = wr * GRID_W
    p = jax.nn.softmax(jnp.concatenate([s_loc.reshape(b, hh, rows, GRID_W, nloc), s_ctx], axis=-1), axis=-1)
    p_loc = p[..., :nloc].reshape(b, hh, rows, GRID_W, wr, GRID_W).astype(v.dtype)
    p_ctx = p[..., nloc:].astype(vc.dtype)
    o = (jnp.einsum('bhrqwk,bhrwkd->bhrqd', p_loc, vband)
         + jnp.einsum('bhrqp,bhpd->bhrqd', p_ctx, vc))
    return o.reshape(b, hh, n, d)


def setup_inputs(seed: int = 0) -> dict:
    key = jax.random.key(seed)
    ks = iter(jax.random.split(key, 40))

    def nrm(shape, s):
        return jax.random.normal(next(ks), shape, jnp.float32) * s

    inp = {}
    inp['x_prompt'] = nrm((BATCH, SEQ, D_MODEL), 1.0)
    inp['x_sample'] = nrm((DEC_BATCH, DEC_SEQ, D_MODEL), 1.0)
    inp['c'] = nrm((DEC_BATCH, D_MODEL), 1.0)
    inp['state_mlstm_C'] = nrm((DEC_BATCH, N_AB, 2, H_A, DQK_A, DV_A), 0.5)
    inp['state_mlstm_n'] = nrm((DEC_BATCH, N_AB, 2, H_A, DQK_A), 0.5)
    inp['state_mlstm_m'] = nrm((DEC_BATCH, N_AB, 2, H_A), 0.5)
    inp['state_delta_S'] = nrm((DEC_BATCH, N_AB, 2, H_B, DK_B, DV_B), 0.1)
    inp['cache_na_k'] = nrm((DEC_BATCH, N_NA, H_C, PAST_LEN, DH_C), 1.0)
    inp['cache_na_v'] = nrm((DEC_BATCH, N_NA, H_C, PAST_LEN, DH_C), 1.0)
    inp['c_ctx'] = nrm((D_MODEL,), 1.0)
    inp['ada_w'] = nrm((DEPTH, D_MODEL, N_ADA * D_MODEL), 0.5 * D_MODEL ** -0.5)
    inp['ada_b'] = nrm((DEPTH, N_ADA * D_MODEL), 0.02)
    inp['norm_g'] = 1.0 + nrm((DEPTH, 3, D_MODEL), 0.02)
    inp['ffn_wg'] = nrm((DEPTH, 2, D_MODEL, D_FF), D_MODEL ** -0.5)
    inp['ffn_wu'] = nrm((DEPTH, 2, D_MODEL, D_FF), D_MODEL ** -0.5)
    inp['ffn_wd'] = nrm((DEPTH, 2, D_FF, D_MODEL), D_FF ** -0.5)
    inp['ab_w_in'] = nrm((N_AB, D_MODEL, AB_IN), D_MODEL ** -0.5)
    inp['ab_w_out'] = nrm((N_AB, AB_OUT, D_MODEL), AB_OUT ** -0.5)
    inp['mlstm_b_i'] = nrm((N_AB, 2, H_A), 0.1)
    inp['mlstm_b_f'] = 3.0 + nrm((N_AB, 2, H_A), 0.5)
    inp['mlstm_norm_g'] = 1.0 + nrm((N_AB, H_A * DV_A), 0.02)
    inp['delta_conv_w'] = nrm((N_AB, CONV_K, QKV_B), CONV_K ** -0.5)
    inp['delta_a_log'] = jnp.log(jax.random.uniform(next(ks), (N_AB, 2, H_B), jnp.float32, minval=1.0, maxval=16.0))
    dt = jnp.exp(jax.random.uniform(next(ks), (N_AB, 2, H_B), jnp.float32,
                                    minval=math.log(1e-3), maxval=math.log(1e-1)))
    inp['delta_dt_bias'] = dt + jnp.log(-jnp.expm1(-dt))
    inp['delta_norm_g'] = 1.0 + nrm((N_AB, DV_B), 0.02)
    inp['na_w_in'] = nrm((N_NA, D_MODEL, 3 * NA_W), D_MODEL ** -0.5)
    inp['na_w_out'] = nrm((N_NA, NA_W, D_MODEL), NA_W ** -0.5)
    inp['na_rel_bias'] = nrm((N_NA, H_C, 2 * WIN_R - 1, 2 * WIN_C - 1), 0.1)
    inp['final_norm_g'] = 1.0 + nrm((D_MODEL,), 0.02)
    return inp


def reference(x_prompt, x_sample, c, state_mlstm_C, state_mlstm_n, state_mlstm_m, state_delta_S,
              cache_na_k, cache_na_v, c_ctx, ada_w, ada_b, norm_g, ffn_wg, ffn_wu, ffn_wd,
              ab_w_in, ab_w_out, mlstm_b_i, mlstm_b_f, mlstm_norm_g, delta_conv_w, delta_a_log,
              delta_dt_bias, delta_norm_g, na_w_in, na_w_out, na_rel_bias, final_norm_g):
    xp, xs = x_prompt, x_sample
    bp = xp.shape[0]
    f32 = jnp.float32
    new_c, new_n, new_m, new_s, new_k, new_v = [], [], [], [], [], []
    for l in range(DEPTH):
        mp = jnp.split((jax.nn.silu(c_ctx) @ ada_w[l] + ada_b[l])[None, None, :], N_ADA, axis=-1)
        ms = jnp.split((jax.nn.silu(c) @ ada_w[l] + ada_b[l])[:, None, :], N_ADA, axis=-1)
        xp = xp + 0.5 * mp[2] * swiglu(modulate(xp, norm_g[l, 0], mp[0], mp[1]), ffn_wg[l, 0], ffn_wu[l, 0], ffn_wd[l, 0])
        xs = xs + 0.5 * ms[2] * swiglu(modulate(xs, norm_g[l, 0], ms[0], ms[1]), ffn_wg[l, 0], ffn_wu[l, 0], ffn_wd[l, 0])
        hp = modulate(xp, norm_g[l, 1], mp[3], mp[4])
        hs = modulate(xs, norm_g[l, 1], ms[3], ms[4])
        a = l // 2
        if l % 2 == 0:
            prm = (ab_w_in[a], ab_w_out[a], mlstm_b_i[a], mlstm_b_f[a], mlstm_norm_g[a], delta_conv_w[a],
                   delta_a_log[a], delta_dt_bias[a], delta_norm_g[a])
            op, (cn, nn_, mn, sn) = ab_mixer(
                hp, *prm,
                jnp.zeros((bp, 2, H_A, DQK_A, DV_A), f32), jnp.zeros((bp, 2, H_A, DQK_A), f32),
                jnp.zeros((bp, 2, H_A), f32), jnp.zeros((bp, 2, H_B, DK_B, DV_B), f32))
            os_, _ = ab_mixer(hs, *prm, state_mlstm_C[:, a], state_mlstm_n[:, a],
                              state_mlstm_m[:, a], state_delta_S[:, a])
            new_c.append(cn)
            new_n.append(nn_)
            new_m.append(mn)
            new_s.append(sn)
        else:
            qp, kp, vp = na_project(hp, na_w_in[a])
            op = merge_heads(context_attention(qp, kp, vp)) @ na_w_out[a]
            qs, ks_, vs = na_project(hs, na_w_in[a])
            os_ = merge_heads(neighbourhood_attention(qs, ks_, vs, cache_na_k[:, a], cache_na_v[:, a],
                                                      na_rel_bias[a])) @ na_w_out[a]
            new_k.append(kp)
            new_v.append(vp)
        xp = xp + mp[5] * op
        xs = xs + ms[5] * os_
        xp = xp + 0.5 * mp[8] * swiglu(modulate(xp, norm_g[l, 2], mp[6], mp[7]), ffn_wg[l, 1], ffn_wu[l, 1], ffn_wd[l, 1])
        xs = xs + 0.5 * ms[8] * swiglu(modulate(xs, norm_g[l, 2], ms[6], ms[7]), ffn_wg[l, 1], ffn_wu[l, 1], ffn_wd[l, 1])
    y_prompt = rmsnorm(xp, final_norm_g)
    y_sample = rmsnorm(xs, final_norm_g)
    return (y_prompt, y_sample, jnp.stack(new_c, axis=1), jnp.stack(new_n, axis=1),
            jnp.stack(new_m, axis=1), jnp.stack(new_s, axis=1), jnp.stack(new_k, axis=1),
            jnp.stack(new_v, axis=1))
```

```python
import functools

import jax
import jax.numpy as jnp
from jax import lax
from jax.experimental import pallas as pl
from jax.experimental.pallas import tpu as pltpu

f32 = jnp.float32
bf16 = jnp.bfloat16
HIGHEST = lax.Precision.HIGHEST

D_MODEL = 1024
BATCH = 16
SEQ = 256
DEPTH = 2
DEC_BATCH = 2
DEC_SEQ = 2048
PAST_LEN = 256
GRID_W = 64
GRID_ROWS = DEC_SEQ // GRID_W
D_FF = 2816
N_ADA = 9
EPS = 1e-6
CHUNK = 64
H_A, DQK_A, DV_A = 4, 64, 128
H_B, DK_B, DV_B = 4, 128, 128
CONV_K = 5
H_C, DH_C = 16, 64
WIN_R, WIN_C = 8, 16
NA_W = H_C * DH_C

N_PROMPT = BATCH * SEQ
N_SAMPLE = DEC_BATCH * DEC_SEQ
N_TOK = N_PROMPT + N_SAMPLE
AB_MAIN = 2 * H_A * DQK_A + 2 * H_A * DV_A + 2 * H_B * DK_B + 2 * H_B * DV_B
LANES = 128
VMEM_LIMIT = 56 * 1024 * 1024

LANE_I, LANE_F, LANE_BETA, LANE_A, LANE_F2, LANE_A2 = 0, 8, 16, 24, 32, 40

NT_DIMS = (((1,), (1,)), ((), ()))
TN_DIMS = (((0,), (0,)), ((), ()))


def _softplus(x):
    return jnp.maximum(x, 0.0) + jnp.log1p(jnp.exp(-jnp.abs(x)))


def _log_sigmoid(x):
    return -_softplus(-x)


def _silu(x):
    return x * jax.nn.sigmoid(x)


def _mod_row_index(i, tm):
    n_p = N_PROMPT // tm
    per_b = DEC_SEQ // tm
    return jnp.where(i < n_p, 0, 1 + (i - n_p) // per_b)


def _modulated(x, g_row, shift_row, scale_row):
    y = x * lax.rsqrt(jnp.mean(x * x, axis=-1, keepdims=True) + EPS) * g_row
    return y * (1.0 + scale_row) + shift_row


def _ada_kernel(cb_ref, w_ref, b_ref, o_ref):
    tn = w_ref.shape[-1]
    for r in range(3):
        cb = cb_ref[r]
        s = cb * jax.nn.sigmoid(cb)
        parts = [jnp.sum(w_ref[:, j * LANES:(j + 1) * LANES] * s, axis=0, keepdims=True)
                 for j in range(tn // LANES)]
        o_ref[r:r + 1, :] = jnp.concatenate(parts, axis=1) + b_ref[...]


def _ada_call(cond3, ada_w, ada_b):
    tn = D_MODEL
    cb = jnp.broadcast_to(cond3[:, :, None], (3, D_MODEL, LANES))
    return pl.pallas_call(
        _ada_kernel,
        grid=(DEPTH, N_ADA * D_MODEL // tn),
        in_specs=[pl.BlockSpec((3, D_MODEL, LANES), lambda l, j: (0, 0, 0)),
                  pl.BlockSpec((None, D_MODEL, tn), lambda l, j: (l, 0, j)),
                  pl.BlockSpec((None, 1, tn), lambda l, j: (l, 0, j))],
        out_specs=pl.BlockSpec((None, 3, tn), lambda l, j: (l, 0, j)),
        out_shape=jax.ShapeDtypeStruct((DEPTH, 3, N_ADA * D_MODEL), f32),
        compiler_params=pltpu.CompilerParams(vmem_limit_bytes=VMEM_LIMIT),
        name="ada_mod",
    )(cb, ada_w, ada_b.reshape(DEPTH, 1, N_ADA * D_MODEL))


def _ffn_kernel(x_ref, mod_ref, g_ref, wg_ref, wu_ref, wd_ref, gf_ref, o_ref, h_scr, acc_scr, *, rows, final):
    j = pl.program_id(1)

    @pl.when(j == 0)
    def _():
        h = _modulated(x_ref[...], g_ref[...], mod_ref[rows[0]:rows[0] + 1, :], mod_ref[rows[1]:rows[1] + 1, :])
        h_scr[...] = h.astype(bf16)
        acc_scr[...] = jnp.zeros_like(acc_scr)

    h = h_scr[...]
    g = jnp.dot(h, wg_ref[...], preferred_element_type=f32)
    u = jnp.dot(h, wu_ref[...], preferred_element_type=f32)
    a = (_silu(g) * u).astype(bf16)
    acc_scr[...] += jnp.dot(a, wd_ref[...], preferred_element_type=f32)

    @pl.when(j == pl.num_programs(1) - 1)
    def _():
        xn = x_ref[...] + (0.5 * mod_ref[rows[2]:rows[2] + 1, :]) * acc_scr[...]
        if final:
            xn = xn * lax.rsqrt(jnp.mean(xn * xn, axis=-1, keepdims=True) + EPS) * gf_ref[...]
        o_ref[...] = xn


def _ffn_call(x, mod_l, g_row, wg, wu, wd, gf_row, rows, final, tm=1024, tf=256):
    return pl.pallas_call(
        functools.partial(_ffn_kernel, rows=rows, final=final),
        grid=(N_TOK // tm, D_FF // tf),
        in_specs=[pl.BlockSpec((tm, D_MODEL), lambda i, j: (i, 0)),
                  pl.BlockSpec((None, N_ADA, D_MODEL), lambda i, j: (_mod_row_index(i, tm), 0, 0)),
                  pl.BlockSpec((1, D_MODEL), lambda i, j: (0, 0)),
                  pl.BlockSpec((D_MODEL, tf), lambda i, j: (0, j)),
                  pl.BlockSpec((D_MODEL, tf), lambda i, j: (0, j)),
                  pl.BlockSpec((tf, D_MODEL), lambda i, j: (j, 0)),
                  pl.BlockSpec((1, D_MODEL), lambda i, j: (0, 0))],
        out_specs=pl.BlockSpec((tm, D_MODEL), lambda i, j: (i, 0)),
        out_shape=jax.ShapeDtypeStruct((N_TOK, D_MODEL), f32),
        scratch_shapes=[pltpu.VMEM((tm, D_MODEL), bf16), pltpu.VMEM((tm, D_MODEL), f32)],
        compiler_params=pltpu.CompilerParams(dimension_semantics=("parallel", "arbitrary"),
                                             vmem_limit_bytes=VMEM_LIMIT),
        name="ffn",
    )(x, mod_l, g_row, wg, wu, wd, gf_row)


def _proj_kernel(*refs, rows, with_gates):
    if with_gates:
        x_ref, mod_ref, g_ref, w_ref, wgate_ref, o_ref, og_ref, h_scr = refs
    else:
        x_ref, mod_ref, g_ref, w_ref, o_ref, h_scr = refs
    j = pl.program_id(1)

    @pl.when(j == 0)
    def _():
        h = _modulated(x_ref[...], g_ref[...], mod_ref[rows[0]:rows[0] + 1, :], mod_ref[rows[1]:rows[1] + 1, :])
        hb = h.astype(bf16)
        h_scr[...] = hb
        if with_gates:
            og_ref[...] = jnp.dot(hb, wgate_ref[...], preferred_element_type=f32)

    o_ref[...] = jnp.dot(h_scr[...], w_ref[...], preferred_element_type=f32)


def _proj_call(x, mod_l, g_row, w, w_gate, rows, tm=1024, tn=512):
    n = w.shape[1]
    with_gates = w_gate is not None
    in_specs = [pl.BlockSpec((tm, D_MODEL), lambda i, j: (i, 0)),
                pl.BlockSpec((None, N_ADA, D_MODEL), lambda i, j: (_mod_row_index(i, tm), 0, 0)),
                pl.BlockSpec((1, D_MODEL), lambda i, j: (0, 0)),
                pl.BlockSpec((D_MODEL, tn), lambda i, j: (0, j))]
    out_specs = [pl.BlockSpec((tm, tn), lambda i, j: (i, j))]
    out_shape = [jax.ShapeDtypeStruct((N_TOK, n), f32)]
    args = [x, mod_l, g_row, w]
    if with_gates:
        in_specs.append(pl.BlockSpec((D_MODEL, LANES), lambda i, j: (0, 0)))
        out_specs.append(pl.BlockSpec((tm, LANES), lambda i, j: (i, 0)))
        out_shape.append(jax.ShapeDtypeStruct((N_TOK, LANES), f32))
        args.append(w_gate)
    return pl.pallas_call(
        functools.partial(_proj_kernel, rows=rows, with_gates=with_gates),
        grid=(N_TOK // tm, n // tn),
        in_specs=in_specs, out_specs=out_specs, out_shape=out_shape,
        scratch_shapes=[pltpu.VMEM((tm, D_MODEL), bf16)],
        compiler_params=pltpu.CompilerParams(dimension_semantics=("parallel", "arbitrary"),
                                             vmem_limit_bytes=VMEM_LIMIT),
        name="in_proj",
    )(*args)


def _chunk_masks():
    r = lax.broadcasted_iota(jnp.int32, (CHUNK, CHUNK), 0)
    c = lax.broadcasted_iota(jnp.int32, (CHUNK, CHUNK), 1)
    return r >= c, r <= c, r > c, r < c


def _lane_col(x, lane, j):
    return jnp.sum(jnp.where(lane == j, x, 0.0), axis=1, keepdims=True)


def _pair_differences(cum, lane, j, j2, extra=None, je=None):
    ones_lanes = (lane == j2) if extra is None else ((lane == j2) | (lane == je))
    a = jnp.where(lane == j, cum, jnp.where(ones_lanes, 1.0, 0.0))
    b = jnp.where(lane == j, 1.0, jnp.where(lane == j2, -cum, 0.0))
    if extra is not None:
        b = jnp.where(lane == je, extra, b)
    return lax.dot_general(a, b, NT_DIMS, precision=HIGHEST, preferred_element_type=f32)


def _mlstm_kernel(*refs, seq, zero_init):
    if zero_init:
        qp_ref, kp_ref, v0_ref, v1_ref, g_ref, par_ref, h_ref, cf_ref, nf_ref, mf_ref = refs
    else:
        (qp_ref, kp_ref, v0_ref, v1_ref, g_ref, par_ref, c0_ref, n0_ref, m0_ref,
         h_ref, cf_ref, nf_ref, mf_ref) = refs
    nc = seq // CHUNK
    pair = pl.program_id(1)
    lane = lax.broadcasted_iota(jnp.int32, (1, LANES), 1)
    tril, triu, _, _ = _chunk_masks()
    bi_row = par_ref[0:1, :]
    bf_row = par_ref[1:2, :]
    ones_col = jnp.where(lane == 0, 1.0, 0.0) + jnp.zeros((CHUNK, LANES), f32)
    v_refs = (v0_ref, v1_ref)
    streams = [(hh, d) for hh in range(2) for d in range(2)]

    h_ref[...] = jnp.zeros_like(h_ref)

    init = []
    for hh, d in streams:
        if zero_init:
            init.append((jnp.zeros((DQK_A, 2 * LANES), f32), jnp.zeros((1, 1), f32)))
        else:
            n_aug = jnp.where(lane == 0, n0_ref[d, hh], 0.0)
            init.append((jnp.concatenate([c0_ref[d, hh], n_aug], axis=1), m0_ref[d, hh]))

    def body(i, carry):
        out = []
        for s_idx, (hh, d) in enumerate(streams):
            c_aug, m = carry[s_idx]
            c = i if d == 0 else nc - 1 - i
            r0 = pl.multiple_of(c * CHUNK, CHUNK)
            head = 2 * pair + hh
            ji = LANE_I + d * H_A + head
            jf = LANE_F + d * H_A + head
            jf2 = LANE_F2 + d * H_A + head
            gates = g_ref[pl.ds(r0, CHUNK), :]
            gi = gates + bi_row
            gf = _log_sigmoid(gates + bf_row)
            mask = tril if d == 0 else triu
            cum = jnp.dot(mask.astype(f32), gf, precision=HIGHEST, preferred_element_type=f32)
            dmat = jnp.where(mask, _pair_differences(cum, lane, jf, jf2, gi, ji), -jnp.inf)
            bcol = _lane_col(cum, lane, jf)
            icol = _lane_col(gi, lane, ji)
            blast = bcol[CHUNK - 1:CHUNK, :] if d == 0 else bcol[0:1, :]
            m_inter = bcol + m
            m_t = jnp.maximum(m_inter, jnp.max(dmat, axis=1, keepdims=True))
            a_inter = jnp.exp(m_inter - m_t)
            q = (qp_ref[pl.ds(r0, CHUNK), hh * DQK_A:(hh + 1) * DQK_A] * DQK_A ** -0.5).astype(bf16)
            k = kp_ref[pl.ds(r0, CHUNK), hh * DQK_A:(hh + 1) * DQK_A].astype(bf16)
            v_aug = jnp.concatenate([v_refs[hh][pl.ds(r0, CHUNK), :], ones_col], axis=1)
            qk = lax.dot_general(q, k, NT_DIMS, preferred_element_type=f32)
            sw = qk * jnp.exp(dmat - m_t)
            nd = (jnp.dot(sw.astype(bf16), v_aug.astype(bf16), preferred_element_type=f32)
                  + a_inter * jnp.dot(q, c_aug.astype(bf16), preferred_element_type=f32))
            den = nd[:, DV_A:DV_A + 1]
            hval = nd[:, :DV_A] / jnp.maximum(jnp.abs(den), jnp.exp(-m_t))
            h_ref[pl.ds(r0, CHUNK), hh * DV_A:(hh + 1) * DV_A] += hval
            gs = blast - bcol + icol
            m_new = jnp.maximum(blast + m, jnp.max(gs, axis=0, keepdims=True))
            ws = jnp.exp(gs - m_new)
            kv = lax.dot_general(k, (ws * v_aug).astype(bf16), TN_DIMS, preferred_element_type=f32)
            out.append((jnp.exp(blast + m - m_new) * c_aug + kv, m_new))
        return tuple(out)

    final = lax.fori_loop(0, nc, body, tuple(init))
    for s_idx, (hh, d) in enumerate(streams):
        c_aug, m = final[s_idx]
        cf_ref[d, hh] = c_aug[:, :DV_A]
        nf_ref[d, hh] = c_aug[:, DV_A:DV_A + 1]
        mf_ref[d, hh] = m


def _mlstm_call(proj, gates, par, nb, seq, off, state):
    zero_init = state is None
    blk = lambda col: pl.BlockSpec((seq, LANES), col)
    in_specs = [blk(lambda b, p: (off + b, p)),
                blk(lambda b, p: (off + b, 2 + p)),
                blk(lambda b, p: (off + b, 4 + 2 * p)),
                blk(lambda b, p: (off + b, 5 + 2 * p)),
                blk(lambda b, p: (off + b, 0)),
                pl.BlockSpec((8, LANES), lambda b, p: (0, 0))]
    st_specs = [pl.BlockSpec((None, 2, 2, DQK_A, DV_A), lambda b, p: (b, 0, p, 0, 0)),
                pl.BlockSpec((None, 2, 2, DQK_A, 1), lambda b, p: (b, 0, p, 0, 0)),
                pl.BlockSpec((None, 2, 2, 1, 1), lambda b, p: (b, 0, p, 0, 0))]
    args = [proj, proj, proj, proj, gates, par]
    if not zero_init:
        in_specs += st_specs
        args += list(state)
    return pl.pallas_call(
        functools.partial(_mlstm_kernel, seq=seq, zero_init=zero_init),
        grid=(nb, H_A // 2),
        in_specs=in_specs,
        out_specs=[pl.BlockSpec((seq, 2 * DV_A), lambda b, p: (b, p))] + st_specs,
        out_shape=[jax.ShapeDtypeStruct((nb * seq, H_A * DV_A), f32),
                   jax.ShapeDtypeStruct((nb, 2, H_A, DQK_A, DV_A), f32),
                   jax.ShapeDtypeStruct((nb, 2, H_A, DQK_A, 1), f32),
                   jax.ShapeDtypeStruct((nb, 2, H_A, 1, 1), f32)],
        compiler_params=pltpu.CompilerParams(vmem_limit_bytes=VMEM_LIMIT),
        name="mlstm",
    )(*args)


def _dconv_kernel(x_ref, w_ref, o_ref, *, seq):
    j = pl.program_id(1)
    x = x_ref[...]
    row = lax.broadcasted_iota(jnp.int32, (seq, 1), 0)
    acc = x * w_ref[CONV_K // 2:CONV_K // 2 + 1, :]
    for tap in range(CONV_K):
        delta = tap - CONV_K // 2
        if delta == 0:
            continue
        shifted = pltpu.roll(x, shift=(-delta) % seq, axis=0)
        ok = (row + delta >= 0) & (row + delta < seq)
        acc = acc + jnp.where(ok, shifted, 0.0) * w_ref[tap:tap + 1, :]
    y = _silu(acc)
    unit = y * lax.rsqrt(jnp.sum(y * y, axis=-1, keepdims=True) + EPS)
    scale = jnp.where(j < H_B, DK_B ** -0.5, 1.0)
    o_ref[...] = jnp.where(j < 2 * H_B, unit * scale, y)


def _dconv_call(proj, conv_w8, nb, seq, off):
    ncol = 3 * H_B
    col0 = (2 * H_A * DQK_A + 2 * H_A * DV_A) // LANES
    return pl.pallas_call(
        functools.partial(_dconv_kernel, seq=seq),
        grid=(nb, ncol),
        in_specs=[pl.BlockSpec((seq, LANES), lambda b, j: (off + b, col0 + j)),
                  pl.BlockSpec((8, LANES), lambda b, j: (0, j))],
        out_specs=pl.BlockSpec((seq, LANES), lambda b, j: (b, j)),
        out_shape=jax.ShapeDtypeStruct((nb * seq, ncol * LANES), f32),
        compiler_params=pltpu.CompilerParams(vmem_limit_bytes=VMEM_LIMIT),
        name="delta_conv",
    )(proj, conv_w8)


def _split_bf16(x):
    hi = x.astype(bf16)
    return hi, (x - hi.astype(f32)).astype(bf16)


def _matmul_3pass(m, x):
    mh, ml = _split_bf16(m)
    xh, xl = _split_bf16(x)
    return jnp.dot(jnp.concatenate([mh, ml, mh], axis=1), jnp.concatenate([xh, xh, xl], axis=0),
                   preferred_element_type=f32)


def _unit_triangular_solve(n, x):
    levels = CHUNK.bit_length() - 1
    w = x.shape[1]
    for lvl in range(levels):
        if lvl < levels - 1:
            r = _matmul_3pass(n, jnp.concatenate([x, n], axis=1))
            x = x + r[:, :w]
            n = r[:, w:]
        else:
            x = x + _matmul_3pass(n, x)
    return x


def _delta_kernel(*refs, seq, zero_init):
    if zero_init:
        q0_ref, q1_ref, k0_ref, k1_ref, v0_ref, v1_ref, g_ref, par_ref, o_ref, sf_ref = refs
    else:
        q0_ref, q1_ref, k0_ref, k1_ref, v0_ref, v1_ref, g_ref, par_ref, s0_ref, o_ref, sf_ref = refs
    nc = seq // CHUNK
    pair = pl.program_id(1)
    lane = lax.broadcasted_iota(jnp.int32, (1, LANES), 1)
    tril, triu, stril, striu = _chunk_masks()
    neg_a_row = -jnp.exp(par_ref[0:1, :])
    dt_row = par_ref[1:2, :]
    q_refs, k_refs, v_refs = (q0_ref, q1_ref), (k0_ref, k1_ref), (v0_ref, v1_ref)
    streams = [(hh, d) for hh in range(2) for d in range(2)]

    o_ref[...] = jnp.zeros_like(o_ref)
    init = tuple(jnp.zeros((DK_B, DV_B), f32) if zero_init else s0_ref[d, hh] for hh, d in streams)

    def body(i, carry):
        out = []
        for s_idx, (hh, d) in enumerate(streams):
            s = carry[s_idx]
            c = i if d == 0 else nc - 1 - i
            r0 = pl.multiple_of(c * CHUNK, CHUNK)
            head = 2 * pair + hh
            jb = LANE_BETA + d * H_B + head
            ja = LANE_A + d * H_B + head
            ja2 = LANE_A2 + d * H_B + head
            gates = g_ref[pl.ds(r0, CHUNK), :]
            beta = _lane_col(jax.nn.sigmoid(gates), lane, jb)
            glog = neg_a_row * _softplus(gates + dt_row)
            mask, smask = (tril, stril) if d == 0 else (triu, striu)
            cum = jnp.dot(mask.astype(f32), glog, precision=HIGHEST, preferred_element_type=f32)
            decay = jnp.exp(jnp.where(mask, _pair_differences(cum, lane, ja, ja2), -jnp.inf))
            gcol = _lane_col(cum, lane, ja)
            glast = gcol[CHUNK - 1:CHUNK, :] if d == 0 else gcol[0:1, :]
            q = q_refs[hh][pl.ds(r0, CHUNK), :]
            k = k_refs[hh][pl.ds(r0, CHUNK), :]
            v = v_refs[hh][pl.ds(r0, CHUNK), :]
            kb16 = k.astype(bf16)
            kbeta = k * beta
            eg = jnp.exp(gcol)
            kk = lax.dot_general(kbeta.astype(bf16), kb16, NT_DIMS, preferred_element_type=f32)
            nmat = -jnp.where(smask, kk * decay, 0.0)
            uw = _unit_triangular_solve(nmat, jnp.concatenate([v * beta, kbeta * eg], axis=1))
            u = uw[:, :DV_B]
            w = uw[:, DV_B:]
            qk = jnp.where(mask, lax.dot_general(q.astype(bf16), kb16, NT_DIMS, preferred_element_type=f32) * decay, 0.0)
            qg = (q * eg).astype(bf16)
            kd = (k * jnp.exp(glast - gcol)).astype(bf16)
            s16 = s.astype(bf16)
            v_new = u - jnp.dot(w.astype(bf16), s16, preferred_element_type=f32)
            v_new16 = v_new.astype(bf16)
            o = (jnp.dot(qg, s16, preferred_element_type=f32)
                 + jnp.dot(qk.astype(bf16), v_new16, preferred_element_type=f32))
            o_ref[pl.ds(r0, CHUNK), hh * DV_B:(hh + 1) * DV_B] += o
            out.append(s * jnp.exp(glast) + lax.dot_general(kd, v_new16, TN_DIMS, preferred_element_type=f32))
        return tuple(out)

    final = lax.fori_loop(0, nc, body, init)
    for s_idx, (hh, d) in enumerate(streams):
        sf_ref[d, hh] = final[s_idx]


def _delta_call(qkv, gates, par, nb, seq, off, state):
    zero_init = state is None
    blk = lambda col: pl.BlockSpec((seq, LANES), col)
    in_specs = [blk(lambda b, p: (b, 2 * p)), blk(lambda b, p: (b, 2 * p + 1)),
                blk(lambda b, p: (b, H_B + 2 * p)), blk(lambda b, p: (b, H_B + 2 * p + 1)),
                blk(lambda b, p: (b, 2 * H_B + 2 * p)), blk(lambda b, p: (b, 2 * H_B + 2 * p + 1)),
                blk(lambda b, p: (off + b, 0)),
                pl.BlockSpec((8, LANES), lambda b, p: (0, 0))]
    st_spec = pl.BlockSpec((None, 2, 2, DK_B, DV_B), lambda b, p: (b, 0, p, 0, 0))
    args = [qkv] * 6 + [gates, par]
    if not zero_init:
        in_specs.append(st_spec)
        args.append(state)
    return pl.pallas_call(
        functools.partial(_delta_kernel, seq=seq, zero_init=zero_init),
        grid=(nb, H_B // 2),
        in_specs=in_specs,
        out_specs=[pl.BlockSpec((seq, 2 * DV_B), lambda b, p: (b, p)), st_spec],
        out_shape=[jax.ShapeDtypeStruct((nb * seq, H_B * DV_B), f32),
                   jax.ShapeDtypeStruct((nb, 2, H_B, DK_B, DV_B), f32)],
        compiler_params=pltpu.CompilerParams(vmem_limit_bytes=VMEM_LIMIT),
        name="delta",
    )(*args)


def _head_rms(x):
    return x * lax.rsqrt(jnp.mean(x * x, axis=-1, keepdims=True) + EPS)


def _ab_out_kernel(x_ref, hs_ref, os_ref, om_ref, zd_ref, gm_ref, gd_ref, w_ref, mod_ref, o_ref, *, gate_row):
    parts = []
    for h in range(H_A):
        sl = slice(h * DV_A, (h + 1) * DV_A)
        parts.append(_head_rms(hs_ref[:, sl]) * gm_ref[:, sl] * jax.nn.sigmoid(om_ref[:, sl]))
    for h in range(H_B):
        sl = slice(h * DV_B, (h + 1) * DV_B)
        parts.append(_head_rms(os_ref[:, sl]) * gd_ref[:, sl] * _silu(zd_ref[:, sl]))
    cat = jnp.concatenate(parts, axis=1).astype(bf16)
    y = jnp.dot(cat, w_ref[...], preferred_element_type=f32)
    o_ref[...] = x_ref[...] + mod_ref[gate_row:gate_row + 1, :] * y


def _ab_out_call(x, hsum, osum, proj, gm_row, gd_row, w_out, mod_l, gate_row, tm=512):
    wide = H_A * DV_A
    om_blk = (2 * H_A * DQK_A) // wide + 1
    zd_blk = AB_MAIN // wide - 1
    return pl.pallas_call(
        functools.partial(_ab_out_kernel, gate_row=gate_row),
        grid=(N_TOK // tm,),
        in_specs=[pl.BlockSpec((tm, D_MODEL), lambda i: (i, 0)),
                  pl.BlockSpec((tm, wide), lambda i: (i, 0)),
                  pl.BlockSpec((tm, wide), lambda i: (i, 0)),
                  pl.BlockSpec((tm, wide), lambda i: (i, om_blk)),
                  pl.BlockSpec((tm, wide), lambda i: (i, zd_blk)),
                  pl.BlockSpec((1, wide), lambda i: (0, 0)),
                  pl.BlockSpec((1, wide), lambda i: (0, 0)),
                  pl.BlockSpec((2 * wide, D_MODEL), lambda i: (0, 0)),
                  pl.BlockSpec((None, N_ADA, D_MODEL), lambda i: (_mod_row_index(i, tm), 0, 0))],
        out_specs=pl.BlockSpec((tm, D_MODEL), lambda i: (i, 0)),
        out_shape=jax.ShapeDtypeStruct((N_TOK, D_MODEL), f32),
        compiler_params=pltpu.CompilerParams(vmem_limit_bytes=VMEM_LIMIT),
        name="ab_out",
    )(x, hsum, osum, proj, proj, gm_row, gd_row, w_out, mod_l)


def _na_out_kernel(x_ref, a_ref, w_ref, mod_ref, o_ref, *, gate_row):
    y = jnp.dot(a_ref[...].astype(bf16), w_ref[...], preferred_element_type=f32)
    o_ref[...] = x_ref[...] + mod_ref[gate_row:gate_row + 1, :] * y


def _na_out_call(x, attn, w_out, mod_l, gate_row, tm=512):
    return pl.pallas_call(
        functools.partial(_na_out_kernel, gate_row=gate_row),
        grid=(N_TOK // tm,),
        in_specs=[pl.BlockSpec((tm, D_MODEL), lambda i: (i, 0)),
                  pl.BlockSpec((tm, NA_W), lambda i: (i, 0)),
                  pl.BlockSpec((NA_W, D_MODEL), lambda i: (0, 0)),
                  pl.BlockSpec((None, N_ADA, D_MODEL), lambda i: (_mod_row_index(i, tm), 0, 0))],
        out_specs=pl.BlockSpec((tm, D_MODEL), lambda i: (i, 0)),
        out_shape=jax.ShapeDtypeStruct((N_TOK, D_MODEL), f32),
        compiler_params=pltpu.CompilerParams(vmem_limit_bytes=VMEM_LIMIT),
        name="na_out",
    )(x, attn, w_out, mod_l)


CTX_HEADS = 8


def _ctx_attn_kernel(q_ref, k_ref, v_ref, o_ref):
    outs = []
    for hh in range(CTX_HEADS):
        sl = slice(hh * DH_C, (hh + 1) * DH_C)
        q = q_ref[:, sl].astype(bf16)
        k = k_ref[:, sl].astype(bf16)
        v = v_ref[:, sl].astype(bf16)
        s = lax.dot_general(q, k, NT_DIMS, preferred_element_type=f32) * DH_C ** -0.5
        p = jnp.exp(s - jnp.max(s, axis=1, keepdims=True))
        o = jnp.dot(p.astype(bf16), v, preferred_element_type=f32)
        outs.append(o / jnp.sum(p, axis=1, keepdims=True))
    o_ref[...] = jnp.concatenate(outs, axis=1)


def _ctx_attn_call(proj):
    w = CTX_HEADS * DH_C
    nblk = NA_W // w
    return pl.pallas_call(
        _ctx_attn_kernel,
        grid=(BATCH, nblk),
        in_specs=[pl.BlockSpec((SEQ, w), lambda b, j: (b, j)),
                  pl.BlockSpec((SEQ, w), lambda b, j: (b, nblk + j)),
                  pl.BlockSpec((SEQ, w), lambda b, j: (b, 2 * nblk + j))],
        out_specs=pl.BlockSpec((SEQ, w), lambda b, j: (b, j)),
        out_shape=jax.ShapeDtypeStruct((N_PROMPT, NA_W), f32),
        compiler_params=pltpu.CompilerParams(vmem_limit_bytes=VMEM_LIMIT),
        name="ctx_attn",
    )(proj, proj, proj)


QROWS = 4
QBLK = QROWS * GRID_W
KROWS_MID = QROWS + WIN_R - 1
N_RIDX = 2 * WIN_R - 1
N_CIDX = 2 * WIN_C - 1
N_QBLK = GRID_ROWS // QROWS


def _nattn_kernel(rb_ref, q_ref, k_ref, v_ref, ck_ref, cv_ref, o_ref, tb_scr, bmid_scr, btop_scr, bbot_scr):
    pair = pl.program_id(1)
    qc = lax.broadcasted_iota(jnp.int32, (GRID_W, GRID_W), 0)
    kc = lax.broadcasted_iota(jnp.int32, (GRID_W, GRID_W), 1)
    cs = jnp.clip(qc - WIN_C // 2, 0, GRID_W - WIN_C)
    valid = (kc >= cs) & (kc < cs + WIN_C)
    cidx = jnp.clip(kc - qc, -(WIN_C - 1), WIN_C - 1) + WIN_C - 1
    neg = jnp.full((GRID_W, GRID_W), -jnp.inf, f32)
    scale = DH_C ** -0.5

    for hh in range(2):
        head = 2 * pair + hh
        sl = slice(hh * DH_C, (hh + 1) * DH_C)

        def ridx_body(ri, _):
            base = (head * N_RIDX + ri) * N_CIDX

            def cidx_body(j, acc):
                return jnp.where(cidx == j, rb_ref[base + j], acc)

            tile = lax.fori_loop(0, N_CIDX, cidx_body, jnp.zeros((GRID_W, GRID_W), f32))
            tb_scr[ri] = jnp.where(valid, tile, -jnp.inf)
            return 0

        lax.fori_loop(0, N_RIDX, ridx_body, 0)

        for i in range(QROWS):
            rs = slice(i * GRID_W, (i + 1) * GRID_W)
            for jj in range(KROWS_MID):
                inside = 0 <= jj - i < WIN_R
                bmid_scr[rs, jj * GRID_W:(jj + 1) * GRID_W] = tb_scr[jj - i + WIN_R // 2 - 1] if inside else neg
            for jj in range(WIN_R):
                btop_scr[rs, jj * GRID_W:(jj + 1) * GRID_W] = tb_scr[jj - i + WIN_R - 1]
                bbot_scr[rs, jj * GRID_W:(jj + 1) * GRID_W] = tb_scr[jj - i + WIN_R // 2 - 1]

        kctx = ck_ref[hh].astype(bf16)
        vctx = cv_ref[hh].astype(bf16)

        def block(q_start, k_start, k_rows, bias_scr):
            nk = k_rows * GRID_W
            q = q_ref[pl.ds(q_start, QBLK), sl].astype(bf16)
            ku = k_ref[pl.ds(k_start, nk), sl].astype(bf16)
            vu = v_ref[pl.ds(k_start, nk), sl].astype(bf16)
            s_loc = lax.dot_general(q, ku, NT_DIMS, preferred_element_type=f32) * scale + bias_scr[...]
            s_ctx = lax.dot_general(q, kctx, NT_DIMS, preferred_element_type=f32) * scale
            m = jnp.maximum(jnp.max(s_loc, axis=1, keepdims=True), jnp.max(s_ctx, axis=1, keepdims=True))
            p_loc = jnp.exp(s_loc - m)
            p_ctx = jnp.exp(s_ctx - m)
            denom = jnp.sum(p_loc, axis=1, keepdims=True) + jnp.sum(p_ctx, axis=1, keepdims=True)
            o = (jnp.dot(p_loc.astype(bf16), vu, preferred_element_type=f32)
                 + jnp.dot(p_ctx.astype(bf16), vctx, preferred_element_type=f32))
            o_ref[pl.ds(q_start, QBLK), sl] = o / denom

        block(0, 0, WIN_R, btop_scr)

        def mid_body(blk, _):
            q_start = pl.multiple_of(blk * QBLK, QBLK)
            k_start = pl.multiple_of(blk * QBLK - (WIN_R // 2) * GRID_W, QBLK)
            block(q_start, k_start, KROWS_MID, bmid_scr)
            return 0

        lax.fori_loop(1, N_QBLK - 1, mid_body, 0)
        block((N_QBLK - 1) * QBLK, (GRID_ROWS - WIN_R) * GRID_W, WIN_R, bbot_scr)


def _nattn_call(proj, cache_k, cache_v, rel_bias, off):
    npair = H_C // 2
    return pl.pallas_call(
        _nattn_kernel,
        grid=(DEC_BATCH, npair),
        in_specs=[pl.BlockSpec(memory_space=pltpu.SMEM),
                  pl.BlockSpec((DEC_SEQ, LANES), lambda b, p: (off + b, p)),
                  pl.BlockSpec((DEC_SEQ, LANES), lambda b, p: (off + b, npair + p)),
                  pl.BlockSpec((DEC_SEQ, LANES), lambda b, p: (off + b, 2 * npair + p)),
                  pl.BlockSpec((None, 2, PAST_LEN, DH_C), lambda b, p: (b, p, 0, 0)),
                  pl.BlockSpec((None, 2, PAST_LEN, DH_C), lambda b, p: (b, p, 0, 0))],
        out_specs=pl.BlockSpec((DEC_SEQ, LANES), lambda b, p: (b, p)),
        out_shape=jax.ShapeDtypeStruct((N_SAMPLE, NA_W), f32),
        scratch_shapes=[pltpu.VMEM((N_RIDX, GRID_W, GRID_W), f32),
                        pltpu.VMEM((QBLK, KROWS_MID * GRID_W), f32),
                        pltpu.VMEM((QBLK, WIN_R * GRID_W), f32),
                        pltpu.VMEM((QBLK, WIN_R * GRID_W), f32)],
        compiler_params=pltpu.CompilerParams(vmem_limit_bytes=VMEM_LIMIT),
        name="nattn",
    )(rel_bias.reshape(-1), proj, proj, proj, cache_k, cache_v)


def _lane_row(pieces):
    row = jnp.zeros((LANES,), f32)
    for off, vals in pieces:
        row = row.at[off:off + vals.shape[0]].set(vals.astype(f32))
    return row


def _param_rows(rows):
    out = jnp.zeros((8, LANES), f32)
    for r, row in enumerate(rows):
        out = out.at[r].set(row)
    return out


def kernel(x_prompt, x_sample, c, state_mlstm_C, state_mlstm_n, state_mlstm_m, state_delta_S, cache_na_k, cache_na_v, c_ctx, ada_w, ada_b, norm_g, ffn_wg, ffn_wu, ffn_wd, ab_w_in, ab_w_out, mlstm_b_i, mlstm_b_f, mlstm_norm_g, delta_conv_w, delta_a_log, delta_dt_bias, delta_norm_g, na_w_in, na_w_out, na_rel_bias, final_norm_g):
    x = jnp.concatenate([x_prompt.reshape(N_PROMPT, D_MODEL), x_sample.reshape(N_SAMPLE, D_MODEL)], axis=0)
    mods = _ada_call(jnp.concatenate([c_ctx[None, :], c], axis=0), ada_w, ada_b)
    mods = mods.reshape(DEPTH, 3, N_ADA, D_MODEL)
    gf_row = final_norm_g.reshape(1, D_MODEL)
    s_off = N_PROMPT // DEC_SEQ
    new_c, new_n, new_m, new_s, new_k, new_v = [], [], [], [], [], []

    for l in range(DEPTH):
        mod_l = mods[l]
        a = l // 2
        x = _ffn_call(x, mod_l, norm_g[l, 0].reshape(1, D_MODEL), ffn_wg[l, 0].astype(bf16),
                      ffn_wu[l, 0].astype(bf16), ffn_wd[l, 0].astype(bf16), gf_row, rows=(0, 1, 2), final=False)
        g_mix = norm_g[l, 1].reshape(1, D_MODEL)
        if l % 2 == 0:
            w_in = ab_w_in[a]
            i0 = 2 * H_A * DQK_A + 2 * H_A * DV_A
            d0 = i0 + 4 * H_A
            b0 = d0 + 2 * H_B * DK_B + 2 * H_B * DV_B
            w_main = jnp.concatenate([w_in[:, :i0], w_in[:, d0:b0]], axis=1).astype(bf16)
            i_pre, f_pre = w_in[:, i0:i0 + 2 * H_A], w_in[:, i0 + 2 * H_A:d0]
            b_pre, a_pre = w_in[:, b0:b0 + 2 * H_B], w_in[:, b0 + 2 * H_B:]
            w_gate = jnp.concatenate([i_pre, f_pre, b_pre, a_pre, f_pre, a_pre,
                                      jnp.zeros((D_MODEL, LANES - 6 * 2 * H_A), f32)], axis=1).astype(bf16)
            proj, gates = _proj_call(x, mod_l, g_mix, w_main, w_gate, rows=(3, 4))

            bi = mlstm_b_i[a].reshape(-1)
            bf_ = mlstm_b_f[a].reshape(-1)
            par_m = _param_rows([_lane_row([(LANE_I, bi)]), _lane_row([(LANE_F, bf_), (LANE_F2, bf_)])])
            al = delta_a_log[a].reshape(-1)
            dt = delta_dt_bias[a].reshape(-1)
            par_d = _param_rows([_lane_row([(LANE_A, al), (LANE_A2, al)]), _lane_row([(LANE_A, dt), (LANE_A2, dt)])])
            conv_w8 = jnp.concatenate([delta_conv_w[a], jnp.zeros((8 - CONV_K, 3 * H_B * DK_B), f32)], axis=0)

            hp, cn, nn_, mn = _mlstm_call(proj, gates, par_m, BATCH, SEQ, 0, None)
            st = (state_mlstm_C[:, a], state_mlstm_n[:, a][..., None], state_mlstm_m[:, a][..., None, None])
            hs, _, _, _ = _mlstm_call(proj, gates, par_m, DEC_BATCH, DEC_SEQ, s_off, st)
            qkv_p = _dconv_call(proj, conv_w8, BATCH, SEQ, 0)
            qkv_s = _dconv_call(proj, conv_w8, DEC_BATCH, DEC_SEQ, s_off)
            op, sn = _delta_call(qkv_p, gates, par_d, BATCH, SEQ, 0, None)
            os_, _ = _delta_call(qkv_s, gates, par_d, DEC_BATCH, DEC_SEQ, s_off, state_delta_S[:, a])
            new_c.append(cn)
            new_n.append(nn_[..., 0])
            new_m.append(mn[..., 0, 0])
            new_s.append(sn)
            x = _ab_out_call(x, jnp.concatenate([hp, hs], axis=0), jnp.concatenate([op, os_], axis=0), proj,
                             mlstm_norm_g[a].reshape(1, -1), jnp.tile(delta_norm_g[a], H_B).reshape(1, -1),
                             ab_w_out[a].astype(bf16), mod_l, gate_row=5)
        else:
            proj = _proj_call(x, mod_l, g_mix, na_w_in[a].astype(bf16), None, rows=(3, 4))[0]
            attn_p = _ctx_attn_call(proj)
            attn_s = _nattn_call(proj, cache_na_k[:, a], cache_na_v[:, a], na_rel_bias[a], s_off)
            kp = proj[:N_PROMPT, NA_W:2 * NA_W].reshape(BATCH, SEQ, H_C, DH_C).transpose(0, 2, 1, 3)
            vp = proj[:N_PROMPT, 2 * NA_W:].reshape(BATCH, SEQ, H_C, DH_C).transpose(0, 2, 1, 3)
            new_k.append(kp)
            new_v.append(vp)
            x = _na_out_call(x, jnp.concatenate([attn_p, attn_s], axis=0), na_w_out[a].astype(bf16), mod_l, gate_row=5)
        x = _ffn_call(x, mod_l, norm_g[l, 2].reshape(1, D_MODEL), ffn_wg[l, 1].astype(bf16),
                      ffn_wu[l, 1].astype(bf16), ffn_wd[l, 1].astype(bf16), gf_row, rows=(6, 7, 8),
                      final=(l == DEPTH - 1))

    y_prompt = x[:N_PROMPT].reshape(BATCH, SEQ, D_MODEL)
    y_sample = x[N_PROMPT:].reshape(DEC_BATCH, DEC_SEQ, D_MODEL)
    return (y_prompt, y_sample, jnp.stack(new_c, axis=1), jnp.stack(new_n, axis=1), jnp.stack(new_m, axis=1),
            jnp.stack(new_s, axis=1), jnp.stack(new_k, axis=1), jnp.stack(new_v, axis=1))
```

```python
import functools

import jax
import jax.numpy as jnp
from jax import lax
from jax.experimental import pallas as pl
from jax.experimental.pallas import tpu as pltpu

f32 = jnp.float32
bf16 = jnp.bfloat16
HIGHEST = lax.Precision.HIGHEST

D_MODEL = 1024
BATCH = 16
SEQ = 256
DEPTH = 2
DEC_BATCH = 2
DEC_SEQ = 2048
PAST_LEN = 256
GRID_W = 64
GRID_ROWS = DEC_SEQ // GRID_W
D_FF = 2816
N_ADA = 9
EPS = 1e-6
CHUNK = 64
H_A, DQK_A, DV_A = 4, 64, 128
H_B, DK_B, DV_B = 4, 128, 128
CONV_K = 5
H_C, DH_C = 16, 64
WIN_R, WIN_C = 8, 16
NA_W = H_C * DH_C

N_PROMPT = BATCH * SEQ
N_SAMPLE = DEC_BATCH * DEC_SEQ
N_TOK = N_PROMPT + N_SAMPLE
AB_MAIN = 2 * H_A * DQK_A + 2 * H_A * DV_A + 2 * H_B * DK_B + 2 * H_B * DV_B
LANES = 128
VMEM_LIMIT = 56 * 1024 * 1024

LANE_I, LANE_F, LANE_BETA, LANE_A = 0, 8, 16, 24

NT_DIMS = (((1,), (1,)), ((), ()))
TN_DIMS = (((0,), (0,)), ((), ()))

CHUNKS_PER_TRIP = 2


def _softplus(x):
    return jnp.maximum(x, 0.0) + jnp.log1p(jnp.exp(-jnp.abs(x)))


def _log_sigmoid(x):
    return -_softplus(-x)


def _silu(x):
    return x * jax.nn.sigmoid(x)


def _mod_row_index(i, tm):
    n_p = N_PROMPT // tm
    per_b = DEC_SEQ // tm
    return jnp.where(i < n_p, 0, 1 + (i - n_p) // per_b)


def _modulated(x, g_row, shift_row, scale_row):
    y = x * lax.rsqrt(jnp.mean(x * x, axis=-1, keepdims=True) + EPS) * g_row
    return y * (1.0 + scale_row) + shift_row


def _ada_kernel(cb_ref, w_ref, b_ref, o_ref):
    tn = w_ref.shape[-1]
    for r in range(3):
        cb = cb_ref[r]
        s = cb * jax.nn.sigmoid(cb)
        parts = [jnp.sum(w_ref[:, j * LANES:(j + 1) * LANES] * s, axis=0, keepdims=True)
                 for j in range(tn // LANES)]
        o_ref[r:r + 1, :] = jnp.concatenate(parts, axis=1) + b_ref[...]


def _ada_call(cond3, ada_w, ada_b):
    tn = D_MODEL
    cb = jnp.broadcast_to(cond3[:, :, None], (3, D_MODEL, LANES))
    return pl.pallas_call(
        _ada_kernel,
        grid=(DEPTH, N_ADA * D_MODEL // tn),
        in_specs=[pl.BlockSpec((3, D_MODEL, LANES), lambda l, j: (0, 0, 0)),
                  pl.BlockSpec((None, D_MODEL, tn), lambda l, j: (l, 0, j)),
                  pl.BlockSpec((None, 1, tn), lambda l, j: (l, 0, j))],
        out_specs=pl.BlockSpec((None, 3, tn), lambda l, j: (l, 0, j)),
        out_shape=jax.ShapeDtypeStruct((DEPTH, 3, N_ADA * D_MODEL), f32),
        compiler_params=pltpu.CompilerParams(vmem_limit_bytes=VMEM_LIMIT),
        name="ada_mod",
    )(cb, ada_w, ada_b.reshape(DEPTH, 1, N_ADA * D_MODEL))


def _ffn_kernel(x_ref, mod_ref, g_ref, wg_ref, wu_ref, wd_ref, gf_ref, o_ref, h_scr, acc_scr, *, rows, final):
    j = pl.program_id(1)

    @pl.when(j == 0)
    def _():
        h = _modulated(x_ref[...], g_ref[...], mod_ref[rows[0]:rows[0] + 1, :], mod_ref[rows[1]:rows[1] + 1, :])
        h_scr[...] = h.astype(bf16)
        acc_scr[...] = jnp.zeros_like(acc_scr)

    h = h_scr[...]
    g = jnp.dot(h, wg_ref[...].astype(bf16), preferred_element_type=f32)
    u = jnp.dot(h, wu_ref[...].astype(bf16), preferred_element_type=f32)
    a = (_silu(g) * u).astype(bf16)
    acc_scr[...] += jnp.dot(a, wd_ref[...].astype(bf16), preferred_element_type=f32)

    @pl.when(j == pl.num_programs(1) - 1)
    def _():
        xn = x_ref[...] + (0.5 * mod_ref[rows[2]:rows[2] + 1, :]) * acc_scr[...]
        if final:
            xn = xn * lax.rsqrt(jnp.mean(xn * xn, axis=-1, keepdims=True) + EPS) * gf_ref[...]
        o_ref[...] = xn


def _ffn_call(x, mod_l, g_row, wg, wu, wd, gf_row, rows, final, tm=1024, tf=256):
    return pl.pallas_call(
        functools.partial(_ffn_kernel, rows=rows, final=final),
        grid=(N_TOK // tm, D_FF // tf),
        in_specs=[pl.BlockSpec((tm, D_MODEL), lambda i, j: (i, 0)),
                  pl.BlockSpec((None, N_ADA, D_MODEL), lambda i, j: (_mod_row_index(i, tm), 0, 0)),
                  pl.BlockSpec((1, D_MODEL), lambda i, j: (0, 0)),
                  pl.BlockSpec((D_MODEL, tf), lambda i, j: (0, j)),
                  pl.BlockSpec((D_MODEL, tf), lambda i, j: (0, j)),
                  pl.BlockSpec((tf, D_MODEL), lambda i, j: (j, 0)),
                  pl.BlockSpec((1, D_MODEL), lambda i, j: (0, 0))],
        out_specs=pl.BlockSpec((tm, D_MODEL), lambda i, j: (i, 0)),
        out_shape=jax.ShapeDtypeStruct((N_TOK, D_MODEL), f32),
        scratch_shapes=[pltpu.VMEM((tm, D_MODEL), bf16), pltpu.VMEM((tm, D_MODEL), f32)],
        compiler_params=pltpu.CompilerParams(dimension_semantics=("parallel", "arbitrary"),
                                             vmem_limit_bytes=VMEM_LIMIT),
        name="ffn",
    )(x, mod_l, g_row, wg, wu, wd, gf_row)


def _proj_kernel(*refs, rows, with_gates):
    if with_gates:
        x_ref, mod_ref, g_ref, w_ref, wgate_ref, o_ref, og_ref, h_scr = refs
    else:
        x_ref, mod_ref, g_ref, w_ref, o_ref, h_scr = refs
    j = pl.program_id(1)

    @pl.when(j == 0)
    def _():
        h = _modulated(x_ref[...], g_ref[...], mod_ref[rows[0]:rows[0] + 1, :], mod_ref[rows[1]:rows[1] + 1, :])
        hb = h.astype(bf16)
        h_scr[...] = hb
        if with_gates:
            og_ref[...] = jnp.dot(hb, wgate_ref[...].astype(bf16), preferred_element_type=f32)

    o_ref[...] = jnp.dot(h_scr[...], w_ref[...].astype(bf16), preferred_element_type=f32)


def _proj_call(x, mod_l, g_row, w, w_gate, rows, tm=1024, tn=512):
    n = w.shape[1]
    with_gates = w_gate is not None
    in_specs = [pl.BlockSpec((tm, D_MODEL), lambda i, j: (i, 0)),
                pl.BlockSpec((None, N_ADA, D_MODEL), lambda i, j: (_mod_row_index(i, tm), 0, 0)),
                pl.BlockSpec((1, D_MODEL), lambda i, j: (0, 0)),
                pl.BlockSpec((D_MODEL, tn), lambda i, j: (0, j))]
    out_specs = [pl.BlockSpec((tm, tn), lambda i, j: (i, j))]
    out_shape = [jax.ShapeDtypeStruct((N_TOK, n), f32)]
    args = [x, mod_l, g_row, w]
    if with_gates:
        in_specs.append(pl.BlockSpec((D_MODEL, LANES), lambda i, j: (0, 0)))
        out_specs.append(pl.BlockSpec((tm, LANES), lambda i, j: (i, 0)))
        out_shape.append(jax.ShapeDtypeStruct((N_TOK, LANES), f32))
        args.append(w_gate)
    return pl.pallas_call(
        functools.partial(_proj_kernel, rows=rows, with_gates=with_gates),
        grid=(N_TOK // tm, n // tn),
        in_specs=in_specs, out_specs=out_specs, out_shape=out_shape,
        scratch_shapes=[pltpu.VMEM((tm, D_MODEL), bf16)],
        compiler_params=pltpu.CompilerParams(dimension_semantics=("parallel", "arbitrary"),
                                             vmem_limit_bytes=VMEM_LIMIT),
        name="in_proj",
    )(*args)


def _chunk_masks():
    r = lax.broadcasted_iota(jnp.int32, (CHUNK, CHUNK), 0)
    c = lax.broadcasted_iota(jnp.int32, (CHUNK, CHUNK), 1)
    return r >= c, r <= c, r > c, r < c


def _lane_col(x, lane, j):
    return jnp.sum(jnp.where(lane == j, x, 0.0), axis=1, keepdims=True)


def _head_lane_select(base, pair):
    r = lax.broadcasted_iota(jnp.int32, (8, LANES), 0)
    ln = lax.broadcasted_iota(jnp.int32, (8, LANES), 1)
    return jnp.where((ln == base + 2 * pair + r) & (r < 2), 1.0, 0.0)


def _chunk_start(trip, u, d, nc):
    step = trip * CHUNKS_PER_TRIP + u
    c = step if d == 0 else nc - 1 - step
    return pl.multiple_of(c * CHUNK, CHUNK)


def _mlstm_kernel(*refs, seq, zero_init):
    if zero_init:
        qp_ref, kp_ref, v0_ref, v1_ref, g_ref, par_ref, h_ref, cf_ref, nf_ref, mf_ref = refs
    else:
        (qp_ref, kp_ref, v0_ref, v1_ref, g_ref, par_ref, c0_ref, n0_ref, m0_ref,
         h_ref, cf_ref, nf_ref, mf_ref) = refs
    nc = seq // CHUNK
    pair = pl.program_id(1)
    lane = lax.broadcasted_iota(jnp.int32, (1, LANES), 1)
    tril, triu, _, _ = _chunk_masks()
    masks = (tril, triu)
    masks_f = (tril.astype(f32), triu.astype(f32))
    sel = [_head_lane_select(LANE_F + d * H_A, pair) for d in range(2)]
    bi_row = par_ref[0:1, :]
    bf_row = par_ref[1:2, :]
    ones_col = jnp.where(lane == 0, 1.0, 0.0) + jnp.zeros((CHUNK, LANES), f32)
    v_refs = (v0_ref, v1_ref)
    streams = [(hh, d) for hh in range(2) for d in range(2)]

    h_ref[...] = jnp.zeros_like(h_ref)

    init = []
    for hh, d in streams:
        if zero_init:
            init.append((jnp.zeros((DQK_A, 2 * LANES), f32), jnp.zeros((1, 1), f32)))
        else:
            n_aug = jnp.where(lane == 0, n0_ref[d, hh], 0.0)
            init.append((jnp.concatenate([c0_ref[d, hh], n_aug], axis=1), m0_ref[d, hh]))

    def body(trip, carry):
        subs = range(CHUNKS_PER_TRIP)
        shared = {}
        for u in subs:
            for d in range(2):
                r0 = _chunk_start(trip, u, d, nc)
                gates = g_ref[pl.ds(r0, CHUNK), :]
                gi = gates + bi_row
                gf = _log_sigmoid(gates + bf_row)
                cum = jnp.dot(masks_f[d], gf, precision=HIGHEST, preferred_element_type=f32)
                stack = jnp.concatenate([cum, pltpu.roll(gi, LANE_F - LANE_I, axis=1)], axis=0)
                rows = lax.dot_general(sel[d], stack, NT_DIMS, precision=HIGHEST, preferred_element_type=f32)
                total = cum[CHUNK - 1:CHUNK, :] if d == 0 else cum[0:1, :]
                shared[u, d] = (r0, gi, cum, rows, total)
        chains = [(u, hh, d) for u in subs for hh, d in streams]
        st = {ch: {} for ch in chains}
        for ch in chains:
            u, hh, d = ch
            r0 = shared[u, d][0]
            s = st[ch]
            s["q"] = (qp_ref[pl.ds(r0, CHUNK), hh * DQK_A:(hh + 1) * DQK_A] * DQK_A ** -0.5).astype(bf16)
            s["k"] = kp_ref[pl.ds(r0, CHUNK), hh * DQK_A:(hh + 1) * DQK_A].astype(bf16)
            s["v_aug"] = jnp.concatenate([v_refs[hh][pl.ds(r0, CHUNK), :], ones_col], axis=1)
            s["qk"] = lax.dot_general(s["q"], s["k"], NT_DIMS, preferred_element_type=f32)
        for ch in chains:
            u, hh, d = ch
            _, gi, cum, rows, total = shared[u, d]
            s = st[ch]
            head = 2 * pair + hh
            jf = LANE_F + d * H_A + head
            ji = LANE_I + d * H_A + head
            bcol = _lane_col(cum, lane, jf)
            icol = _lane_col(gi, lane, ji)
            dmat = jnp.where(masks[d], bcol - rows[hh:hh + 1, :CHUNK] + rows[hh:hh + 1, CHUNK:], -jnp.inf)
            m_loc = jnp.max(dmat, axis=1, keepdims=True)
            s["sw"] = (s["qk"] * jnp.exp(dmat - m_loc)).astype(bf16)
            blast = _lane_col(total, lane, jf)
            gs = blast - bcol + icol
            ms_loc = jnp.max(gs, axis=0, keepdims=True)
            s["wv"] = (jnp.exp(gs - ms_loc) * s["v_aug"]).astype(bf16)
            s.update(bcol=bcol, m_loc=m_loc, blast=blast, ms_loc=ms_loc)
        for ch in chains:
            s = st[ch]
            s["num"] = jnp.dot(s["sw"], s["v_aug"].astype(bf16), preferred_element_type=f32)
            s["kv"] = lax.dot_general(s["k"], s["wv"], TN_DIMS, preferred_element_type=f32)
        state = list(carry)
        for u in subs:
            qc = [jnp.dot(st[u, hh, d]["q"], state[i][0].astype(bf16), preferred_element_type=f32)
                  for i, (hh, d) in enumerate(streams)]
            for i, (hh, d) in enumerate(streams):
                s = st[u, hh, d]
                c_aug, m = state[i]
                m_inter = s["bcol"] + m
                m_t = jnp.maximum(m_inter, s["m_loc"])
                nd = jnp.exp(s["m_loc"] - m_t) * s["num"] + jnp.exp(m_inter - m_t) * qc[i]
                den = nd[:, DV_A:DV_A + 1]
                hval = nd[:, :DV_A] / jnp.maximum(jnp.abs(den), jnp.exp(-m_t))
                r0 = shared[u, d][0]
                h_ref[pl.ds(r0, CHUNK), hh * DV_A:(hh + 1) * DV_A] += hval
                m_new = jnp.maximum(s["blast"] + m, s["ms_loc"])
                c_new = jnp.exp(s["blast"] + m - m_new) * c_aug + jnp.exp(s["ms_loc"] - m_new) * s["kv"]
                state[i] = (c_new, m_new)
        return tuple(state)

    final = lax.fori_loop(0, nc // CHUNKS_PER_TRIP, body, tuple(init))
    for i, (hh, d) in enumerate(streams):
        c_aug, m = final[i]
        cf_ref[d, hh] = c_aug[:, :DV_A]
        nf_ref[d, hh] = c_aug[:, DV_A:DV_A + 1]
        mf_ref[d, hh] = m


def _mlstm_call(proj, gates, par, nb, seq, off, state):
    zero_init = state is None
    blk = lambda col: pl.BlockSpec((seq, LANES), col)
    in_specs = [blk(lambda b, p: (off + b, p)),
                blk(lambda b, p: (off + b, 2 + p)),
                blk(lambda b, p: (off + b, 4 + 2 * p)),
                blk(lambda b, p: (off + b, 5 + 2 * p)),
                blk(lambda b, p: (off + b, 0)),
                pl.BlockSpec((8, LANES), lambda b, p: (0, 0))]
    st_specs = [pl.BlockSpec((None, 2, 2, DQK_A, DV_A), lambda b, p: (b, 0, p, 0, 0)),
                pl.BlockSpec((None, 2, 2, DQK_A, 1), lambda b, p: (b, 0, p, 0, 0)),
                pl.BlockSpec((None, 2, 2, 1, 1), lambda b, p: (b, 0, p, 0, 0))]
    args = [proj, proj, proj, proj, gates, par]
    if not zero_init:
        in_specs += st_specs
        args += list(state)
    return pl.pallas_call(
        functools.partial(_mlstm_kernel, seq=seq, zero_init=zero_init),
        grid=(nb, H_A // 2),
        in_specs=in_specs,
        out_specs=[pl.BlockSpec((seq, 2 * DV_A), lambda b, p: (b, p))] + st_specs,
        out_shape=[jax.ShapeDtypeStruct((nb * seq, H_A * DV_A), f32),
                   jax.ShapeDtypeStruct((nb, 2, H_A, DQK_A, DV_A), f32),
                   jax.ShapeDtypeStruct((nb, 2, H_A, DQK_A, 1), f32),
                   jax.ShapeDtypeStruct((nb, 2, H_A, 1, 1), f32)],
        compiler_params=pltpu.CompilerParams(vmem_limit_bytes=VMEM_LIMIT),
        name="mlstm",
    )(*args)


def _short_conv_silu(x, w_ref, seq):
    row = lax.broadcasted_iota(jnp.int32, (seq, 1), 0)
    acc = x * w_ref[CONV_K // 2:CONV_K // 2 + 1, :]
    for tap in range(CONV_K):
        delta = tap - CONV_K // 2
        if delta == 0:
            continue
        shifted = pltpu.roll(x, shift=(-delta) % seq, axis=0)
        ok = (row + delta >= 0) & (row + delta < seq)
        acc = acc + jnp.where(ok, shifted, 0.0) * w_ref[tap:tap + 1, :]
    return _silu(acc)


def _l2_unit(y):
    return y * lax.rsqrt(jnp.sum(y * y, axis=-1, keepdims=True) + EPS)


def _split_bf16(x):
    hi = x.astype(bf16)
    return hi, (x - hi.astype(f32)).astype(bf16)


def _matmul_3pass(m, x):
    mh, ml = _split_bf16(m)
    xh, xl = _split_bf16(x)
    return jnp.dot(jnp.concatenate([mh, ml, mh], axis=1), jnp.concatenate([xh, xh, xl], axis=0),
                   preferred_element_type=f32)


def _unit_triangular_solves(ns, xs):
    levels = CHUNK.bit_length() - 1
    ns, xs = list(ns), list(xs)
    w = xs[0].shape[1]
    for lvl in range(levels):
        for i in range(len(ns)):
            if lvl < levels - 1:
                r = _matmul_3pass(ns[i], jnp.concatenate([xs[i], ns[i]], axis=1))
                xs[i] = xs[i] + r[:, :w]
                ns[i] = r[:, w:]
            else:
                xs[i] = xs[i] + _matmul_3pass(ns[i], xs[i])
    return xs


def _delta_kernel(*refs, seq, zero_init):
    if zero_init:
        (q0_ref, q1_ref, k0_ref, k1_ref, v0_ref, v1_ref, wq0_ref, wq1_ref, wk0_ref, wk1_ref, wv0_ref, wv1_ref,
         g_ref, par_ref, o_ref, sf_ref, q_scr, k_scr, v_scr) = refs
    else:
        (q0_ref, q1_ref, k0_ref, k1_ref, v0_ref, v1_ref, wq0_ref, wq1_ref, wk0_ref, wk1_ref, wv0_ref, wv1_ref,
         g_ref, par_ref, s0_ref, o_ref, sf_ref, q_scr, k_scr, v_scr) = refs
    nc = seq // CHUNK
    pair = pl.program_id(1)
    lane = lax.broadcasted_iota(jnp.int32, (1, LANES), 1)
    tril, triu, stril, striu = _chunk_masks()
    masks, smasks = (tril, triu), (stril, striu)
    masks_f = (tril.astype(f32), triu.astype(f32))
    sel = [_head_lane_select(LANE_A + d * H_B, pair) for d in range(2)]
    neg_a_row = -jnp.exp(par_ref[0:1, :])
    dt_row = par_ref[1:2, :]
    streams = [(hh, d) for hh in range(2) for d in range(2)]

    for hh, (q_ref, k_ref, v_ref, wq_ref, wk_ref, wv_ref) in enumerate(
            ((q0_ref, k0_ref, v0_ref, wq0_ref, wk0_ref, wv0_ref), (q1_ref, k1_ref, v1_ref, wq1_ref, wk1_ref, wv1_ref))):
        q_scr[hh] = _l2_unit(_short_conv_silu(q_ref[...], wq_ref, seq)) * DK_B ** -0.5
        k_scr[hh] = _l2_unit(_short_conv_silu(k_ref[...], wk_ref, seq))
        v_scr[hh] = _short_conv_silu(v_ref[...], wv_ref, seq)

    o_ref[...] = jnp.zeros_like(o_ref)
    init = tuple(jnp.zeros((DK_B, DV_B), f32) if zero_init else s0_ref[d, hh] for hh, d in streams)

    def body(trip, carry):
        subs = range(CHUNKS_PER_TRIP)
        shared = {}
        for u in subs:
            for d in range(2):
                r0 = _chunk_start(trip, u, d, nc)
                gates = g_ref[pl.ds(r0, CHUNK), :]
                beta_all = jax.nn.sigmoid(gates)
                glog = neg_a_row * _softplus(gates + dt_row)
                cum = jnp.dot(masks_f[d], glog, precision=HIGHEST, preferred_element_type=f32)
                rows = lax.dot_general(sel[d], cum, NT_DIMS, precision=HIGHEST, preferred_element_type=f32)
                shared[u, d] = (r0, beta_all, cum, rows)
        chains = [(u, hh, d) for u in subs for hh, d in streams]
        st = {ch: {} for ch in chains}
        for ch in chains:
            u, hh, d = ch
            r0, beta_all, cum, rows = shared[u, d]
            s = st[ch]
            head = 2 * pair + hh
            beta = _lane_col(beta_all, lane, LANE_BETA + d * H_B + head)
            gcol = _lane_col(cum, lane, LANE_A + d * H_B + head)
            glast = gcol[CHUNK - 1:CHUNK, :] if d == 0 else gcol[0:1, :]
            q = q_scr[hh, pl.ds(r0, CHUNK), :]
            k = k_scr[hh, pl.ds(r0, CHUNK), :]
            v = v_scr[hh, pl.ds(r0, CHUNK), :]
            k16 = k.astype(bf16)
            kbeta = k * beta
            eg = jnp.exp(gcol)
            decay = jnp.exp(jnp.where(masks[d], gcol - rows[hh:hh + 1, :], -jnp.inf))
            kk = lax.dot_general(kbeta.astype(bf16), k16, NT_DIMS, preferred_element_type=f32)
            qk = lax.dot_general(q.astype(bf16), k16, NT_DIMS, preferred_element_type=f32)
            s["n"] = -jnp.where(smasks[d], kk * decay, 0.0)
            s["x"] = jnp.concatenate([v * beta, kbeta * eg], axis=1)
            s["qk"] = jnp.where(masks[d], qk * decay, 0.0).astype(bf16)
            s["qg"] = (q * eg).astype(bf16)
            s["kd_t"] = (k * jnp.exp(glast - gcol)).T.astype(bf16)
            s["gl"] = jnp.exp(glast)
        solved = _unit_triangular_solves([st[ch]["n"] for ch in chains], [st[ch]["x"] for ch in chains])
        for ch, uw in zip(chains, solved):
            s = st[ch]
            s["u"] = uw[:, :DV_B]
            s["w_qg"] = jnp.concatenate([uw[:, DV_B:].astype(bf16), s["qg"]], axis=0)
        state = list(carry)
        for u in subs:
            ws = [jnp.dot(st[u, hh, d]["w_qg"], state[i].astype(bf16), preferred_element_type=f32)
                  for i, (hh, d) in enumerate(streams)]
            v_new = [(st[u, hh, d]["u"] - ws[i][:CHUNK]).astype(bf16) for i, (hh, d) in enumerate(streams)]
            for i, (hh, d) in enumerate(streams):
                s = st[u, hh, d]
                o = ws[i][CHUNK:] + jnp.dot(s["qk"], v_new[i], preferred_element_type=f32)
                r0 = shared[u, d][0]
                o_ref[pl.ds(r0, CHUNK), hh * DV_B:(hh + 1) * DV_B] += o
                state[i] = state[i] * s["gl"] + jnp.dot(s["kd_t"], v_new[i], preferred_element_type=f32)
        return tuple(state)

    final = lax.fori_loop(0, nc // CHUNKS_PER_TRIP, body, init)
    for i, (hh, d) in enumerate(streams):
        sf_ref[d, hh] = final[i]


def _delta_call(proj, conv_w8, gates, par, nb, seq, off, state):
    zero_init = state is None
    col0 = (2 * H_A * DQK_A + 2 * H_A * DV_A) // LANES
    blk = lambda col: pl.BlockSpec((seq, LANES), col)
    wblk = lambda col: pl.BlockSpec((8, LANES), col)
    in_specs = [blk(lambda b, p: (off + b, col0 + 2 * p)), blk(lambda b, p: (off + b, col0 + 2 * p + 1)),
                blk(lambda b, p: (off + b, col0 + H_B + 2 * p)), blk(lambda b, p: (off + b, col0 + H_B + 2 * p + 1)),
                blk(lambda b, p: (off + b, col0 + 2 * H_B + 2 * p)), blk(lambda b, p: (off + b, col0 + 2 * H_B + 2 * p + 1)),
                wblk(lambda b, p: (0, 2 * p)), wblk(lambda b, p: (0, 2 * p + 1)),
                wblk(lambda b, p: (0, H_B + 2 * p)), wblk(lambda b, p: (0, H_B + 2 * p + 1)),
                wblk(lambda b, p: (0, 2 * H_B + 2 * p)), wblk(lambda b, p: (0, 2 * H_B + 2 * p + 1)),
                blk(lambda b, p: (off + b, 0)),
                pl.BlockSpec((8, LANES), lambda b, p: (0, 0))]
    st_spec = pl.BlockSpec((None, 2, 2, DK_B, DV_B), lambda b, p: (b, 0, p, 0, 0))
    args = [proj] * 6 + [conv_w8] * 6 + [gates, par]
    if not zero_init:
        in_specs.append(st_spec)
        args.append(state)
    return pl.pallas_call(
        functools.partial(_delta_kernel, seq=seq, zero_init=zero_init),
        grid=(nb, H_B // 2),
        in_specs=in_specs,
        out_specs=[pl.BlockSpec((seq, 2 * DV_B), lambda b, p: (b, p)), st_spec],
        out_shape=[jax.ShapeDtypeStruct((nb * seq, H_B * DV_B), f32),
                   jax.ShapeDtypeStruct((nb, 2, H_B, DK_B, DV_B), f32)],
        scratch_shapes=[pltpu.VMEM((2, seq, LANES), f32)] * 3,
        compiler_params=pltpu.CompilerParams(vmem_limit_bytes=VMEM_LIMIT),
        name="delta",
    )(*args)


def _head_rms(x):
    return x * lax.rsqrt(jnp.mean(x * x, axis=-1, keepdims=True) + EPS)


def _prompt_or_sample(p_ref, s_ref, tm):
    return jnp.where(pl.program_id(0) < N_PROMPT // tm, p_ref[...], s_ref[...])


def _split_specs(tm, width):
    n_p = N_PROMPT // tm
    return [pl.BlockSpec((tm, width), lambda i: (jnp.minimum(i, n_p - 1), 0)),
            pl.BlockSpec((tm, width), lambda i: (jnp.maximum(i - n_p, 0), 0))]


def _ab_out_kernel(x_ref, hp_ref, hs_ref, op_ref, os_ref, om_ref, zd_ref, gm_ref, gd_ref, w_ref, mod_ref, o_ref,
                   *, gate_row, tm):
    hsum = _prompt_or_sample(hp_ref, hs_ref, tm)
    osum = _prompt_or_sample(op_ref, os_ref, tm)
    parts = []
    for h in range(H_A):
        sl = slice(h * DV_A, (h + 1) * DV_A)
        parts.append(_head_rms(hsum[:, sl]) * gm_ref[:, sl] * jax.nn.sigmoid(om_ref[:, sl]))
    for h in range(H_B):
        sl = slice(h * DV_B, (h + 1) * DV_B)
        parts.append(_head_rms(osum[:, sl]) * gd_ref[:, sl] * _silu(zd_ref[:, sl]))
    cat = jnp.concatenate(parts, axis=1).astype(bf16)
    y = jnp.dot(cat, w_ref[...].astype(bf16), preferred_element_type=f32)
    o_ref[...] = x_ref[...] + mod_ref[gate_row:gate_row + 1, :] * y


def _ab_out_call(x, hp, hs, op, os_, proj, gm_row, gd_row, w_out, mod_l, gate_row, tm=512):
    wide = H_A * DV_A
    om_blk = (2 * H_A * DQK_A) // wide + 1
    zd_blk = AB_MAIN // wide - 1
    return pl.pallas_call(
        functools.partial(_ab_out_kernel, gate_row=gate_row, tm=tm),
        grid=(N_TOK // tm,),
        in_specs=[pl.BlockSpec((tm, D_MODEL), lambda i: (i, 0))] + _split_specs(tm, wide) + _split_specs(tm, wide) + [
            pl.BlockSpec((tm, wide), lambda i: (i, om_blk)),
            pl.BlockSpec((tm, wide), lambda i: (i, zd_blk)),
            pl.BlockSpec((1, wide), lambda i: (0, 0)),
            pl.BlockSpec((1, wide), lambda i: (0, 0)),
            pl.BlockSpec((2 * wide, D_MODEL), lambda i: (0, 0)),
            pl.BlockSpec((None, N_ADA, D_MODEL), lambda i: (_mod_row_index(i, tm), 0, 0))],
        out_specs=pl.BlockSpec((tm, D_MODEL), lambda i: (i, 0)),
        out_shape=jax.ShapeDtypeStruct((N_TOK, D_MODEL), f32),
        compiler_params=pltpu.CompilerParams(vmem_limit_bytes=VMEM_LIMIT),
        name="ab_out",
    )(x, hp, hs, op, os_, proj, proj, gm_row, gd_row, w_out, mod_l)


def _na_out_kernel(x_ref, ap_ref, as_ref, w_ref, mod_ref, o_ref, *, gate_row, tm):
    a = _prompt_or_sample(ap_ref, as_ref, tm).astype(bf16)
    y = jnp.dot(a, w_ref[...].astype(bf16), preferred_element_type=f32)
    o_ref[...] = x_ref[...] + mod_ref[gate_row:gate_row + 1, :] * y


def _na_out_call(x, attn_p, attn_s, w_out, mod_l, gate_row, tm=512):
    return pl.pallas_call(
        functools.partial(_na_out_kernel, gate_row=gate_row, tm=tm),
        grid=(N_TOK // tm,),
        in_specs=[pl.BlockSpec((tm, D_MODEL), lambda i: (i, 0))] + _split_specs(tm, NA_W) + [
            pl.BlockSpec((NA_W, D_MODEL), lambda i: (0, 0)),
            pl.BlockSpec((None, N_ADA, D_MODEL), lambda i: (_mod_row_index(i, tm), 0, 0))],
        out_specs=pl.BlockSpec((tm, D_MODEL), lambda i: (i, 0)),
        out_shape=jax.ShapeDtypeStruct((N_TOK, D_MODEL), f32),
        compiler_params=pltpu.CompilerParams(vmem_limit_bytes=VMEM_LIMIT),
        name="na_out",
    )(x, attn_p, attn_s, w_out, mod_l)


CTX_HEADS = 8


def _ctx_attn_kernel(q_ref, k_ref, v_ref, o_ref):
    outs = []
    for hh in range(CTX_HEADS):
        sl = slice(hh * DH_C, (hh + 1) * DH_C)
        q = q_ref[:, sl].astype(bf16)
        k = k_ref[:, sl].astype(bf16)
        v = v_ref[:, sl].astype(bf16)
        s = lax.dot_general(q, k, NT_DIMS, preferred_element_type=f32) * DH_C ** -0.5
        p = jnp.exp(s - jnp.max(s, axis=1, keepdims=True))
        o = jnp.dot(p.astype(bf16), v, preferred_element_type=f32)
        outs.append(o / jnp.sum(p, axis=1, keepdims=True))
    o_ref[...] = jnp.concatenate(outs, axis=1)


def _ctx_attn_call(proj):
    w = CTX_HEADS * DH_C
    nblk = NA_W // w
    return pl.pallas_call(
        _ctx_attn_kernel,
        grid=(BATCH, nblk),
        in_specs=[pl.BlockSpec((SEQ, w), lambda b, j: (b, j)),
                  pl.BlockSpec((SEQ, w), lambda b, j: (b, nblk + j)),
                  pl.BlockSpec((SEQ, w), lambda b, j: (b, 2 * nblk + j))],
        out_specs=pl.BlockSpec((SEQ, w), lambda b, j: (b, j)),
        out_shape=jax.ShapeDtypeStruct((N_PROMPT, NA_W), f32),
        compiler_params=pltpu.CompilerParams(vmem_limit_bytes=VMEM_LIMIT),
        name="ctx_attn",
    )(proj, proj, proj)


QROWS = 4
QBLK = QROWS * GRID_W
KROWS_MID = QROWS + WIN_R - 1
N_RIDX = 2 * WIN_R - 1
N_CIDX = 2 * WIN_C - 1
N_QBLK = GRID_ROWS // QROWS


def _nattn_kernel(rb_ref, q_ref, k_ref, v_ref, ck_ref, cv_ref, o_ref, tb_scr, bmid_scr, btop_scr, bbot_scr):
    pair = pl.program_id(1)
    qc = lax.broadcasted_iota(jnp.int32, (GRID_W, GRID_W), 0)
    kc = lax.broadcasted_iota(jnp.int32, (GRID_W, GRID_W), 1)
    cs = jnp.clip(qc - WIN_C // 2, 0, GRID_W - WIN_C)
    valid = (kc >= cs) & (kc < cs + WIN_C)
    cidx = jnp.clip(kc - qc, -(WIN_C - 1), WIN_C - 1) + WIN_C - 1
    neg = jnp.full((GRID_W, GRID_W), -jnp.inf, f32)
    scale = DH_C ** -0.5

    for hh in range(2):
        head = 2 * pair + hh
        sl = slice(hh * DH_C, (hh + 1) * DH_C)

        def ridx_body(ri, _):
            base = (head * N_RIDX + ri) * N_CIDX

            def cidx_body(j, acc):
                return jnp.where(cidx == j, rb_ref[base + j], acc)

            tile = lax.fori_loop(0, N_CIDX, cidx_body, jnp.zeros((GRID_W, GRID_W), f32))
            tb_scr[ri] = jnp.where(valid, tile, -jnp.inf)
            return 0

        lax.fori_loop(0, N_RIDX, ridx_body, 0)

        for i in range(QROWS):
            rs = slice(i * GRID_W, (i + 1) * GRID_W)
            for jj in range(KROWS_MID):
                inside = 0 <= jj - i < WIN_R
                bmid_scr[rs, jj * GRID_W:(jj + 1) * GRID_W] = tb_scr[jj - i + WIN_R // 2 - 1] if inside else neg
            for jj in range(WIN_R):
                btop_scr[rs, jj * GRID_W:(jj + 1) * GRID_W] = tb_scr[jj - i + WIN_R - 1]
                bbot_scr[rs, jj * GRID_W:(jj + 1) * GRID_W] = tb_scr[jj - i + WIN_R // 2 - 1]

        kctx = ck_ref[hh].astype(bf16)
        vctx = cv_ref[hh].astype(bf16)

        def block(q_start, k_start, k_rows, bias_scr):
            nk = k_rows * GRID_W
            q = q_ref[pl.ds(q_start, QBLK), sl].astype(bf16)
            ku = k_ref[pl.ds(k_start, nk), sl].astype(bf16)
            vu = v_ref[pl.ds(k_start, nk), sl].astype(bf16)
            s_loc = lax.dot_general(q, ku, NT_DIMS, preferred_element_type=f32) * scale + bias_scr[...]
            s_ctx = lax.dot_general(q, kctx, NT_DIMS, preferred_element_type=f32) * scale
            m = jnp.maximum(jnp.max(s_loc, axis=1, keepdims=True), jnp.max(s_ctx, axis=1, keepdims=True))
            p_loc = jnp.exp(s_loc - m)
            p_ctx = jnp.exp(s_ctx - m)
            denom = jnp.sum(p_loc, axis=1, keepdims=True) + jnp.sum(p_ctx, axis=1, keepdims=True)
            o = (jnp.dot(p_loc.astype(bf16), vu, preferred_element_type=f32)
                 + jnp.dot(p_ctx.astype(bf16), vctx, preferred_element_type=f32))
            o_ref[pl.ds(q_start, QBLK), sl] = o / denom

        block(0, 0, WIN_R, btop_scr)

        def mid_body(blk, _):
            q_start = pl.multiple_of(blk * QBLK, QBLK)
            k_start = pl.multiple_of(blk * QBLK - (WIN_R // 2) * GRID_W, QBLK)
            block(q_start, k_start, KROWS_MID, bmid_scr)
            return 0

        lax.fori_loop(1, N_QBLK - 1, mid_body, 0)
        block((N_QBLK - 1) * QBLK, (GRID_ROWS - WIN_R) * GRID_W, WIN_R, bbot_scr)


def _nattn_call(proj, cache_k, cache_v, rel_bias, off):
    npair = H_C // 2
    return pl.pallas_call(
        _nattn_kernel,
        grid=(DEC_BATCH, npair),
        in_specs=[pl.BlockSpec(memory_space=pltpu.SMEM),
                  pl.BlockSpec((DEC_SEQ, LANES), lambda b, p: (off + b, p)),
                  pl.BlockSpec((DEC_SEQ, LANES), lambda b, p: (off + b, npair + p)),
                  pl.BlockSpec((DEC_SEQ, LANES), lambda b, p: (off + b, 2 * npair + p)),
                  pl.BlockSpec((None, 2, PAST_LEN, DH_C), lambda b, p: (b, p, 0, 0)),
                  pl.BlockSpec((None, 2, PAST_LEN, DH_C), lambda b, p: (b, p, 0, 0))],
        out_specs=pl.BlockSpec((DEC_SEQ, LANES), lambda b, p: (b, p)),
        out_shape=jax.ShapeDtypeStruct((N_SAMPLE, NA_W), f32),
        scratch_shapes=[pltpu.VMEM((N_RIDX, GRID_W, GRID_W), f32),
                        pltpu.VMEM((QBLK, KROWS_MID * GRID_W), f32),
                        pltpu.VMEM((QBLK, WIN_R * GRID_W), f32),
                        pltpu.VMEM((QBLK, WIN_R * GRID_W), f32)],
        compiler_params=pltpu.CompilerParams(vmem_limit_bytes=VMEM_LIMIT),
        name="nattn",
    )(rel_bias.reshape(-1), proj, proj, proj, cache_k, cache_v)


def _lane_row(pieces):
    row = jnp.zeros((LANES,), f32)
    for off, vals in pieces:
        row = row.at[off:off + vals.shape[0]].set(vals.astype(f32))
    return row


def _param_rows(rows):
    out = jnp.zeros((8, LANES), f32)
    for r, row in enumerate(rows):
        out = out.at[r].set(row)
    return out


def kernel(x_prompt, x_sample, c, state_mlstm_C, state_mlstm_n, state_mlstm_m, state_delta_S, cache_na_k, cache_na_v, c_ctx, ada_w, ada_b, norm_g, ffn_wg, ffn_wu, ffn_wd, ab_w_in, ab_w_out, mlstm_b_i, mlstm_b_f, mlstm_norm_g, delta_conv_w, delta_a_log, delta_dt_bias, delta_norm_g, na_w_in, na_w_out, na_rel_bias, final_norm_g):
    x = jnp.concatenate([x_prompt.reshape(N_PROMPT, D_MODEL), x_sample.reshape(N_SAMPLE, D_MODEL)], axis=0)
    mods = _ada_call(jnp.concatenate([c_ctx[None, :], c], axis=0), ada_w, ada_b)
    mods = mods.reshape(DEPTH, 3, N_ADA, D_MODEL)
    gf_row = final_norm_g.reshape(1, D_MODEL)
    s_off = N_PROMPT // DEC_SEQ
    new_c, new_n, new_m, new_s, new_k, new_v = [], [], [], [], [], []

    for l in range(DEPTH):
        mod_l = mods[l]
        a = l // 2
        x = _ffn_call(x, mod_l, norm_g[l, 0].reshape(1, D_MODEL), ffn_wg[l, 0], ffn_wu[l, 0], ffn_wd[l, 0],
                      gf_row, rows=(0, 1, 2), final=False)
        g_mix = norm_g[l, 1].reshape(1, D_MODEL)
        if l % 2 == 0:
            w_in = ab_w_in[a]
            i0 = 2 * H_A * DQK_A + 2 * H_A * DV_A
            d0 = i0 + 4 * H_A
            b0 = d0 + 2 * H_B * DK_B + 2 * H_B * DV_B
            w_main = jnp.concatenate([w_in[:, :i0], w_in[:, d0:b0]], axis=1)
            w_gate = jnp.concatenate([w_in[:, i0:d0], w_in[:, b0:],
                                      jnp.zeros((D_MODEL, LANES - 4 * H_A - 4 * H_B), f32)], axis=1)
            proj, gates = _proj_call(x, mod_l, g_mix, w_main, w_gate, rows=(3, 4))

            par_m = _param_rows([_lane_row([(LANE_I, mlstm_b_i[a].reshape(-1))]),
                                 _lane_row([(LANE_F, mlstm_b_f[a].reshape(-1))])])
            par_d = _param_rows([_lane_row([(LANE_A, delta_a_log[a].reshape(-1))]),
                                 _lane_row([(LANE_A, delta_dt_bias[a].reshape(-1))])])
            conv_w8 = jnp.concatenate([delta_conv_w[a], jnp.zeros((8 - CONV_K, 3 * H_B * DK_B), f32)], axis=0)

            hp, cn, nn_, mn = _mlstm_call(proj, gates, par_m, BATCH, SEQ, 0, None)
            st = (state_mlstm_C[:, a], state_mlstm_n[:, a][..., None], state_mlstm_m[:, a][..., None, None])
            hs, _, _, _ = _mlstm_call(proj, gates, par_m, DEC_BATCH, DEC_SEQ, s_off, st)
            op, sn = _delta_call(proj, conv_w8, gates, par_d, BATCH, SEQ, 0, None)
            os_, _ = _delta_call(proj, conv_w8, gates, par_d, DEC_BATCH, DEC_SEQ, s_off, state_delta_S[:, a])
            new_c.append(cn)
            new_n.append(nn_[..., 0])
            new_m.append(mn[..., 0, 0])
            new_s.append(sn)
            x = _ab_out_call(x, hp, hs, op, os_, proj, mlstm_norm_g[a].reshape(1, -1),
                             jnp.tile(delta_norm_g[a], H_B).reshape(1, -1), ab_w_out[a], mod_l, gate_row=5)
        else:
            proj = _proj_call(x, mod_l, g_mix, na_w_in[a], None, rows=(3, 4))[0]
            attn_p = _ctx_attn_call(proj)
            attn_s = _nattn_call(proj, cache_na_k[:, a], cache_na_v[:, a], na_rel_bias[a], s_off)
            kp = proj[:N_PROMPT, NA_W:2 * NA_W].reshape(BATCH, SEQ, H_C, DH_C).transpose(0, 2, 1, 3)
            vp = proj[:N_PROMPT, 2 * NA_W:].reshape(BATCH, SEQ, H_C, DH_C).transpose(0, 2, 1, 3)
            new_k.append(kp)
            new_v.append(vp)
            x = _na_out_call(x, attn_p, attn_s, na_w_out[a], mod_l, gate_row=5)
        x = _ffn_call(x, mod_l, norm_g[l, 2].reshape(1, D_MODEL), ffn_wg[l, 1], ffn_wu[l, 1], ffn_wd[l, 1],
                      gf_row, rows=(6, 7, 8), final=(l == DEPTH - 1))

    y_prompt = x[:N_PROMPT].reshape(BATCH, SEQ, D_MODEL)
    y_sample = x[N_PROMPT:].reshape(DEC_BATCH, DEC_SEQ, D_MODEL)
    return (y_prompt, y_sample, jnp.stack(new_c, axis=1), jnp.stack(new_n, axis=1), jnp.stack(new_m, axis=1),
            jnp.stack(new_s, axis=1), jnp.stack(new_k, axis=1), jnp.stack(new_v, axis=1))
```

```python
import functools

import jax
import jax.numpy as jnp
from jax import lax
from jax.experimental import pallas as pl
from jax.experimental.pallas import tpu as pltpu

f32 = jnp.float32
bf16 = jnp.bfloat16

D_MODEL = 1024
BATCH = 16
SEQ = 256
DEPTH = 2
DEC_BATCH = 2
DEC_SEQ = 2048
PAST_LEN = 256
GRID_W = 64
GRID_ROWS = DEC_SEQ // GRID_W
D_FF = 2816
N_ADA = 9
EPS = 1e-6
CHUNK = 64
H_A, DQK_A, DV_A = 4, 64, 128
H_B, DK_B, DV_B = 4, 128, 128
CONV_K = 5
H_C, DH_C = 16, 64
WIN_R, WIN_C = 8, 16
NA_W = H_C * DH_C

N_PROMPT = BATCH * SEQ
N_SAMPLE = DEC_BATCH * DEC_SEQ
N_TOK = N_PROMPT + N_SAMPLE
AB_MAIN = 2 * H_A * DQK_A + 2 * H_A * DV_A + 2 * H_B * DK_B + 2 * H_B * DV_B
LANES = 128
VMEM_LIMIT = 56 * 1024 * 1024

LANE_I, LANE_F, LANE_BETA, LANE_A = 0, 8, 16, 24

NT_DIMS = (((1,), (1,)), ((), ()))
TN_DIMS = (((0,), (0,)), ((), ()))

CHUNKS_PER_TRIP = 4
FFN_TM = 2048
FFN_TM_SPLIT = 1024


def _softplus(x):
    return jnp.maximum(x, 0.0) + jnp.log1p(jnp.exp(-jnp.abs(x)))


def _log_sigmoid(x):
    return -_softplus(-x)


def _silu(x):
    return x * jax.nn.sigmoid(x)


def _mod_row_index(i, tm):
    n_p = N_PROMPT // tm
    per_b = DEC_SEQ // tm
    return jnp.where(i < n_p, 0, 1 + (i - n_p) // per_b)


def _modulated(x, g_row, shift_row, scale_row):
    y = x * lax.rsqrt(jnp.mean(x * x, axis=-1, keepdims=True) + EPS) * g_row
    return y * (1.0 + scale_row) + shift_row


def _ada_kernel(cb_ref, w_ref, b_ref, o_ref):
    tn = w_ref.shape[-1]
    for r in range(3):
        cb = cb_ref[r]
        s = cb * jax.nn.sigmoid(cb)
        parts = [jnp.sum(w_ref[:, j * LANES:(j + 1) * LANES] * s, axis=0, keepdims=True)
                 for j in range(tn // LANES)]
        o_ref[r:r + 1, :] = jnp.concatenate(parts, axis=1) + b_ref[...]


def _ada_call(cond3, ada_w, ada_b):
    tn = D_MODEL
    cb = jnp.broadcast_to(cond3[:, :, None], (3, D_MODEL, LANES))
    return pl.pallas_call(
        _ada_kernel,
        grid=(DEPTH, N_ADA * D_MODEL // tn),
        in_specs=[pl.BlockSpec((3, D_MODEL, LANES), lambda l, j: (0, 0, 0)),
                  pl.BlockSpec((None, D_MODEL, tn), lambda l, j: (l, 0, j)),
                  pl.BlockSpec((None, 1, tn), lambda l, j: (l, 0, j))],
        out_specs=pl.BlockSpec((None, 3, tn), lambda l, j: (l, 0, j)),
        out_shape=jax.ShapeDtypeStruct((DEPTH, 3, N_ADA * D_MODEL), f32),
        compiler_params=pltpu.CompilerParams(vmem_limit_bytes=VMEM_LIMIT),
        name="ada_mod",
    )(cb, ada_w, ada_b.reshape(DEPTH, 1, N_ADA * D_MODEL))


def _prompt_or_sample(p_ref, s_ref, tm):
    return jnp.where(pl.program_id(0) < N_PROMPT // tm, p_ref[...], s_ref[...])


def _split_specs(tm, width, n_grid_axes=1):
    n_p = N_PROMPT // tm
    if n_grid_axes == 1:
        return [pl.BlockSpec((tm, width), lambda i: (jnp.minimum(i, n_p - 1), 0)),
                pl.BlockSpec((tm, width), lambda i: (jnp.maximum(i - n_p, 0), 0))]
    return [pl.BlockSpec((tm, width), lambda i, j: (jnp.minimum(i, n_p - 1), 0)),
            pl.BlockSpec((tm, width), lambda i, j: (jnp.maximum(i - n_p, 0), 0))]


def _ffn_kernel(*refs, rows, final, split_in, split_out, tm):
    it = iter(refs)
    x_refs = [next(it) for _ in range(2 if split_in else 1)]
    mod_ref, g_ref, wg_ref, wu_ref, wd_ref, gf_ref = [next(it) for _ in range(6)]
    o_refs = [next(it) for _ in range(2 if split_out else 1)]
    h_scr = next(it)
    acc_ref = next(it) if split_out else o_refs[0]
    j = pl.program_id(1)

    def load_x():
        return _prompt_or_sample(x_refs[0], x_refs[1], tm) if split_in else x_refs[0][...]

    @pl.when(j == 0)
    def _():
        h = _modulated(load_x(), g_ref[...], mod_ref[rows[0]:rows[0] + 1, :], mod_ref[rows[1]:rows[1] + 1, :])
        h_scr[...] = h.astype(bf16)
        acc_ref[...] = jnp.zeros_like(acc_ref)

    h = h_scr[...]
    g = jnp.dot(h, wg_ref[...].astype(bf16), preferred_element_type=f32)
    u = jnp.dot(h, wu_ref[...].astype(bf16), preferred_element_type=f32)
    a = (_silu(g) * u).astype(bf16)
    acc_ref[...] += jnp.dot(a, wd_ref[...].astype(bf16), preferred_element_type=f32)

    @pl.when(j == pl.num_programs(1) - 1)
    def _():
        xn = load_x() + (0.5 * mod_ref[rows[2]:rows[2] + 1, :]) * acc_ref[...]
        if final:
            xn = xn * lax.rsqrt(jnp.mean(xn * xn, axis=-1, keepdims=True) + EPS) * gf_ref[...]
        if split_out:
            is_prompt = pl.program_id(0) < N_PROMPT // tm

            @pl.when(is_prompt)
            def _():
                o_refs[0][...] = xn

            @pl.when(jnp.logical_not(is_prompt))
            def _():
                o_refs[1][...] = xn
        else:
            o_refs[0][...] = xn


def _ffn_call(xs, mod_l, g_row, ffn_wg, ffn_wu, ffn_wd, layer, half, gf_row, rows, final, split_out, tm, tf=256):
    split_in = len(xs) == 2
    x_specs = (_split_specs(tm, D_MODEL, 2) if split_in else [pl.BlockSpec((tm, D_MODEL), lambda i, j: (i, 0))])
    if split_out:
        out_specs = _split_specs(tm, D_MODEL, 2)
        out_shape = [jax.ShapeDtypeStruct((N_PROMPT, D_MODEL), f32), jax.ShapeDtypeStruct((N_SAMPLE, D_MODEL), f32)]
        scratch = [pltpu.VMEM((tm, D_MODEL), bf16), pltpu.VMEM((tm, D_MODEL), f32)]
    else:
        out_specs = [pl.BlockSpec((tm, D_MODEL), lambda i, j: (i, 0))]
        out_shape = [jax.ShapeDtypeStruct((N_TOK, D_MODEL), f32)]
        scratch = [pltpu.VMEM((tm, D_MODEL), bf16)]
    return pl.pallas_call(
        functools.partial(_ffn_kernel, rows=rows, final=final, split_in=split_in, split_out=split_out, tm=tm),
        grid=(N_TOK // tm, D_FF // tf),
        in_specs=x_specs + [
            pl.BlockSpec((None, N_ADA, D_MODEL), lambda i, j: (_mod_row_index(i, tm), 0, 0)),
            pl.BlockSpec((1, D_MODEL), lambda i, j: (0, 0)),
            pl.BlockSpec((None, None, D_MODEL, tf), lambda i, j: (layer, half, 0, j)),
            pl.BlockSpec((None, None, D_MODEL, tf), lambda i, j: (layer, half, 0, j)),
            pl.BlockSpec((None, None, tf, D_MODEL), lambda i, j: (layer, half, j, 0)),
            pl.BlockSpec((1, D_MODEL), lambda i, j: (0, 0))],
        out_specs=out_specs, out_shape=out_shape, scratch_shapes=scratch,
        compiler_params=pltpu.CompilerParams(dimension_semantics=("parallel", "arbitrary"),
                                             vmem_limit_bytes=VMEM_LIMIT),
        name="ffn",
    )(*xs, mod_l, g_row, ffn_wg, ffn_wu, ffn_wd, gf_row)


def _proj_kernel(*refs, rows, with_gates):
    if with_gates:
        x_ref, mod_ref, g_ref, w_ref, wgate_ref, o_ref, og_ref, h_scr = refs
    else:
        x_ref, mod_ref, g_ref, w_ref, o_ref, h_scr = refs
    j = pl.program_id(1)

    @pl.when(j == 0)
    def _():
        h = _modulated(x_ref[...], g_ref[...], mod_ref[rows[0]:rows[0] + 1, :], mod_ref[rows[1]:rows[1] + 1, :])
        hb = h.astype(bf16)
        h_scr[...] = hb
        if with_gates:
            og_ref[...] = jnp.dot(hb, wgate_ref[...].astype(bf16), preferred_element_type=f32)

    o_ref[...] = jnp.dot(h_scr[...], w_ref[...].astype(bf16), preferred_element_type=f32)


def _proj_call(x, mod_l, g_row, w, w_gate, rows, tm=1024):
    n = w.shape[1]
    tn = n // 2
    with_gates = w_gate is not None
    in_specs = [pl.BlockSpec((tm, D_MODEL), lambda i, j: (i, 0)),
                pl.BlockSpec((None, N_ADA, D_MODEL), lambda i, j: (_mod_row_index(i, tm), 0, 0)),
                pl.BlockSpec((1, D_MODEL), lambda i, j: (0, 0)),
                pl.BlockSpec((D_MODEL, tn), lambda i, j: (0, j))]
    out_specs = [pl.BlockSpec((tm, tn), lambda i, j: (i, j))]
    out_shape = [jax.ShapeDtypeStruct((N_TOK, n), f32)]
    args = [x, mod_l, g_row, w]
    if with_gates:
        in_specs.append(pl.BlockSpec((D_MODEL, LANES), lambda i, j: (0, 0)))
        out_specs.append(pl.BlockSpec((tm, LANES), lambda i, j: (i, 0)))
        out_shape.append(jax.ShapeDtypeStruct((N_TOK, LANES), f32))
        args.append(w_gate)
    return pl.pallas_call(
        functools.partial(_proj_kernel, rows=rows, with_gates=with_gates),
        grid=(N_TOK // tm, n // tn),
        in_specs=in_specs, out_specs=out_specs, out_shape=out_shape,
        scratch_shapes=[pltpu.VMEM((tm, D_MODEL), bf16)],
        compiler_params=pltpu.CompilerParams(dimension_semantics=("parallel", "arbitrary"),
                                             vmem_limit_bytes=VMEM_LIMIT),
        name="in_proj",
    )(*args)


def _chunk_masks():
    r = lax.broadcasted_iota(jnp.int32, (CHUNK, CHUNK), 0)
    c = lax.broadcasted_iota(jnp.int32, (CHUNK, CHUNK), 1)
    return r >= c, r <= c, r > c, r < c


def _lane_col(x, lane, j):
    return jnp.sum(jnp.where(lane == j, x, 0.0), axis=1, keepdims=True)


def _head_lane_select(base, pair):
    r = lax.broadcasted_iota(jnp.int32, (8, LANES), 0)
    ln = lax.broadcasted_iota(jnp.int32, (8, LANES), 1)
    return jnp.where((ln == base + 2 * pair + r) & (r < 2), 1.0, 0.0)


def _split3_bf16(x):
    hi = x.astype(bf16)
    r = x - hi.astype(f32)
    mid = r.astype(bf16)
    return hi, mid, (r - mid.astype(f32)).astype(bf16)


def _mask_matmul_f32(mask16, x):
    return jnp.dot(jnp.concatenate([mask16] * 3, axis=1), jnp.concatenate(_split3_bf16(x), axis=0),
                   preferred_element_type=f32)


def _select_rows_f32(sel16, x):
    return lax.dot_general(jnp.concatenate([sel16] * 3, axis=1), jnp.concatenate(_split3_bf16(x), axis=1),
                           NT_DIMS, preferred_element_type=f32)


def _chunk_start(trip, u, d, nc):
    step = trip * CHUNKS_PER_TRIP + u
    c = step if d == 0 else nc - 1 - step
    return pl.multiple_of(c * CHUNK, CHUNK)


def _mlstm_kernel(*refs, seq, zero_init):
    if zero_init:
        qp_ref, kp_ref, v0_ref, v1_ref, g_ref, par_ref, h_ref, cf_ref, nf_ref, mf_ref = refs
    else:
        (qp_ref, kp_ref, v0_ref, v1_ref, g_ref, par_ref, c0_ref, n0_ref, m0_ref,
         h_ref, cf_ref, nf_ref, mf_ref) = refs
    nc = seq // CHUNK
    pair = pl.program_id(1)
    lane = lax.broadcasted_iota(jnp.int32, (1, LANES), 1)
    tril, triu, _, _ = _chunk_masks()
    masks = (tril, triu)
    masks16 = (tril.astype(bf16), triu.astype(bf16))
    sel = [_head_lane_select(LANE_F + d * H_A, pair).astype(bf16) for d in range(2)]
    bi_row = par_ref[0:1, :]
    bf_row = par_ref[1:2, :]
    ones_col = jnp.where(lane == 0, 1.0, 0.0) + jnp.zeros((CHUNK, LANES), f32)
    v_refs = (v0_ref, v1_ref)
    streams = [(hh, d) for hh in range(2) for d in range(2)]

    h_ref[...] = jnp.zeros_like(h_ref)

    init = []
    for hh, d in streams:
        if zero_init:
            init.append((jnp.zeros((DQK_A, 2 * LANES), f32), jnp.zeros((1, 1), f32)))
        else:
            n_aug = jnp.where(lane == 0, n0_ref[d, hh], 0.0)
            init.append((jnp.concatenate([c0_ref[d, hh], n_aug], axis=1), m0_ref[d, hh]))

    def body(trip, carry):
        subs = range(CHUNKS_PER_TRIP)
        shared = {}
        for u in subs:
            for d in range(2):
                r0 = _chunk_start(trip, u, d, nc)
                gates = g_ref[pl.ds(r0, CHUNK), :]
                gi = gates + bi_row
                gf = _log_sigmoid(gates + bf_row)
                cum = _mask_matmul_f32(masks16[d], gf)
                stack = jnp.concatenate([cum, pltpu.roll(gi, LANE_F - LANE_I, axis=1)], axis=0)
                rows = _select_rows_f32(sel[d], stack)
                total = cum[CHUNK - 1:CHUNK, :] if d == 0 else cum[0:1, :]
                shared[u, d] = (r0, gi, cum, rows, total)
        chains = [(u, hh, d) for u in subs for hh, d in streams]
        st = {ch: {} for ch in chains}
        for ch in chains:
            u, hh, d = ch
            r0 = shared[u, d][0]
            s = st[ch]
            s["q"] = (qp_ref[pl.ds(r0, CHUNK), hh * DQK_A:(hh + 1) * DQK_A] * DQK_A ** -0.5).astype(bf16)
            s["k"] = kp_ref[pl.ds(r0, CHUNK), hh * DQK_A:(hh + 1) * DQK_A].astype(bf16)
            s["v_aug"] = jnp.concatenate([v_refs[hh][pl.ds(r0, CHUNK), :], ones_col], axis=1)
            s["qk"] = lax.dot_general(s["q"], s["k"], NT_DIMS, preferred_element_type=f32)
        for ch in chains:
            u, hh, d = ch
            _, gi, cum, rows, total = shared[u, d]
            s = st[ch]
            head = 2 * pair + hh
            jf = LANE_F + d * H_A + head
            ji = LANE_I + d * H_A + head
            bcol = _lane_col(cum, lane, jf)
            icol = _lane_col(gi, lane, ji)
            dmat = jnp.where(masks[d], bcol - rows[hh:hh + 1, :CHUNK] + rows[hh:hh + 1, CHUNK:], -jnp.inf)
            m_loc = jnp.max(dmat, axis=1, keepdims=True)
            s["sw"] = (s["qk"] * jnp.exp(dmat - m_loc)).astype(bf16)
            blast = _lane_col(total, lane, jf)
            gs = blast - bcol + icol
            ms_loc = jnp.max(gs, axis=0, keepdims=True)
            s["wv"] = (jnp.exp(gs - ms_loc) * s["v_aug"]).astype(bf16)
            s.update(bcol=bcol, m_loc=m_loc, blast=blast, ms_loc=ms_loc)
        for ch in chains:
            s = st[ch]
            s["num"] = jnp.dot(s["sw"], s["v_aug"].astype(bf16), preferred_element_type=f32)
            s["kv"] = lax.dot_general(s["k"], s["wv"], TN_DIMS, preferred_element_type=f32)
        state = list(carry)
        for u in subs:
            qc = [jnp.dot(st[u, hh, d]["q"], state[i][0].astype(bf16), preferred_element_type=f32)
                  for i, (hh, d) in enumerate(streams)]
            for i, (hh, d) in enumerate(streams):
                s = st[u, hh, d]
                c_aug, m = state[i]
                m_inter = s["bcol"] + m
                m_t = jnp.maximum(m_inter, s["m_loc"])
                nd = jnp.exp(s["m_loc"] - m_t) * s["num"] + jnp.exp(m_inter - m_t) * qc[i]
                den = nd[:, DV_A:DV_A + 1]
                hval = nd[:, :DV_A] / jnp.maximum(jnp.abs(den), jnp.exp(-m_t))
                r0 = shared[u, d][0]
                h_ref[pl.ds(r0, CHUNK), hh * DV_A:(hh + 1) * DV_A] += hval
                m_new = jnp.maximum(s["blast"] + m, s["ms_loc"])
                c_new = jnp.exp(s["blast"] + m - m_new) * c_aug + jnp.exp(s["ms_loc"] - m_new) * s["kv"]
                state[i] = (c_new, m_new)
        return tuple(state)

    final = lax.fori_loop(0, nc // CHUNKS_PER_TRIP, body, tuple(init))
    for i, (hh, d) in enumerate(streams):
        c_aug, m = final[i]
        cf_ref[d, hh] = c_aug[:, :DV_A]
        nf_ref[d, hh] = c_aug[:, DV_A:DV_A + 1]
        mf_ref[d, hh] = m


def _mlstm_call(proj, gates, par, nb, seq, off, state):
    zero_init = state is None
    blk = lambda col: pl.BlockSpec((seq, LANES), col)
    in_specs = [blk(lambda b, p: (off + b, p)),
                blk(lambda b, p: (off + b, 2 + p)),
                blk(lambda b, p: (off + b, 4 + 2 * p)),
                blk(lambda b, p: (off + b, 5 + 2 * p)),
                blk(lambda b, p: (off + b, 0)),
                pl.BlockSpec((8, LANES), lambda b, p: (0, 0))]
    st_specs = [pl.BlockSpec((None, 2, 2, DQK_A, DV_A), lambda b, p: (b, 0, p, 0, 0)),
                pl.BlockSpec((None, 2, 2, DQK_A, 1), lambda b, p: (b, 0, p, 0, 0)),
                pl.BlockSpec((None, 2, 2, 1, 1), lambda b, p: (b, 0, p, 0, 0))]
    args = [proj, proj, proj, proj, gates, par]
    if not zero_init:
        in_specs += st_specs
        args += list(state)
    return pl.pallas_call(
        functools.partial(_mlstm_kernel, seq=seq, zero_init=zero_init),
        grid=(nb, H_A // 2),
        in_specs=in_specs,
        out_specs=[pl.BlockSpec((seq, 2 * DV_A), lambda b, p: (b, p))] + st_specs,
        out_shape=[jax.ShapeDtypeStruct((nb * seq, H_A * DV_A), f32),
                   jax.ShapeDtypeStruct((nb, 2, H_A, DQK_A, DV_A), f32),
                   jax.ShapeDtypeStruct((nb, 2, H_A, DQK_A, 1), f32),
                   jax.ShapeDtypeStruct((nb, 2, H_A, 1, 1), f32)],
        compiler_params=pltpu.CompilerParams(vmem_limit_bytes=VMEM_LIMIT),
        name="mlstm",
    )(*args)


def _short_conv_silu(x, w_ref, seq):
    row = lax.broadcasted_iota(jnp.int32, (seq, 1), 0)
    acc = x * w_ref[CONV_K // 2:CONV_K // 2 + 1, :]
    for tap in range(CONV_K):
        delta = tap - CONV_K // 2
        if delta == 0:
            continue
        shifted = pltpu.roll(x, shift=(-delta) % seq, axis=0)
        ok = (row + delta >= 0) & (row + delta < seq)
        acc = acc + jnp.where(ok, shifted, 0.0) * w_ref[tap:tap + 1, :]
    return _silu(acc)


def _l2_unit(y):
    return y * lax.rsqrt(jnp.sum(y * y, axis=-1, keepdims=True) + EPS)


def _split_bf16(x):
    hi = x.astype(bf16)
    return hi, (x - hi.astype(f32)).astype(bf16)


def _matmul_3pass(m, x):
    mh, ml = _split_bf16(m)
    xh, xl = _split_bf16(x)
    return jnp.dot(jnp.concatenate([mh, ml, mh], axis=1), jnp.concatenate([xh, xh, xl], axis=0),
                   preferred_element_type=f32)


def _unit_triangular_solves(ns, xs):
    levels = CHUNK.bit_length() - 1
    r = lax.broadcasted_iota(jnp.int32, (CHUNK, CHUNK), 0)
    c = lax.broadcasted_iota(jnp.int32, (CHUNK, CHUNK), 1)
    eye = jnp.where(r == c, 1.0, 0.0)
    ts = [eye + n for n in ns]
    ms = [_matmul_3pass(n, n) for n in ns]
    for lvl in range(1, levels):
        for i in range(len(ns)):
            if lvl < levels - 1:
                prod = _matmul_3pass(ms[i], jnp.concatenate([ts[i], ms[i]], axis=1))
                ts[i] = ts[i] + prod[:, :CHUNK]
                ms[i] = prod[:, CHUNK:]
            else:
                ts[i] = ts[i] + _matmul_3pass(ms[i], ts[i])
    return [_matmul_3pass(t, x) for t, x in zip(ts, xs)]


def _delta_kernel(*refs, seq, zero_init):
    if zero_init:
        (q0_ref, q1_ref, k0_ref, k1_ref, v0_ref, v1_ref, wq0_ref, wq1_ref, wk0_ref, wk1_ref, wv0_ref, wv1_ref,
         g_ref, par_ref, o_ref, sf_ref, q_scr, k_scr, v_scr) = refs
    else:
        (q0_ref, q1_ref, k0_ref, k1_ref, v0_ref, v1_ref, wq0_ref, wq1_ref, wk0_ref, wk1_ref, wv0_ref, wv1_ref,
         g_ref, par_ref, s0_ref, o_ref, sf_ref, q_scr, k_scr, v_scr) = refs
    nc = seq // CHUNK
    pair = pl.program_id(1)
    lane = lax.broadcasted_iota(jnp.int32, (1, LANES), 1)
    tril, triu, stril, striu = _chunk_masks()
    masks, smasks = (tril, triu), (stril, striu)
    masks16 = (tril.astype(bf16), triu.astype(bf16))
    sel = [_head_lane_select(LANE_A + d * H_B, pair).astype(bf16) for d in range(2)]
    neg_a_row = -jnp.exp(par_ref[0:1, :])
    dt_row = par_ref[1:2, :]
    streams = [(hh, d) for hh in range(2) for d in range(2)]

    for hh, (q_ref, k_ref, v_ref, wq_ref, wk_ref, wv_ref) in enumerate(
            ((q0_ref, k0_ref, v0_ref, wq0_ref, wk0_ref, wv0_ref), (q1_ref, k1_ref, v1_ref, wq1_ref, wk1_ref, wv1_ref))):
        q_scr[hh] = _l2_unit(_short_conv_silu(q_ref[...], wq_ref, seq)) * DK_B ** -0.5
        k_scr[hh] = _l2_unit(_short_conv_silu(k_ref[...], wk_ref, seq))
        v_scr[hh] = _short_conv_silu(v_ref[...], wv_ref, seq)

    o_ref[...] = jnp.zeros_like(o_ref)
    init = tuple(jnp.zeros((DK_B, DV_B), f32) if zero_init else s0_ref[d, hh] for hh, d in streams)

    def body(trip, carry):
        subs = range(CHUNKS_PER_TRIP)
        shared = {}
        for u in subs:
            for d in range(2):
                r0 = _chunk_start(trip, u, d, nc)
                gates = g_ref[pl.ds(r0, CHUNK), :]
                beta_all = jax.nn.sigmoid(gates)
                glog = neg_a_row * _softplus(gates + dt_row)
                cum = _mask_matmul_f32(masks16[d], glog)
                rows = _select_rows_f32(sel[d], cum)
                shared[u, d] = (r0, beta_all, cum, rows)
        chains = [(u, hh, d) for u in subs for hh, d in streams]
        st = {ch: {} for ch in chains}
        for ch in chains:
            u, hh, d = ch
            r0, beta_all, cum, rows = shared[u, d]
            s = st[ch]
            head = 2 * pair + hh
            beta = _lane_col(beta_all, lane, LANE_BETA + d * H_B + head)
            gcol = _lane_col(cum, lane, LANE_A + d * H_B + head)
            glast = gcol[CHUNK - 1:CHUNK, :] if d == 0 else gcol[0:1, :]
            q = q_scr[hh, pl.ds(r0, CHUNK), :]
            k = k_scr[hh, pl.ds(r0, CHUNK), :]
            v = v_scr[hh, pl.ds(r0, CHUNK), :]
            k16 = k.astype(bf16)
            kbeta = k * beta
            eg = jnp.exp(gcol)
            decay = jnp.exp(jnp.where(masks[d], gcol - rows[hh:hh + 1, :], -jnp.inf))
            kk = lax.dot_general(kbeta.astype(bf16), k16, NT_DIMS, preferred_element_type=f32)
            qk = lax.dot_general(q.astype(bf16), k16, NT_DIMS, preferred_element_type=f32)
            s["n"] = -jnp.where(smasks[d], kk * decay, 0.0)
            s["x"] = jnp.concatenate([v * beta, kbeta * eg], axis=1)
            s["qk"] = jnp.where(masks[d], qk * decay, 0.0).astype(bf16)
            s["qg"] = (q * eg).astype(bf16)
            s["kd_t"] = (k * jnp.exp(glast - gcol)).T.astype(bf16)
            s["gl"] = jnp.exp(glast)
        solved = _unit_triangular_solves([st[ch]["n"] for ch in chains], [st[ch]["x"] for ch in chains])
        for ch, uw in zip(chains, solved):
            s = st[ch]
            s["u"] = uw[:, :DV_B]
            s["w_qg"] = jnp.concatenate([uw[:, DV_B:].astype(bf16), s["qg"]], axis=0)
        state = list(carry)
        for u in subs:
            ws = [jnp.dot(st[u, hh, d]["w_qg"], state[i].astype(bf16), preferred_element_type=f32)
                  for i, (hh, d) in enumerate(streams)]
            v_new = [(st[u, hh, d]["u"] - ws[i][:CHUNK]).astype(bf16) for i, (hh, d) in enumerate(streams)]
            for i, (hh, d) in enumerate(streams):
                s = st[u, hh, d]
                o = ws[i][CHUNK:] + jnp.dot(s["qk"], v_new[i], preferred_element_type=f32)
                r0 = shared[u, d][0]
                o_ref[pl.ds(r0, CHUNK), hh * DV_B:(hh + 1) * DV_B] += o
                state[i] = state[i] * s["gl"] + jnp.dot(s["kd_t"], v_new[i], preferred_element_type=f32)
        return tuple(state)

    final = lax.fori_loop(0, nc // CHUNKS_PER_TRIP, body, init)
    for i, (hh, d) in enumerate(streams):
        sf_ref[d, hh] = final[i]


def _delta_call(proj, conv_w8, gates, par, nb, seq, off, state):
    zero_init = state is None
    col0 = (2 * H_A * DQK_A + 2 * H_A * DV_A) // LANES
    blk = lambda col: pl.BlockSpec((seq, LANES), col)
    wblk = lambda col: pl.BlockSpec((8, LANES), col)
    in_specs = [blk(lambda b, p: (off + b, col0 + 2 * p)), blk(lambda b, p: (off + b, col0 + 2 * p + 1)),
                blk(lambda b, p: (off + b, col0 + H_B + 2 * p)), blk(lambda b, p: (off + b, col0 + H_B + 2 * p + 1)),
                blk(lambda b, p: (off + b, col0 + 2 * H_B + 2 * p)), blk(lambda b, p: (off + b, col0 + 2 * H_B + 2 * p + 1)),
                wblk(lambda b, p: (0, 2 * p)), wblk(lambda b, p: (0, 2 * p + 1)),
                wblk(lambda b, p: (0, H_B + 2 * p)), wblk(lambda b, p: (0, H_B + 2 * p + 1)),
                wblk(lambda b, p: (0, 2 * H_B + 2 * p)), wblk(lambda b, p: (0, 2 * H_B + 2 * p + 1)),
                blk(lambda b, p: (off + b, 0)),
                pl.BlockSpec((8, LANES), lambda b, p: (0, 0))]
    st_spec = pl.BlockSpec((None, 2, 2, DK_B, DV_B), lambda b, p: (b, 0, p, 0, 0))
    args = [proj] * 6 + [conv_w8] * 6 + [gates, par]
    if not zero_init:
        in_specs.append(st_spec)
        args.append(state)
    return pl.pallas_call(
        functools.partial(_delta_kernel, seq=seq, zero_init=zero_init),
        grid=(nb, H_B // 2),
        in_specs=in_specs,
        out_specs=[pl.BlockSpec((seq, 2 * DV_B), lambda b, p: (b, p)), st_spec],
        out_shape=[jax.ShapeDtypeStruct((nb * seq, H_B * DV_B), f32),
                   jax.ShapeDtypeStruct((nb, 2, H_B, DK_B, DV_B), f32)],
        scratch_shapes=[pltpu.VMEM((2, seq, LANES), f32)] * 3,
        compiler_params=pltpu.CompilerParams(vmem_limit_bytes=VMEM_LIMIT),
        name="delta",
    )(*args)


def _head_rms(x):
    return x * lax.rsqrt(jnp.mean(x * x, axis=-1, keepdims=True) + EPS)


def _ab_out_kernel(x_ref, hp_ref, hs_ref, op_ref, os_ref, om_ref, zd_ref, gm_ref, gd_ref, w_ref, mod_ref, o_ref,
                   *, gate_row, tm):
    hsum = _prompt_or_sample(hp_ref, hs_ref, tm)
    osum = _prompt_or_sample(op_ref, os_ref, tm)
    parts = []
    for h in range(H_A):
        sl = slice(h * DV_A, (h + 1) * DV_A)
        parts.append(_head_rms(hsum[:, sl]) * gm_ref[:, sl] * jax.nn.sigmoid(om_ref[:, sl]))
    for h in range(H_B):
        sl = slice(h * DV_B, (h + 1) * DV_B)
        parts.append(_head_rms(osum[:, sl]) * gd_ref[:, sl] * _silu(zd_ref[:, sl]))
    cat = jnp.concatenate(parts, axis=1).astype(bf16)
    y = jnp.dot(cat, w_ref[...].astype(bf16), preferred_element_type=f32)
    o_ref[...] = x_ref[...] + mod_ref[gate_row:gate_row + 1, :] * y


def _ab_out_call(x, hp, hs, op, os_, proj, gm_row, gd_row, w_out, mod_l, gate_row, tm=512):
    wide = H_A * DV_A
    om_blk = (2 * H_A * DQK_A) // wide + 1
    zd_blk = AB_MAIN // wide - 1
    return pl.pallas_call(
        functools.partial(_ab_out_kernel, gate_row=gate_row, tm=tm),
        grid=(N_TOK // tm,),
        in_specs=[pl.BlockSpec((tm, D_MODEL), lambda i: (i, 0))] + _split_specs(tm, wide) + _split_specs(tm, wide) + [
            pl.BlockSpec((tm, wide), lambda i: (i, om_blk)),
            pl.BlockSpec((tm, wide), lambda i: (i, zd_blk)),
            pl.BlockSpec((1, wide), lambda i: (0, 0)),
            pl.BlockSpec((1, wide), lambda i: (0, 0)),
            pl.BlockSpec((2 * wide, D_MODEL), lambda i: (0, 0)),
            pl.BlockSpec((None, N_ADA, D_MODEL), lambda i: (_mod_row_index(i, tm), 0, 0))],
        out_specs=pl.BlockSpec((tm, D_MODEL), lambda i: (i, 0)),
        out_shape=jax.ShapeDtypeStruct((N_TOK, D_MODEL), f32),
        compiler_params=pltpu.CompilerParams(vmem_limit_bytes=VMEM_LIMIT),
        name="ab_out",
    )(x, hp, hs, op, os_, proj, proj, gm_row, gd_row, w_out, mod_l)


def _na_out_kernel(x_ref, ap_ref, as_ref, w_ref, mod_ref, o_ref, *, gate_row, tm):
    a = _prompt_or_sample(ap_ref, as_ref, tm).astype(bf16)
    y = jnp.dot(a, w_ref[...].astype(bf16), preferred_element_type=f32)
    o_ref[...] = x_ref[...] + mod_ref[gate_row:gate_row + 1, :] * y


def _na_out_call(x, attn_p, attn_s, w_out, mod_l, gate_row, tm=512):
    return pl.pallas_call(
        functools.partial(_na_out_kernel, gate_row=gate_row, tm=tm),
        grid=(N_TOK // tm,),
        in_specs=[pl.BlockSpec((tm, D_MODEL), lambda i: (i, 0))] + _split_specs(tm, NA_W) + [
            pl.BlockSpec((NA_W, D_MODEL), lambda i: (0, 0)),
            pl.BlockSpec((None, N_ADA, D_MODEL), lambda i: (_mod_row_index(i, tm), 0, 0))],
        out_specs=pl.BlockSpec((tm, D_MODEL), lambda i: (i, 0)),
        out_shape=jax.ShapeDtypeStruct((N_TOK, D_MODEL), f32),
        compiler_params=pltpu.CompilerParams(vmem_limit_bytes=VMEM_LIMIT),
        name="na_out",
    )(x, attn_p, attn_s, w_out, mod_l)


CTX_HEADS = 8


def _ctx_attn_kernel(q_ref, k_ref, v_ref, o_ref, nk_ref, nv_ref):
    outs = []
    for hh in range(CTX_HEADS):
        sl = slice(hh * DH_C, (hh + 1) * DH_C)
        q = q_ref[:, sl].astype(bf16)
        nk_ref[hh] = k_ref[:, sl]
        nv_ref[hh] = v_ref[:, sl]
        k = k_ref[:, sl].astype(bf16)
        v = v_ref[:, sl].astype(bf16)
        s = lax.dot_general(q, k, NT_DIMS, preferred_element_type=f32) * DH_C ** -0.5
        p = jnp.exp(s - jnp.max(s, axis=1, keepdims=True))
        o = jnp.dot(p.astype(bf16), v, preferred_element_type=f32)
        outs.append(o / jnp.sum(p, axis=1, keepdims=True))
    o_ref[...] = jnp.concatenate(outs, axis=1)


def _ctx_attn_call(proj):
    w = CTX_HEADS * DH_C
    nblk = NA_W // w
    return pl.pallas_call(
        _ctx_attn_kernel,
        grid=(BATCH, nblk),
        in_specs=[pl.BlockSpec((SEQ, w), lambda b, j: (b, j)),
                  pl.BlockSpec((SEQ, w), lambda b, j: (b, nblk + j)),
                  pl.BlockSpec((SEQ, w), lambda b, j: (b, 2 * nblk + j))],
        out_specs=[pl.BlockSpec((SEQ, w), lambda b, j: (b, j)),
                   pl.BlockSpec((None, CTX_HEADS, SEQ, DH_C), lambda b, j: (b, j, 0, 0)),
                   pl.BlockSpec((None, CTX_HEADS, SEQ, DH_C), lambda b, j: (b, j, 0, 0))],
        out_shape=[jax.ShapeDtypeStruct((N_PROMPT, NA_W), f32),
                   jax.ShapeDtypeStruct((BATCH, H_C, SEQ, DH_C), f32),
                   jax.ShapeDtypeStruct((BATCH, H_C, SEQ, DH_C), f32)],
        compiler_params=pltpu.CompilerParams(vmem_limit_bytes=VMEM_LIMIT),
        name="ctx_attn",
    )(proj, proj, proj)


QROWS = 4
QBLK = QROWS * GRID_W
KROWS_MID = QROWS + WIN_R - 1
N_RIDX = 2 * WIN_R - 1
N_CIDX = 2 * WIN_C - 1
N_QBLK = GRID_ROWS // QROWS


def _nattn_bias_tables(rb_ref, tb_scr, bmid_scr, btop_scr, bbot_scr):
    qc = lax.broadcasted_iota(jnp.int32, (GRID_W, GRID_W), 0)
    kc = lax.broadcasted_iota(jnp.int32, (GRID_W, GRID_W), 1)
    cs = jnp.clip(qc - WIN_C // 2, 0, GRID_W - WIN_C)
    valid = (kc >= cs) & (kc < cs + WIN_C)
    neg = jnp.full((GRID_W, GRID_W), -jnp.inf, f32)
    for hh in range(2):
        for ri in range(N_RIDX):
            row = jnp.broadcast_to(rb_ref[hh, ri:ri + 1, :], (GRID_W, LANES))
            tile = pltpu.roll(row, LANES - (WIN_C - 1), 1, stride=1, stride_axis=0)[:, :GRID_W]
            tb_scr[ri] = jnp.where(valid, tile, -jnp.inf)
        for i in range(QROWS):
            rs = slice(i * GRID_W, (i + 1) * GRID_W)
            for jj in range(KROWS_MID):
                inside = 0 <= jj - i < WIN_R
                bmid_scr[hh, rs, jj * GRID_W:(jj + 1) * GRID_W] = tb_scr[jj - i + WIN_R // 2 - 1] if inside else neg
            for jj in range(WIN_R):
                btop_scr[hh, rs, jj * GRID_W:(jj + 1) * GRID_W] = tb_scr[jj - i + WIN_R - 1]
                bbot_scr[hh, rs, jj * GRID_W:(jj + 1) * GRID_W] = tb_scr[jj - i + WIN_R // 2 - 1]


def _nattn_kernel(rb_ref, q_ref, k_ref, v_ref, ck_ref, cv_ref, o_ref, tb_scr, bmid_scr, btop_scr, bbot_scr):
    scale = DH_C ** -0.5

    @pl.when(pl.program_id(1) == 0)
    def _():
        _nattn_bias_tables(rb_ref, tb_scr, bmid_scr, btop_scr, bbot_scr)

    for hh in range(2):
        sl = slice(hh * DH_C, (hh + 1) * DH_C)
        kctx = ck_ref[hh].astype(bf16)
        vctx = cv_ref[hh].astype(bf16)

        def block(q_start, k_start, k_rows, bias_scr):
            nk = k_rows * GRID_W
            q = q_ref[pl.ds(q_start, QBLK), sl].astype(bf16)
            ku = k_ref[pl.ds(k_start, nk), sl].astype(bf16)
            vu = v_ref[pl.ds(k_start, nk), sl].astype(bf16)
            s_loc = lax.dot_general(q, ku, NT_DIMS, preferred_element_type=f32) * scale + bias_scr[hh]
            s_ctx = lax.dot_general(q, kctx, NT_DIMS, preferred_element_type=f32) * scale
            m = jnp.maximum(jnp.max(s_loc, axis=1, keepdims=True), jnp.max(s_ctx, axis=1, keepdims=True))
            p_loc = jnp.exp(s_loc - m)
            p_ctx = jnp.exp(s_ctx - m)
            denom = jnp.sum(p_loc, axis=1, keepdims=True) + jnp.sum(p_ctx, axis=1, keepdims=True)
            o = (jnp.dot(p_loc.astype(bf16), vu, preferred_element_type=f32)
                 + jnp.dot(p_ctx.astype(bf16), vctx, preferred_element_type=f32))
            o_ref[pl.ds(q_start, QBLK), sl] = o / denom

        block(0, 0, WIN_R, btop_scr)

        def mid_body(blk, _):
            q_start = pl.multiple_of(blk * QBLK, QBLK)
            k_start = pl.multiple_of(blk * QBLK - (WIN_R // 2) * GRID_W, QBLK)
            block(q_start, k_start, KROWS_MID, bmid_scr)
            return 0

        lax.fori_loop(1, N_QBLK - 1, mid_body, 0)
        block((N_QBLK - 1) * QBLK, (GRID_ROWS - WIN_R) * GRID_W, WIN_R, bbot_scr)


def _nattn_call(proj, cache_k, cache_v, rel_bias, off):
    npair = H_C // 2
    return pl.pallas_call(
        _nattn_kernel,
        grid=(npair, DEC_BATCH),
        in_specs=[pl.BlockSpec((2, 2 * WIN_R, LANES), lambda p, b: (p, 0, 0)),
                  pl.BlockSpec((DEC_SEQ, LANES), lambda p, b: (off + b, p)),
                  pl.BlockSpec((DEC_SEQ, LANES), lambda p, b: (off + b, npair + p)),
                  pl.BlockSpec((DEC_SEQ, LANES), lambda p, b: (off + b, 2 * npair + p)),
                  pl.BlockSpec((None, 2, PAST_LEN, DH_C), lambda p, b: (b, p, 0, 0)),
                  pl.BlockSpec((None, 2, PAST_LEN, DH_C), lambda p, b: (b, p, 0, 0))],
        out_specs=pl.BlockSpec((DEC_SEQ, LANES), lambda p, b: (b, p)),
        out_shape=jax.ShapeDtypeStruct((N_SAMPLE, NA_W), f32),
        scratch_shapes=[pltpu.VMEM((N_RIDX, GRID_W, GRID_W), f32),
                        pltpu.VMEM((2, QBLK, KROWS_MID * GRID_W), f32),
                        pltpu.VMEM((2, QBLK, WIN_R * GRID_W), f32),
                        pltpu.VMEM((2, QBLK, WIN_R * GRID_W), f32)],
        compiler_params=pltpu.CompilerParams(dimension_semantics=("arbitrary", "arbitrary"),
                                             vmem_limit_bytes=VMEM_LIMIT),
        name="nattn",
    )(jnp.pad(rel_bias, ((0, 0), (0, 2 * WIN_R - N_RIDX), (0, LANES - N_CIDX))), proj, proj, proj, cache_k, cache_v)


def _lane_row(pieces):
    row = jnp.zeros((LANES,), f32)
    for off, vals in pieces:
        row = row.at[off:off + vals.shape[0]].set(vals.astype(f32))
    return row


def _param_rows(rows):
    out = jnp.zeros((8, LANES), f32)
    for r, row in enumerate(rows):
        out = out.at[r].set(row)
    return out


def kernel(x_prompt, x_sample, c, state_mlstm_C, state_mlstm_n, state_mlstm_m, state_delta_S, cache_na_k, cache_na_v, c_ctx, ada_w, ada_b, norm_g, ffn_wg, ffn_wu, ffn_wd, ab_w_in, ab_w_out, mlstm_b_i, mlstm_b_f, mlstm_norm_g, delta_conv_w, delta_a_log, delta_dt_bias, delta_norm_g, na_w_in, na_w_out, na_rel_bias, final_norm_g):
    xs = (x_prompt.reshape(N_PROMPT, D_MODEL), x_sample.reshape(N_SAMPLE, D_MODEL))
    mods = _ada_call(jnp.concatenate([c_ctx[None, :], c], axis=0), ada_w, ada_b)
    mods = mods.reshape(DEPTH, 3, N_ADA, D_MODEL)
    gf_row = final_norm_g.reshape(1, D_MODEL)
    s_off = N_PROMPT // DEC_SEQ
    new_c, new_n, new_m, new_s, new_k, new_v = [], [], [], [], [], []

    for l in range(DEPTH):
        mod_l = mods[l]
        a = l // 2
        x = _ffn_call(xs if l == 0 else (x,), mod_l, norm_g[l, 0].reshape(1, D_MODEL), ffn_wg, ffn_wu, ffn_wd, l, 0,
                      gf_row, rows=(0, 1, 2), final=False, split_out=False, tm=FFN_TM_SPLIT if l == 0 else FFN_TM)[0]
        g_mix = norm_g[l, 1].reshape(1, D_MODEL)
        if l % 2 == 0:
            w_in = ab_w_in[a]
            i0 = 2 * H_A * DQK_A + 2 * H_A * DV_A
            d0 = i0 + 4 * H_A
            b0 = d0 + 2 * H_B * DK_B + 2 * H_B * DV_B
            w_main = jnp.concatenate([w_in[:, :i0], w_in[:, d0:b0]], axis=1)
            w_gate = jnp.concatenate([w_in[:, i0:d0], w_in[:, b0:],
                                      jnp.zeros((D_MODEL, LANES - 4 * H_A - 4 * H_B), f32)], axis=1)
            proj, gates = _proj_call(x, mod_l, g_mix, w_main, w_gate, rows=(3, 4))

            par_m = _param_rows([_lane_row([(LANE_I, mlstm_b_i[a].reshape(-1))]),
                                 _lane_row([(LANE_F, mlstm_b_f[a].reshape(-1))])])
            par_d = _param_rows([_lane_row([(LANE_A, delta_a_log[a].reshape(-1))]),
                                 _lane_row([(LANE_A, delta_dt_bias[a].reshape(-1))])])
            conv_w8 = jnp.concatenate([delta_conv_w[a], jnp.zeros((8 - CONV_K, 3 * H_B * DK_B), f32)], axis=0)

            hp, cn, nn_, mn = _mlstm_call(proj, gates, par_m, BATCH, SEQ, 0, None)
            st = (state_mlstm_C[:, a], state_mlstm_n[:, a][..., None], state_mlstm_m[:, a][..., None, None])
            hs, _, _, _ = _mlstm_call(proj, gates, par_m, DEC_BATCH, DEC_SEQ, s_off, st)
            op, sn = _delta_call(proj, conv_w8, gates, par_d, BATCH, SEQ, 0, None)
            os_, _ = _delta_call(proj, conv_w8, gates, par_d, DEC_BATCH, DEC_SEQ, s_off, state_delta_S[:, a])
            new_c.append(cn)
            new_n.append(nn_[..., 0])
            new_m.append(mn[..., 0, 0])
            new_s.append(sn)
            x = _ab_out_call(x, hp, hs, op, os_, proj, mlstm_norm_g[a].reshape(1, -1),
                             jnp.tile(delta_norm_g[a], H_B).reshape(1, -1), ab_w_out[a], mod_l, gate_row=5)
        else:
            proj = _proj_call(x, mod_l, g_mix, na_w_in[a], None, rows=(3, 4))[0]
            attn_p, kp, vp = _ctx_attn_call(proj)
            attn_s = _nattn_call(proj, cache_na_k[:, a], cache_na_v[:, a], na_rel_bias[a], s_off)
            new_k.append(kp)
            new_v.append(vp)
            x = _na_out_call(x, attn_p, attn_s, na_w_out[a], mod_l, gate_row=5)
        last = l == DEPTH - 1
        outs = _ffn_call((x,), mod_l, norm_g[l, 2].reshape(1, D_MODEL), ffn_wg, ffn_wu, ffn_wd, l, 1,
                         gf_row, rows=(6, 7, 8), final=last, split_out=last, tm=FFN_TM_SPLIT if last else FFN_TM)
        x = outs[0]

    y_prompt = outs[0].reshape(BATCH, SEQ, D_MODEL)
    y_sample = outs[1].reshape(DEC_BATCH, DEC_SEQ, D_MODEL)
    return (y_prompt, y_sample, jnp.stack(new_c, axis=1), jnp.stack(new_n, axis=1), jnp.stack(new_m, axis=1),
            jnp.stack(new_s, axis=1), jnp.stack(new_k, axis=1), jnp.stack(new_v, axis=1))
```

```python
import functools

import jax
import jax.numpy as jnp
from jax import lax
from jax.experimental import pallas as pl
from jax.experimental.pallas import tpu as pltpu

f32 = jnp.float32
bf16 = jnp.bfloat16

D_MODEL = 1024
BATCH = 16
SEQ = 256
DEPTH = 2
DEC_BATCH = 2
DEC_SEQ = 2048
PAST_LEN = 256
GRID_W = 64
GRID_ROWS = DEC_SEQ // GRID_W
D_FF = 2816
N_ADA = 9
EPS = 1e-6
CHUNK = 64
H_A, DQK_A, DV_A = 4, 64, 128
H_B, DK_B, DV_B = 4, 128, 128
CONV_K = 5
H_C, DH_C = 16, 64
WIN_R, WIN_C = 8, 16
NA_W = H_C * DH_C

N_PROMPT = BATCH * SEQ
N_SAMPLE = DEC_BATCH * DEC_SEQ
N_TOK = N_PROMPT + N_SAMPLE
AB_MAIN = 2 * H_A * DQK_A + 2 * H_A * DV_A + 2 * H_B * DK_B + 2 * H_B * DV_B
LANES = 128
VMEM_LIMIT = 56 * 1024 * 1024

LANE_I, LANE_F, LANE_BETA, LANE_A = 0, 8, 16, 24

NT_DIMS = (((1,), (1,)), ((), ()))
TN_DIMS = (((0,), (0,)), ((), ()))

CHUNKS_PER_TRIP = 4
FFN_TM = 2048
FFN_TM_SPLIT = 1024


def _softplus(x):
    return jnp.maximum(x, 0.0) + jnp.log1p(jnp.exp(-jnp.abs(x)))


def _log_sigmoid(x):
    return -_softplus(-x)


def _silu(x):
    return x * jax.nn.sigmoid(x)


def _mod_row_index(i, tm):
    n_p = N_PROMPT // tm
    per_b = DEC_SEQ // tm
    return jnp.where(i < n_p, 0, 1 + (i - n_p) // per_b)


def _modulated(x, g_row, shift_row, scale_row):
    y = x * lax.rsqrt(jnp.mean(x * x, axis=-1, keepdims=True) + EPS) * g_row
    return y * (1.0 + scale_row) + shift_row


def _ada_kernel(cb_ref, w_ref, b_ref, o_ref):
    tn = w_ref.shape[-1]
    for r in range(3):
        cb = cb_ref[r]
        s = cb * jax.nn.sigmoid(cb)
        parts = [jnp.sum(w_ref[:, j * LANES:(j + 1) * LANES] * s, axis=0, keepdims=True)
                 for j in range(tn // LANES)]
        o_ref[r:r + 1, :] = jnp.concatenate(parts, axis=1) + b_ref[...]


def _ada_call(cond3, ada_w, ada_b):
    tn = D_MODEL
    cb = jnp.broadcast_to(cond3[:, :, None], (3, D_MODEL, LANES))
    return pl.pallas_call(
        _ada_kernel,
        grid=(DEPTH, N_ADA * D_MODEL // tn),
        in_specs=[pl.BlockSpec((3, D_MODEL, LANES), lambda l, j: (0, 0, 0)),
                  pl.BlockSpec((None, D_MODEL, tn), lambda l, j: (l, 0, j)),
                  pl.BlockSpec((None, 1, tn), lambda l, j: (l, 0, j))],
        out_specs=pl.BlockSpec((None, 3, tn), lambda l, j: (l, 0, j)),
        out_shape=jax.ShapeDtypeStruct((DEPTH, 3, N_ADA * D_MODEL), f32),
        compiler_params=pltpu.CompilerParams(vmem_limit_bytes=VMEM_LIMIT),
        name="ada_mod",
    )(cb, ada_w, ada_b.reshape(DEPTH, 1, N_ADA * D_MODEL))


def _prompt_or_sample(p_ref, s_ref, tm):
    return jnp.where(pl.program_id(0) < N_PROMPT // tm, p_ref[...], s_ref[...])


def _split_specs(tm, width, n_grid_axes=1):
    n_p = N_PROMPT // tm
    if n_grid_axes == 1:
        return [pl.BlockSpec((tm, width), lambda i: (jnp.minimum(i, n_p - 1), 0)),
                pl.BlockSpec((tm, width), lambda i: (jnp.maximum(i - n_p, 0), 0))]
    return [pl.BlockSpec((tm, width), lambda i, j: (jnp.minimum(i, n_p - 1), 0)),
            pl.BlockSpec((tm, width), lambda i, j: (jnp.maximum(i - n_p, 0), 0))]


def _ffn_kernel(*refs, rows, final, split_in, split_out, tm):
    it = iter(refs)
    x_refs = [next(it) for _ in range(2 if split_in else 1)]
    mod_ref, g_ref, wg_ref, wu_ref, wd_ref, gf_ref = [next(it) for _ in range(6)]
    o_refs = [next(it) for _ in range(2 if split_out else 1)]
    h_scr = next(it)
    acc_ref = next(it) if split_out else o_refs[0]
    j = pl.program_id(1)

    def load_x():
        return _prompt_or_sample(x_refs[0], x_refs[1], tm) if split_in else x_refs[0][...]

    @pl.when(j == 0)
    def _():
        h = _modulated(load_x(), g_ref[...], mod_ref[rows[0]:rows[0] + 1, :], mod_ref[rows[1]:rows[1] + 1, :])
        h_scr[...] = h.astype(bf16)
        acc_ref[...] = jnp.zeros_like(acc_ref)

    h = h_scr[...]
    g = jnp.dot(h, wg_ref[...].astype(bf16), preferred_element_type=f32)
    u = jnp.dot(h, wu_ref[...].astype(bf16), preferred_element_type=f32)
    a = (_silu(g) * u).astype(bf16)
    acc_ref[...] += jnp.dot(a, wd_ref[...].astype(bf16), preferred_element_type=f32)

    @pl.when(j == pl.num_programs(1) - 1)
    def _():
        xn = load_x() + (0.5 * mod_ref[rows[2]:rows[2] + 1, :]) * acc_ref[...]
        if final:
            xn = xn * lax.rsqrt(jnp.mean(xn * xn, axis=-1, keepdims=True) + EPS) * gf_ref[...]
        if split_out:
            is_prompt = pl.program_id(0) < N_PROMPT // tm

            @pl.when(is_prompt)
            def _():
                o_refs[0][...] = xn

            @pl.when(jnp.logical_not(is_prompt))
            def _():
                o_refs[1][...] = xn
        else:
            o_refs[0][...] = xn


def _ffn_call(xs, mod_l, g_row, ffn_wg, ffn_wu, ffn_wd, layer, half, gf_row, rows, final, split_out, tm, tf=256):
    split_in = len(xs) == 2
    x_specs = (_split_specs(tm, D_MODEL, 2) if split_in else [pl.BlockSpec((tm, D_MODEL), lambda i, j: (i, 0))])
    if split_out:
        out_specs = _split_specs(tm, D_MODEL, 2)
        out_shape = [jax.ShapeDtypeStruct((N_PROMPT, D_MODEL), f32), jax.ShapeDtypeStruct((N_SAMPLE, D_MODEL), f32)]
        scratch = [pltpu.VMEM((tm, D_MODEL), bf16), pltpu.VMEM((tm, D_MODEL), f32)]
    else:
        out_specs = [pl.BlockSpec((tm, D_MODEL), lambda i, j: (i, 0))]
        out_shape = [jax.ShapeDtypeStruct((N_TOK, D_MODEL), f32)]
        scratch = [pltpu.VMEM((tm, D_MODEL), bf16)]
    return pl.pallas_call(
        functools.partial(_ffn_kernel, rows=rows, final=final, split_in=split_in, split_out=split_out, tm=tm),
        grid=(N_TOK // tm, D_FF // tf),
        in_specs=x_specs + [
            pl.BlockSpec((None, N_ADA, D_MODEL), lambda i, j: (_mod_row_index(i, tm), 0, 0)),
            pl.BlockSpec((1, D_MODEL), lambda i, j: (0, 0)),
            pl.BlockSpec((None, None, D_MODEL, tf), lambda i, j: (layer, half, 0, j)),
            pl.BlockSpec((None, None, D_MODEL, tf), lambda i, j: (layer, half, 0, j)),
            pl.BlockSpec((None, None, tf, D_MODEL), lambda i, j: (layer, half, j, 0)),
            pl.BlockSpec((1, D_MODEL), lambda i, j: (0, 0))],
        out_specs=out_specs, out_shape=out_shape, scratch_shapes=scratch,
        compiler_params=pltpu.CompilerParams(dimension_semantics=("parallel", "arbitrary"),
                                             vmem_limit_bytes=VMEM_LIMIT),
        name="ffn",
    )(*xs, mod_l, g_row, ffn_wg, ffn_wu, ffn_wd, gf_row)


def _proj_kernel(*refs, rows, with_gates):
    if with_gates:
        x_ref, mod_ref, g_ref, w_ref, wgate_ref, o_ref, og_ref, h_scr = refs
    else:
        x_ref, mod_ref, g_ref, w_ref, o_ref, h_scr = refs
    j = pl.program_id(1)

    @pl.when(j == 0)
    def _():
        h = _modulated(x_ref[...], g_ref[...], mod_ref[rows[0]:rows[0] + 1, :], mod_ref[rows[1]:rows[1] + 1, :])
        hb = h.astype(bf16)
        h_scr[...] = hb
        if with_gates:
            og_ref[...] = jnp.dot(hb, wgate_ref[...].astype(bf16), preferred_element_type=f32)

    o_ref[...] = jnp.dot(h_scr[...], w_ref[...].astype(bf16), preferred_element_type=f32)


def _proj_call(x, mod_l, g_row, w, w_gate, rows, tm=1024):
    n = w.shape[1]
    tn = n // 2
    with_gates = w_gate is not None
    in_specs = [pl.BlockSpec((tm, D_MODEL), lambda i, j: (i, 0)),
                pl.BlockSpec((None, N_ADA, D_MODEL), lambda i, j: (_mod_row_index(i, tm), 0, 0)),
                pl.BlockSpec((1, D_MODEL), lambda i, j: (0, 0)),
                pl.BlockSpec((D_MODEL, tn), lambda i, j: (0, j))]
    out_specs = [pl.BlockSpec((tm, tn), lambda i, j: (i, j))]
    out_shape = [jax.ShapeDtypeStruct((N_TOK, n), f32)]
    args = [x, mod_l, g_row, w]
    if with_gates:
        in_specs.append(pl.BlockSpec((D_MODEL, LANES), lambda i, j: (0, 0)))
        out_specs.append(pl.BlockSpec((tm, LANES), lambda i, j: (i, 0)))
        out_shape.append(jax.ShapeDtypeStruct((N_TOK, LANES), f32))
        args.append(w_gate)
    return pl.pallas_call(
        functools.partial(_proj_kernel, rows=rows, with_gates=with_gates),
        grid=(N_TOK // tm, n // tn),
        in_specs=in_specs, out_specs=out_specs, out_shape=out_shape,
        scratch_shapes=[pltpu.VMEM((tm, D_MODEL), bf16)],
        compiler_params=pltpu.CompilerParams(dimension_semantics=("parallel", "arbitrary"),
                                             vmem_limit_bytes=VMEM_LIMIT),
        name="in_proj",
    )(*args)


def _chunk_masks():
    r = lax.broadcasted_iota(jnp.int32, (CHUNK, CHUNK), 0)
    c = lax.broadcasted_iota(jnp.int32, (CHUNK, CHUNK), 1)
    return r >= c, r <= c, r > c, r < c


def _lane_col(x, lane, j):
    return jnp.sum(jnp.where(lane == j, x, 0.0), axis=1, keepdims=True)


def _head_lane_select(base, pair):
    r = lax.broadcasted_iota(jnp.int32, (8, LANES), 0)
    ln = lax.broadcasted_iota(jnp.int32, (8, LANES), 1)
    return jnp.where((ln == base + 2 * pair + r) & (r < 2), 1.0, 0.0)


def _split3_bf16(x):
    hi = x.astype(bf16)
    r = x - hi.astype(f32)
    mid = r.astype(bf16)
    return hi, mid, (r - mid.astype(f32)).astype(bf16)


def _thrice(a16):
    return jnp.concatenate([a16] * 3, axis=1)


def _mask_matmul_f32(mask16x3, x):
    return jnp.dot(mask16x3, jnp.concatenate(_split3_bf16(x), axis=0), preferred_element_type=f32)


def _select_rows_f32(sel16x3, x):
    return lax.dot_general(sel16x3, jnp.concatenate(_split3_bf16(x), axis=1), NT_DIMS, preferred_element_type=f32)


def _chunk_start(trip, u, d, nc):
    step = trip * CHUNKS_PER_TRIP + u
    c = step if d == 0 else nc - 1 - step
    return pl.multiple_of(c * CHUNK, CHUNK)


def _mlstm_kernel(*refs, seq, zero_init):
    if zero_init:
        qp_ref, kp_ref, v0_ref, v1_ref, g_ref, par_ref, h_ref, cf_ref, nf_ref, mf_ref = refs
    else:
        (qp_ref, kp_ref, v0_ref, v1_ref, g_ref, par_ref, c0_ref, n0_ref, m0_ref,
         h_ref, cf_ref, nf_ref, mf_ref) = refs
    nc = seq // CHUNK
    pair = pl.program_id(1)
    lane = lax.broadcasted_iota(jnp.int32, (1, LANES), 1)
    tril, triu, _, _ = _chunk_masks()
    masks = (tril, triu)
    masks16 = (_thrice(tril.astype(bf16)), _thrice(triu.astype(bf16)))
    sel = [_thrice(_head_lane_select(LANE_F + d * H_A, pair).astype(bf16)) for d in range(2)]
    bi_row = par_ref[0:1, :]
    bf_row = par_ref[1:2, :]
    ones_col = jnp.where(lane == 0, 1.0, 0.0) + jnp.zeros((CHUNK, LANES), f32)
    v_refs = (v0_ref, v1_ref)
    streams = [(hh, d) for hh in range(2) for d in range(2)]

    h_ref[...] = jnp.zeros_like(h_ref)

    init = []
    for hh, d in streams:
        if zero_init:
            init.append((jnp.zeros((DQK_A, 2 * LANES), f32), jnp.zeros((1, 1), f32)))
        else:
            n_aug = jnp.where(lane == 0, n0_ref[d, hh], 0.0)
            init.append((jnp.concatenate([c0_ref[d, hh], n_aug], axis=1), m0_ref[d, hh]))

    def body(trip, carry):
        subs = range(CHUNKS_PER_TRIP)
        shared = {}
        for u in subs:
            for d in range(2):
                r0 = _chunk_start(trip, u, d, nc)
                gates = g_ref[pl.ds(r0, CHUNK), :]
                gi = gates + bi_row
                gf = _log_sigmoid(gates + bf_row)
                cum = _mask_matmul_f32(masks16[d], gf)
                stack = jnp.concatenate([cum, pltpu.roll(gi, LANE_F - LANE_I, axis=1)], axis=0)
                rows = _select_rows_f32(sel[d], stack)
                total = cum[CHUNK - 1:CHUNK, :] if d == 0 else cum[0:1, :]
                shared[u, d] = (r0, gi, cum, rows, total)
        chains = [(u, hh, d) for u in subs for hh, d in streams]
        st = {ch: {} for ch in chains}
        for ch in chains:
            u, hh, d = ch
            r0 = shared[u, d][0]
            s = st[ch]
            s["q"] = (qp_ref[pl.ds(r0, CHUNK), hh * DQK_A:(hh + 1) * DQK_A] * DQK_A ** -0.5).astype(bf16)
            s["k"] = kp_ref[pl.ds(r0, CHUNK), hh * DQK_A:(hh + 1) * DQK_A].astype(bf16)
            s["v_aug"] = jnp.concatenate([v_refs[hh][pl.ds(r0, CHUNK), :], ones_col], axis=1)
            s["qk"] = lax.dot_general(s["q"], s["k"], NT_DIMS, preferred_element_type=f32)
        for ch in chains:
            u, hh, d = ch
            _, gi, cum, rows, total = shared[u, d]
            s = st[ch]
            head = 2 * pair + hh
            jf = LANE_F + d * H_A + head
            ji = LANE_I + d * H_A + head
            bcol = _lane_col(cum, lane, jf)
            icol = _lane_col(gi, lane, ji)
            dmat = jnp.where(masks[d], bcol - rows[hh:hh + 1, :CHUNK] + rows[hh:hh + 1, CHUNK:], -jnp.inf)
            m_loc = jnp.max(dmat, axis=1, keepdims=True)
            s["sw"] = (s["qk"] * jnp.exp(dmat - m_loc)).astype(bf16)
            blast = _lane_col(total, lane, jf)
            gs = blast - bcol + icol
            ms_loc = jnp.max(gs, axis=0, keepdims=True)
            s["wv"] = (jnp.exp(gs - ms_loc) * s["v_aug"]).astype(bf16)
            s.update(bcol=bcol, m_loc=m_loc, blast=blast, ms_loc=ms_loc)
        for ch in chains:
            s = st[ch]
            s["num"] = jnp.dot(s["sw"], s["v_aug"].astype(bf16), preferred_element_type=f32)
            s["kv"] = lax.dot_general(s["k"], s["wv"], TN_DIMS, preferred_element_type=f32)
        state = list(carry)
        for u in subs:
            qc = [jnp.dot(st[u, hh, d]["q"], state[i][0].astype(bf16), preferred_element_type=f32)
                  for i, (hh, d) in enumerate(streams)]
            for i, (hh, d) in enumerate(streams):
                s = st[u, hh, d]
                c_aug, m = state[i]
                m_inter = s["bcol"] + m
                m_t = jnp.maximum(m_inter, s["m_loc"])
                nd = jnp.exp(s["m_loc"] - m_t) * s["num"] + jnp.exp(m_inter - m_t) * qc[i]
                den = nd[:, DV_A:DV_A + 1]
                hval = nd[:, :DV_A] / jnp.maximum(jnp.abs(den), jnp.exp(-m_t))
                r0 = shared[u, d][0]
                h_ref[pl.ds(r0, CHUNK), hh * DV_A:(hh + 1) * DV_A] += hval
                m_new = jnp.maximum(s["blast"] + m, s["ms_loc"])
                c_new = jnp.exp(s["blast"] + m - m_new) * c_aug + jnp.exp(s["ms_loc"] - m_new) * s["kv"]
                state[i] = (c_new, m_new)
        return tuple(state)

    final = lax.fori_loop(0, nc // CHUNKS_PER_TRIP, body, tuple(init))
    for i, (hh, d) in enumerate(streams):
        c_aug, m = final[i]
        cf_ref[d, hh] = c_aug[:, :DV_A]
        nf_ref[d, hh] = c_aug[:, DV_A:DV_A + 1]
        mf_ref[d, hh] = m


def _mlstm_call(proj, gates, par, nb, seq, off, state):
    zero_init = state is None
    blk = lambda col: pl.BlockSpec((seq, LANES), col)
    in_specs = [blk(lambda b, p: (off + b, p)),
                blk(lambda b, p: (off + b, 2 + p)),
                blk(lambda b, p: (off + b, 4 + 2 * p)),
                blk(lambda b, p: (off + b, 5 + 2 * p)),
                blk(lambda b, p: (off + b, 0)),
                pl.BlockSpec((8, LANES), lambda b, p: (0, 0))]
    st_specs = [pl.BlockSpec((None, 2, 2, DQK_A, DV_A), lambda b, p: (b, 0, p, 0, 0)),
                pl.BlockSpec((None, 2, 2, DQK_A, 1), lambda b, p: (b, 0, p, 0, 0)),
                pl.BlockSpec((None, 2, 2, 1, 1), lambda b, p: (b, 0, p, 0, 0))]
    args = [proj, proj, proj, proj, gates, par]
    if not zero_init:
        in_specs += st_specs
        args += list(state)
    return pl.pallas_call(
        functools.partial(_mlstm_kernel, seq=seq, zero_init=zero_init),
        grid=(nb, H_A // 2),
        in_specs=in_specs,
        out_specs=[pl.BlockSpec((seq, 2 * DV_A), lambda b, p: (b, p))] + st_specs,
        out_shape=[jax.ShapeDtypeStruct((nb * seq, H_A * DV_A), f32),
                   jax.ShapeDtypeStruct((nb, 2, H_A, DQK_A, DV_A), f32),
                   jax.ShapeDtypeStruct((nb, 2, H_A, DQK_A, 1), f32),
                   jax.ShapeDtypeStruct((nb, 2, H_A, 1, 1), f32)],
        compiler_params=pltpu.CompilerParams(vmem_limit_bytes=VMEM_LIMIT),
        name="mlstm",
    )(*args)


def _short_conv_silu(x, w_ref, seq):
    sub = 8
    acc = x * w_ref[CONV_K // 2:CONV_K // 2 + 1, :]
    first, last = x[:sub], x[seq - sub:]
    row = lax.broadcasted_iota(jnp.int32, (sub, 1), 0)
    fix_first = jnp.zeros((sub, LANES), f32)
    fix_last = jnp.zeros((sub, LANES), f32)
    for tap in range(CONV_K):
        delta = tap - CONV_K // 2
        if delta == 0:
            continue
        w_tap = w_ref[tap:tap + 1, :]
        acc = acc + pltpu.roll(x, shift=(-delta) % seq, axis=0) * w_tap
        if delta < 0:
            fix_first = fix_first + jnp.where(row + delta < 0, pltpu.roll(last, shift=-delta, axis=0), 0.0) * w_tap
        else:
            fix_last = fix_last + jnp.where(row + delta >= sub, pltpu.roll(first, shift=sub - delta, axis=0), 0.0) * w_tap
    acc = jnp.concatenate([acc[:sub] - fix_first, acc[sub:seq - sub], acc[seq - sub:] - fix_last], axis=0)
    return _silu(acc)


def _l2_unit(y):
    return y * lax.rsqrt(jnp.sum(y * y, axis=-1, keepdims=True) + EPS)


def _split_bf16(x):
    hi = x.astype(bf16)
    return hi, (x - hi.astype(f32)).astype(bf16)


def _matmul_3pass(m, x):
    mh, ml = _split_bf16(m)
    xh, xl = _split_bf16(x)
    return jnp.dot(jnp.concatenate([mh, ml, mh], axis=1), jnp.concatenate([xh, xh, xl], axis=0),
                   preferred_element_type=f32)


def _unit_triangular_solves(ns, xs):
    levels = CHUNK.bit_length() - 1
    es = list(ns)
    ms = [_matmul_3pass(n, n) for n in ns]
    for lvl in range(1, levels):
        for i in range(len(ns)):
            if lvl < levels - 1:
                prod = _matmul_3pass(ms[i], jnp.concatenate([es[i], ms[i]], axis=1))
                es[i] = es[i] + ms[i] + prod[:, :CHUNK]
                ms[i] = prod[:, CHUNK:]
            else:
                es[i] = es[i] + ms[i] + _matmul_3pass(ms[i], es[i])
    return [x + _matmul_3pass(e, x) for e, x in zip(es, xs)]


def _delta_kernel(*refs, seq, zero_init):
    if zero_init:
        (q0_ref, q1_ref, k0_ref, k1_ref, v0_ref, v1_ref, wq0_ref, wq1_ref, wk0_ref, wk1_ref, wv0_ref, wv1_ref,
         g_ref, par_ref, o_ref, sf_ref, q_scr, k_scr, v_scr) = refs
    else:
        (q0_ref, q1_ref, k0_ref, k1_ref, v0_ref, v1_ref, wq0_ref, wq1_ref, wk0_ref, wk1_ref, wv0_ref, wv1_ref,
         g_ref, par_ref, s0_ref, o_ref, sf_ref, q_scr, k_scr, v_scr) = refs
    nc = seq // CHUNK
    pair = pl.program_id(1)
    lane = lax.broadcasted_iota(jnp.int32, (1, LANES), 1)
    tril, triu, stril, striu = _chunk_masks()
    masks, smasks = (tril, triu), (stril, striu)
    masks16 = (_thrice(tril.astype(bf16)), _thrice(triu.astype(bf16)))
    sel = [_thrice(_head_lane_select(LANE_A + d * H_B, pair).astype(bf16)) for d in range(2)]
    neg_a_row = -jnp.exp(par_ref[0:1, :])
    dt_row = par_ref[1:2, :]
    streams = [(hh, d) for hh in range(2) for d in range(2)]

    for hh, (q_ref, k_ref, v_ref, wq_ref, wk_ref, wv_ref) in enumerate(
            ((q0_ref, k0_ref, v0_ref, wq0_ref, wk0_ref, wv0_ref), (q1_ref, k1_ref, v1_ref, wq1_ref, wk1_ref, wv1_ref))):
        q_scr[hh] = _l2_unit(_short_conv_silu(q_ref[...], wq_ref, seq)) * DK_B ** -0.5
        k_scr[hh] = _l2_unit(_short_conv_silu(k_ref[...], wk_ref, seq))
        v_scr[hh] = _short_conv_silu(v_ref[...], wv_ref, seq)

    o_ref[...] = jnp.zeros_like(o_ref)
    init = tuple(jnp.zeros((DK_B, DV_B), f32) if zero_init else s0_ref[d, hh] for hh, d in streams)

    def body(trip, carry):
        subs = range(CHUNKS_PER_TRIP)
        shared = {}
        for u in subs:
            for d in range(2):
                r0 = _chunk_start(trip, u, d, nc)
                gates = g_ref[pl.ds(r0, CHUNK), :]
                beta_all = jax.nn.sigmoid(gates)
                glog = neg_a_row * _softplus(gates + dt_row)
                cum = _mask_matmul_f32(masks16[d], glog)
                rows = _select_rows_f32(sel[d], cum)
                shared[u, d] = (r0, beta_all, cum, rows)
        chains = [(u, hh, d) for u in subs for hh, d in streams]
        st = {ch: {} for ch in chains}
        for ch in chains:
            u, hh, d = ch
            r0, beta_all, cum, rows = shared[u, d]
            s = st[ch]
            head = 2 * pair + hh
            beta = _lane_col(beta_all, lane, LANE_BETA + d * H_B + head)
            gcol = _lane_col(cum, lane, LANE_A + d * H_B + head)
            glast = gcol[CHUNK - 1:CHUNK, :] if d == 0 else gcol[0:1, :]
            q = q_scr[hh, pl.ds(r0, CHUNK), :]
            k = k_scr[hh, pl.ds(r0, CHUNK), :]
            v = v_scr[hh, pl.ds(r0, CHUNK), :]
            k16 = k.astype(bf16)
            kbeta = k * beta
            eg = jnp.exp(gcol)
            decay = jnp.exp(jnp.where(masks[d], gcol - rows[hh:hh + 1, :], -jnp.inf))
            kk = lax.dot_general(kbeta.astype(bf16), k16, NT_DIMS, preferred_element_type=f32)
            qk = lax.dot_general(q.astype(bf16), k16, NT_DIMS, preferred_element_type=f32)
            s["n"] = -jnp.where(smasks[d], kk * decay, 0.0)
            s["x"] = jnp.concatenate([v * beta, kbeta * eg], axis=1)
            s["qk"] = (qk * decay).astype(bf16)
            s["qg"] = (q * eg).astype(bf16)
            s["kd_t"] = (k * jnp.exp(glast - gcol)).T.astype(bf16)
            s["gl"] = jnp.exp(glast)
        solved = _unit_triangular_solves([st[ch]["n"] for ch in chains], [st[ch]["x"] for ch in chains])
        for ch, uw in zip(chains, solved):
            s = st[ch]
            s["u"] = uw[:, :DV_B]
            s["w_qg"] = jnp.concatenate([uw[:, DV_B:].astype(bf16), s["qg"]], axis=0)
        state = list(carry)
        for u in subs:
            ws = [jnp.dot(st[u, hh, d]["w_qg"], state[i].astype(bf16), preferred_element_type=f32)
                  for i, (hh, d) in enumerate(streams)]
            v_new = [(st[u, hh, d]["u"] - ws[i][:CHUNK]).astype(bf16) for i, (hh, d) in enumerate(streams)]
            for i, (hh, d) in enumerate(streams):
                s = st[u, hh, d]
                o = ws[i][CHUNK:] + jnp.dot(s["qk"], v_new[i], preferred_element_type=f32)
                r0 = shared[u, d][0]
                o_ref[pl.ds(r0, CHUNK), hh * DV_B:(hh + 1) * DV_B] += o
                state[i] = state[i] * s["gl"] + jnp.dot(s["kd_t"], v_new[i], preferred_element_type=f32)
        return tuple(state)

    final = lax.fori_loop(0, nc // CHUNKS_PER_TRIP, body, init)
    for i, (hh, d) in enumerate(streams):
        sf_ref[d, hh] = final[i]


def _delta_call(proj, conv_w8, gates, par, nb, seq, off, state):
    zero_init = state is None
    col0 = (2 * H_A * DQK_A + 2 * H_A * DV_A) // LANES
    blk = lambda col: pl.BlockSpec((seq, LANES), col)
    wblk = lambda col: pl.BlockSpec((8, LANES), col)
    in_specs = [blk(lambda b, p: (off + b, col0 + 2 * p)), blk(lambda b, p: (off + b, col0 + 2 * p + 1)),
                blk(lambda b, p: (off + b, col0 + H_B + 2 * p)), blk(lambda b, p: (off + b, col0 + H_B + 2 * p + 1)),
                blk(lambda b, p: (off + b, col0 + 2 * H_B + 2 * p)), blk(lambda b, p: (off + b, col0 + 2 * H_B + 2 * p + 1)),
                wblk(lambda b, p: (0, 2 * p)), wblk(lambda b, p: (0, 2 * p + 1)),
                wblk(lambda b, p: (0, H_B + 2 * p)), wblk(lambda b, p: (0, H_B + 2 * p + 1)),
                wblk(lambda b, p: (0, 2 * H_B + 2 * p)), wblk(lambda b, p: (0, 2 * H_B + 2 * p + 1)),
                blk(lambda b, p: (off + b, 0)),
                pl.BlockSpec((8, LANES), lambda b, p: (0, 0))]
    st_spec = pl.BlockSpec((None, 2, 2, DK_B, DV_B), lambda b, p: (b, 0, p, 0, 0))
    args = [proj] * 6 + [conv_w8] * 6 + [gates, par]
    if not zero_init:
        in_specs.append(st_spec)
        args.append(state)
    return pl.pallas_call(
        functools.partial(_delta_kernel, seq=seq, zero_init=zero_init),
        grid=(nb, H_B // 2),
        in_specs=in_specs,
        out_specs=[pl.BlockSpec((seq, 2 * DV_B), lambda b, p: (b, p)), st_spec],
        out_shape=[jax.ShapeDtypeStruct((nb * seq, H_B * DV_B), f32),
                   jax.ShapeDtypeStruct((nb, 2, H_B, DK_B, DV_B), f32)],
        scratch_shapes=[pltpu.VMEM((2, seq, LANES), f32)] * 3,
        compiler_params=pltpu.CompilerParams(vmem_limit_bytes=VMEM_LIMIT),
        name="delta",
    )(*args)


def _head_rms(x):
    return x * lax.rsqrt(jnp.mean(x * x, axis=-1, keepdims=True) + EPS)


def _ab_out_kernel(x_ref, hp_ref, hs_ref, op_ref, os_ref, om_ref, zd_ref, gm_ref, gd_ref, w_ref, mod_ref, o_ref,
                   *, gate_row, tm):
    hsum = _prompt_or_sample(hp_ref, hs_ref, tm)
    osum = _prompt_or_sample(op_ref, os_ref, tm)
    parts = []
    for h in range(H_A):
        sl = slice(h * DV_A, (h + 1) * DV_A)
        parts.append(_head_rms(hsum[:, sl]) * gm_ref[:, sl] * jax.nn.sigmoid(om_ref[:, sl]))
    for h in range(H_B):
        sl = slice(h * DV_B, (h + 1) * DV_B)
        parts.append(_head_rms(osum[:, sl]) * gd_ref[:, sl] * _silu(zd_ref[:, sl]))
    cat = jnp.concatenate(parts, axis=1).astype(bf16)
    y = jnp.dot(cat, w_ref[...].astype(bf16), preferred_element_type=f32)
    o_ref[...] = x_ref[...] + mod_ref[gate_row:gate_row + 1, :] * y


def _ab_out_call(x, hp, hs, op, os_, proj, gm_row, gd_row, w_out, mod_l, gate_row, tm=512):
    wide = H_A * DV_A
    om_blk = (2 * H_A * DQK_A) // wide + 1
    zd_blk = AB_MAIN // wide - 1
    return pl.pallas_call(
        functools.partial(_ab_out_kernel, gate_row=gate_row, tm=tm),
        grid=(N_TOK // tm,),
        in_specs=[pl.BlockSpec((tm, D_MODEL), lambda i: (i, 0))] + _split_specs(tm, wide) + _split_specs(tm, wide) + [
            pl.BlockSpec((tm, wide), lambda i: (i, om_blk)),
            pl.BlockSpec((tm, wide), lambda i: (i, zd_blk)),
            pl.BlockSpec((1, wide), lambda i: (0, 0)),
            pl.BlockSpec((1, wide), lambda i: (0, 0)),
            pl.BlockSpec((2 * wide, D_MODEL), lambda i: (0, 0)),
            pl.BlockSpec((None, N_ADA, D_MODEL), lambda i: (_mod_row_index(i, tm), 0, 0))],
        out_specs=pl.BlockSpec((tm, D_MODEL), lambda i: (i, 0)),
        out_shape=jax.ShapeDtypeStruct((N_TOK, D_MODEL), f32),
        compiler_params=pltpu.CompilerParams(vmem_limit_bytes=VMEM_LIMIT),
        name="ab_out",
    )(x, hp, hs, op, os_, proj, proj, gm_row, gd_row, w_out, mod_l)


def _na_out_kernel(x_ref, ap_ref, as_ref, w_ref, mod_ref, o_ref, *, gate_row, tm):
    a = _prompt_or_sample(ap_ref, as_ref, tm).astype(bf16)
    y = jnp.dot(a, w_ref[...].astype(bf16), preferred_element_type=f32)
    o_ref[...] = x_ref[...] + mod_ref[gate_row:gate_row + 1, :] * y


def _na_out_call(x, attn_p, attn_s, w_out, mod_l, gate_row, tm=512):
    return pl.pallas_call(
        functools.partial(_na_out_kernel, gate_row=gate_row, tm=tm),
        grid=(N_TOK // tm,),
        in_specs=[pl.BlockSpec((tm, D_MODEL), lambda i: (i, 0))] + _split_specs(tm, NA_W) + [
            pl.BlockSpec((NA_W, D_MODEL), lambda i: (0, 0)),
            pl.BlockSpec((None, N_ADA, D_MODEL), lambda i: (_mod_row_index(i, tm), 0, 0))],
        out_specs=pl.BlockSpec((tm, D_MODEL), lambda i: (i, 0)),
        out_shape=jax.ShapeDtypeStruct((N_TOK, D_MODEL), f32),
        compiler_params=pltpu.CompilerParams(vmem_limit_bytes=VMEM_LIMIT),
        name="na_out",
    )(x, attn_p, attn_s, w_out, mod_l)


CTX_HEADS = 8


def _ctx_attn_kernel(q_ref, k_ref, v_ref, o_ref, nk_ref, nv_ref):
    low = lax.broadcasted_iota(jnp.int32, (1, LANES), 1) < DH_C
    for hh in range(CTX_HEADS):
        sl = slice(hh * DH_C, (hh + 1) * DH_C)
        nk_ref[hh] = k_ref[:, sl]
        nv_ref[hh] = v_ref[:, sl]
    for pair in range(CTX_HEADS // 2):
        sl = slice(pair * LANES, (pair + 1) * LANES)
        q = q_ref[:, sl]
        k = k_ref[:, sl].astype(bf16)
        v = v_ref[:, sl].astype(bf16)
        outs = []
        for hh in range(2):
            qh = jnp.where(low if hh == 0 else jnp.logical_not(low), q, 0.0).astype(bf16)
            s = lax.dot_general(qh, k, NT_DIMS, preferred_element_type=f32) * DH_C ** -0.5
            p = jnp.exp(s - jnp.max(s, axis=1, keepdims=True))
            o = jnp.dot(p.astype(bf16), v, preferred_element_type=f32)
            outs.append(o / jnp.sum(p, axis=1, keepdims=True))
        o_ref[:, sl] = jnp.where(low, outs[0], outs[1])


def _ctx_attn_call(proj):
    w = CTX_HEADS * DH_C
    nblk = NA_W // w
    return pl.pallas_call(
        _ctx_attn_kernel,
        grid=(BATCH, nblk),
        in_specs=[pl.BlockSpec((SEQ, w), lambda b, j: (b, j)),
                  pl.BlockSpec((SEQ, w), lambda b, j: (b, nblk + j)),
                  pl.BlockSpec((SEQ, w), lambda b, j: (b, 2 * nblk + j))],
        out_specs=[pl.BlockSpec((SEQ, w), lambda b, j: (b, j)),
                   pl.BlockSpec((None, CTX_HEADS, SEQ, DH_C), lambda b, j: (b, j, 0, 0)),
                   pl.BlockSpec((None, CTX_HEADS, SEQ, DH_C), lambda b, j: (b, j, 0, 0))],
        out_shape=[jax.ShapeDtypeStruct((N_PROMPT, NA_W), f32),
                   jax.ShapeDtypeStruct((BATCH, H_C, SEQ, DH_C), f32),
                   jax.ShapeDtypeStruct((BATCH, H_C, SEQ, DH_C), f32)],
        compiler_params=pltpu.CompilerParams(vmem_limit_bytes=VMEM_LIMIT),
        name="ctx_attn",
    )(proj, proj, proj)


QROWS = 4
QBLK = QROWS * GRID_W
KROWS_MID = QROWS + WIN_R - 1
N_RIDX = 2 * WIN_R - 1
N_CIDX = 2 * WIN_C - 1
N_QBLK = GRID_ROWS // QROWS


def _nattn_bias_tables(rb_ref, tb_scr, bmid_scr, btop_scr, bbot_scr):
    qc = lax.broadcasted_iota(jnp.int32, (GRID_W, GRID_W), 0)
    kc = lax.broadcasted_iota(jnp.int32, (GRID_W, GRID_W), 1)
    cs = jnp.clip(qc - WIN_C // 2, 0, GRID_W - WIN_C)
    valid = (kc >= cs) & (kc < cs + WIN_C)
    neg = jnp.full((GRID_W, GRID_W), -jnp.inf, f32)
    for hh in range(2):
        for ri in range(N_RIDX):
            row = jnp.broadcast_to(rb_ref[hh, ri:ri + 1, :], (GRID_W, LANES))
            tile = pltpu.roll(row, LANES - (WIN_C - 1), 1, stride=1, stride_axis=0)[:, :GRID_W]
            tb_scr[ri] = jnp.where(valid, tile, -jnp.inf)
        for i in range(QROWS):
            rs = slice(i * GRID_W, (i + 1) * GRID_W)
            for jj in range(KROWS_MID):
                inside = 0 <= jj - i < WIN_R
                bmid_scr[hh, rs, jj * GRID_W:(jj + 1) * GRID_W] = tb_scr[jj - i + WIN_R // 2 - 1] if inside else neg
            for jj in range(WIN_R):
                btop_scr[hh, rs, jj * GRID_W:(jj + 1) * GRID_W] = tb_scr[jj - i + WIN_R - 1]
                bbot_scr[hh, rs, jj * GRID_W:(jj + 1) * GRID_W] = tb_scr[jj - i + WIN_R // 2 - 1]


def _nattn_kernel(rb_ref, q_ref, k_ref, v_ref, ck_ref, cv_ref, o_ref, tb_scr, bmid_scr, btop_scr, bbot_scr):
    scale = DH_C ** -0.5

    @pl.when(pl.program_id(1) == 0)
    def _():
        _nattn_bias_tables(rb_ref, tb_scr, bmid_scr, btop_scr, bbot_scr)

    low = lax.broadcasted_iota(jnp.int32, (1, LANES), 1) < DH_C
    kctx = jnp.concatenate([ck_ref[0], ck_ref[1]], axis=1).astype(bf16)
    vctx = jnp.concatenate([cv_ref[0], cv_ref[1]], axis=1).astype(bf16)

    def block(q_start, k_start, k_rows, bias_scr):
        nk = k_rows * GRID_W
        q = q_ref[pl.ds(q_start, QBLK), :]
        ku = k_ref[pl.ds(k_start, nk), :].astype(bf16)
        vu = v_ref[pl.ds(k_start, nk), :].astype(bf16)
        outs = []
        for hh in range(2):
            qh = jnp.where(low if hh == 0 else jnp.logical_not(low), q, 0.0).astype(bf16)
            s_loc = lax.dot_general(qh, ku, NT_DIMS, preferred_element_type=f32) * scale + bias_scr[hh]
            s_ctx = lax.dot_general(qh, kctx, NT_DIMS, preferred_element_type=f32) * scale
            m = jnp.maximum(jnp.max(s_loc, axis=1, keepdims=True), jnp.max(s_ctx, axis=1, keepdims=True))
            p_loc = jnp.exp(s_loc - m)
            p_ctx = jnp.exp(s_ctx - m)
            denom = jnp.sum(p_loc, axis=1, keepdims=True) + jnp.sum(p_ctx, axis=1, keepdims=True)
            o = (jnp.dot(p_loc.astype(bf16), vu, preferred_element_type=f32)
                 + jnp.dot(p_ctx.astype(bf16), vctx, preferred_element_type=f32))
            outs.append(o / denom)
        o_ref[pl.ds(q_start, QBLK), :] = jnp.where(low, outs[0], outs[1])

    block(0, 0, WIN_R, btop_scr)

    def mid_body(blk, _):
        q_start = pl.multiple_of(blk * QBLK, QBLK)
        k_start = pl.multiple_of(blk * QBLK - (WIN_R // 2) * GRID_W, QBLK)
        block(q_start, k_start, KROWS_MID, bmid_scr)
        return 0

    lax.fori_loop(1, N_QBLK - 1, mid_body, 0)
    block((N_QBLK - 1) * QBLK, (GRID_ROWS - WIN_R) * GRID_W, WIN_R, bbot_scr)


def _nattn_call(proj, cache_k, cache_v, rel_bias, off):
    npair = H_C // 2
    return pl.pallas_call(
        _nattn_kernel,
        grid=(npair, DEC_BATCH),
        in_specs=[pl.BlockSpec((2, 2 * WIN_R, LANES), lambda p, b: (p, 0, 0)),
                  pl.BlockSpec((DEC_SEQ, LANES), lambda p, b: (off + b, p)),
                  pl.BlockSpec((DEC_SEQ, LANES), lambda p, b: (off + b, npair + p)),
                  pl.BlockSpec((DEC_SEQ, LANES), lambda p, b: (off + b, 2 * npair + p)),
                  pl.BlockSpec((None, 2, PAST_LEN, DH_C), lambda p, b: (b, p, 0, 0)),
                  pl.BlockSpec((None, 2, PAST_LEN, DH_C), lambda p, b: (b, p, 0, 0))],
        out_specs=pl.BlockSpec((DEC_SEQ, LANES), lambda p, b: (b, p)),
        out_shape=jax.ShapeDtypeStruct((N_SAMPLE, NA_W), f32),
        scratch_shapes=[pltpu.VMEM((N_RIDX, GRID_W, GRID_W), f32),
                        pltpu.VMEM((2, QBLK, KROWS_MID * GRID_W), f32),
                        pltpu.VMEM((2, QBLK, WIN_R * GRID_W), f32),
                        pltpu.VMEM((2, QBLK, WIN_R * GRID_W), f32)],
        compiler_params=pltpu.CompilerParams(dimension_semantics=("arbitrary", "arbitrary"),
                                             vmem_limit_bytes=VMEM_LIMIT),
        name="nattn",
    )(jnp.pad(rel_bias, ((0, 0), (0, 2 * WIN_R - N_RIDX), (0, LANES - N_CIDX))), proj, proj, proj, cache_k, cache_v)


def _lane_row(pieces):
    row = jnp.zeros((LANES,), f32)
    for off, vals in pieces:
        row = row.at[off:off + vals.shape[0]].set(vals.astype(f32))
    return row


def _param_rows(rows):
    out = jnp.zeros((8, LANES), f32)
    for r, row in enumerate(rows):
        out = out.at[r].set(row)
    return out


def kernel(x_prompt, x_sample, c, state_mlstm_C, state_mlstm_n, state_mlstm_m, state_delta_S, cache_na_k, cache_na_v, c_ctx, ada_w, ada_b, norm_g, ffn_wg, ffn_wu, ffn_wd, ab_w_in, ab_w_out, mlstm_b_i, mlstm_b_f, mlstm_norm_g, delta_conv_w, delta_a_log, delta_dt_bias, delta_norm_g, na_w_in, na_w_out, na_rel_bias, final_norm_g):
    xs = (x_prompt.reshape(N_PROMPT, D_MODEL), x_sample.reshape(N_SAMPLE, D_MODEL))
    mods = _ada_call(jnp.concatenate([c_ctx[None, :], c], axis=0), ada_w, ada_b)
    mods = mods.reshape(DEPTH, 3, N_ADA, D_MODEL)
    gf_row = final_norm_g.reshape(1, D_MODEL)
    s_off = N_PROMPT // DEC_SEQ
    new_c, new_n, new_m, new_s, new_k, new_v = [], [], [], [], [], []

    for l in range(DEPTH):
        mod_l = mods[l]
        a = l // 2
        x = _ffn_call(xs if l == 0 else (x,), mod_l, norm_g[l, 0].reshape(1, D_MODEL), ffn_wg, ffn_wu, ffn_wd, l, 0,
                      gf_row, rows=(0, 1, 2), final=False, split_out=False, tm=FFN_TM_SPLIT if l == 0 else FFN_TM)[0]
        g_mix = norm_g[l, 1].reshape(1, D_MODEL)
        if l % 2 == 0:
            w_in = ab_w_in[a]
            i0 = 2 * H_A * DQK_A + 2 * H_A * DV_A
            d0 = i0 + 4 * H_A
            b0 = d0 + 2 * H_B * DK_B + 2 * H_B * DV_B
            w_main = jnp.concatenate([w_in[:, :i0], w_in[:, d0:b0]], axis=1)
            w_gate = jnp.concatenate([w_in[:, i0:d0], w_in[:, b0:],
                                      jnp.zeros((D_MODEL, LANES - 4 * H_A - 4 * H_B), f32)], axis=1)
            proj, gates = _proj_call(x, mod_l, g_mix, w_main, w_gate, rows=(3, 4))

            par_m = _param_rows([_lane_row([(LANE_I, mlstm_b_i[a].reshape(-1))]),
                                 _lane_row([(LANE_F, mlstm_b_f[a].reshape(-1))])])
            par_d = _param_rows([_lane_row([(LANE_A, delta_a_log[a].reshape(-1))]),
                                 _lane_row([(LANE_A, delta_dt_bias[a].reshape(-1))])])
            conv_w8 = jnp.concatenate([delta_conv_w[a], jnp.zeros((8 - CONV_K, 3 * H_B * DK_B), f32)], axis=0)

            hp, cn, nn_, mn = _mlstm_call(proj, gates, par_m, BATCH, SEQ, 0, None)
            st = (state_mlstm_C[:, a], state_mlstm_n[:, a][..., None], state_mlstm_m[:, a][..., None, None])
            hs, _, _, _ = _mlstm_call(proj, gates, par_m, DEC_BATCH, DEC_SEQ, s_off, st)
            op, sn = _delta_call(proj, conv_w8, gates, par_d, BATCH, SEQ, 0, None)
            os_, _ = _delta_call(proj, conv_w8, gates, par_d, DEC_BATCH, DEC_SEQ, s_off, state_delta_S[:, a])
            new_c.append(cn)
            new_n.append(nn_[..., 0])
            new_m.append(mn[..., 0, 0])
            new_s.append(sn)
            x = _ab_out_call(x, hp, hs, op, os_, proj, mlstm_norm_g[a].reshape(1, -1),
                             jnp.tile(delta_norm_g[a], H_B).reshape(1, -1), ab_w_out[a], mod_l, gate_row=5)
        else:
            proj = _proj_call(x, mod_l, g_mix, na_w_in[a], None, rows=(3, 4))[0]
            attn_p, kp, vp = _ctx_attn_call(proj)
            attn_s = _nattn_call(proj, cache_na_k[:, a], cache_na_v[:, a], na_rel_bias[a], s_off)
            new_k.append(kp)
            new_v.append(vp)
            x = _na_out_call(x, attn_p, attn_s, na_w_out[a], mod_l, gate_row=5)
        last = l == DEPTH - 1
        outs = _ffn_call((x,), mod_l, norm_g[l, 2].reshape(1, D_MODEL), ffn_wg, ffn_wu, ffn_wd, l, 1,
                         gf_row, rows=(6, 7, 8), final=last, split_out=last, tm=FFN_TM_SPLIT if last else FFN_TM)
        x = outs[0]

    y_prompt = outs[0].reshape(BATCH, SEQ, D_MODEL)
    y_sample = outs[1].reshape(DEC_BATCH, DEC_SEQ, D_MODEL)
    return (y_prompt, y_sample, jnp.stack(new_c, axis=1), jnp.stack(new_n, axis=1), jnp.stack(new_m, axis=1),
            jnp.stack(new_s, axis=1), jnp.stack(new_k, axis=1), jnp.stack(new_v, axis=1))
```

```python
import functools

import jax
import jax.numpy as jnp
from jax import lax
from jax.experimental import pallas as pl
from jax.experimental.pallas import tpu as pltpu

f32 = jnp.float32
bf16 = jnp.bfloat16

D_MODEL = 1024
BATCH = 16
SEQ = 256
DEPTH = 2
DEC_BATCH = 2
DEC_SEQ = 2048
PAST_LEN = 256
GRID_W = 64
GRID_ROWS = DEC_SEQ // GRID_W
D_FF = 2816
N_ADA = 9
EPS = 1e-6
CHUNK = 64
H_A, DQK_A, DV_A = 4, 64, 128
H_B, DK_B, DV_B = 4, 128, 128
CONV_K = 5
H_C, DH_C = 16, 64
WIN_R, WIN_C = 8, 16
NA_W = H_C * DH_C

N_PROMPT = BATCH * SEQ
N_SAMPLE = DEC_BATCH * DEC_SEQ
N_TOK = N_PROMPT + N_SAMPLE
AB_MAIN = 2 * H_A * DQK_A + 2 * H_A * DV_A + 2 * H_B * DK_B + 2 * H_B * DV_B
LANES = 128
VMEM_LIMIT = 56 * 1024 * 1024

LANE_I, LANE_F, LANE_BETA, LANE_A = 0, 8, 16, 24

NT_DIMS = (((1,), (1,)), ((), ()))
TN_DIMS = (((0,), (0,)), ((), ()))

CHUNKS_PER_TRIP = 4
FFN_TM = 2048
FFN_TM_SPLIT = 1024
FFN_ROW_BLOCK = 512
PROJ_ROW_BLOCK = 256


def _softplus(x):
    return jnp.maximum(x, 0.0) + jnp.log1p(jnp.exp(-jnp.abs(x)))


def _log_sigmoid(x):
    return -_softplus(-x)


def _silu(x):
    return x * jax.nn.sigmoid(x)


def _mod_row_index(i, tm):
    n_p = N_PROMPT // tm
    per_b = DEC_SEQ // tm
    return jnp.where(i < n_p, 0, 1 + (i - n_p) // per_b)


def _modulated(x, g_row, shift_row, scale_row):
    y = x * lax.rsqrt(jnp.mean(x * x, axis=-1, keepdims=True) + EPS) * g_row
    return y * (1.0 + scale_row) + shift_row


def _ada_kernel(cb_ref, w_ref, b_ref, o_ref):
    tn = w_ref.shape[-1]
    for r in range(3):
        cb = cb_ref[r]
        s = cb * jax.nn.sigmoid(cb)
        parts = [jnp.sum(w_ref[:, j * LANES:(j + 1) * LANES] * s, axis=0, keepdims=True)
                 for j in range(tn // LANES)]
        o_ref[r:r + 1, :] = jnp.concatenate(parts, axis=1) + b_ref[...]


def _ada_call(cond3, ada_w, ada_b):
    tn = D_MODEL
    cb = jnp.broadcast_to(cond3[:, :, None], (3, D_MODEL, LANES))
    return pl.pallas_call(
        _ada_kernel,
        grid=(DEPTH, N_ADA * D_MODEL // tn),
        in_specs=[pl.BlockSpec((3, D_MODEL, LANES), lambda l, j: (0, 0, 0)),
                  pl.BlockSpec((None, D_MODEL, tn), lambda l, j: (l, 0, j)),
                  pl.BlockSpec((None, 1, tn), lambda l, j: (l, 0, j))],
        out_specs=pl.BlockSpec((None, 3, tn), lambda l, j: (l, 0, j)),
        out_shape=jax.ShapeDtypeStruct((DEPTH, 3, N_ADA * D_MODEL), f32),
        compiler_params=pltpu.CompilerParams(vmem_limit_bytes=VMEM_LIMIT),
        name="ada_mod",
    )(cb, ada_w, ada_b.reshape(DEPTH, 1, N_ADA * D_MODEL))


def _prompt_or_sample(p_ref, s_ref, tm):
    return jnp.where(pl.program_id(0) < N_PROMPT // tm, p_ref[...], s_ref[...])


def _split_specs(tm, width, n_grid_axes=1):
    n_p = N_PROMPT // tm
    if n_grid_axes == 1:
        return [pl.BlockSpec((tm, width), lambda i: (jnp.minimum(i, n_p - 1), 0)),
                pl.BlockSpec((tm, width), lambda i: (jnp.maximum(i - n_p, 0), 0))]
    return [pl.BlockSpec((tm, width), lambda i, j: (jnp.minimum(i, n_p - 1), 0)),
            pl.BlockSpec((tm, width), lambda i, j: (jnp.maximum(i - n_p, 0), 0))]


def _ffn_kernel(*refs, rows, final, split_in, split_out, tm):
    it = iter(refs)
    x_refs = [next(it) for _ in range(2 if split_in else 1)]
    mod_ref, g_ref, wg_ref, wu_ref, wd_ref, gf_ref = [next(it) for _ in range(6)]
    o_refs = [next(it) for _ in range(2 if split_out else 1)]
    h_scr = next(it)
    acc_ref = next(it) if split_out else o_refs[0]
    j = pl.program_id(1)
    last_j = pl.num_programs(1) - 1
    is_prompt = pl.program_id(0) < N_PROMPT // tm

    def x_rows(rs):
        if split_in:
            return jnp.where(is_prompt, x_refs[0][rs, :], x_refs[1][rs, :])
        return x_refs[0][rs, :]

    def step(first, last, out_ref):
        wg = wg_ref[...].astype(bf16)
        wu = wu_ref[...].astype(bf16)
        wd = wd_ref[...].astype(bf16)
        for r in range(tm // FFN_ROW_BLOCK):
            rs = slice(r * FFN_ROW_BLOCK, (r + 1) * FFN_ROW_BLOCK)
            if first:
                h = _modulated(x_rows(rs), g_ref[...], mod_ref[rows[0]:rows[0] + 1, :],
                               mod_ref[rows[1]:rows[1] + 1, :]).astype(bf16)
                h_scr[rs, :] = h
            else:
                h = h_scr[rs, :]
            g = jnp.dot(h, wg, preferred_element_type=f32)
            u = jnp.dot(h, wu, preferred_element_type=f32)
            part = jnp.dot((_silu(g) * u).astype(bf16), wd, preferred_element_type=f32)
            acc = part if first else acc_ref[rs, :] + part
            if last:
                xn = x_rows(rs) + (0.5 * mod_ref[rows[2]:rows[2] + 1, :]) * acc
                if final:
                    xn = xn * lax.rsqrt(jnp.mean(xn * xn, axis=-1, keepdims=True) + EPS) * gf_ref[...]
                out_ref[rs, :] = xn
            else:
                acc_ref[rs, :] = acc

    pl.when(j == 0)(lambda: step(True, False, None))
    pl.when((j > 0) & (j < last_j))(lambda: step(False, False, None))
    if split_out:
        pl.when((j == last_j) & is_prompt)(lambda: step(False, True, o_refs[0]))
        pl.when((j == last_j) & jnp.logical_not(is_prompt))(lambda: step(False, True, o_refs[1]))
    else:
        pl.when(j == last_j)(lambda: step(False, True, o_refs[0]))


def _ffn_call(xs, mod_l, g_row, ffn_wg, ffn_wu, ffn_wd, layer, half, gf_row, rows, final, split_out, tm, tf=256):
    split_in = len(xs) == 2
    x_specs = (_split_specs(tm, D_MODEL, 2) if split_in else [pl.BlockSpec((tm, D_MODEL), lambda i, j: (i, 0))])
    if split_out:
        out_specs = _split_specs(tm, D_MODEL, 2)
        out_shape = [jax.ShapeDtypeStruct((N_PROMPT, D_MODEL), f32), jax.ShapeDtypeStruct((N_SAMPLE, D_MODEL), f32)]
        scratch = [pltpu.VMEM((tm, D_MODEL), bf16), pltpu.VMEM((tm, D_MODEL), f32)]
    else:
        out_specs = [pl.BlockSpec((tm, D_MODEL), lambda i, j: (i, 0))]
        out_shape = [jax.ShapeDtypeStruct((N_TOK, D_MODEL), f32)]
        scratch = [pltpu.VMEM((tm, D_MODEL), bf16)]
    return pl.pallas_call(
        functools.partial(_ffn_kernel, rows=rows, final=final, split_in=split_in, split_out=split_out, tm=tm),
        grid=(N_TOK // tm, D_FF // tf),
        in_specs=x_specs + [
            pl.BlockSpec((None, N_ADA, D_MODEL), lambda i, j: (_mod_row_index(i, tm), 0, 0)),
            pl.BlockSpec((1, D_MODEL), lambda i, j: (0, 0)),
            pl.BlockSpec((None, None, D_MODEL, tf), lambda i, j: (layer, half, 0, j)),
            pl.BlockSpec((None, None, D_MODEL, tf), lambda i, j: (layer, half, 0, j)),
            pl.BlockSpec((None, None, tf, D_MODEL), lambda i, j: (layer, half, j, 0)),
            pl.BlockSpec((1, D_MODEL), lambda i, j: (0, 0))],
        out_specs=out_specs, out_shape=out_shape, scratch_shapes=scratch,
        compiler_params=pltpu.CompilerParams(dimension_semantics=("parallel", "arbitrary"),
                                             vmem_limit_bytes=VMEM_LIMIT),
        name="ffn",
    )(*xs, mod_l, g_row, ffn_wg, ffn_wu, ffn_wd, gf_row)


def _proj_kernel(*refs, rows, with_gates):
    if with_gates:
        x_ref, mod_ref, g_ref, w_ref, wgate_ref, o_ref, og_ref, h_scr = refs
    else:
        x_ref, mod_ref, g_ref, w_ref, o_ref, h_scr = refs
    j = pl.program_id(1)
    tm = x_ref.shape[0]

    @pl.when(j == 0)
    def _():
        w = w_ref[...].astype(bf16)
        for r in range(tm // PROJ_ROW_BLOCK):
            rs = slice(r * PROJ_ROW_BLOCK, (r + 1) * PROJ_ROW_BLOCK)
            hb = _modulated(x_ref[rs, :], g_ref[...], mod_ref[rows[0]:rows[0] + 1, :],
                            mod_ref[rows[1]:rows[1] + 1, :]).astype(bf16)
            h_scr[rs, :] = hb
            o_ref[rs, :] = jnp.dot(hb, w, preferred_element_type=f32).astype(o_ref.dtype)
            if with_gates:
                og_ref[rs, :] = jnp.dot(hb, wgate_ref[...].astype(bf16), preferred_element_type=f32)

    @pl.when(j > 0)
    def _():
        o_ref[...] = jnp.dot(h_scr[...], w_ref[...].astype(bf16), preferred_element_type=f32).astype(o_ref.dtype)


def _proj_call(x, mod_l, g_row, w, w_gate, rows, out_dtype, tm=1024):
    n = w.shape[1]
    tn = n // 2
    with_gates = w_gate is not None
    in_specs = [pl.BlockSpec((tm, D_MODEL), lambda i, j: (i, 0)),
                pl.BlockSpec((None, N_ADA, D_MODEL), lambda i, j: (_mod_row_index(i, tm), 0, 0)),
                pl.BlockSpec((1, D_MODEL), lambda i, j: (0, 0)),
                pl.BlockSpec((D_MODEL, tn), lambda i, j: (0, j))]
    out_specs = [pl.BlockSpec((tm, tn), lambda i, j: (i, j))]
    out_shape = [jax.ShapeDtypeStruct((N_TOK, n), out_dtype)]
    args = [x, mod_l, g_row, w]
    if with_gates:
        in_specs.append(pl.BlockSpec((D_MODEL, LANES), lambda i, j: (0, 0)))
        out_specs.append(pl.BlockSpec((tm, LANES), lambda i, j: (i, 0)))
        out_shape.append(jax.ShapeDtypeStruct((N_TOK, LANES), f32))
        args.append(w_gate)
    return pl.pallas_call(
        functools.partial(_proj_kernel, rows=rows, with_gates=with_gates),
        grid=(N_TOK // tm, n // tn),
        in_specs=in_specs, out_specs=out_specs, out_shape=out_shape,
        scratch_shapes=[pltpu.VMEM((tm, D_MODEL), bf16)],
        compiler_params=pltpu.CompilerParams(dimension_semantics=("parallel", "arbitrary"),
                                             vmem_limit_bytes=VMEM_LIMIT),
        name="in_proj",
    )(*args)


def _chunk_masks():
    r = lax.broadcasted_iota(jnp.int32, (CHUNK, CHUNK), 0)
    c = lax.broadcasted_iota(jnp.int32, (CHUNK, CHUNK), 1)
    return r >= c, r <= c, r > c, r < c


def _lane_col(x, lane, j):
    return jnp.sum(jnp.where(lane == j, x, 0.0), axis=1, keepdims=True)


def _head_lane_select(base, pair):
    r = lax.broadcasted_iota(jnp.int32, (8, LANES), 0)
    ln = lax.broadcasted_iota(jnp.int32, (8, LANES), 1)
    return jnp.where((ln == base + 2 * pair + r) & (r < 2), 1.0, 0.0)


def _split3_bf16(x):
    hi = x.astype(bf16)
    r = x - hi.astype(f32)
    mid = r.astype(bf16)
    return hi, mid, (r - mid.astype(f32)).astype(bf16)


def _thrice(a16):
    return jnp.concatenate([a16] * 3, axis=1)


def _mask_matmul_f32(mask16x3, x):
    return jnp.dot(mask16x3, jnp.concatenate(_split3_bf16(x), axis=0), preferred_element_type=f32)


def _select_rows_f32(sel16x3, x):
    return lax.dot_general(sel16x3, jnp.concatenate(_split3_bf16(x), axis=1), NT_DIMS, preferred_element_type=f32)


def _chunk_start(trip, u, d, nc):
    step = trip * CHUNKS_PER_TRIP + u
    c = step if d == 0 else nc - 1 - step
    return pl.multiple_of(c * CHUNK, CHUNK)


def _mlstm_kernel(*refs, seq, zero_init):
    if zero_init:
        qp_ref, kp_ref, v0_ref, v1_ref, g_ref, par_ref, h_ref, cf_ref, nf_ref, mf_ref = refs
    else:
        (qp_ref, kp_ref, v0_ref, v1_ref, g_ref, par_ref, c0_ref, n0_ref, m0_ref,
         h_ref, cf_ref, nf_ref, mf_ref) = refs
    nc = seq // CHUNK
    pair = pl.program_id(1)
    lane = lax.broadcasted_iota(jnp.int32, (1, LANES), 1)
    tril, triu, _, _ = _chunk_masks()
    masks = (tril, triu)
    masks16 = (_thrice(tril.astype(bf16)), _thrice(triu.astype(bf16)))
    sel = [_thrice(_head_lane_select(LANE_F + d * H_A, pair).astype(bf16)) for d in range(2)]
    bi_row = par_ref[0:1, :]
    bf_row = par_ref[1:2, :]
    ones_col = jnp.where(lane == 0, 1.0, 0.0) + jnp.zeros((CHUNK, LANES), f32)
    v_refs = (v0_ref, v1_ref)
    streams = [(hh, d) for hh in range(2) for d in range(2)]

    h_ref[...] = jnp.zeros_like(h_ref)

    init = []
    for hh, d in streams:
        if zero_init:
            init.append((jnp.zeros((DQK_A, 2 * LANES), f32), jnp.zeros((1, 1), f32)))
        else:
            n_aug = jnp.where(lane == 0, n0_ref[d, hh], 0.0)
            init.append((jnp.concatenate([c0_ref[d, hh], n_aug], axis=1), m0_ref[d, hh]))

    def body(trip, carry):
        subs = range(CHUNKS_PER_TRIP)
        shared = {}
        for u in subs:
            for d in range(2):
                r0 = _chunk_start(trip, u, d, nc)
                gates = g_ref[pl.ds(r0, CHUNK), :]
                gi = gates + bi_row
                gf = _log_sigmoid(gates + bf_row)
                cum = _mask_matmul_f32(masks16[d], gf)
                stack = jnp.concatenate([cum, pltpu.roll(gi, LANE_F - LANE_I, axis=1)], axis=0)
                rows = _select_rows_f32(sel[d], stack)
                total = cum[CHUNK - 1:CHUNK, :] if d == 0 else cum[0:1, :]
                shared[u, d] = (r0, gi, cum, rows, total)
        chains = [(u, hh, d) for u in subs for hh, d in streams]
        st = {ch: {} for ch in chains}
        for ch in chains:
            u, hh, d = ch
            r0 = shared[u, d][0]
            s = st[ch]
            s["q"] = (qp_ref[pl.ds(r0, CHUNK), hh * DQK_A:(hh + 1) * DQK_A] * DQK_A ** -0.5).astype(bf16)
            s["k"] = kp_ref[pl.ds(r0, CHUNK), hh * DQK_A:(hh + 1) * DQK_A].astype(bf16)
            s["v_aug"] = jnp.concatenate([v_refs[hh][pl.ds(r0, CHUNK), :].astype(f32), ones_col], axis=1)
            s["qk"] = lax.dot_general(s["q"], s["k"], NT_DIMS, preferred_element_type=f32)
        for ch in chains:
            u, hh, d = ch
            _, gi, cum, rows, total = shared[u, d]
            s = st[ch]
            head = 2 * pair + hh
            jf = LANE_F + d * H_A + head
            ji = LANE_I + d * H_A + head
            bcol = _lane_col(cum, lane, jf)
            icol = _lane_col(gi, lane, ji)
            dmat = jnp.where(masks[d], bcol - rows[hh:hh + 1, :CHUNK] + rows[hh:hh + 1, CHUNK:], -jnp.inf)
            m_loc = jnp.max(dmat, axis=1, keepdims=True)
            s["sw"] = (s["qk"] * jnp.exp(dmat - m_loc)).astype(bf16)
            blast = _lane_col(total, lane, jf)
            gs = blast - bcol + icol
            ms_loc = jnp.max(gs, axis=0, keepdims=True)
            s["wv"] = (jnp.exp(gs - ms_loc) * s["v_aug"]).astype(bf16)
            s.update(bcol=bcol, m_loc=m_loc, blast=blast, ms_loc=ms_loc)
        for ch in chains:
            s = st[ch]
            s["num"] = jnp.dot(s["sw"], s["v_aug"].astype(bf16), preferred_element_type=f32)
            s["kv"] = lax.dot_general(s["k"], s["wv"], TN_DIMS, preferred_element_type=f32)
        state = list(carry)
        for u in subs:
            qc = [jnp.dot(st[u, hh, d]["q"], state[i][0].astype(bf16), preferred_element_type=f32)
                  for i, (hh, d) in enumerate(streams)]
            for i, (hh, d) in enumerate(streams):
                s = st[u, hh, d]
                c_aug, m = state[i]
                m_inter = s["bcol"] + m
                m_t = jnp.maximum(m_inter, s["m_loc"])
                nd = jnp.exp(s["m_loc"] - m_t) * s["num"] + jnp.exp(m_inter - m_t) * qc[i]
                den = nd[:, DV_A:DV_A + 1]
                hval = nd[:, :DV_A] / jnp.maximum(jnp.abs(den), jnp.exp(-m_t))
                r0 = shared[u, d][0]
                h_ref[pl.ds(r0, CHUNK), hh * DV_A:(hh + 1) * DV_A] += hval
                m_new = jnp.maximum(s["blast"] + m, s["ms_loc"])
                c_new = jnp.exp(s["blast"] + m - m_new) * c_aug + jnp.exp(s["ms_loc"] - m_new) * s["kv"]
                state[i] = (c_new, m_new)
        return tuple(state)

    final = lax.fori_loop(0, nc // CHUNKS_PER_TRIP, body, tuple(init))
    for i, (hh, d) in enumerate(streams):
        c_aug, m = final[i]
        cf_ref[d, hh] = c_aug[:, :DV_A]
        nf_ref[d, hh] = c_aug[:, DV_A:DV_A + 1]
        mf_ref[d, hh] = m


def _mlstm_call(proj, gates, par, nb, seq, off, state):
    zero_init = state is None
    blk = lambda col: pl.BlockSpec((seq, LANES), col)
    in_specs = [blk(lambda b, p: (off + b, p)),
                blk(lambda b, p: (off + b, 2 + p)),
                blk(lambda b, p: (off + b, 4 + 2 * p)),
                blk(lambda b, p: (off + b, 5 + 2 * p)),
                blk(lambda b, p: (off + b, 0)),
                pl.BlockSpec((8, LANES), lambda b, p: (0, 0))]
    st_specs = [pl.BlockSpec((None, 2, 2, DQK_A, DV_A), lambda b, p: (b, 0, p, 0, 0)),
                pl.BlockSpec((None, 2, 2, DQK_A, 1), lambda b, p: (b, 0, p, 0, 0)),
                pl.BlockSpec((None, 2, 2, 1, 1), lambda b, p: (b, 0, p, 0, 0))]
    args = [proj, proj, proj, proj, gates, par]
    if not zero_init:
        in_specs += st_specs
        args += list(state)
    return pl.pallas_call(
        functools.partial(_mlstm_kernel, seq=seq, zero_init=zero_init),
        grid=(nb, H_A // 2),
        in_specs=in_specs,
        out_specs=[pl.BlockSpec((seq, 2 * DV_A), lambda b, p: (b, p))] + st_specs,
        out_shape=[jax.ShapeDtypeStruct((nb * seq, H_A * DV_A), f32),
                   jax.ShapeDtypeStruct((nb, 2, H_A, DQK_A, DV_A), f32),
                   jax.ShapeDtypeStruct((nb, 2, H_A, DQK_A, 1), f32),
                   jax.ShapeDtypeStruct((nb, 2, H_A, 1, 1), f32)],
        compiler_params=pltpu.CompilerParams(vmem_limit_bytes=VMEM_LIMIT),
        name="mlstm",
    )(*args)


def _short_conv_silu(x, w_ref, seq):
    sub = 8
    acc = x * w_ref[CONV_K // 2:CONV_K // 2 + 1, :]
    first, last = x[:sub], x[seq - sub:]
    row = lax.broadcasted_iota(jnp.int32, (sub, 1), 0)
    fix_first = jnp.zeros((sub, LANES), f32)
    fix_last = jnp.zeros((sub, LANES), f32)
    for tap in range(CONV_K):
        delta = tap - CONV_K // 2
        if delta == 0:
            continue
        w_tap = w_ref[tap:tap + 1, :]
        acc = acc + pltpu.roll(x, shift=(-delta) % seq, axis=0) * w_tap
        if delta < 0:
            fix_first = fix_first + jnp.where(row + delta < 0, pltpu.roll(last, shift=-delta, axis=0), 0.0) * w_tap
        else:
            fix_last = fix_last + jnp.where(row + delta >= sub, pltpu.roll(first, shift=sub - delta, axis=0), 0.0) * w_tap
    acc = jnp.concatenate([acc[:sub] - fix_first, acc[sub:seq - sub], acc[seq - sub:] - fix_last], axis=0)
    return _silu(acc)


def _l2_unit(y):
    return y * lax.rsqrt(jnp.sum(y * y, axis=-1, keepdims=True) + EPS)


def _split_bf16(x):
    hi = x.astype(bf16)
    return hi, (x - hi.astype(f32)).astype(bf16)


def _matmul_3pass(m, x):
    mh, ml = _split_bf16(m)
    xh, xl = _split_bf16(x)
    return jnp.dot(jnp.concatenate([mh, ml, mh], axis=1), jnp.concatenate([xh, xh, xl], axis=0),
                   preferred_element_type=f32)


def _unit_triangular_solves(ns, xs):
    levels = CHUNK.bit_length() - 1
    r = lax.broadcasted_iota(jnp.int32, (CHUNK, CHUNK), 0)
    c = lax.broadcasted_iota(jnp.int32, (CHUNK, CHUNK), 1)
    eye = jnp.where(r == c, 1.0, 0.0)
    ts = [eye + n for n in ns]
    ms = [_matmul_3pass(n, n) for n in ns]
    for lvl in range(1, levels):
        for i in range(len(ns)):
            if lvl < levels - 1:
                prod = _matmul_3pass(ms[i], jnp.concatenate([ts[i], ms[i]], axis=1))
                ts[i] = ts[i] + prod[:, :CHUNK]
                ms[i] = prod[:, CHUNK:]
            else:
                ts[i] = ts[i] + _matmul_3pass(ms[i], ts[i])
    return [_matmul_3pass(t, x) for t, x in zip(ts, xs)]


def _delta_kernel(*refs, seq, zero_init):
    if zero_init:
        (q0_ref, q1_ref, k0_ref, k1_ref, v0_ref, v1_ref, wq0_ref, wq1_ref, wk0_ref, wk1_ref, wv0_ref, wv1_ref,
         g_ref, par_ref, o_ref, sf_ref, q_scr, k_scr, v_scr) = refs
    else:
        (q0_ref, q1_ref, k0_ref, k1_ref, v0_ref, v1_ref, wq0_ref, wq1_ref, wk0_ref, wk1_ref, wv0_ref, wv1_ref,
         g_ref, par_ref, s0_ref, o_ref, sf_ref, q_scr, k_scr, v_scr) = refs
    nc = seq // CHUNK
    pair = pl.program_id(1)
    lane = lax.broadcasted_iota(jnp.int32, (1, LANES), 1)
    tril, triu, stril, striu = _chunk_masks()
    masks, smasks = (tril, triu), (stril, striu)
    masks16 = (_thrice(tril.astype(bf16)), _thrice(triu.astype(bf16)))
    sel = [_thrice(_head_lane_select(LANE_A + d * H_B, pair).astype(bf16)) for d in range(2)]
    neg_a_row = -jnp.exp(par_ref[0:1, :])
    dt_row = par_ref[1:2, :]
    streams = [(hh, d) for hh in range(2) for d in range(2)]

    for hh, (q_ref, k_ref, v_ref, wq_ref, wk_ref, wv_ref) in enumerate(
            ((q0_ref, k0_ref, v0_ref, wq0_ref, wk0_ref, wv0_ref), (q1_ref, k1_ref, v1_ref, wq1_ref, wk1_ref, wv1_ref))):
        q_scr[hh] = _l2_unit(_short_conv_silu(q_ref[...].astype(f32), wq_ref, seq)) * DK_B ** -0.5
        k_scr[hh] = _l2_unit(_short_conv_silu(k_ref[...].astype(f32), wk_ref, seq))
        v_scr[hh] = _short_conv_silu(v_ref[...].astype(f32), wv_ref, seq)

    o_ref[...] = jnp.zeros_like(o_ref)
    init = tuple(jnp.zeros((DK_B, DV_B), f32) if zero_init else s0_ref[d, hh] for hh, d in streams)

    def body(trip, carry):
        subs = range(CHUNKS_PER_TRIP)
        shared = {}
        for u in subs:
            for d in range(2):
                r0 = _chunk_start(trip, u, d, nc)
                gates = g_ref[pl.ds(r0, CHUNK), :]
                beta_all = jax.nn.sigmoid(gates)
                glog = neg_a_row * _softplus(gates + dt_row)
                cum = _mask_matmul_f32(masks16[d], glog)
                rows = _select_rows_f32(sel[d], cum)
                shared[u, d] = (r0, beta_all, cum, rows)
        chains = [(u, hh, d) for u in subs for hh, d in streams]
        st = {ch: {} for ch in chains}
        for ch in chains:
            u, hh, d = ch
            r0, beta_all, cum, rows = shared[u, d]
            s = st[ch]
            head = 2 * pair + hh
            beta = _lane_col(beta_all, lane, LANE_BETA + d * H_B + head)
            gcol = _lane_col(cum, lane, LANE_A + d * H_B + head)
            glast = gcol[CHUNK - 1:CHUNK, :] if d == 0 else gcol[0:1, :]
            q = q_scr[hh, pl.ds(r0, CHUNK), :]
            k = k_scr[hh, pl.ds(r0, CHUNK), :]
            v = v_scr[hh, pl.ds(r0, CHUNK), :]
            k16 = k.astype(bf16)
            kbeta = k * beta
            eg = jnp.exp(gcol)
            decay = jnp.exp(jnp.where(masks[d], gcol - rows[hh:hh + 1, :], -jnp.inf))
            kk = lax.dot_general(kbeta.astype(bf16), k16, NT_DIMS, preferred_element_type=f32)
            qk = lax.dot_general(q.astype(bf16), k16, NT_DIMS, preferred_element_type=f32)
            s["n"] = -jnp.where(smasks[d], kk * decay, 0.0)
            s["x"] = jnp.concatenate([v * beta, kbeta * eg], axis=1)
            s["qk"] = (qk * decay).astype(bf16)
            s["qg"] = (q * eg).astype(bf16)
            s["kd_t"] = (k * jnp.exp(glast - gcol)).T.astype(bf16)
            s["gl"] = jnp.exp(glast)
        solved = _unit_triangular_solves([st[ch]["n"] for ch in chains], [st[ch]["x"] for ch in chains])
        for ch, uw in zip(chains, solved):
            s = st[ch]
            s["u"] = uw[:, :DV_B]
            s["w_qg"] = jnp.concatenate([uw[:, DV_B:].astype(bf16), s["qg"]], axis=0)
        state = list(carry)
        for u in subs:
            ws = [jnp.dot(st[u, hh, d]["w_qg"], state[i].astype(bf16), preferred_element_type=f32)
                  for i, (hh, d) in enumerate(streams)]
            v_new = [(st[u, hh, d]["u"] - ws[i][:CHUNK]).astype(bf16) for i, (hh, d) in enumerate(streams)]
            for i, (hh, d) in enumerate(streams):
                s = st[u, hh, d]
                o = ws[i][CHUNK:] + jnp.dot(s["qk"], v_new[i], preferred_element_type=f32)
                r0 = shared[u, d][0]
                o_ref[pl.ds(r0, CHUNK), hh * DV_B:(hh + 1) * DV_B] += o
                state[i] = state[i] * s["gl"] + jnp.dot(s["kd_t"], v_new[i], preferred_element_type=f32)
        return tuple(state)

    final = lax.fori_loop(0, nc // CHUNKS_PER_TRIP, body, init)
    for i, (hh, d) in enumerate(streams):
        sf_ref[d, hh] = final[i]


def _delta_call(proj, conv_w8, gates, par, nb, seq, off, state):
    zero_init = state is None
    col0 = (2 * H_A * DQK_A + 2 * H_A * DV_A) // LANES
    blk = lambda col: pl.BlockSpec((seq, LANES), col)
    wblk = lambda col: pl.BlockSpec((8, LANES), col)
    in_specs = [blk(lambda b, p: (off + b, col0 + 2 * p)), blk(lambda b, p: (off + b, col0 + 2 * p + 1)),
                blk(lambda b, p: (off + b, col0 + H_B + 2 * p)), blk(lambda b, p: (off + b, col0 + H_B + 2 * p + 1)),
                blk(lambda b, p: (off + b, col0 + 2 * H_B + 2 * p)), blk(lambda b, p: (off + b, col0 + 2 * H_B + 2 * p + 1)),
                wblk(lambda b, p: (0, 2 * p)), wblk(lambda b, p: (0, 2 * p + 1)),
                wblk(lambda b, p: (0, H_B + 2 * p)), wblk(lambda b, p: (0, H_B + 2 * p + 1)),
                wblk(lambda b, p: (0, 2 * H_B + 2 * p)), wblk(lambda b, p: (0, 2 * H_B + 2 * p + 1)),
                blk(lambda b, p: (off + b, 0)),
                pl.BlockSpec((8, LANES), lambda b, p: (0, 0))]
    st_spec = pl.BlockSpec((None, 2, 2, DK_B, DV_B), lambda b, p: (b, 0, p, 0, 0))
    args = [proj] * 6 + [conv_w8] * 6 + [gates, par]
    if not zero_init:
        in_specs.append(st_spec)
        args.append(state)
    return pl.pallas_call(
        functools.partial(_delta_kernel, seq=seq, zero_init=zero_init),
        grid=(nb, H_B // 2),
        in_specs=in_specs,
        out_specs=[pl.BlockSpec((seq, 2 * DV_B), lambda b, p: (b, p)), st_spec],
        out_shape=[jax.ShapeDtypeStruct((nb * seq, H_B * DV_B), f32),
                   jax.ShapeDtypeStruct((nb, 2, H_B, DK_B, DV_B), f32)],
        scratch_shapes=[pltpu.VMEM((2, seq, LANES), f32)] * 3,
        compiler_params=pltpu.CompilerParams(vmem_limit_bytes=VMEM_LIMIT),
        name="delta",
    )(*args)


def _head_rms(x):
    return x * lax.rsqrt(jnp.mean(x * x, axis=-1, keepdims=True) + EPS)


def _ab_out_kernel(x_ref, hp_ref, hs_ref, op_ref, os_ref, om_ref, zd_ref, gm_ref, gd_ref, w_ref, mod_ref, o_ref,
                   *, gate_row, tm):
    hsum = _prompt_or_sample(hp_ref, hs_ref, tm)
    osum = _prompt_or_sample(op_ref, os_ref, tm)
    parts = []
    for h in range(H_A):
        sl = slice(h * DV_A, (h + 1) * DV_A)
        parts.append(_head_rms(hsum[:, sl]) * gm_ref[:, sl] * jax.nn.sigmoid(om_ref[:, sl].astype(f32)))
    for h in range(H_B):
        sl = slice(h * DV_B, (h + 1) * DV_B)
        parts.append(_head_rms(osum[:, sl]) * gd_ref[:, sl] * _silu(zd_ref[:, sl].astype(f32)))
    cat = jnp.concatenate(parts, axis=1).astype(bf16)
    y = jnp.dot(cat, w_ref[...].astype(bf16), preferred_element_type=f32)
    o_ref[...] = x_ref[...] + mod_ref[gate_row:gate_row + 1, :] * y


def _ab_out_call(x, hp, hs, op, os_, proj, gm_row, gd_row, w_out, mod_l, gate_row, tm=512):
    wide = H_A * DV_A
    om_blk = (2 * H_A * DQK_A) // wide + 1
    zd_blk = AB_MAIN // wide - 1
    return pl.pallas_call(
        functools.partial(_ab_out_kernel, gate_row=gate_row, tm=tm),
        grid=(N_TOK // tm,),
        in_specs=[pl.BlockSpec((tm, D_MODEL), lambda i: (i, 0))] + _split_specs(tm, wide) + _split_specs(tm, wide) + [
            pl.BlockSpec((tm, wide), lambda i: (i, om_blk)),
            pl.BlockSpec((tm, wide), lambda i: (i, zd_blk)),
            pl.BlockSpec((1, wide), lambda i: (0, 0)),
            pl.BlockSpec((1, wide), lambda i: (0, 0)),
            pl.BlockSpec((2 * wide, D_MODEL), lambda i: (0, 0)),
            pl.BlockSpec((None, N_ADA, D_MODEL), lambda i: (_mod_row_index(i, tm), 0, 0))],
        out_specs=pl.BlockSpec((tm, D_MODEL), lambda i: (i, 0)),
        out_shape=jax.ShapeDtypeStruct((N_TOK, D_MODEL), f32),
        compiler_params=pltpu.CompilerParams(vmem_limit_bytes=VMEM_LIMIT),
        name="ab_out",
    )(x, hp, hs, op, os_, proj, proj, gm_row, gd_row, w_out, mod_l)


def _na_out_kernel(x_ref, ap_ref, as_ref, w_ref, mod_ref, o_ref, *, gate_row, tm):
    a = _prompt_or_sample(ap_ref, as_ref, tm).astype(bf16)
    y = jnp.dot(a, w_ref[...].astype(bf16), preferred_element_type=f32)
    o_ref[...] = x_ref[...] + mod_ref[gate_row:gate_row + 1, :] * y


def _na_out_call(x, attn_p, attn_s, w_out, mod_l, gate_row, tm=512):
    return pl.pallas_call(
        functools.partial(_na_out_kernel, gate_row=gate_row, tm=tm),
        grid=(N_TOK // tm,),
        in_specs=[pl.BlockSpec((tm, D_MODEL), lambda i: (i, 0))] + _split_specs(tm, NA_W) + [
            pl.BlockSpec((NA_W, D_MODEL), lambda i: (0, 0)),
            pl.BlockSpec((None, N_ADA, D_MODEL), lambda i: (_mod_row_index(i, tm), 0, 0))],
        out_specs=pl.BlockSpec((tm, D_MODEL), lambda i: (i, 0)),
        out_shape=jax.ShapeDtypeStruct((N_TOK, D_MODEL), f32),
        compiler_params=pltpu.CompilerParams(vmem_limit_bytes=VMEM_LIMIT),
        name="na_out",
    )(x, attn_p, attn_s, w_out, mod_l)


CTX_HEADS = 8


def _ctx_attn_kernel(q_ref, k_ref, v_ref, o_ref, nk_ref, nv_ref):
    low = lax.broadcasted_iota(jnp.int32, (1, LANES), 1) < DH_C
    for hh in range(CTX_HEADS):
        sl = slice(hh * DH_C, (hh + 1) * DH_C)
        nk_ref[hh] = k_ref[:, sl]
        nv_ref[hh] = v_ref[:, sl]
    for pair in range(CTX_HEADS // 2):
        sl = slice(pair * LANES, (pair + 1) * LANES)
        q = q_ref[:, sl]
        k = k_ref[:, sl].astype(bf16)
        v = v_ref[:, sl].astype(bf16)
        outs = []
        for hh in range(2):
            qh = jnp.where(low if hh == 0 else jnp.logical_not(low), q, 0.0).astype(bf16)
            s = lax.dot_general(qh, k, NT_DIMS, preferred_element_type=f32) * DH_C ** -0.5
            p = jnp.exp(s - jnp.max(s, axis=1, keepdims=True))
            o = jnp.dot(p.astype(bf16), v, preferred_element_type=f32)
            outs.append(o / jnp.sum(p, axis=1, keepdims=True))
        o_ref[:, sl] = jnp.where(low, outs[0], outs[1]).astype(o_ref.dtype)


def _ctx_attn_call(proj):
    w = CTX_HEADS * DH_C
    nblk = NA_W // w
    return pl.pallas_call(
        _ctx_attn_kernel,
        grid=(BATCH, nblk),
        in_specs=[pl.BlockSpec((SEQ, w), lambda b, j: (b, j)),
                  pl.BlockSpec((SEQ, w), lambda b, j: (b, nblk + j)),
                  pl.BlockSpec((SEQ, w), lambda b, j: (b, 2 * nblk + j))],
        out_specs=[pl.BlockSpec((SEQ, w), lambda b, j: (b, j)),
                   pl.BlockSpec((None, CTX_HEADS, SEQ, DH_C), lambda b, j: (b, j, 0, 0)),
                   pl.BlockSpec((None, CTX_HEADS, SEQ, DH_C), lambda b, j: (b, j, 0, 0))],
        out_shape=[jax.ShapeDtypeStruct((N_PROMPT, NA_W), bf16),
                   jax.ShapeDtypeStruct((BATCH, H_C, SEQ, DH_C), f32),
                   jax.ShapeDtypeStruct((BATCH, H_C, SEQ, DH_C), f32)],
        compiler_params=pltpu.CompilerParams(vmem_limit_bytes=VMEM_LIMIT),
        name="ctx_attn",
    )(proj, proj, proj)


QROWS = 4
QBLK = QROWS * GRID_W
KROWS_MID = QROWS + WIN_R - 1
N_RIDX = 2 * WIN_R - 1
N_CIDX = 2 * WIN_C - 1
N_QBLK = GRID_ROWS // QROWS


def _nattn_bias_tables(rb_ref, tb_scr, bmid_scr, btop_scr, bbot_scr):
    qc = lax.broadcasted_iota(jnp.int32, (GRID_W, GRID_W), 0)
    kc = lax.broadcasted_iota(jnp.int32, (GRID_W, GRID_W), 1)
    cs = jnp.clip(qc - WIN_C // 2, 0, GRID_W - WIN_C)
    valid = (kc >= cs) & (kc < cs + WIN_C)
    neg = jnp.full((GRID_W, GRID_W), -jnp.inf, f32)
    for hh in range(2):
        for ri in range(N_RIDX):
            row = jnp.broadcast_to(rb_ref[hh, ri:ri + 1, :], (GRID_W, LANES))
            tile = pltpu.roll(row, LANES - (WIN_C - 1), 1, stride=1, stride_axis=0)[:, :GRID_W]
            tb_scr[ri] = jnp.where(valid, tile, -jnp.inf)
        for i in range(QROWS):
            rs = slice(i * GRID_W, (i + 1) * GRID_W)
            for jj in range(KROWS_MID):
                inside = 0 <= jj - i < WIN_R
                bmid_scr[hh, rs, jj * GRID_W:(jj + 1) * GRID_W] = tb_scr[jj - i + WIN_R // 2 - 1] if inside else neg
            for jj in range(WIN_R):
                btop_scr[hh, rs, jj * GRID_W:(jj + 1) * GRID_W] = tb_scr[jj - i + WIN_R - 1]
                bbot_scr[hh, rs, jj * GRID_W:(jj + 1) * GRID_W] = tb_scr[jj - i + WIN_R // 2 - 1]


def _nattn_kernel(rb_ref, q_ref, k_ref, v_ref, ck_ref, cv_ref, o_ref, tb_scr, bmid_scr, btop_scr, bbot_scr):
    scale = DH_C ** -0.5

    @pl.when(pl.program_id(1) == 0)
    def _():
        _nattn_bias_tables(rb_ref, tb_scr, bmid_scr, btop_scr, bbot_scr)

    low = lax.broadcasted_iota(jnp.int32, (1, LANES), 1) < DH_C
    kctx = jnp.concatenate([ck_ref[0], ck_ref[1]], axis=1).astype(bf16)
    vctx = jnp.concatenate([cv_ref[0], cv_ref[1]], axis=1).astype(bf16)

    def block(q_start, k_start, k_rows, bias_scr):
        nk = k_rows * GRID_W
        q = q_ref[pl.ds(q_start, QBLK), :]
        ku = k_ref[pl.ds(k_start, nk), :].astype(bf16)
        vu = v_ref[pl.ds(k_start, nk), :].astype(bf16)
        outs = []
        for hh in range(2):
            qh = jnp.where(low if hh == 0 else jnp.logical_not(low), q, 0.0).astype(bf16)
            s_loc = lax.dot_general(qh, ku, NT_DIMS, preferred_element_type=f32) * scale + bias_scr[hh]
            s_ctx = lax.dot_general(qh, kctx, NT_DIMS, preferred_element_type=f32) * scale
            m = jnp.maximum(jnp.max(s_loc, axis=1, keepdims=True), jnp.max(s_ctx, axis=1, keepdims=True))
            p_loc = jnp.exp(s_loc - m)
            p_ctx = jnp.exp(s_ctx - m)
            denom = jnp.sum(p_loc, axis=1, keepdims=True) + jnp.sum(p_ctx, axis=1, keepdims=True)
            o = (jnp.dot(p_loc.astype(bf16), vu, preferred_element_type=f32)
                 + jnp.dot(p_ctx.astype(bf16), vctx, preferred_element_type=f32))
            outs.append(o / denom)
        o_ref[pl.ds(q_start, QBLK), :] = jnp.where(low, outs[0], outs[1]).astype(o_ref.dtype)

    block(0, 0, WIN_R, btop_scr)

    def mid_body(blk, _):
        q_start = pl.multiple_of(blk * QBLK, QBLK)
        k_start = pl.multiple_of(blk * QBLK - (WIN_R // 2) * GRID_W, QBLK)
        block(q_start, k_start, KROWS_MID, bmid_scr)
        return 0

    lax.fori_loop(1, N_QBLK - 1, mid_body, 0)
    block((N_QBLK - 1) * QBLK, (GRID_ROWS - WIN_R) * GRID_W, WIN_R, bbot_scr)


def _nattn_call(proj, cache_k, cache_v, rel_bias, off):
    npair = H_C // 2
    return pl.pallas_call(
        _nattn_kernel,
        grid=(npair, DEC_BATCH),
        in_specs=[pl.BlockSpec((2, 2 * WIN_R, LANES), lambda p, b: (p, 0, 0)),
                  pl.BlockSpec((DEC_SEQ, LANES), lambda p, b: (off + b, p)),
                  pl.BlockSpec((DEC_SEQ, LANES), lambda p, b: (off + b, npair + p)),
                  pl.BlockSpec((DEC_SEQ, LANES), lambda p, b: (off + b, 2 * npair + p)),
                  pl.BlockSpec((None, 2, PAST_LEN, DH_C), lambda p, b: (b, p, 0, 0)),
                  pl.BlockSpec((None, 2, PAST_LEN, DH_C), lambda p, b: (b, p, 0, 0))],
        out_specs=pl.BlockSpec((DEC_SEQ, LANES), lambda p, b: (b, p)),
        out_shape=jax.ShapeDtypeStruct((N_SAMPLE, NA_W), bf16),
        scratch_shapes=[pltpu.VMEM((N_RIDX, GRID_W, GRID_W), f32),
                        pltpu.VMEM((2, QBLK, KROWS_MID * GRID_W), f32),
                        pltpu.VMEM((2, QBLK, WIN_R * GRID_W), f32),
                        pltpu.VMEM((2, QBLK, WIN_R * GRID_W), f32)],
        compiler_params=pltpu.CompilerParams(dimension_semantics=("arbitrary", "arbitrary"),
                                             vmem_limit_bytes=VMEM_LIMIT),
        name="nattn",
    )(jnp.pad(rel_bias, ((0, 0), (0, 2 * WIN_R - N_RIDX), (0, LANES - N_CIDX))), proj, proj, proj, cache_k, cache_v)


def _lane_row(pieces):
    row = jnp.zeros((LANES,), f32)
    for off, vals in pieces:
        row = row.at[off:off + vals.shape[0]].set(vals.astype(f32))
    return row


def _param_rows(rows):
    out = jnp.zeros((8, LANES), f32)
    for r, row in enumerate(rows):
        out = out.at[r].set(row)
    return out


def kernel(x_prompt, x_sample, c, state_mlstm_C, state_mlstm_n, state_mlstm_m, state_delta_S, cache_na_k, cache_na_v, c_ctx, ada_w, ada_b, norm_g, ffn_wg, ffn_wu, ffn_wd, ab_w_in, ab_w_out, mlstm_b_i, mlstm_b_f, mlstm_norm_g, delta_conv_w, delta_a_log, delta_dt_bias, delta_norm_g, na_w_in, na_w_out, na_rel_bias, final_norm_g):
    xs = (x_prompt.reshape(N_PROMPT, D_MODEL), x_sample.reshape(N_SAMPLE, D_MODEL))
    mods = _ada_call(jnp.concatenate([c_ctx[None, :], c], axis=0), ada_w, ada_b)
    mods = mods.reshape(DEPTH, 3, N_ADA, D_MODEL)
    gf_row = final_norm_g.reshape(1, D_MODEL)
    s_off = N_PROMPT // DEC_SEQ
    new_c, new_n, new_m, new_s, new_k, new_v = [], [], [], [], [], []

    for l in range(DEPTH):
        mod_l = mods[l]
        a = l // 2
        x = _ffn_call(xs if l == 0 else (x,), mod_l, norm_g[l, 0].reshape(1, D_MODEL), ffn_wg, ffn_wu, ffn_wd, l, 0,
                      gf_row, rows=(0, 1, 2), final=False, split_out=False, tm=FFN_TM_SPLIT if l == 0 else FFN_TM)[0]
        g_mix = norm_g[l, 1].reshape(1, D_MODEL)
        if l % 2 == 0:
            w_in = ab_w_in[a]
            i0 = 2 * H_A * DQK_A + 2 * H_A * DV_A
            d0 = i0 + 4 * H_A
            b0 = d0 + 2 * H_B * DK_B + 2 * H_B * DV_B
            w_main = jnp.concatenate([w_in[:, :i0], w_in[:, d0:b0]], axis=1)
            w_gate = jnp.concatenate([w_in[:, i0:d0], w_in[:, b0:],
                                      jnp.zeros((D_MODEL, LANES - 4 * H_A - 4 * H_B), f32)], axis=1)
            proj, gates = _proj_call(x, mod_l, g_mix, w_main, w_gate, rows=(3, 4), out_dtype=bf16)

            par_m = _param_rows([_lane_row([(LANE_I, mlstm_b_i[a].reshape(-1))]),
                                 _lane_row([(LANE_F, mlstm_b_f[a].reshape(-1))])])
            par_d = _param_rows([_lane_row([(LANE_A, delta_a_log[a].reshape(-1))]),
                                 _lane_row([(LANE_A, delta_dt_bias[a].reshape(-1))])])
            conv_w8 = jnp.concatenate([delta_conv_w[a], jnp.zeros((8 - CONV_K, 3 * H_B * DK_B), f32)], axis=0)

            hp, cn, nn_, mn = _mlstm_call(proj, gates, par_m, BATCH, SEQ, 0, None)
            st = (state_mlstm_C[:, a], state_mlstm_n[:, a][..., None], state_mlstm_m[:, a][..., None, None])
            hs, _, _, _ = _mlstm_call(proj, gates, par_m, DEC_BATCH, DEC_SEQ, s_off, st)
            op, sn = _delta_call(proj, conv_w8, gates, par_d, BATCH, SEQ, 0, None)
            os_, _ = _delta_call(proj, conv_w8, gates, par_d, DEC_BATCH, DEC_SEQ, s_off, state_delta_S[:, a])
            new_c.append(cn)
            new_n.append(nn_[..., 0])
            new_m.append(mn[..., 0, 0])
            new_s.append(sn)
            x = _ab_out_call(x, hp, hs, op, os_, proj, mlstm_norm_g[a].reshape(1, -1),
                             jnp.tile(delta_norm_g[a], H_B).reshape(1, -1), ab_w_out[a], mod_l, gate_row=5)
        else:
            proj = _proj_call(x, mod_l, g_mix, na_w_in[a], None, rows=(3, 4), out_dtype=f32)[0]
            attn_p, kp, vp = _ctx_attn_call(proj)
            attn_s = _nattn_call(proj, cache_na_k[:, a], cache_na_v[:, a], na_rel_bias[a], s_off)
            new_k.append(kp)
            new_v.append(vp)
            x = _na_out_call(x, attn_p, attn_s, na_w_out[a], mod_l, gate_row=5)
        last = l == DEPTH - 1
        outs = _ffn_call((x,), mod_l, norm_g[l, 2].reshape(1, D_MODEL), ffn_wg, ffn_wu, ffn_wd, l, 1,
                         gf_row, rows=(6, 7, 8), final=last, split_out=last, tm=FFN_TM_SPLIT if last else FFN_TM)
        x = outs[0]

    y_prompt = outs[0].reshape(BATCH, SEQ, D_MODEL)
    y_sample = outs[1].reshape(DEC_BATCH, DEC_SEQ, D_MODEL)
    return (y_prompt, y_sample, jnp.stack(new_c, axis=1), jnp.stack(new_n, axis=1), jnp.stack(new_m, axis=1),
            jnp.stack(new_s, axis=1), jnp.stack(new_k, axis=1), jnp.stack(new_v, axis=1))
```

```python
import functools

import jax
import jax.numpy as jnp
from jax import lax
from jax.experimental import pallas as pl
from jax.experimental.pallas import tpu as pltpu

f32 = jnp.float32
bf16 = jnp.bfloat16

D_MODEL = 1024
BATCH = 16
SEQ = 256
DEPTH = 2
DEC_BATCH = 2
DEC_SEQ = 2048
PAST_LEN = 256
GRID_W = 64
GRID_ROWS = DEC_SEQ // GRID_W
D_FF = 2816
N_ADA = 9
EPS = 1e-6
CHUNK = 64
H_A, DQK_A, DV_A = 4, 64, 128
H_B, DK_B, DV_B = 4, 128, 128
CONV_K = 5
H_C, DH_C = 16, 64
WIN_R, WIN_C = 8, 16
NA_W = H_C * DH_C

N_PROMPT = BATCH * SEQ
N_SAMPLE = DEC_BATCH * DEC_SEQ
N_TOK = N_PROMPT + N_SAMPLE
AB_MAIN = 2 * H_A * DQK_A + 2 * H_A * DV_A + 2 * H_B * DK_B + 2 * H_B * DV_B
LANES = 128
VMEM_LIMIT = 56 * 1024 * 1024

LANE_I, LANE_F, LANE_BETA, LANE_A = 0, 8, 16, 24

NT_DIMS = (((1,), (1,)), ((), ()))
TN_DIMS = (((0,), (0,)), ((), ()))

MAX_CHUNKS_PER_TRIP = 4
FFN_TM = 2048
FFN_TM_SPLIT = 1024
FFN_ROW_BLOCK = 512
PROJ_ROW_BLOCK = 256


def _softplus(x):
    return jnp.maximum(x, 0.0) + jnp.log1p(jnp.exp(-jnp.abs(x)))


def _log_sigmoid(x):
    return -_softplus(-x)


def _silu(x):
    return x * jax.nn.sigmoid(x)


def _mod_row_index(i, tm):
    n_p = N_PROMPT // tm
    per_b = DEC_SEQ // tm
    return jnp.where(i < n_p, 0, 1 + (i - n_p) // per_b)


def _modulated(x, g_row, shift_row, scale_row):
    y = x * lax.rsqrt(jnp.mean(x * x, axis=-1, keepdims=True) + EPS) * g_row
    return y * (1.0 + scale_row) + shift_row


def _ada_kernel(cb_ref, w_ref, b_ref, o_ref, s_scr):
    tn = w_ref.shape[-1]

    @pl.when((pl.program_id(0) == 0) & (pl.program_id(1) == 0))
    def _():
        cb = cb_ref[...]
        s_scr[...] = cb * jax.nn.sigmoid(cb)

    for j in range(tn // LANES):
        cols = slice(j * LANES, (j + 1) * LANES)
        w = w_ref[:, cols]
        for r in range(3):
            o_ref[r:r + 1, cols] = jnp.sum(w * s_scr[r], axis=0, keepdims=True) + b_ref[:, cols]


def _ada_call(cond3, ada_w, ada_b):
    tn = D_MODEL
    cb = jnp.broadcast_to(cond3[:, :, None], (3, D_MODEL, LANES))
    return pl.pallas_call(
        _ada_kernel,
        grid=(DEPTH, N_ADA * D_MODEL // tn),
        in_specs=[pl.BlockSpec((3, D_MODEL, LANES), lambda l, j: (0, 0, 0)),
                  pl.BlockSpec((None, D_MODEL, tn), lambda l, j: (l, 0, j)),
                  pl.BlockSpec((None, 1, tn), lambda l, j: (l, 0, j))],
        out_specs=pl.BlockSpec((None, 3, tn), lambda l, j: (l, 0, j)),
        out_shape=jax.ShapeDtypeStruct((DEPTH, 3, N_ADA * D_MODEL), f32),
        scratch_shapes=[pltpu.VMEM((3, D_MODEL, LANES), f32)],
        compiler_params=pltpu.CompilerParams(dimension_semantics=("arbitrary", "arbitrary"),
                                             vmem_limit_bytes=VMEM_LIMIT),
        name="ada_mod",
    )(cb, ada_w, ada_b.reshape(DEPTH, 1, N_ADA * D_MODEL))


def _prompt_or_sample(p_ref, s_ref, tm):
    return jnp.where(pl.program_id(0) < N_PROMPT // tm, p_ref[...], s_ref[...])


def _split_specs(tm, width, n_grid_axes=1, single_buffered=False):
    n_p = N_PROMPT // tm
    kw = dict(pipeline_mode=pl.Buffered(1)) if single_buffered else {}
    if n_grid_axes == 1:
        return [pl.BlockSpec((tm, width), lambda i: (jnp.minimum(i, n_p - 1), 0), **kw),
                pl.BlockSpec((tm, width), lambda i: (jnp.maximum(i - n_p, 0), 0), **kw)]
    return [pl.BlockSpec((tm, width), lambda i, j: (jnp.minimum(i, n_p - 1), 0), **kw),
            pl.BlockSpec((tm, width), lambda i, j: (jnp.maximum(i - n_p, 0), 0), **kw)]


def _ffn_kernel(*refs, rows, final, split_in, split_out, tm):
    it = iter(refs)
    x_refs = [next(it) for _ in range(2 if split_in else 1)]
    mod_ref, g_ref, wg_ref, wu_ref, wd_ref, gf_ref = [next(it) for _ in range(6)]
    o_refs = [next(it) for _ in range(2 if split_out else 1)]
    h_scr = next(it)
    acc_ref = next(it) if split_out else o_refs[0]
    j = pl.program_id(1)
    last_j = pl.num_programs(1) - 1
    is_prompt = pl.program_id(0) < N_PROMPT // tm

    def x_rows(rs):
        if split_in:
            return jnp.where(is_prompt, x_refs[0][rs, :], x_refs[1][rs, :])
        return x_refs[0][rs, :]

    def step(first, last, out_ref):
        wg = wg_ref[...].astype(bf16)
        wu = wu_ref[...].astype(bf16)
        wd = wd_ref[...].astype(bf16)
        for r in range(tm // FFN_ROW_BLOCK):
            rs = slice(r * FFN_ROW_BLOCK, (r + 1) * FFN_ROW_BLOCK)
            if first:
                h = _modulated(x_rows(rs), g_ref[...], mod_ref[rows[0]:rows[0] + 1, :],
                               mod_ref[rows[1]:rows[1] + 1, :]).astype(bf16)
                h_scr[rs, :] = h
            else:
                h = h_scr[rs, :]
            g = jnp.dot(h, wg, preferred_element_type=f32)
            u = jnp.dot(h, wu, preferred_element_type=f32)
            part = jnp.dot((_silu(g) * u).astype(bf16), wd, preferred_element_type=f32)
            acc = part if first else acc_ref[rs, :] + part
            if last:
                xn = x_rows(rs) + (0.5 * mod_ref[rows[2]:rows[2] + 1, :]) * acc
                if final:
                    xn = xn * lax.rsqrt(jnp.mean(xn * xn, axis=-1, keepdims=True) + EPS) * gf_ref[...]
                out_ref[rs, :] = xn
            else:
                acc_ref[rs, :] = acc

    pl.when(j == 0)(lambda: step(True, False, None))
    pl.when((j > 0) & (j < last_j))(lambda: step(False, False, None))
    if split_out:
        pl.when((j == last_j) & is_prompt)(lambda: step(False, True, o_refs[0]))
        pl.when((j == last_j) & jnp.logical_not(is_prompt))(lambda: step(False, True, o_refs[1]))
    else:
        pl.when(j == last_j)(lambda: step(False, True, o_refs[0]))


def _ffn_call(xs, mod_l, g_row, ffn_wg, ffn_wu, ffn_wd, layer, half, gf_row, rows, final, split_out, tm, tf=256):
    split_in = len(xs) == 2
    x_specs = (_split_specs(tm, D_MODEL, 2, single_buffered=True) if split_in
               else [pl.BlockSpec((tm, D_MODEL), lambda i, j: (i, 0))])
    if split_out:
        out_specs = _split_specs(tm, D_MODEL, 2)
        out_shape = [jax.ShapeDtypeStruct((N_PROMPT, D_MODEL), f32), jax.ShapeDtypeStruct((N_SAMPLE, D_MODEL), f32)]
        scratch = [pltpu.VMEM((tm, D_MODEL), bf16), pltpu.VMEM((tm, D_MODEL), f32)]
    else:
        out_specs = [pl.BlockSpec((tm, D_MODEL), lambda i, j: (i, 0))]
        out_shape = [jax.ShapeDtypeStruct((N_TOK, D_MODEL), f32)]
        scratch = [pltpu.VMEM((tm, D_MODEL), bf16)]
    return pl.pallas_call(
        functools.partial(_ffn_kernel, rows=rows, final=final, split_in=split_in, split_out=split_out, tm=tm),
        grid=(N_TOK // tm, D_FF // tf),
        in_specs=x_specs + [
            pl.BlockSpec((None, N_ADA, D_MODEL), lambda i, j: (_mod_row_index(i, tm), 0, 0)),
            pl.BlockSpec((1, D_MODEL), lambda i, j: (0, 0)),
            pl.BlockSpec((None, None, D_MODEL, tf), lambda i, j: (layer, half, 0, j)),
            pl.BlockSpec((None, None, D_MODEL, tf), lambda i, j: (layer, half, 0, j)),
            pl.BlockSpec((None, None, tf, D_MODEL), lambda i, j: (layer, half, j, 0)),
            pl.BlockSpec((1, D_MODEL), lambda i, j: (0, 0))],
        out_specs=out_specs, out_shape=out_shape, scratch_shapes=scratch,
        compiler_params=pltpu.CompilerParams(dimension_semantics=("parallel", "arbitrary"),
                                             vmem_limit_bytes=VMEM_LIMIT),
        name="ffn",
    )(*xs, mod_l, g_row, ffn_wg, ffn_wu, ffn_wd, gf_row)


def _proj_kernel(*refs, rows, with_gates):
    if with_gates:
        x_ref, mod_ref, g_ref, w_ref, wgate_ref, o_ref, og_ref, h_scr = refs
    else:
        x_ref, mod_ref, g_ref, w_ref, o_ref, h_scr = refs
    j = pl.program_id(1)
    tm = x_ref.shape[0]

    @pl.when(j == 0)
    def _():
        w = w_ref[...].astype(bf16)
        for r in range(tm // PROJ_ROW_BLOCK):
            rs = slice(r * PROJ_ROW_BLOCK, (r + 1) * PROJ_ROW_BLOCK)
            hb = _modulated(x_ref[rs, :], g_ref[...], mod_ref[rows[0]:rows[0] + 1, :],
                            mod_ref[rows[1]:rows[1] + 1, :]).astype(bf16)
            h_scr[rs, :] = hb
            o_ref[rs, :] = jnp.dot(hb, w, preferred_element_type=f32).astype(o_ref.dtype)
            if with_gates:
                og_ref[rs, :] = jnp.dot(hb, wgate_ref[...].astype(bf16), preferred_element_type=f32)

    @pl.when(j > 0)
    def _():
        o_ref[...] = jnp.dot(h_scr[...], w_ref[...].astype(bf16), preferred_element_type=f32).astype(o_ref.dtype)


def _proj_call(x, mod_l, g_row, w, w_gate, rows, out_dtype, tm=1024):
    n = w.shape[1]
    tn = n // 2
    with_gates = w_gate is not None
    in_specs = [pl.BlockSpec((tm, D_MODEL), lambda i, j: (i, 0)),
                pl.BlockSpec((None, N_ADA, D_MODEL), lambda i, j: (_mod_row_index(i, tm), 0, 0)),
                pl.BlockSpec((1, D_MODEL), lambda i, j: (0, 0)),
                pl.BlockSpec((D_MODEL, tn), lambda i, j: (0, j))]
    out_specs = [pl.BlockSpec((tm, tn), lambda i, j: (i, j))]
    out_shape = [jax.ShapeDtypeStruct((N_TOK, n), out_dtype)]
    args = [x, mod_l, g_row, w]
    if with_gates:
        in_specs.append(pl.BlockSpec((D_MODEL, LANES), lambda i, j: (0, 0)))
        out_specs.append(pl.BlockSpec((tm, LANES), lambda i, j: (i, 0)))
        out_shape.append(jax.ShapeDtypeStruct((N_TOK, LANES), f32))
        args.append(w_gate)
    return pl.pallas_call(
        functools.partial(_proj_kernel, rows=rows, with_gates=with_gates),
        grid=(N_TOK // tm, n // tn),
        in_specs=in_specs, out_specs=out_specs, out_shape=out_shape,
        scratch_shapes=[pltpu.VMEM((tm, D_MODEL), bf16)],
        compiler_params=pltpu.CompilerParams(dimension_semantics=("parallel", "arbitrary"),
                                             vmem_limit_bytes=VMEM_LIMIT),
        name="in_proj",
    )(*args)


def _chunk_masks():
    r = lax.broadcasted_iota(jnp.int32, (CHUNK, CHUNK), 0)
    c = lax.broadcasted_iota(jnp.int32, (CHUNK, CHUNK), 1)
    return r >= c, r <= c, r > c, r < c


def _lane_col(x, lane, j):
    return jnp.sum(jnp.where(lane == j, x, 0.0), axis=1, keepdims=True)


def _head_lane_select(base, pair):
    r = lax.broadcasted_iota(jnp.int32, (8, LANES), 0)
    ln = lax.broadcasted_iota(jnp.int32, (8, LANES), 1)
    return jnp.where((ln == base + 2 * pair + r) & (r < 2), 1.0, 0.0)


def _split3_bf16(x):
    hi = x.astype(bf16)
    r = x - hi.astype(f32)
    mid = r.astype(bf16)
    return hi, mid, (r - mid.astype(f32)).astype(bf16)


def _thrice(a16):
    return jnp.concatenate([a16] * 3, axis=1)


def _mask_matmul_f32(mask16x3, x):
    return jnp.dot(mask16x3, jnp.concatenate(_split3_bf16(x), axis=0), preferred_element_type=f32)


def _select_rows_f32(sel16x3, x):
    return lax.dot_general(sel16x3, jnp.concatenate(_split3_bf16(x), axis=1), NT_DIMS, preferred_element_type=f32)


def _chunks_per_trip(nc):
    return min(MAX_CHUNKS_PER_TRIP, nc)


def _chunk_start(trip, u, d, nc):
    step = trip * _chunks_per_trip(nc) + u
    c = step if d == 0 else nc - 1 - step
    return pl.multiple_of(c * CHUNK, CHUNK)


def _mlstm_kernel(*refs, seq, zero_init):
    if zero_init:
        qp_ref, kp_ref, v0_ref, v1_ref, g_ref, par_ref, h_ref, cf_ref, nf_ref, mf_ref = refs
    else:
        (qp_ref, kp_ref, v0_ref, v1_ref, g_ref, par_ref, c0_ref, n0_ref, m0_ref,
         h_ref, cf_ref, nf_ref, mf_ref) = refs
    nc = seq // CHUNK
    pair = pl.program_id(1)
    lane = lax.broadcasted_iota(jnp.int32, (1, LANES), 1)
    tril, triu, _, _ = _chunk_masks()
    masks = (tril, triu)
    masks16 = (_thrice(tril.astype(bf16)), _thrice(triu.astype(bf16)))
    sel = [_thrice(_head_lane_select(LANE_F + d * H_A, pair).astype(bf16)) for d in range(2)]
    bi_row = par_ref[0:1, :]
    bf_row = par_ref[1:2, :]
    ones_col = jnp.where(lane == 0, 1.0, 0.0) + jnp.zeros((CHUNK, LANES), f32)
    v_refs = (v0_ref, v1_ref)
    streams = [(hh, d) for hh in range(2) for d in range(2)]

    h_ref[...] = jnp.zeros_like(h_ref)

    init = []
    for hh, d in streams:
        if zero_init:
            init.append((jnp.zeros((DQK_A, 2 * LANES), f32), jnp.zeros((1, 1), f32)))
        else:
            n_aug = jnp.where(lane == 0, n0_ref[d, hh], 0.0)
            init.append((jnp.concatenate([c0_ref[d, hh], n_aug], axis=1), m0_ref[d, hh]))

    def body(trip, carry):
        subs = range(_chunks_per_trip(nc))
        shared = {}
        for u in subs:
            for d in range(2):
                r0 = _chunk_start(trip, u, d, nc)
                gates = g_ref[pl.ds(r0, CHUNK), :]
                gi = gates + bi_row
                gf = _log_sigmoid(gates + bf_row)
                cum = _mask_matmul_f32(masks16[d], gf)
                stack = jnp.concatenate([cum, pltpu.roll(gi, LANE_F - LANE_I, axis=1)], axis=0)
                rows = _select_rows_f32(sel[d], stack)
                total = cum[CHUNK - 1:CHUNK, :] if d == 0 else cum[0:1, :]
                shared[u, d] = (r0, gi, cum, rows, total)
        chains = [(u, hh, d) for u in subs for hh, d in streams]
        st = {ch: {} for ch in chains}
        for ch in chains:
            u, hh, d = ch
            r0 = shared[u, d][0]
            s = st[ch]
            s["q"] = (qp_ref[pl.ds(r0, CHUNK), hh * DQK_A:(hh + 1) * DQK_A] * DQK_A ** -0.5).astype(bf16)
            s["k"] = kp_ref[pl.ds(r0, CHUNK), hh * DQK_A:(hh + 1) * DQK_A].astype(bf16)
            s["v_aug"] = jnp.concatenate([v_refs[hh][pl.ds(r0, CHUNK), :].astype(f32), ones_col], axis=1)
            s["qk"] = lax.dot_general(s["q"], s["k"], NT_DIMS, preferred_element_type=f32)
        for ch in chains:
            u, hh, d = ch
            _, gi, cum, rows, total = shared[u, d]
            s = st[ch]
            head = 2 * pair + hh
            jf = LANE_F + d * H_A + head
            ji = LANE_I + d * H_A + head
            bcol = _lane_col(cum, lane, jf)
            icol = _lane_col(gi, lane, ji)
            dmat = jnp.where(masks[d], bcol - rows[hh:hh + 1, :CHUNK] + rows[hh:hh + 1, CHUNK:], -jnp.inf)
            m_loc = jnp.max(dmat, axis=1, keepdims=True)
            s["sw"] = (s["qk"] * jnp.exp(dmat - m_loc)).astype(bf16)
            blast = _lane_col(total, lane, jf)
            gs = blast - bcol + icol
            ms_loc = jnp.max(gs, axis=0, keepdims=True)
            s["wv"] = (jnp.exp(gs - ms_loc) * s["v_aug"]).astype(bf16)
            s.update(bcol=bcol, m_loc=m_loc, blast=blast, ms_loc=ms_loc)
        for ch in chains:
            s = st[ch]
            s["num"] = jnp.dot(s["sw"], s["v_aug"].astype(bf16), preferred_element_type=f32)
            s["kv"] = lax.dot_general(s["k"], s["wv"], TN_DIMS, preferred_element_type=f32)
        state = list(carry)
        for u in subs:
            qc = [jnp.dot(st[u, hh, d]["q"], state[i][0].astype(bf16), preferred_element_type=f32)
                  for i, (hh, d) in enumerate(streams)]
            for i, (hh, d) in enumerate(streams):
                s = st[u, hh, d]
                c_aug, m = state[i]
                m_inter = s["bcol"] + m
                m_t = jnp.maximum(m_inter, s["m_loc"])
                nd = jnp.exp(s["m_loc"] - m_t) * s["num"] + jnp.exp(m_inter - m_t) * qc[i]
                den = nd[:, DV_A:DV_A + 1]
                hval = nd[:, :DV_A] / jnp.maximum(jnp.abs(den), jnp.exp(-m_t))
                r0 = shared[u, d][0]
                h_ref[pl.ds(r0, CHUNK), hh * DV_A:(hh + 1) * DV_A] += hval
                m_new = jnp.maximum(s["blast"] + m, s["ms_loc"])
                c_new = jnp.exp(s["blast"] + m - m_new) * c_aug + jnp.exp(s["ms_loc"] - m_new) * s["kv"]
                state[i] = (c_new, m_new)
        return tuple(state)

    final = lax.fori_loop(0, nc // _chunks_per_trip(nc), body, tuple(init))
    for i, (hh, d) in enumerate(streams):
        c_aug, m = final[i]
        cf_ref[d, hh] = c_aug[:, :DV_A]
        nf_ref[d, hh] = c_aug[:, DV_A:DV_A + 1]
        mf_ref[d, hh] = m


def _mlstm_call(proj, gates, par, nb, seq, off, state):
    zero_init = state is None
    blk = lambda col: pl.BlockSpec((seq, LANES), col)
    in_specs = [blk(lambda b, p: (off + b, p)),
                blk(lambda b, p: (off + b, 2 + p)),
                blk(lambda b, p: (off + b, 4 + 2 * p)),
                blk(lambda b, p: (off + b, 5 + 2 * p)),
                blk(lambda b, p: (off + b, 0)),
                pl.BlockSpec((8, LANES), lambda b, p: (0, 0))]
    st_specs = [pl.BlockSpec((None, 2, 2, DQK_A, DV_A), lambda b, p: (b, 0, p, 0, 0)),
                pl.BlockSpec((None, 2, 2, DQK_A, 1), lambda b, p: (b, 0, p, 0, 0)),
                pl.BlockSpec((None, 2, 2, 1, 1), lambda b, p: (b, 0, p, 0, 0))]
    args = [proj, proj, proj, proj, gates, par]
    if not zero_init:
        in_specs += st_specs
        args += list(state)
    return pl.pallas_call(
        functools.partial(_mlstm_kernel, seq=seq, zero_init=zero_init),
        grid=(nb, H_A // 2),
        in_specs=in_specs,
        out_specs=[pl.BlockSpec((seq, 2 * DV_A), lambda b, p: (b, p))] + st_specs,
        out_shape=[jax.ShapeDtypeStruct((nb * seq, H_A * DV_A), f32),
                   jax.ShapeDtypeStruct((nb, 2, H_A, DQK_A, DV_A), f32),
                   jax.ShapeDtypeStruct((nb, 2, H_A, DQK_A, 1), f32),
                   jax.ShapeDtypeStruct((nb, 2, H_A, 1, 1), f32)],
        compiler_params=pltpu.CompilerParams(vmem_limit_bytes=VMEM_LIMIT),
        name="mlstm",
    )(*args)


CONV_PAD = 8


def _short_conv_silu(x, w_ref, pad_scr, seq):
    pad_scr[:CONV_PAD, :] = jnp.zeros((CONV_PAD, LANES), f32)
    pad_scr[CONV_PAD + seq:, :] = jnp.zeros((CONV_PAD, LANES), f32)
    pad_scr[CONV_PAD:CONV_PAD + seq, :] = x
    acc = x * w_ref[CONV_K // 2:CONV_K // 2 + 1, :]
    for tap in range(CONV_K):
        delta = tap - CONV_K // 2
        if delta != 0:
            acc = acc + pad_scr[CONV_PAD + delta:CONV_PAD + delta + seq, :] * w_ref[tap:tap + 1, :]
    return _silu(acc)


def _l2_unit(y):
    return y * lax.rsqrt(jnp.sum(y * y, axis=-1, keepdims=True) + EPS)


def _split_bf16(x):
    hi = x.astype(bf16)
    return hi, (x - hi.astype(f32)).astype(bf16)


def _matmul_3pass(m, x):
    mh, ml = _split_bf16(m)
    xh, xl = _split_bf16(x)
    return jnp.dot(jnp.concatenate([mh, ml, mh], axis=1), jnp.concatenate([xh, xh, xl], axis=0),
                   preferred_element_type=f32)


def _unit_triangular_solves(ns, xs):
    levels = CHUNK.bit_length() - 1
    r = lax.broadcasted_iota(jnp.int32, (CHUNK, CHUNK), 0)
    c = lax.broadcasted_iota(jnp.int32, (CHUNK, CHUNK), 1)
    eye = jnp.where(r == c, 1.0, 0.0)
    ts = [eye + n for n in ns]
    ms = [_matmul_3pass(n, n) for n in ns]
    for lvl in range(1, levels):
        for i in range(len(ns)):
            if lvl < levels - 1:
                prod = _matmul_3pass(ms[i], jnp.concatenate([ts[i], ms[i]], axis=1))
                ts[i] = ts[i] + prod[:, :CHUNK]
                ms[i] = prod[:, CHUNK:]
            else:
                ts[i] = ts[i] + _matmul_3pass(ms[i], ts[i])
    return [_matmul_3pass(t, x) for t, x in zip(ts, xs)]


def _delta_kernel(*refs, seq, zero_init):
    if zero_init:
        (q0_ref, q1_ref, k0_ref, k1_ref, v0_ref, v1_ref, wq0_ref, wq1_ref, wk0_ref, wk1_ref, wv0_ref, wv1_ref,
         g_ref, par_ref, o_ref, sf_ref, q_scr, k_scr, v_scr, pad_scr) = refs
    else:
        (q0_ref, q1_ref, k0_ref, k1_ref, v0_ref, v1_ref, wq0_ref, wq1_ref, wk0_ref, wk1_ref, wv0_ref, wv1_ref,
         g_ref, par_ref, s0_ref, o_ref, sf_ref, q_scr, k_scr, v_scr, pad_scr) = refs
    nc = seq // CHUNK
    pair = pl.program_id(1)
    lane = lax.broadcasted_iota(jnp.int32, (1, LANES), 1)
    tril, triu, stril, striu = _chunk_masks()
    masks, smasks = (tril, triu), (stril, striu)
    masks16 = (_thrice(tril.astype(bf16)), _thrice(triu.astype(bf16)))
    sel = [_thrice(_head_lane_select(LANE_A + d * H_B, pair).astype(bf16)) for d in range(2)]
    neg_a_row = -jnp.exp(par_ref[0:1, :])
    dt_row = par_ref[1:2, :]
    streams = [(hh, d) for hh in range(2) for d in range(2)]

    for hh, (q_ref, k_ref, v_ref, wq_ref, wk_ref, wv_ref) in enumerate(
            ((q0_ref, k0_ref, v0_ref, wq0_ref, wk0_ref, wv0_ref), (q1_ref, k1_ref, v1_ref, wq1_ref, wk1_ref, wv1_ref))):
        q_scr[hh] = _l2_unit(_short_conv_silu(q_ref[...].astype(f32), wq_ref, pad_scr, seq)) * DK_B ** -0.5
        k_scr[hh] = _l2_unit(_short_conv_silu(k_ref[...].astype(f32), wk_ref, pad_scr, seq))
        v_scr[hh] = _short_conv_silu(v_ref[...].astype(f32), wv_ref, pad_scr, seq)

    o_ref[...] = jnp.zeros_like(o_ref)
    init = tuple(jnp.zeros((DK_B, DV_B), f32) if zero_init else s0_ref[d, hh] for hh, d in streams)

    def body(trip, carry):
        subs = range(_chunks_per_trip(nc))
        shared = {}
        for u in subs:
            for d in range(2):
                r0 = _chunk_start(trip, u, d, nc)
                gates = g_ref[pl.ds(r0, CHUNK), :]
                beta_all = jax.nn.sigmoid(gates)
                glog = neg_a_row * _softplus(gates + dt_row)
                cum = _mask_matmul_f32(masks16[d], glog)
                rows = _select_rows_f32(sel[d], cum)
                shared[u, d] = (r0, beta_all, cum, rows)
        chains = [(u, hh, d) for u in subs for hh, d in streams]
        st = {ch: {} for ch in chains}
        for ch in chains:
            u, hh, d = ch
            r0, beta_all, cum, rows = shared[u, d]
            s = st[ch]
            head = 2 * pair + hh
            beta = _lane_col(beta_all, lane, LANE_BETA + d * H_B + head)
            gcol = _lane_col(cum, lane, LANE_A + d * H_B + head)
            glast = gcol[CHUNK - 1:CHUNK, :] if d == 0 else gcol[0:1, :]
            q = q_scr[hh, pl.ds(r0, CHUNK), :]
            k = k_scr[hh, pl.ds(r0, CHUNK), :]
            v = v_scr[hh, pl.ds(r0, CHUNK), :]
            k16 = k.astype(bf16)
            kbeta = k * beta
            eg = jnp.exp(gcol)
            decay = jnp.exp(jnp.where(masks[d], gcol - rows[hh:hh + 1, :], -jnp.inf))
            kk = lax.dot_general(kbeta.astype(bf16), k16, NT_DIMS, preferred_element_type=f32)
            qk = lax.dot_general(q.astype(bf16), k16, NT_DIMS, preferred_element_type=f32)
            s["n"] = -jnp.where(smasks[d], kk * decay, 0.0)
            s["x"] = jnp.concatenate([v * beta, kbeta * eg], axis=1)
            s["qk"] = (qk * decay).astype(bf16)
            s["qg"] = (q * eg).astype(bf16)
            s["kd_t"] = (k * jnp.exp(glast - gcol)).T.astype(bf16)
            s["gl"] = jnp.exp(glast)
        solved = _unit_triangular_solves([st[ch]["n"] for ch in chains], [st[ch]["x"] for ch in chains])
        for ch, uw in zip(chains, solved):
            s = st[ch]
            s["u"] = uw[:, :DV_B]
            s["w_qg"] = jnp.concatenate([uw[:, DV_B:].astype(bf16), s["qg"]], axis=0)
        state = list(carry)
        for u in subs:
            ws = [jnp.dot(st[u, hh, d]["w_qg"], state[i].astype(bf16), preferred_element_type=f32)
                  for i, (hh, d) in enumerate(streams)]
            v_new = [(st[u, hh, d]["u"] - ws[i][:CHUNK]).astype(bf16) for i, (hh, d) in enumerate(streams)]
            for i, (hh, d) in enumerate(streams):
                s = st[u, hh, d]
                o = ws[i][CHUNK:] + jnp.dot(s["qk"], v_new[i], preferred_element_type=f32)
                r0 = shared[u, d][0]
                o_ref[pl.ds(r0, CHUNK), hh * DV_B:(hh + 1) * DV_B] += o
                state[i] = state[i] * s["gl"] + jnp.dot(s["kd_t"], v_new[i], preferred_element_type=f32)
        return tuple(state)

    final = lax.fori_loop(0, nc // _chunks_per_trip(nc), body, init)
    for i, (hh, d) in enumerate(streams):
        sf_ref[d, hh] = final[i]


def _delta_call(proj, conv_w8, gates, par, nb, seq, off, state):
    zero_init = state is None
    col0 = (2 * H_A * DQK_A + 2 * H_A * DV_A) // LANES
    blk = lambda col: pl.BlockSpec((seq, LANES), col)
    wblk = lambda col: pl.BlockSpec((8, LANES), col)
    in_specs = [blk(lambda b, p: (off + b, col0 + 2 * p)), blk(lambda b, p: (off + b, col0 + 2 * p + 1)),
                blk(lambda b, p: (off + b, col0 + H_B + 2 * p)), blk(lambda b, p: (off + b, col0 + H_B + 2 * p + 1)),
                blk(lambda b, p: (off + b, col0 + 2 * H_B + 2 * p)), blk(lambda b, p: (off + b, col0 + 2 * H_B + 2 * p + 1)),
                wblk(lambda b, p: (0, 2 * p)), wblk(lambda b, p: (0, 2 * p + 1)),
                wblk(lambda b, p: (0, H_B + 2 * p)), wblk(lambda b, p: (0, H_B + 2 * p + 1)),
                wblk(lambda b, p: (0, 2 * H_B + 2 * p)), wblk(lambda b, p: (0, 2 * H_B + 2 * p + 1)),
                blk(lambda b, p: (off + b, 0)),
                pl.BlockSpec((8, LANES), lambda b, p: (0, 0))]
    st_spec = pl.BlockSpec((None, 2, 2, DK_B, DV_B), lambda b, p: (b, 0, p, 0, 0))
    args = [proj] * 6 + [conv_w8] * 6 + [gates, par]
    if not zero_init:
        in_specs.append(st_spec)
        args.append(state)
    return pl.pallas_call(
        functools.partial(_delta_kernel, seq=seq, zero_init=zero_init),
        grid=(nb, H_B // 2),
        in_specs=in_specs,
        out_specs=[pl.BlockSpec((seq, 2 * DV_B), lambda b, p: (b, p)), st_spec],
        out_shape=[jax.ShapeDtypeStruct((nb * seq, H_B * DV_B), f32),
                   jax.ShapeDtypeStruct((nb, 2, H_B, DK_B, DV_B), f32)],
        scratch_shapes=[pltpu.VMEM((2, seq, LANES), f32)] * 3 + [pltpu.VMEM((seq + 2 * CONV_PAD, LANES), f32)],
        compiler_params=pltpu.CompilerParams(vmem_limit_bytes=VMEM_LIMIT),
        name="delta",
    )(*args)


def _head_rms(x):
    return x * lax.rsqrt(jnp.mean(x * x, axis=-1, keepdims=True) + EPS)


def _ab_out_kernel(x_ref, hp_ref, hs_ref, op_ref, os_ref, om_ref, zd_ref, gm_ref, gd_ref, w_ref, mod_ref, o_ref,
                   *, gate_row, tm):
    hsum = _prompt_or_sample(hp_ref, hs_ref, tm)
    osum = _prompt_or_sample(op_ref, os_ref, tm)
    parts = []
    for h in range(H_A):
        sl = slice(h * DV_A, (h + 1) * DV_A)
        parts.append(_head_rms(hsum[:, sl]) * gm_ref[:, sl] * jax.nn.sigmoid(om_ref[:, sl].astype(f32)))
    for h in range(H_B):
        sl = slice(h * DV_B, (h + 1) * DV_B)
        parts.append(_head_rms(osum[:, sl]) * gd_ref[:, sl] * _silu(zd_ref[:, sl].astype(f32)))
    cat = jnp.concatenate(parts, axis=1).astype(bf16)
    y = jnp.dot(cat, w_ref[...].astype(bf16), preferred_element_type=f32)
    o_ref[...] = x_ref[...] + mod_ref[gate_row:gate_row + 1, :] * y


def _ab_out_call(x, hp, hs, op, os_, proj, gm_row, gd_row, w_out, mod_l, gate_row, tm=512):
    wide = H_A * DV_A
    om_blk = (2 * H_A * DQK_A) // wide + 1
    zd_blk = AB_MAIN // wide - 1
    return pl.pallas_call(
        functools.partial(_ab_out_kernel, gate_row=gate_row, tm=tm),
        grid=(N_TOK // tm,),
        in_specs=[pl.BlockSpec((tm, D_MODEL), lambda i: (i, 0))] + _split_specs(tm, wide) + _split_specs(tm, wide) + [
            pl.BlockSpec((tm, wide), lambda i: (i, om_blk)),
            pl.BlockSpec((tm, wide), lambda i: (i, zd_blk)),
            pl.BlockSpec((1, wide), lambda i: (0, 0)),
            pl.BlockSpec((1, wide), lambda i: (0, 0)),
            pl.BlockSpec((2 * wide, D_MODEL), lambda i: (0, 0)),
            pl.BlockSpec((None, N_ADA, D_MODEL), lambda i: (_mod_row_index(i, tm), 0, 0))],
        out_specs=pl.BlockSpec((tm, D_MODEL), lambda i: (i, 0)),
        out_shape=jax.ShapeDtypeStruct((N_TOK, D_MODEL), f32),
        compiler_params=pltpu.CompilerParams(vmem_limit_bytes=VMEM_LIMIT),
        name="ab_out",
    )(x, hp, hs, op, os_, proj, proj, gm_row, gd_row, w_out, mod_l)


def _na_out_kernel(x_ref, ap_ref, as_ref, w_ref, mod_ref, o_ref, *, gate_row, tm):
    a = _prompt_or_sample(ap_ref, as_ref, tm).astype(bf16)
    y = jnp.dot(a, w_ref[...].astype(bf16), preferred_element_type=f32)
    o_ref[...] = x_ref[...] + mod_ref[gate_row:gate_row + 1, :] * y


def _na_out_call(x, attn_p, attn_s, w_out, mod_l, gate_row, tm=512):
    return pl.pallas_call(
        functools.partial(_na_out_kernel, gate_row=gate_row, tm=tm),
        grid=(N_TOK // tm,),
        in_specs=[pl.BlockSpec((tm, D_MODEL), lambda i: (i, 0))] + _split_specs(tm, NA_W) + [
            pl.BlockSpec((NA_W, D_MODEL), lambda i: (0, 0)),
            pl.BlockSpec((None, N_ADA, D_MODEL), lambda i: (_mod_row_index(i, tm), 0, 0))],
        out_specs=pl.BlockSpec((tm, D_MODEL), lambda i: (i, 0)),
        out_shape=jax.ShapeDtypeStruct((N_TOK, D_MODEL), f32),
        compiler_params=pltpu.CompilerParams(vmem_limit_bytes=VMEM_LIMIT),
        name="na_out",
    )(x, attn_p, attn_s, w_out, mod_l)


CTX_HEADS = 8


def _ctx_attn_kernel(q_ref, k_ref, v_ref, o_ref, nk_ref, nv_ref):
    low = lax.broadcasted_iota(jnp.int32, (1, LANES), 1) < DH_C
    for hh in range(CTX_HEADS):
        sl = slice(hh * DH_C, (hh + 1) * DH_C)
        nk_ref[hh] = k_ref[:, sl]
        nv_ref[hh] = v_ref[:, sl]
    for pair in range(CTX_HEADS // 2):
        sl = slice(pair * LANES, (pair + 1) * LANES)
        q = q_ref[:, sl]
        k = k_ref[:, sl].astype(bf16)
        v = v_ref[:, sl].astype(bf16)
        outs = []
        for hh in range(2):
            qh = jnp.where(low if hh == 0 else jnp.logical_not(low), q, 0.0).astype(bf16)
            s = lax.dot_general(qh, k, NT_DIMS, preferred_element_type=f32) * DH_C ** -0.5
            p = jnp.exp(s - jnp.max(s, axis=1, keepdims=True))
            o = jnp.dot(p.astype(bf16), v, preferred_element_type=f32)
            outs.append(o / jnp.sum(p, axis=1, keepdims=True))
        o_ref[:, sl] = jnp.where(low, outs[0], outs[1]).astype(o_ref.dtype)


def _ctx_attn_call(proj):
    w = CTX_HEADS * DH_C
    nblk = NA_W // w
    return pl.pallas_call(
        _ctx_attn_kernel,
        grid=(BATCH, nblk),
        in_specs=[pl.BlockSpec((SEQ, w), lambda b, j: (b, j)),
                  pl.BlockSpec((SEQ, w), lambda b, j: (b, nblk + j)),
                  pl.BlockSpec((SEQ, w), lambda b, j: (b, 2 * nblk + j))],
        out_specs=[pl.BlockSpec((SEQ, w), lambda b, j: (b, j)),
                   pl.BlockSpec((None, CTX_HEADS, SEQ, DH_C), lambda b, j: (b, j, 0, 0)),
                   pl.BlockSpec((None, CTX_HEADS, SEQ, DH_C), lambda b, j: (b, j, 0, 0))],
        out_shape=[jax.ShapeDtypeStruct((N_PROMPT, NA_W), bf16),
                   jax.ShapeDtypeStruct((BATCH, H_C, SEQ, DH_C), f32),
                   jax.ShapeDtypeStruct((BATCH, H_C, SEQ, DH_C), f32)],
        compiler_params=pltpu.CompilerParams(vmem_limit_bytes=VMEM_LIMIT),
        name="ctx_attn",
    )(proj, proj, proj)


QROWS = 4
QBLK = QROWS * GRID_W
KROWS_MID = QROWS + WIN_R - 1
N_RIDX = 2 * WIN_R - 1
N_CIDX = 2 * WIN_C - 1
N_QBLK = GRID_ROWS // QROWS


def _nattn_bias_tables(rb_ref, tb_scr, bmid_scr, btop_scr, bbot_scr):
    qc = lax.broadcasted_iota(jnp.int32, (GRID_W, GRID_W), 0)
    kc = lax.broadcasted_iota(jnp.int32, (GRID_W, GRID_W), 1)
    cs = jnp.clip(qc - WIN_C // 2, 0, GRID_W - WIN_C)
    valid = (kc >= cs) & (kc < cs + WIN_C)
    neg = jnp.full((GRID_W, GRID_W), -jnp.inf, f32)
    for hh in range(2):
        for ri in range(N_RIDX):
            row = jnp.broadcast_to(rb_ref[hh, ri:ri + 1, :], (GRID_W, LANES))
            tile = pltpu.roll(row, LANES - (WIN_C - 1), 1, stride=1, stride_axis=0)[:, :GRID_W]
            tb_scr[ri] = jnp.where(valid, tile, -jnp.inf)
        for i in range(QROWS):
            rs = slice(i * GRID_W, (i + 1) * GRID_W)
            for jj in range(KROWS_MID):
                inside = 0 <= jj - i < WIN_R
                bmid_scr[hh, rs, jj * GRID_W:(jj + 1) * GRID_W] = tb_scr[jj - i + WIN_R // 2 - 1] if inside else neg
            for jj in range(WIN_R):
                btop_scr[hh, rs, jj * GRID_W:(jj + 1) * GRID_W] = tb_scr[jj - i + WIN_R - 1]
                bbot_scr[hh, rs, jj * GRID_W:(jj + 1) * GRID_W] = tb_scr[jj - i + WIN_R // 2 - 1]


def _nattn_kernel(rb_ref, q_ref, k_ref, v_ref, ck_ref, cv_ref, o_ref, tb_scr, bmid_scr, btop_scr, bbot_scr):
    scale = DH_C ** -0.5

    @pl.when(pl.program_id(1) == 0)
    def _():
        _nattn_bias_tables(rb_ref, tb_scr, bmid_scr, btop_scr, bbot_scr)

    low = lax.broadcasted_iota(jnp.int32, (1, LANES), 1) < DH_C
    kctx = jnp.concatenate([ck_ref[0], ck_ref[1]], axis=1).astype(bf16)
    vctx = jnp.concatenate([cv_ref[0], cv_ref[1]], axis=1).astype(bf16)

    def block(q_start, k_start, k_rows, bias_scr):
        nk = k_rows * GRID_W
        q = q_ref[pl.ds(q_start, QBLK), :]
        ku = k_ref[pl.ds(k_start, nk), :].astype(bf16)
        vu = v_ref[pl.ds(k_start, nk), :].astype(bf16)
        outs = []
        for hh in range(2):
            qh = jnp.where(low if hh == 0 else jnp.logical_not(low), q, 0.0).astype(bf16)
            s_loc = lax.dot_general(qh, ku, NT_DIMS, preferred_element_type=f32) * scale + bias_scr[hh]
            s_ctx = lax.dot_general(qh, kctx, NT_DIMS, preferred_element_type=f32) * scale
            m = jnp.maximum(jnp.max(s_loc, axis=1, keepdims=True), jnp.max(s_ctx, axis=1, keepdims=True))
            p_loc = jnp.exp(s_loc - m)
            p_ctx = jnp.exp(s_ctx - m)
            denom = jnp.sum(p_loc, axis=1, keepdims=True) + jnp.sum(p_ctx, axis=1, keepdims=True)
            o = (jnp.dot(p_loc.astype(bf16), vu, preferred_element_type=f32)
                 + jnp.dot(p_ctx.astype(bf16), vctx, preferred_element_type=f32))
            outs.append(o / denom)
        o_ref[pl.ds(q_start, QBLK), :] = jnp.where(low, outs[0], outs[1]).astype(o_ref.dtype)

    block(0, 0, WIN_R, btop_scr)

    def mid_body(blk, _):
        q_start = pl.multiple_of(blk * QBLK, QBLK)
        k_start = pl.multiple_of(blk * QBLK - (WIN_R // 2) * GRID_W, QBLK)
        block(q_start, k_start, KROWS_MID, bmid_scr)
        return 0

    lax.fori_loop(1, N_QBLK - 1, mid_body, 0)
    block((N_QBLK - 1) * QBLK, (GRID_ROWS - WIN_R) * GRID_W, WIN_R, bbot_scr)


def _nattn_call(proj, cache_k, cache_v, rel_bias, off):
    npair = H_C // 2
    return pl.pallas_call(
        _nattn_kernel,
        grid=(npair, DEC_BATCH),
        in_specs=[pl.BlockSpec((2, 2 * WIN_R, LANES), lambda p, b: (p, 0, 0)),
                  pl.BlockSpec((DEC_SEQ, LANES), lambda p, b: (off + b, p)),
                  pl.BlockSpec((DEC_SEQ, LANES), lambda p, b: (off + b, npair + p)),
                  pl.BlockSpec((DEC_SEQ, LANES), lambda p, b: (off + b, 2 * npair + p)),
                  pl.BlockSpec((None, 2, PAST_LEN, DH_C), lambda p, b: (b, p, 0, 0)),
                  pl.BlockSpec((None, 2, PAST_LEN, DH_C), lambda p, b: (b, p, 0, 0))],
        out_specs=pl.BlockSpec((DEC_SEQ, LANES), lambda p, b: (b, p)),
        out_shape=jax.ShapeDtypeStruct((N_SAMPLE, NA_W), bf16),
        scratch_shapes=[pltpu.VMEM((N_RIDX, GRID_W, GRID_W), f32),
                        pltpu.VMEM((2, QBLK, KROWS_MID * GRID_W), f32),
                        pltpu.VMEM((2, QBLK, WIN_R * GRID_W), f32),
                        pltpu.VMEM((2, QBLK, WIN_R * GRID_W), f32)],
        compiler_params=pltpu.CompilerParams(dimension_semantics=("arbitrary", "arbitrary"),
                                             vmem_limit_bytes=VMEM_LIMIT),
        name="nattn",
    )(jnp.pad(rel_bias, ((0, 0), (0, 2 * WIN_R - N_RIDX), (0, LANES - N_CIDX))), proj, proj, proj, cache_k, cache_v)


def _lane_row(pieces):
    row = jnp.zeros((LANES,), f32)
    for off, vals in pieces:
        row = row.at[off:off + vals.shape[0]].set(vals.astype(f32))
    return row


def _param_rows(rows):
    out = jnp.zeros((8, LANES), f32)
    for r, row in enumerate(rows):
        out = out.at[r].set(row)
    return out


def kernel(x_prompt, x_sample, c, state_mlstm_C, state_mlstm_n, state_mlstm_m, state_delta_S, cache_na_k, cache_na_v, c_ctx, ada_w, ada_b, norm_g, ffn_wg, ffn_wu, ffn_wd, ab_w_in, ab_w_out, mlstm_b_i, mlstm_b_f, mlstm_norm_g, delta_conv_w, delta_a_log, delta_dt_bias, delta_norm_g, na_w_in, na_w_out, na_rel_bias, final_norm_g):
    xs = (x_prompt.reshape(N_PROMPT, D_MODEL), x_sample.reshape(N_SAMPLE, D_MODEL))
    mods = _ada_call(jnp.concatenate([c_ctx[None, :], c], axis=0), ada_w, ada_b)
    mods = mods.reshape(DEPTH, 3, N_ADA, D_MODEL)
    gf_row = final_norm_g.reshape(1, D_MODEL)
    s_off = N_PROMPT // DEC_SEQ
    new_c, new_n, new_m, new_s, new_k, new_v = [], [], [], [], [], []

    for l in range(DEPTH):
        mod_l = mods[l]
        a = l // 2
        x = _ffn_call(xs if l == 0 else (x,), mod_l, norm_g[l, 0].reshape(1, D_MODEL), ffn_wg, ffn_wu, ffn_wd, l, 0,
                      gf_row, rows=(0, 1, 2), final=False, split_out=False, tm=FFN_TM)[0]
        g_mix = norm_g[l, 1].reshape(1, D_MODEL)
        if l % 2 == 0:
            w_in = ab_w_in[a]
            i0 = 2 * H_A * DQK_A + 2 * H_A * DV_A
            d0 = i0 + 4 * H_A
            b0 = d0 + 2 * H_B * DK_B + 2 * H_B * DV_B
            w_main = jnp.concatenate([w_in[:, :i0], w_in[:, d0:b0]], axis=1)
            w_gate = jnp.concatenate([w_in[:, i0:d0], w_in[:, b0:],
                                      jnp.zeros((D_MODEL, LANES - 4 * H_A - 4 * H_B), f32)], axis=1)
            proj, gates = _proj_call(x, mod_l, g_mix, w_main, w_gate, rows=(3, 4), out_dtype=bf16)

            par_m = _param_rows([_lane_row([(LANE_I, mlstm_b_i[a].reshape(-1))]),
                                 _lane_row([(LANE_F, mlstm_b_f[a].reshape(-1))])])
            par_d = _param_rows([_lane_row([(LANE_A, delta_a_log[a].reshape(-1))]),
                                 _lane_row([(LANE_A, delta_dt_bias[a].reshape(-1))])])
            conv_w8 = jnp.concatenate([delta_conv_w[a], jnp.zeros((8 - CONV_K, 3 * H_B * DK_B), f32)], axis=0)

            hp, cn, nn_, mn = _mlstm_call(proj, gates, par_m, BATCH, SEQ, 0, None)
            st = (state_mlstm_C[:, a], state_mlstm_n[:, a][..., None], state_mlstm_m[:, a][..., None, None])
            hs, _, _, _ = _mlstm_call(proj, gates, par_m, DEC_BATCH, DEC_SEQ, s_off, st)
            op, sn = _delta_call(proj, conv_w8, gates, par_d, BATCH, SEQ, 0, None)
            os_, _ = _delta_call(proj, conv_w8, gates, par_d, DEC_BATCH, DEC_SEQ, s_off, state_delta_S[:, a])
            new_c.append(cn)
            new_n.append(nn_[..., 0])
            new_m.append(mn[..., 0, 0])
            new_s.append(sn)
            x = _ab_out_call(x, hp, hs, op, os_, proj, mlstm_norm_g[a].reshape(1, -1),
                             jnp.tile(delta_norm_g[a], H_B).reshape(1, -1), ab_w_out[a], mod_l, gate_row=5)
        else:
            proj = _proj_call(x, mod_l, g_mix, na_w_in[a], None, rows=(3, 4), out_dtype=f32)[0]
            attn_p, kp, vp = _ctx_attn_call(proj)
            attn_s = _nattn_call(proj, cache_na_k[:, a], cache_na_v[:, a], na_rel_bias[a], s_off)
            new_k.append(kp)
            new_v.append(vp)
            x = _na_out_call(x, attn_p, attn_s, na_w_out[a], mod_l, gate_row=5)
        last = l == DEPTH - 1
        outs = _ffn_call((x,), mod_l, norm_g[l, 2].reshape(1, D_MODEL), ffn_wg, ffn_wu, ffn_wd, l, 1,
                         gf_row, rows=(6, 7, 8), final=last, split_out=last, tm=FFN_TM_SPLIT if last else FFN_TM)
        x = outs[0]

    y_prompt = outs[0].reshape(BATCH, SEQ, D_MODEL)
    y_sample = outs[1].reshape(DEC_BATCH, DEC_SEQ, D_MODEL)
    return (y_prompt, y_sample, jnp.stack(new_c, axis=1), jnp.stack(new_n, axis=1), jnp.stack(new_m, axis=1),
            jnp.stack(new_s, axis=1), jnp.stack(new_k, axis=1), jnp.stack(new_v, axis=1))
```

```python
import functools

import jax
import jax.numpy as jnp
from jax import lax
from jax.experimental import pallas as pl
from jax.experimental.pallas import tpu as pltpu

f32 = jnp.float32
bf16 = jnp.bfloat16

D_MODEL = 1024
BATCH = 16
SEQ = 256
DEPTH = 2
DEC_BATCH = 2
DEC_SEQ = 2048
PAST_LEN = 256
GRID_W = 64
GRID_ROWS = DEC_SEQ // GRID_W
D_FF = 2816
N_ADA = 9
EPS = 1e-6
CHUNK = 64
H_A, DQK_A, DV_A = 4, 64, 128
H_B, DK_B, DV_B = 4, 128, 128
CONV_K = 5
H_C, DH_C = 16, 64
WIN_R, WIN_C = 8, 16
NA_W = H_C * DH_C
assert DH_C ** -0.5 == 2.0 ** -3 and DQK_A ** -0.5 == 2.0 ** -3

N_PROMPT = BATCH * SEQ
N_SAMPLE = DEC_BATCH * DEC_SEQ
N_TOK = N_PROMPT + N_SAMPLE
AB_MAIN = 2 * H_A * DQK_A + 2 * H_A * DV_A + 2 * H_B * DK_B + 2 * H_B * DV_B
LANES = 128
VMEM_LIMIT = 56 * 1024 * 1024

LANE_I, LANE_F, LANE_BETA, LANE_A = 0, 8, 16, 24

NT_DIMS = (((1,), (1,)), ((), ()))
TN_DIMS = (((0,), (0,)), ((), ()))

MAX_CHUNKS_PER_TRIP = 4
FFN_TM = 2048
FFN_TM_SPLIT = 1024
FFN_ROW_BLOCK = 512
PROJ_ROW_BLOCK = 256


def _softplus(x):
    return jnp.maximum(x, 0.0) + jnp.log1p(jnp.exp(-jnp.abs(x)))


def _log_sigmoid(x):
    return -_softplus(-x)


def _silu(x):
    return x * jax.nn.sigmoid(x)


def _mod_row_index(i, tm):
    n_p = N_PROMPT // tm
    per_b = DEC_SEQ // tm
    return jnp.where(i < n_p, 0, 1 + (i - n_p) // per_b)


def _modulated(x, g_row, shift_row, scale_row):
    y = x * lax.rsqrt(jnp.mean(x * x, axis=-1, keepdims=True) + EPS) * g_row
    return y * (1.0 + scale_row) + shift_row


def _ada_kernel(cb_ref, w_ref, b_ref, o_ref, s_scr):
    tn = w_ref.shape[-1]

    @pl.when((pl.program_id(0) == 0) & (pl.program_id(1) == 0))
    def _():
        cb = cb_ref[...]
        s_scr[...] = cb * jax.nn.sigmoid(cb)

    for j in range(tn // LANES):
        cols = slice(j * LANES, (j + 1) * LANES)
        w = w_ref[:, cols]
        for r in range(3):
            o_ref[r:r + 1, cols] = jnp.sum(w * s_scr[r], axis=0, keepdims=True) + b_ref[:, cols]


def _ada_call(cond3, ada_w, ada_b):
    tn = N_ADA * D_MODEL // 4
    cb = jnp.broadcast_to(cond3[:, :, None], (3, D_MODEL, LANES))
    return pl.pallas_call(
        _ada_kernel,
        grid=(DEPTH, N_ADA * D_MODEL // tn),
        in_specs=[pl.BlockSpec((3, D_MODEL, LANES), lambda l, j: (0, 0, 0)),
                  pl.BlockSpec((None, D_MODEL, tn), lambda l, j: (l, 0, j)),
                  pl.BlockSpec((None, 1, tn), lambda l, j: (l, 0, j))],
        out_specs=pl.BlockSpec((None, 3, tn), lambda l, j: (l, 0, j)),
        out_shape=jax.ShapeDtypeStruct((DEPTH, 3, N_ADA * D_MODEL), f32),
        scratch_shapes=[pltpu.VMEM((3, D_MODEL, LANES), f32)],
        compiler_params=pltpu.CompilerParams(dimension_semantics=("arbitrary", "arbitrary"),
                                             vmem_limit_bytes=VMEM_LIMIT),
        name="ada_mod",
    )(cb, ada_w, ada_b.reshape(DEPTH, 1, N_ADA * D_MODEL))


def _prompt_or_sample(p_ref, s_ref, tm):
    return jnp.where(pl.program_id(0) < N_PROMPT // tm, p_ref[...], s_ref[...])


def _split_specs(tm, width, n_grid_axes=1, single_buffered=False):
    n_p = N_PROMPT // tm
    kw = dict(pipeline_mode=pl.Buffered(1)) if single_buffered else {}
    if n_grid_axes == 1:
        return [pl.BlockSpec((tm, width), lambda i: (jnp.minimum(i, n_p - 1), 0), **kw),
                pl.BlockSpec((tm, width), lambda i: (jnp.maximum(i - n_p, 0), 0), **kw)]
    return [pl.BlockSpec((tm, width), lambda i, j: (jnp.minimum(i, n_p - 1), 0), **kw),
            pl.BlockSpec((tm, width), lambda i, j: (jnp.maximum(i - n_p, 0), 0), **kw)]


def _ffn_kernel(*refs, rows, final, split_in, split_out, tm):
    it = iter(refs)
    x_refs = [next(it) for _ in range(2 if split_in else 1)]
    mod_ref, g_ref, wg_ref, wu_ref, wd_ref, gf_ref = [next(it) for _ in range(6)]
    o_refs = [next(it) for _ in range(2 if split_out else 1)]
    h_scr = next(it)
    acc_ref = next(it) if split_out else o_refs[0]
    j = pl.program_id(1)
    last_j = pl.num_programs(1) - 1
    is_prompt = pl.program_id(0) < N_PROMPT // tm

    def x_rows(rs):
        if split_in:
            return jnp.where(is_prompt, x_refs[0][rs, :], x_refs[1][rs, :])
        return x_refs[0][rs, :]

    def step(first, last, out_ref):
        wg = wg_ref[...].astype(bf16)
        wu = wu_ref[...].astype(bf16)
        wd = wd_ref[...].astype(bf16)
        for r in range(tm // FFN_ROW_BLOCK):
            rs = slice(r * FFN_ROW_BLOCK, (r + 1) * FFN_ROW_BLOCK)
            if first:
                h = _modulated(x_rows(rs), g_ref[...], mod_ref[rows[0]:rows[0] + 1, :],
                               mod_ref[rows[1]:rows[1] + 1, :]).astype(bf16)
                h_scr[rs, :] = h
            else:
                h = h_scr[rs, :]
            g = jnp.dot(h, wg, preferred_element_type=f32)
            u = jnp.dot(h, wu, preferred_element_type=f32)
            part = jnp.dot((_silu(g) * u).astype(bf16), wd, preferred_element_type=f32)
            acc = part if first else acc_ref[rs, :] + part
            if last:
                xn = x_rows(rs) + (0.5 * mod_ref[rows[2]:rows[2] + 1, :]) * acc
                if final:
                    xn = xn * lax.rsqrt(jnp.mean(xn * xn, axis=-1, keepdims=True) + EPS) * gf_ref[...]
                out_ref[rs, :] = xn
            else:
                acc_ref[rs, :] = acc

    pl.when(j == 0)(lambda: step(True, False, None))
    pl.when((j > 0) & (j < last_j))(lambda: step(False, False, None))
    if split_out:
        pl.when((j == last_j) & is_prompt)(lambda: step(False, True, o_refs[0]))
        pl.when((j == last_j) & jnp.logical_not(is_prompt))(lambda: step(False, True, o_refs[1]))
    else:
        pl.when(j == last_j)(lambda: step(False, True, o_refs[0]))


def _ffn_call(xs, mod_l, g_row, ffn_wg, ffn_wu, ffn_wd, layer, half, gf_row, rows, final, split_out, tm, tf=256):
    split_in = len(xs) == 2
    x_specs = (_split_specs(tm, D_MODEL, 2, single_buffered=True) if split_in
               else [pl.BlockSpec((tm, D_MODEL), lambda i, j: (i, 0))])
    if split_out:
        out_specs = _split_specs(tm, D_MODEL, 2)
        out_shape = [jax.ShapeDtypeStruct((N_PROMPT, D_MODEL), f32), jax.ShapeDtypeStruct((N_SAMPLE, D_MODEL), f32)]
        scratch = [pltpu.VMEM((tm, D_MODEL), bf16), pltpu.VMEM((tm, D_MODEL), f32)]
    else:
        out_specs = [pl.BlockSpec((tm, D_MODEL), lambda i, j: (i, 0))]
        out_shape = [jax.ShapeDtypeStruct((N_TOK, D_MODEL), f32)]
        scratch = [pltpu.VMEM((tm, D_MODEL), bf16)]
    return pl.pallas_call(
        functools.partial(_ffn_kernel, rows=rows, final=final, split_in=split_in, split_out=split_out, tm=tm),
        grid=(N_TOK // tm, D_FF // tf),
        in_specs=x_specs + [
            pl.BlockSpec((None, N_ADA, D_MODEL), lambda i, j: (_mod_row_index(i, tm), 0, 0)),
            pl.BlockSpec((1, D_MODEL), lambda i, j: (0, 0)),
            pl.BlockSpec((None, None, D_MODEL, tf), lambda i, j: (layer, half, 0, j)),
            pl.BlockSpec((None, None, D_MODEL, tf), lambda i, j: (layer, half, 0, j)),
            pl.BlockSpec((None, None, tf, D_MODEL), lambda i, j: (layer, half, j, 0)),
            pl.BlockSpec((1, D_MODEL), lambda i, j: (0, 0))],
        out_specs=out_specs, out_shape=out_shape, scratch_shapes=scratch,
        compiler_params=pltpu.CompilerParams(dimension_semantics=("parallel", "arbitrary"),
                                             vmem_limit_bytes=VMEM_LIMIT),
        name="ffn",
    )(*xs, mod_l, g_row, ffn_wg, ffn_wu, ffn_wd, gf_row)


def _proj_kernel(*refs, rows, with_gates):
    if with_gates:
        x_ref, mod_ref, g_ref, w_ref, wgate_ref, o_ref, og_ref, h_scr = refs
    else:
        x_ref, mod_ref, g_ref, w_ref, o_ref, h_scr = refs
    j = pl.program_id(1)
    tm = x_ref.shape[0]

    @pl.when(j == 0)
    def _():
        w = w_ref[...].astype(bf16)
        for r in range(tm // PROJ_ROW_BLOCK):
            rs = slice(r * PROJ_ROW_BLOCK, (r + 1) * PROJ_ROW_BLOCK)
            hb = _modulated(x_ref[rs, :], g_ref[...], mod_ref[rows[0]:rows[0] + 1, :],
                            mod_ref[rows[1]:rows[1] + 1, :]).astype(bf16)
            h_scr[rs, :] = hb
            o_ref[rs, :] = jnp.dot(hb, w, preferred_element_type=f32).astype(o_ref.dtype)
            if with_gates:
                og_ref[rs, :] = jnp.dot(hb, wgate_ref[...].astype(bf16), preferred_element_type=f32)

    @pl.when(j > 0)
    def _():
        o_ref[...] = jnp.dot(h_scr[...], w_ref[...].astype(bf16), preferred_element_type=f32).astype(o_ref.dtype)


AB_I0 = 2 * H_A * DQK_A + 2 * H_A * DV_A
AB_D0 = AB_I0 + 4 * H_A
AB_B0 = AB_D0 + 2 * H_B * DK_B + 2 * H_B * DV_B
AB_IN = AB_B0 + 4 * H_B


def _ab_weight_kernel(w_ref, wm_ref, wg_ref):
    wm_ref[:, :AB_I0] = w_ref[:, :AB_I0]
    wm_ref[:, AB_I0:] = w_ref[:, AB_D0:AB_B0]
    pad = jnp.zeros((w_ref.shape[0], LANES - 4 * H_A - 4 * H_B), f32)
    wg_ref[...] = jnp.concatenate([w_ref[:, AB_I0:AB_D0], w_ref[:, AB_B0:], pad], axis=1)


def _ab_weight_call(ab_w_in, layer, tk=256):
    return pl.pallas_call(
        _ab_weight_kernel,
        grid=(D_MODEL // tk,),
        in_specs=[pl.BlockSpec((None, tk, AB_IN), lambda i: (layer, i, 0))],
        out_specs=[pl.BlockSpec((tk, AB_MAIN), lambda i: (i, 0)), pl.BlockSpec((tk, LANES), lambda i: (i, 0))],
        out_shape=[jax.ShapeDtypeStruct((D_MODEL, AB_MAIN), f32), jax.ShapeDtypeStruct((D_MODEL, LANES), f32)],
        compiler_params=pltpu.CompilerParams(vmem_limit_bytes=VMEM_LIMIT),
        name="ab_weights",
    )(ab_w_in)


def _column_tile(n, cap):
    return max(t for t in range(LANES, cap + 1, LANES) if n % t == 0)


def _proj_call(x, mod_l, g_row, w, w_gate, rows, out_dtype, tm=2048):
    n = w.shape[1]
    tn = _column_tile(n, 1024)
    with_gates = w_gate is not None
    in_specs = [pl.BlockSpec((tm, D_MODEL), lambda i, j: (i, 0)),
                pl.BlockSpec((None, N_ADA, D_MODEL), lambda i, j: (_mod_row_index(i, tm), 0, 0)),
                pl.BlockSpec((1, D_MODEL), lambda i, j: (0, 0)),
                pl.BlockSpec((D_MODEL, tn), lambda i, j: (0, j))]
    out_specs = [pl.BlockSpec((tm, tn), lambda i, j: (i, j))]
    out_shape = [jax.ShapeDtypeStruct((N_TOK, n), out_dtype)]
    args = [x, mod_l, g_row, w]
    if with_gates:
        in_specs.append(pl.BlockSpec((D_MODEL, LANES), lambda i, j: (0, 0)))
        out_specs.append(pl.BlockSpec((tm, LANES), lambda i, j: (i, 0)))
        out_shape.append(jax.ShapeDtypeStruct((N_TOK, LANES), f32))
        args.append(w_gate)
    return pl.pallas_call(
        functools.partial(_proj_kernel, rows=rows, with_gates=with_gates),
        grid=(N_TOK // tm, n // tn),
        in_specs=in_specs, out_specs=out_specs, out_shape=out_shape,
        scratch_shapes=[pltpu.VMEM((tm, D_MODEL), bf16)],
        compiler_params=pltpu.CompilerParams(dimension_semantics=("parallel", "arbitrary"),
                                             vmem_limit_bytes=VMEM_LIMIT),
        name="in_proj",
    )(*args)


def _chunk_masks():
    r = lax.broadcasted_iota(jnp.int32, (CHUNK, CHUNK), 0)
    c = lax.broadcasted_iota(jnp.int32, (CHUNK, CHUNK), 1)
    return r >= c, r <= c, r > c, r < c


def _lane_col(x, lane, j):
    return jnp.sum(jnp.where(lane == j, x, 0.0), axis=1, keepdims=True)


def _head_lane_select(base, pair):
    r = lax.broadcasted_iota(jnp.int32, (8, LANES), 0)
    ln = lax.broadcasted_iota(jnp.int32, (8, LANES), 1)
    return jnp.where((ln == base + 2 * pair + r) & (r < 2), 1.0, 0.0)


def _split3_bf16(x):
    hi = x.astype(bf16)
    r = x - hi.astype(f32)
    mid = r.astype(bf16)
    return hi, mid, (r - mid.astype(f32)).astype(bf16)


def _thrice(a16):
    return jnp.concatenate([a16] * 3, axis=1)


def _mask_matmul_f32(mask16x3, x):
    return jnp.dot(mask16x3, jnp.concatenate(_split3_bf16(x), axis=0), preferred_element_type=f32)


def _select_rows_f32(sel16x3, x):
    return lax.dot_general(sel16x3, jnp.concatenate(_split3_bf16(x), axis=1), NT_DIMS, preferred_element_type=f32)


def _chunks_per_trip(nc):
    return min(MAX_CHUNKS_PER_TRIP, nc)


def _chunk_start(trip, u, d, nc):
    step = trip * _chunks_per_trip(nc) + u
    c = step if d == 0 else nc - 1 - step
    return pl.multiple_of(c * CHUNK, CHUNK)


def _mlstm_kernel(*refs, seq, zero_init):
    if zero_init:
        qp_ref, kp_ref, v0_ref, v1_ref, g_ref, par_ref, h_ref, cf_ref, nf_ref, mf_ref = refs
    else:
        (qp_ref, kp_ref, v0_ref, v1_ref, g_ref, par_ref, c0_ref, n0_ref, m0_ref,
         h_ref, cf_ref, nf_ref, mf_ref) = refs
    nc = seq // CHUNK
    pair = pl.program_id(1)
    lane = lax.broadcasted_iota(jnp.int32, (1, LANES), 1)
    tril, triu, _, _ = _chunk_masks()
    masks = (tril, triu)
    masks16 = (_thrice(tril.astype(bf16)), _thrice(triu.astype(bf16)))
    sel = [_thrice(_head_lane_select(LANE_F + d * H_A, pair).astype(bf16)) for d in range(2)]
    bi_row = par_ref[0:1, :]
    bf_row = par_ref[1:2, :]
    ones_col = jnp.where(lane == 0, 1.0, 0.0) + jnp.zeros((CHUNK, LANES), f32)
    v_refs = (v0_ref, v1_ref)
    streams = [(hh, d) for hh in range(2) for d in range(2)]

    h_ref[...] = jnp.zeros_like(h_ref)

    init = []
    for hh, d in streams:
        if zero_init:
            init.append((jnp.zeros((DQK_A, 2 * LANES), f32), jnp.zeros((1, 1), f32)))
        else:
            n_aug = jnp.where(lane == 0, n0_ref[d, hh], 0.0)
            init.append((jnp.concatenate([c0_ref[d, hh], n_aug], axis=1), m0_ref[d, hh]))

    def body(trip, carry):
        subs = range(_chunks_per_trip(nc))
        shared = {}
        for u in subs:
            for d in range(2):
                r0 = _chunk_start(trip, u, d, nc)
                gates = g_ref[pl.ds(r0, CHUNK), :]
                gi = gates + bi_row
                gf = _log_sigmoid(gates + bf_row)
                cum = _mask_matmul_f32(masks16[d], gf)
                stack = jnp.concatenate([cum, pltpu.roll(gi, LANE_F - LANE_I, axis=1)], axis=0)
                rows = _select_rows_f32(sel[d], stack)
                total = cum[CHUNK - 1:CHUNK, :] if d == 0 else cum[0:1, :]
                shared[u, d] = (r0, gi, cum, rows, total)
        chains = [(u, hh, d) for u in subs for hh, d in streams]
        st = {ch: {} for ch in chains}
        for ch in chains:
            u, hh, d = ch
            r0 = shared[u, d][0]
            s = st[ch]
            s["q"] = (qp_ref[pl.ds(r0, CHUNK), hh * DQK_A:(hh + 1) * DQK_A] * DQK_A ** -0.5).astype(bf16)
            s["k"] = kp_ref[pl.ds(r0, CHUNK), hh * DQK_A:(hh + 1) * DQK_A].astype(bf16)
            s["v_aug"] = jnp.concatenate([v_refs[hh][pl.ds(r0, CHUNK), :].astype(f32), ones_col], axis=1)
            s["qk"] = lax.dot_general(s["q"], s["k"], NT_DIMS, preferred_element_type=f32)
        for ch in chains:
            u, hh, d = ch
            _, gi, cum, rows, total = shared[u, d]
            s = st[ch]
            head = 2 * pair + hh
            jf = LANE_F + d * H_A + head
            ji = LANE_I + d * H_A + head
            bcol = _lane_col(cum, lane, jf)
            icol = _lane_col(gi, lane, ji)
            dmat = jnp.where(masks[d], bcol - rows[hh:hh + 1, :CHUNK] + rows[hh:hh + 1, CHUNK:], -jnp.inf)
            m_loc = jnp.max(dmat, axis=1, keepdims=True)
            s["sw"] = (s["qk"] * jnp.exp(dmat - m_loc)).astype(bf16)
            blast = _lane_col(total, lane, jf)
            gs = blast - bcol + icol
            ms_loc = jnp.max(gs, axis=0, keepdims=True)
            s["wv"] = (jnp.exp(gs - ms_loc) * s["v_aug"]).astype(bf16)
            s.update(bcol=bcol, m_loc=m_loc, blast=blast, ms_loc=ms_loc)
        for ch in chains:
            s = st[ch]
            s["num"] = jnp.dot(s["sw"], s["v_aug"].astype(bf16), preferred_element_type=f32)
            s["kv"] = lax.dot_general(s["k"], s["wv"], TN_DIMS, preferred_element_type=f32)
        state = list(carry)
        for u in subs:
            qc = [jnp.dot(st[u, hh, d]["q"], state[i][0].astype(bf16), preferred_element_type=f32)
                  for i, (hh, d) in enumerate(streams)]
            for i, (hh, d) in enumerate(streams):
                s = st[u, hh, d]
                c_aug, m = state[i]
                m_inter = s["bcol"] + m
                m_t = jnp.maximum(m_inter, s["m_loc"])
                nd = jnp.exp(s["m_loc"] - m_t) * s["num"] + jnp.exp(m_inter - m_t) * qc[i]
                den = nd[:, DV_A:DV_A + 1]
                hval = nd[:, :DV_A] / jnp.maximum(jnp.abs(den), jnp.exp(-m_t))
                r0 = shared[u, d][0]
                h_ref[pl.ds(r0, CHUNK), hh * DV_A:(hh + 1) * DV_A] += hval
                m_new = jnp.maximum(s["blast"] + m, s["ms_loc"])
                c_new = jnp.exp(s["blast"] + m - m_new) * c_aug + jnp.exp(s["ms_loc"] - m_new) * s["kv"]
                state[i] = (c_new, m_new)
        return tuple(state)

    final = lax.fori_loop(0, nc // _chunks_per_trip(nc), body, tuple(init))
    for i, (hh, d) in enumerate(streams):
        c_aug, m = final[i]
        cf_ref[d, hh] = c_aug[:, :DV_A]
        nf_ref[d, hh] = c_aug[:, DV_A:DV_A + 1]
        mf_ref[d, hh] = m


def _mlstm_call(proj, gates, par, nb, seq, off, state):
    zero_init = state is None
    blk = lambda col: pl.BlockSpec((seq, LANES), col)
    in_specs = [blk(lambda b, p: (off + b, p)),
                blk(lambda b, p: (off + b, 2 + p)),
                blk(lambda b, p: (off + b, 4 + 2 * p)),
                blk(lambda b, p: (off + b, 5 + 2 * p)),
                blk(lambda b, p: (off + b, 0)),
                pl.BlockSpec((8, LANES), lambda b, p: (0, 0))]
    st_specs = [pl.BlockSpec((None, 2, 2, DQK_A, DV_A), lambda b, p: (b, 0, p, 0, 0)),
                pl.BlockSpec((None, 2, 2, DQK_A, 1), lambda b, p: (b, 0, p, 0, 0)),
                pl.BlockSpec((None, 2, 2, 1, 1), lambda b, p: (b, 0, p, 0, 0))]
    args = [proj, proj, proj, proj, gates, par]
    if not zero_init:
        in_specs += st_specs
        args += list(state)
    return pl.pallas_call(
        functools.partial(_mlstm_kernel, seq=seq, zero_init=zero_init),
        grid=(nb, H_A // 2),
        in_specs=in_specs,
        out_specs=[pl.BlockSpec((seq, 2 * DV_A), lambda b, p: (b, p))] + st_specs,
        out_shape=[jax.ShapeDtypeStruct((nb * seq, H_A * DV_A), f32),
                   jax.ShapeDtypeStruct((nb, 2, H_A, DQK_A, DV_A), f32),
                   jax.ShapeDtypeStruct((nb, 2, H_A, DQK_A, 1), f32),
                   jax.ShapeDtypeStruct((nb, 2, H_A, 1, 1), f32)],
        compiler_params=pltpu.CompilerParams(vmem_limit_bytes=VMEM_LIMIT),
        name="mlstm",
    )(*args)


CONV_PAD = 8


def _short_conv_silu(x, w_ref, pad_scr, seq):
    pad_scr[:CONV_PAD, :] = jnp.zeros((CONV_PAD, LANES), f32)
    pad_scr[CONV_PAD + seq:, :] = jnp.zeros((CONV_PAD, LANES), f32)
    pad_scr[CONV_PAD:CONV_PAD + seq, :] = x
    acc = x * w_ref[CONV_K // 2:CONV_K // 2 + 1, :]
    for tap in range(CONV_K):
        delta = tap - CONV_K // 2
        if delta != 0:
            acc = acc + pad_scr[CONV_PAD + delta:CONV_PAD + delta + seq, :] * w_ref[tap:tap + 1, :]
    return _silu(acc)


def _l2_unit(y):
    return y * lax.rsqrt(jnp.sum(y * y, axis=-1, keepdims=True) + EPS)


def _split_bf16(x):
    hi = x.astype(bf16)
    return hi, (x - hi.astype(f32)).astype(bf16)


def _matmul_3pass(m, x):
    mh, ml = _split_bf16(m)
    xh, xl = _split_bf16(x)
    return jnp.dot(jnp.concatenate([mh, ml, mh], axis=1), jnp.concatenate([xh, xh, xl], axis=0),
                   preferred_element_type=f32)


def _unit_triangular_solves(ns, xs):
    levels = CHUNK.bit_length() - 1
    r = lax.broadcasted_iota(jnp.int32, (CHUNK, CHUNK), 0)
    c = lax.broadcasted_iota(jnp.int32, (CHUNK, CHUNK), 1)
    eye = jnp.where(r == c, 1.0, 0.0)
    ts = [eye + n for n in ns]
    ms = [_matmul_3pass(n, n) for n in ns]
    for lvl in range(1, levels):
        for i in range(len(ns)):
            if lvl < levels - 1:
                prod = _matmul_3pass(ms[i], jnp.concatenate([ts[i], ms[i]], axis=1))
                ts[i] = ts[i] + prod[:, :CHUNK]
                ms[i] = prod[:, CHUNK:]
            else:
                ts[i] = ts[i] + _matmul_3pass(ms[i], ts[i])
    return [_matmul_3pass(t, x) for t, x in zip(ts, xs)]


def _delta_kernel(*refs, seq, zero_init):
    if zero_init:
        (q0_ref, q1_ref, k0_ref, k1_ref, v0_ref, v1_ref, wq0_ref, wq1_ref, wk0_ref, wk1_ref, wv0_ref, wv1_ref,
         g_ref, par_ref, o_ref, sf_ref, q_scr, k_scr, v_scr, pad_scr) = refs
    else:
        (q0_ref, q1_ref, k0_ref, k1_ref, v0_ref, v1_ref, wq0_ref, wq1_ref, wk0_ref, wk1_ref, wv0_ref, wv1_ref,
         g_ref, par_ref, s0_ref, o_ref, sf_ref, q_scr, k_scr, v_scr, pad_scr) = refs
    nc = seq // CHUNK
    pair = pl.program_id(1)
    lane = lax.broadcasted_iota(jnp.int32, (1, LANES), 1)
    tril, triu, stril, striu = _chunk_masks()
    masks, smasks = (tril, triu), (stril, striu)
    masks16 = (_thrice(tril.astype(bf16)), _thrice(triu.astype(bf16)))
    sel = [_thrice(_head_lane_select(LANE_A + d * H_B, pair).astype(bf16)) for d in range(2)]
    neg_a_row = -jnp.exp(par_ref[0:1, :])
    dt_row = par_ref[1:2, :]
    streams = [(hh, d) for hh in range(2) for d in range(2)]

    for hh, (q_ref, k_ref, v_ref, wq_ref, wk_ref, wv_ref) in enumerate(
            ((q0_ref, k0_ref, v0_ref, wq0_ref, wk0_ref, wv0_ref), (q1_ref, k1_ref, v1_ref, wq1_ref, wk1_ref, wv1_ref))):
        q_scr[hh] = _l2_unit(_short_conv_silu(q_ref[...].astype(f32), wq_ref, pad_scr, seq)) * DK_B ** -0.5
        k_scr[hh] = _l2_unit(_short_conv_silu(k_ref[...].astype(f32), wk_ref, pad_scr, seq))
        v_scr[hh] = _short_conv_silu(v_ref[...].astype(f32), wv_ref, pad_scr, seq)

    o_ref[...] = jnp.zeros_like(o_ref)
    init = tuple(jnp.zeros((DK_B, DV_B), f32) if zero_init else s0_ref[d, hh] for hh, d in streams)

    def body(trip, carry):
        subs = range(_chunks_per_trip(nc))
        shared = {}
        for u in subs:
            for d in range(2):
                r0 = _chunk_start(trip, u, d, nc)
                gates = g_ref[pl.ds(r0, CHUNK), :]
                beta_all = jax.nn.sigmoid(gates)
                glog = neg_a_row * _softplus(gates + dt_row)
                cum = _mask_matmul_f32(masks16[d], glog)
                rows = _select_rows_f32(sel[d], cum)
                shared[u, d] = (r0, beta_all, cum, rows)
        chains = [(u, hh, d) for u in subs for hh, d in streams]
        st = {ch: {} for ch in chains}
        for ch in chains:
            u, hh, d = ch
            r0, beta_all, cum, rows = shared[u, d]
            s = st[ch]
            head = 2 * pair + hh
            beta = _lane_col(beta_all, lane, LANE_BETA + d * H_B + head)
            gcol = _lane_col(cum, lane, LANE_A + d * H_B + head)
            glast = gcol[CHUNK - 1:CHUNK, :] if d == 0 else gcol[0:1, :]
            q = q_scr[hh, pl.ds(r0, CHUNK), :]
            k = k_scr[hh, pl.ds(r0, CHUNK), :]
            v = v_scr[hh, pl.ds(r0, CHUNK), :]
            k16 = k.astype(bf16)
            kbeta = k * beta
            eg = jnp.exp(gcol)
            decay = jnp.exp(jnp.where(masks[d], gcol - rows[hh:hh + 1, :], -jnp.inf))
            kk = lax.dot_general(kbeta.astype(bf16), k16, NT_DIMS, preferred_element_type=f32)
            qk = lax.dot_general(q.astype(bf16), k16, NT_DIMS, preferred_element_type=f32)
            s["n"] = -jnp.where(smasks[d], kk * decay, 0.0)
            s["x"] = jnp.concatenate([v * beta, kbeta * eg], axis=1)
            s["qk"] = (qk * decay).astype(bf16)
            s["qg"] = (q * eg).astype(bf16)
            s["kd_t"] = (k * jnp.exp(glast - gcol)).T.astype(bf16)
            s["gl"] = jnp.exp(glast)
        solved = _unit_triangular_solves([st[ch]["n"] for ch in chains], [st[ch]["x"] for ch in chains])
        for ch, uw in zip(chains, solved):
            s = st[ch]
            s["u"] = uw[:, :DV_B]
            s["w_qg"] = jnp.concatenate([uw[:, DV_B:].astype(bf16), s["qg"]], axis=0)
        state = list(carry)
        for u in subs:
            ws = [jnp.dot(st[u, hh, d]["w_qg"], state[i].astype(bf16), preferred_element_type=f32)
                  for i, (hh, d) in enumerate(streams)]
            v_new = [(st[u, hh, d]["u"] - ws[i][:CHUNK]).astype(bf16) for i, (hh, d) in enumerate(streams)]
            for i, (hh, d) in enumerate(streams):
                s = st[u, hh, d]
                o = ws[i][CHUNK:] + jnp.dot(s["qk"], v_new[i], preferred_element_type=f32)
                r0 = shared[u, d][0]
                o_ref[pl.ds(r0, CHUNK), hh * DV_B:(hh + 1) * DV_B] += o
                state[i] = state[i] * s["gl"] + jnp.dot(s["kd_t"], v_new[i], preferred_element_type=f32)
        return tuple(state)

    final = lax.fori_loop(0, nc // _chunks_per_trip(nc), body, init)
    for i, (hh, d) in enumerate(streams):
        sf_ref[d, hh] = final[i]


def _delta_call(proj, conv_w8, gates, par, nb, seq, off, state):
    zero_init = state is None
    col0 = (2 * H_A * DQK_A + 2 * H_A * DV_A) // LANES
    blk = lambda col: pl.BlockSpec((seq, LANES), col)
    wblk = lambda col: pl.BlockSpec((8, LANES), col)
    in_specs = [blk(lambda b, p: (off + b, col0 + 2 * p)), blk(lambda b, p: (off + b, col0 + 2 * p + 1)),
                blk(lambda b, p: (off + b, col0 + H_B + 2 * p)), blk(lambda b, p: (off + b, col0 + H_B + 2 * p + 1)),
                blk(lambda b, p: (off + b, col0 + 2 * H_B + 2 * p)), blk(lambda b, p: (off + b, col0 + 2 * H_B + 2 * p + 1)),
                wblk(lambda b, p: (0, 2 * p)), wblk(lambda b, p: (0, 2 * p + 1)),
                wblk(lambda b, p: (0, H_B + 2 * p)), wblk(lambda b, p: (0, H_B + 2 * p + 1)),
                wblk(lambda b, p: (0, 2 * H_B + 2 * p)), wblk(lambda b, p: (0, 2 * H_B + 2 * p + 1)),
                blk(lambda b, p: (off + b, 0)),
                pl.BlockSpec((8, LANES), lambda b, p: (0, 0))]
    st_spec = pl.BlockSpec((None, 2, 2, DK_B, DV_B), lambda b, p: (b, 0, p, 0, 0))
    args = [proj] * 6 + [conv_w8] * 6 + [gates, par]
    if not zero_init:
        in_specs.append(st_spec)
        args.append(state)
    return pl.pallas_call(
        functools.partial(_delta_kernel, seq=seq, zero_init=zero_init),
        grid=(nb, H_B // 2),
        in_specs=in_specs,
        out_specs=[pl.BlockSpec((seq, 2 * DV_B), lambda b, p: (b, p)), st_spec],
        out_shape=[jax.ShapeDtypeStruct((nb * seq, H_B * DV_B), f32),
                   jax.ShapeDtypeStruct((nb, 2, H_B, DK_B, DV_B), f32)],
        scratch_shapes=[pltpu.VMEM((2, seq, LANES), f32)] * 3 + [pltpu.VMEM((seq + 2 * CONV_PAD, LANES), f32)],
        compiler_params=pltpu.CompilerParams(vmem_limit_bytes=VMEM_LIMIT),
        name="delta",
    )(*args)


def _head_rms(x):
    return x * lax.rsqrt(jnp.mean(x * x, axis=-1, keepdims=True) + EPS)


def _ab_out_kernel(x_ref, hp_ref, hs_ref, op_ref, os_ref, om_ref, zd_ref, gm_ref, gd_ref, w_ref, mod_ref, o_ref,
                   *, gate_row, tm):
    hsum = _prompt_or_sample(hp_ref, hs_ref, tm)
    osum = _prompt_or_sample(op_ref, os_ref, tm)
    parts = []
    for h in range(H_A):
        sl = slice(h * DV_A, (h + 1) * DV_A)
        parts.append(_head_rms(hsum[:, sl]) * gm_ref[:, sl] * jax.nn.sigmoid(om_ref[:, sl].astype(f32)))
    for h in range(H_B):
        sl = slice(h * DV_B, (h + 1) * DV_B)
        parts.append(_head_rms(osum[:, sl]) * gd_ref[:, sl] * _silu(zd_ref[:, sl].astype(f32)))
    cat = jnp.concatenate(parts, axis=1).astype(bf16)
    y = jnp.dot(cat, w_ref[...].astype(bf16), preferred_element_type=f32)
    o_ref[...] = x_ref[...] + mod_ref[gate_row:gate_row + 1, :] * y


def _ab_out_call(x, hp, hs, op, os_, proj, gm_row, gd_row, w_out, mod_l, gate_row, tm=512):
    wide = H_A * DV_A
    om_blk = (2 * H_A * DQK_A) // wide + 1
    zd_blk = AB_MAIN // wide - 1
    return pl.pallas_call(
        functools.partial(_ab_out_kernel, gate_row=gate_row, tm=tm),
        grid=(N_TOK // tm,),
        in_specs=[pl.BlockSpec((tm, D_MODEL), lambda i: (i, 0))] + _split_specs(tm, wide) + _split_specs(tm, wide) + [
            pl.BlockSpec((tm, wide), lambda i: (i, om_blk)),
            pl.BlockSpec((tm, wide), lambda i: (i, zd_blk)),
            pl.BlockSpec((1, wide), lambda i: (0, 0)),
            pl.BlockSpec((1, wide), lambda i: (0, 0)),
            pl.BlockSpec((2 * wide, D_MODEL), lambda i: (0, 0)),
            pl.BlockSpec((None, N_ADA, D_MODEL), lambda i: (_mod_row_index(i, tm), 0, 0))],
        out_specs=pl.BlockSpec((tm, D_MODEL), lambda i: (i, 0)),
        out_shape=jax.ShapeDtypeStruct((N_TOK, D_MODEL), f32),
        compiler_params=pltpu.CompilerParams(vmem_limit_bytes=VMEM_LIMIT),
        name="ab_out",
    )(x, hp, hs, op, os_, proj, proj, gm_row, gd_row, w_out, mod_l)


def _na_out_kernel(x_ref, ap_ref, as_ref, w_ref, mod_ref, o_ref, *, gate_row, tm):
    a = _prompt_or_sample(ap_ref, as_ref, tm).astype(bf16)
    y = jnp.dot(a, w_ref[...].astype(bf16), preferred_element_type=f32)
    o_ref[...] = x_ref[...] + mod_ref[gate_row:gate_row + 1, :] * y


def _na_out_call(x, attn_p, attn_s, w_out, mod_l, gate_row, tm=512):
    return pl.pallas_call(
        functools.partial(_na_out_kernel, gate_row=gate_row, tm=tm),
        grid=(N_TOK // tm,),
        in_specs=[pl.BlockSpec((tm, D_MODEL), lambda i: (i, 0))] + _split_specs(tm, NA_W) + [
            pl.BlockSpec((NA_W, D_MODEL), lambda i: (0, 0)),
            pl.BlockSpec((None, N_ADA, D_MODEL), lambda i: (_mod_row_index(i, tm), 0, 0))],
        out_specs=pl.BlockSpec((tm, D_MODEL), lambda i: (i, 0)),
        out_shape=jax.ShapeDtypeStruct((N_TOK, D_MODEL), f32),
        compiler_params=pltpu.CompilerParams(vmem_limit_bytes=VMEM_LIMIT),
        name="na_out",
    )(x, attn_p, attn_s, w_out, mod_l)


CTX_HEADS = 8


def _ctx_attn_kernel(q_ref, k_ref, v_ref, o_ref, nk_ref, nv_ref):
    low = lax.broadcasted_iota(jnp.int32, (1, LANES), 1) < DH_C
    for hh in range(CTX_HEADS):
        sl = slice(hh * DH_C, (hh + 1) * DH_C)
        nk_ref[hh] = k_ref[:, sl]
        nv_ref[hh] = v_ref[:, sl]
    for pair in range(CTX_HEADS // 2):
        sl = slice(pair * LANES, (pair + 1) * LANES)
        q = q_ref[:, sl] * DH_C ** -0.5
        k = k_ref[:, sl].astype(bf16)
        v = v_ref[:, sl].astype(bf16)
        outs = []
        for hh in range(2):
            qh = jnp.where(low if hh == 0 else jnp.logical_not(low), q, 0.0).astype(bf16)
            s = lax.dot_general(qh, k, NT_DIMS, preferred_element_type=f32)
            p = jnp.exp(s - jnp.max(s, axis=1, keepdims=True))
            o = jnp.dot(p.astype(bf16), v, preferred_element_type=f32)
            outs.append(o / jnp.sum(p, axis=1, keepdims=True))
        o_ref[:, sl] = jnp.where(low, outs[0], outs[1]).astype(o_ref.dtype)


def _ctx_attn_call(proj):
    w = CTX_HEADS * DH_C
    nblk = NA_W // w
    return pl.pallas_call(
        _ctx_attn_kernel,
        grid=(BATCH, nblk),
        in_specs=[pl.BlockSpec((SEQ, w), lambda b, j: (b, j)),
                  pl.BlockSpec((SEQ, w), lambda b, j: (b, nblk + j)),
                  pl.BlockSpec((SEQ, w), lambda b, j: (b, 2 * nblk + j))],
        out_specs=[pl.BlockSpec((SEQ, w), lambda b, j: (b, j)),
                   pl.BlockSpec((None, CTX_HEADS, SEQ, DH_C), lambda b, j: (b, j, 0, 0)),
                   pl.BlockSpec((None, CTX_HEADS, SEQ, DH_C), lambda b, j: (b, j, 0, 0))],
        out_shape=[jax.ShapeDtypeStruct((N_PROMPT, NA_W), bf16),
                   jax.ShapeDtypeStruct((BATCH, H_C, SEQ, DH_C), f32),
                   jax.ShapeDtypeStruct((BATCH, H_C, SEQ, DH_C), f32)],
        compiler_params=pltpu.CompilerParams(vmem_limit_bytes=VMEM_LIMIT),
        name="ctx_attn",
    )(proj, proj, proj)


QROWS = 4
QBLK = QROWS * GRID_W
KROWS_MID = QROWS + WIN_R - 1
N_RIDX = 2 * WIN_R - 1
N_CIDX = 2 * WIN_C - 1
N_QBLK = GRID_ROWS // QROWS


def _nattn_bias_tables(rb_ref, tb_scr, bmid_scr, btop_scr, bbot_scr):
    qc = lax.broadcasted_iota(jnp.int32, (GRID_W, GRID_W), 0)
    kc = lax.broadcasted_iota(jnp.int32, (GRID_W, GRID_W), 1)
    cs = jnp.clip(qc - WIN_C // 2, 0, GRID_W - WIN_C)
    valid = (kc >= cs) & (kc < cs + WIN_C)
    neg = jnp.full((GRID_W, GRID_W), -jnp.inf, f32)
    for hh in range(2):
        for ri in range(N_RIDX):
            row = jnp.broadcast_to(rb_ref[hh, ri:ri + 1, :], (GRID_W, LANES))
            tile = pltpu.roll(row, LANES - (WIN_C - 1), 1, stride=1, stride_axis=0)[:, :GRID_W]
            tb_scr[ri] = jnp.where(valid, tile, -jnp.inf)
        for i in range(QROWS):
            rs = slice(i * GRID_W, (i + 1) * GRID_W)
            for jj in range(KROWS_MID):
                inside = 0 <= jj - i < WIN_R
                bmid_scr[hh, rs, jj * GRID_W:(jj + 1) * GRID_W] = tb_scr[jj - i + WIN_R // 2 - 1] if inside else neg
            for jj in range(WIN_R):
                btop_scr[hh, rs, jj * GRID_W:(jj + 1) * GRID_W] = tb_scr[jj - i + WIN_R - 1]
                bbot_scr[hh, rs, jj * GRID_W:(jj + 1) * GRID_W] = tb_scr[jj - i + WIN_R // 2 - 1]


def _nattn_kernel(rb_ref, q_ref, k_ref, v_ref, ck_ref, cv_ref, o_ref, tb_scr, bmid_scr, btop_scr, bbot_scr):
    scale = DH_C ** -0.5

    @pl.when(pl.program_id(1) == 0)
    def _():
        _nattn_bias_tables(rb_ref, tb_scr, bmid_scr, btop_scr, bbot_scr)

    low = lax.broadcasted_iota(jnp.int32, (1, LANES), 1) < DH_C
    kctx = jnp.concatenate([ck_ref[0], ck_ref[1]], axis=1).astype(bf16)
    vctx = jnp.concatenate([cv_ref[0], cv_ref[1]], axis=1).astype(bf16)

    def block(q_start, k_start, k_rows, bias_scr):
        nk = k_rows * GRID_W
        q = q_ref[pl.ds(q_start, QBLK), :] * scale
        ku = k_ref[pl.ds(k_start, nk), :].astype(bf16)
        vu = v_ref[pl.ds(k_start, nk), :].astype(bf16)
        outs = []
        for hh in range(2):
            qh = jnp.where(low if hh == 0 else jnp.logical_not(low), q, 0.0).astype(bf16)
            s_loc = lax.dot_general(qh, ku, NT_DIMS, preferred_element_type=f32) + bias_scr[hh]
            s_ctx = lax.dot_general(qh, kctx, NT_DIMS, preferred_element_type=f32)
            m = jnp.maximum(jnp.max(s_loc, axis=1, keepdims=True), jnp.max(s_ctx, axis=1, keepdims=True))
            p_loc = jnp.exp(s_loc - m)
            p_ctx = jnp.exp(s_ctx - m)
            denom = jnp.sum(p_loc, axis=1, keepdims=True) + jnp.sum(p_ctx, axis=1, keepdims=True)
            o = (jnp.dot(p_loc.astype(bf16), vu, preferred_element_type=f32)
                 + jnp.dot(p_ctx.astype(bf16), vctx, preferred_element_type=f32))
            outs.append(o / denom)
        o_ref[pl.ds(q_start, QBLK), :] = jnp.where(low, outs[0], outs[1]).astype(o_ref.dtype)

    block(0, 0, WIN_R, btop_scr)

    def mid_body(blk, _):
        q_start = pl.multiple_of(blk * QBLK, QBLK)
        k_start = pl.multiple_of(blk * QBLK - (WIN_R // 2) * GRID_W, QBLK)
        block(q_start, k_start, KROWS_MID, bmid_scr)
        return 0

    lax.fori_loop(1, N_QBLK - 1, mid_body, 0)
    block((N_QBLK - 1) * QBLK, (GRID_ROWS - WIN_R) * GRID_W, WIN_R, bbot_scr)


def _nattn_call(proj, cache_k, cache_v, layer, rel_bias, off):
    npair = H_C // 2
    return pl.pallas_call(
        _nattn_kernel,
        grid=(npair, DEC_BATCH),
        in_specs=[pl.BlockSpec((2, 2 * WIN_R, LANES), lambda p, b: (p, 0, 0)),
                  pl.BlockSpec((DEC_SEQ, LANES), lambda p, b: (off + b, p)),
                  pl.BlockSpec((DEC_SEQ, LANES), lambda p, b: (off + b, npair + p)),
                  pl.BlockSpec((DEC_SEQ, LANES), lambda p, b: (off + b, 2 * npair + p)),
                  pl.BlockSpec((None, None, 2, PAST_LEN, DH_C), lambda p, b: (b, layer, p, 0, 0)),
                  pl.BlockSpec((None, None, 2, PAST_LEN, DH_C), lambda p, b: (b, layer, p, 0, 0))],
        out_specs=pl.BlockSpec((DEC_SEQ, LANES), lambda p, b: (b, p)),
        out_shape=jax.ShapeDtypeStruct((N_SAMPLE, NA_W), bf16),
        scratch_shapes=[pltpu.VMEM((N_RIDX, GRID_W, GRID_W), f32),
                        pltpu.VMEM((2, QBLK, KROWS_MID * GRID_W), f32),
                        pltpu.VMEM((2, QBLK, WIN_R * GRID_W), f32),
                        pltpu.VMEM((2, QBLK, WIN_R * GRID_W), f32)],
        compiler_params=pltpu.CompilerParams(dimension_semantics=("arbitrary", "arbitrary"),
                                             vmem_limit_bytes=VMEM_LIMIT),
        name="nattn",
    )(jnp.pad(rel_bias, ((0, 0), (0, 2 * WIN_R - N_RIDX), (0, LANES - N_CIDX))), proj, proj, proj, cache_k, cache_v)


def _lane_row(pieces):
    row = jnp.zeros((LANES,), f32)
    for off, vals in pieces:
        row = row.at[off:off + vals.shape[0]].set(vals.astype(f32))
    return row


def _param_rows(rows):
    out = jnp.zeros((8, LANES), f32)
    for r, row in enumerate(rows):
        out = out.at[r].set(row)
    return out


def kernel(x_prompt, x_sample, c, state_mlstm_C, state_mlstm_n, state_mlstm_m, state_delta_S, cache_na_k, cache_na_v, c_ctx, ada_w, ada_b, norm_g, ffn_wg, ffn_wu, ffn_wd, ab_w_in, ab_w_out, mlstm_b_i, mlstm_b_f, mlstm_norm_g, delta_conv_w, delta_a_log, delta_dt_bias, delta_norm_g, na_w_in, na_w_out, na_rel_bias, final_norm_g):
    xs = (x_prompt.reshape(N_PROMPT, D_MODEL), x_sample.reshape(N_SAMPLE, D_MODEL))
    mods = _ada_call(jnp.concatenate([c_ctx[None, :], c], axis=0), ada_w, ada_b)
    mods = mods.reshape(DEPTH, 3, N_ADA, D_MODEL)
    gf_row = final_norm_g.reshape(1, D_MODEL)
    s_off = N_PROMPT // DEC_SEQ
    new_c, new_n, new_m, new_s, new_k, new_v = [], [], [], [], [], []

    for l in range(DEPTH):
        mod_l = mods[l]
        a = l // 2
        x = _ffn_call(xs if l == 0 else (x,), mod_l, norm_g[l, 0].reshape(1, D_MODEL), ffn_wg, ffn_wu, ffn_wd, l, 0,
                      gf_row, rows=(0, 1, 2), final=False, split_out=False, tm=FFN_TM)[0]
        g_mix = norm_g[l, 1].reshape(1, D_MODEL)
        if l % 2 == 0:
            w_main, w_gate = _ab_weight_call(ab_w_in, a)
            proj, gates = _proj_call(x, mod_l, g_mix, w_main, w_gate, rows=(3, 4), out_dtype=bf16)

            par_m = _param_rows([_lane_row([(LANE_I, mlstm_b_i[a].reshape(-1))]),
                                 _lane_row([(LANE_F, mlstm_b_f[a].reshape(-1))])])
            par_d = _param_rows([_lane_row([(LANE_A, delta_a_log[a].reshape(-1))]),
                                 _lane_row([(LANE_A, delta_dt_bias[a].reshape(-1))])])
            conv_w8 = jnp.concatenate([delta_conv_w[a], jnp.zeros((8 - CONV_K, 3 * H_B * DK_B), f32)], axis=0)

            hp, cn, nn_, mn = _mlstm_call(proj, gates, par_m, BATCH, SEQ, 0, None)
            st = (state_mlstm_C[:, a], state_mlstm_n[:, a][..., None], state_mlstm_m[:, a][..., None, None])
            hs, _, _, _ = _mlstm_call(proj, gates, par_m, DEC_BATCH, DEC_SEQ, s_off, st)
            op, sn = _delta_call(proj, conv_w8, gates, par_d, BATCH, SEQ, 0, None)
            os_, _ = _delta_call(proj, conv_w8, gates, par_d, DEC_BATCH, DEC_SEQ, s_off, state_delta_S[:, a])
            new_c.append(cn)
            new_n.append(nn_[..., 0])
            new_m.append(mn[..., 0, 0])
            new_s.append(sn)
            x = _ab_out_call(x, hp, hs, op, os_, proj, mlstm_norm_g[a].reshape(1, -1),
                             jnp.tile(delta_norm_g[a], H_B).reshape(1, -1), ab_w_out[a], mod_l, gate_row=5)
        else:
            proj = _proj_call(x, mod_l, g_mix, na_w_in[a], None, rows=(3, 4), out_dtype=f32)[0]
            attn_p, kp, vp = _ctx_attn_call(proj)
            attn_s = _nattn_call(proj, cache_na_k, cache_na_v, a, na_rel_bias[a], s_off)
            new_k.append(kp)
            new_v.append(vp)
            x = _na_out_call(x, attn_p, attn_s, na_w_out[a], mod_l, gate_row=5)
        last = l == DEPTH - 1
        outs = _ffn_call((x,), mod_l, norm_g[l, 2].reshape(1, D_MODEL), ffn_wg, ffn_wu, ffn_wd, l, 1,
                         gf_row, rows=(6, 7, 8), final=last, split_out=last, tm=FFN_TM_SPLIT if last else FFN_TM)
        x = outs[0]

    y_prompt = outs[0].reshape(BATCH, SEQ, D_MODEL)
    y_sample = outs[1].reshape(DEC_BATCH, DEC_SEQ, D_MODEL)
    return (y_prompt, y_sample, jnp.stack(new_c, axis=1), jnp.stack(new_n, axis=1), jnp.stack(new_m, axis=1),
            jnp.stack(new_s, axis=1), jnp.stack(new_k, axis=1), jnp.stack(new_v, axis=1))
```

```python
import functools

import jax
import jax.numpy as jnp
from jax import lax
from jax.experimental import pallas as pl
from jax.experimental.pallas import tpu as pltpu

f32 = jnp.float32
bf16 = jnp.bfloat16

D_MODEL = 1024
BATCH = 16
SEQ = 256
DEPTH = 2
DEC_BATCH = 2
DEC_SEQ = 2048
PAST_LEN = 256
GRID_W = 64
GRID_ROWS = DEC_SEQ // GRID_W
D_FF = 2816
N_ADA = 9
EPS = 1e-6
CHUNK = 64
H_A, DQK_A, DV_A = 4, 64, 128
H_B, DK_B, DV_B = 4, 128, 128
CONV_K = 5
H_C, DH_C = 16, 64
WIN_R, WIN_C = 8, 16
NA_W = H_C * DH_C
assert DH_C ** -0.5 == 2.0 ** -3 and DQK_A ** -0.5 == 2.0 ** -3

N_PROMPT = BATCH * SEQ
N_SAMPLE = DEC_BATCH * DEC_SEQ
N_TOK = N_PROMPT + N_SAMPLE
AB_MAIN = 2 * H_A * DQK_A + 2 * H_A * DV_A + 2 * H_B * DK_B + 2 * H_B * DV_B
LANES = 128
VMEM_LIMIT = 56 * 1024 * 1024

LANE_I, LANE_F, LANE_BETA, LANE_A = 0, 8, 16, 24

NT_DIMS = (((1,), (1,)), ((), ()))
TN_DIMS = (((0,), (0,)), ((), ()))

MAX_CHUNKS_PER_TRIP = 4
FFN_TM = 2048
FFN_TM_SPLIT = 1024
FFN_ROW_BLOCK = 512
PROJ_ROW_BLOCK = 256


def _softplus(x):
    return jnp.maximum(x, 0.0) + jnp.log1p(jnp.exp(-jnp.abs(x)))


def _log_sigmoid(x):
    return -_softplus(-x)


def _silu(x):
    return x * jax.nn.sigmoid(x)


def _mod_row_index(i, tm):
    n_p = N_PROMPT // tm
    per_b = DEC_SEQ // tm
    return jnp.where(i < n_p, 0, 1 + (i - n_p) // per_b)


def _modulated(x, g_row, shift_row, scale_row):
    y = x * lax.rsqrt(jnp.mean(x * x, axis=-1, keepdims=True) + EPS) * g_row
    return y * (1.0 + scale_row) + shift_row


def _ada_kernel(cb_ref, w_ref, b_ref, o_ref, s_scr):
    tn = w_ref.shape[-1]

    @pl.when((pl.program_id(0) == 0) & (pl.program_id(1) == 0))
    def _():
        cb = cb_ref[...]
        s_scr[...] = cb * jax.nn.sigmoid(cb)

    for j in range(tn // LANES):
        cols = slice(j * LANES, (j + 1) * LANES)
        w = w_ref[:, cols]
        for r in range(3):
            o_ref[r:r + 1, cols] = jnp.sum(w * s_scr[r], axis=0, keepdims=True) + b_ref[:, cols]


def _ada_call(cond3, ada_w, ada_b):
    tn = N_ADA * D_MODEL // 4
    cb = jnp.broadcast_to(cond3[:, :, None], (3, D_MODEL, LANES))
    return pl.pallas_call(
        _ada_kernel,
        grid=(DEPTH, N_ADA * D_MODEL // tn),
        in_specs=[pl.BlockSpec((3, D_MODEL, LANES), lambda l, j: (0, 0, 0)),
                  pl.BlockSpec((None, D_MODEL, tn), lambda l, j: (l, 0, j)),
                  pl.BlockSpec((None, 1, tn), lambda l, j: (l, 0, j))],
        out_specs=pl.BlockSpec((None, 3, tn), lambda l, j: (l, 0, j)),
        out_shape=jax.ShapeDtypeStruct((DEPTH, 3, N_ADA * D_MODEL), f32),
        scratch_shapes=[pltpu.VMEM((3, D_MODEL, LANES), f32)],
        compiler_params=pltpu.CompilerParams(dimension_semantics=("arbitrary", "arbitrary"),
                                             vmem_limit_bytes=VMEM_LIMIT),
        name="ada_mod",
    )(cb, ada_w, ada_b.reshape(DEPTH, 1, N_ADA * D_MODEL))


def _prompt_or_sample(p_ref, s_ref, tm):
    return jnp.where(pl.program_id(0) < N_PROMPT // tm, p_ref[...], s_ref[...])


def _split_specs(tm, width, n_grid_axes=1, single_buffered=False):
    n_p = N_PROMPT // tm
    kw = dict(pipeline_mode=pl.Buffered(1)) if single_buffered else {}
    if n_grid_axes == 1:
        return [pl.BlockSpec((tm, width), lambda i: (jnp.minimum(i, n_p - 1), 0), **kw),
                pl.BlockSpec((tm, width), lambda i: (jnp.maximum(i - n_p, 0), 0), **kw)]
    return [pl.BlockSpec((tm, width), lambda i, j: (jnp.minimum(i, n_p - 1), 0), **kw),
            pl.BlockSpec((tm, width), lambda i, j: (jnp.maximum(i - n_p, 0), 0), **kw)]


def _ffn_kernel(*refs, rows, final, split_in, split_out, tm):
    it = iter(refs)
    x_refs = [next(it) for _ in range(2 if split_in else 1)]
    mod_ref, g_ref, wg_ref, wu_ref, wd_ref, gf_ref = [next(it) for _ in range(6)]
    o_refs = [next(it) for _ in range(2 if split_out else 1)]
    h_scr = next(it)
    acc_ref = next(it) if split_out else o_refs[0]
    j = pl.program_id(1)
    last_j = pl.num_programs(1) - 1
    is_prompt = pl.program_id(0) < N_PROMPT // tm

    def x_rows(rs):
        if split_in:
            return jnp.where(is_prompt, x_refs[0][rs, :], x_refs[1][rs, :])
        return x_refs[0][rs, :]

    def step(first, last, out_ref):
        wg = wg_ref[...].astype(bf16)
        wu = wu_ref[...].astype(bf16)
        wd = wd_ref[...].astype(bf16)
        for r in range(tm // FFN_ROW_BLOCK):
            rs = slice(r * FFN_ROW_BLOCK, (r + 1) * FFN_ROW_BLOCK)
            if first:
                h = _modulated(x_rows(rs), g_ref[...], mod_ref[rows[0]:rows[0] + 1, :],
                               mod_ref[rows[1]:rows[1] + 1, :]).astype(bf16)
                h_scr[rs, :] = h
            else:
                h = h_scr[rs, :]
            g = jnp.dot(h, wg, preferred_element_type=f32)
            u = jnp.dot(h, wu, preferred_element_type=f32)
            part = jnp.dot((_silu(g) * u).astype(bf16), wd, preferred_element_type=f32)
            acc = part if first else acc_ref[rs, :] + part
            if last:
                xn = x_rows(rs) + (0.5 * mod_ref[rows[2]:rows[2] + 1, :]) * acc
                if final:
                    xn = xn * lax.rsqrt(jnp.mean(xn * xn, axis=-1, keepdims=True) + EPS) * gf_ref[...]
                out_ref[rs, :] = xn
            else:
                acc_ref[rs, :] = acc

    pl.when(j == 0)(lambda: step(True, False, None))
    pl.when((j > 0) & (j < last_j))(lambda: step(False, False, None))
    if split_out:
        pl.when((j == last_j) & is_prompt)(lambda: step(False, True, o_refs[0]))
        pl.when((j == last_j) & jnp.logical_not(is_prompt))(lambda: step(False, True, o_refs[1]))
    else:
        pl.when(j == last_j)(lambda: step(False, True, o_refs[0]))


def _ffn_call(xs, mod_l, g_row, ffn_wg, ffn_wu, ffn_wd, layer, half, gf_row, rows, final, split_out, tm, tf=256):
    split_in = len(xs) == 2
    x_specs = (_split_specs(tm, D_MODEL, 2, single_buffered=True) if split_in
               else [pl.BlockSpec((tm, D_MODEL), lambda i, j: (i, 0))])
    if split_out:
        out_specs = _split_specs(tm, D_MODEL, 2)
        out_shape = [jax.ShapeDtypeStruct((N_PROMPT, D_MODEL), f32), jax.ShapeDtypeStruct((N_SAMPLE, D_MODEL), f32)]
        scratch = [pltpu.VMEM((tm, D_MODEL), bf16), pltpu.VMEM((tm, D_MODEL), f32)]
    else:
        out_specs = [pl.BlockSpec((tm, D_MODEL), lambda i, j: (i, 0))]
        out_shape = [jax.ShapeDtypeStruct((N_TOK, D_MODEL), f32)]
        scratch = [pltpu.VMEM((tm, D_MODEL), bf16)]
    return pl.pallas_call(
        functools.partial(_ffn_kernel, rows=rows, final=final, split_in=split_in, split_out=split_out, tm=tm),
        grid=(N_TOK // tm, D_FF // tf),
        in_specs=x_specs + [
            pl.BlockSpec((None, N_ADA, D_MODEL), lambda i, j: (_mod_row_index(i, tm), 0, 0)),
            pl.BlockSpec((1, D_MODEL), lambda i, j: (0, 0)),
            pl.BlockSpec((None, None, D_MODEL, tf), lambda i, j: (layer, half, 0, j)),
            pl.BlockSpec((None, None, D_MODEL, tf), lambda i, j: (layer, half, 0, j)),
            pl.BlockSpec((None, None, tf, D_MODEL), lambda i, j: (layer, half, j, 0)),
            pl.BlockSpec((1, D_MODEL), lambda i, j: (0, 0))],
        out_specs=out_specs, out_shape=out_shape, scratch_shapes=scratch,
        compiler_params=pltpu.CompilerParams(dimension_semantics=("parallel", "arbitrary"),
                                             vmem_limit_bytes=VMEM_LIMIT),
        name="ffn",
    )(*xs, mod_l, g_row, ffn_wg, ffn_wu, ffn_wd, gf_row)


def _proj_kernel(*refs, rows, with_gates, w_transposed):
    if with_gates:
        x_ref, mod_ref, g_ref, w_ref, wgate_ref, o_ref, og_ref, h_scr = refs
    else:
        x_ref, mod_ref, g_ref, w_ref, o_ref, h_scr = refs
    j = pl.program_id(1)
    tm = x_ref.shape[0]
    dims = NT_DIMS if w_transposed else (((1,), (0,)), ((), ()))

    def matmul(h, w):
        return lax.dot_general(h, w, dims, preferred_element_type=f32)

    @pl.when(j == 0)
    def _():
        w = w_ref[...].astype(bf16)
        for r in range(tm // PROJ_ROW_BLOCK):
            rs = slice(r * PROJ_ROW_BLOCK, (r + 1) * PROJ_ROW_BLOCK)
            hb = _modulated(x_ref[rs, :], g_ref[...], mod_ref[rows[0]:rows[0] + 1, :],
                            mod_ref[rows[1]:rows[1] + 1, :]).astype(bf16)
            h_scr[rs, :] = hb
            o_ref[rs, :] = matmul(hb, w).astype(o_ref.dtype)
            if with_gates:
                og_ref[rs, :] = matmul(hb, wgate_ref[...].astype(bf16))

    @pl.when(j > 0)
    def _():
        o_ref[...] = matmul(h_scr[...], w_ref[...].astype(bf16)).astype(o_ref.dtype)


AB_I0 = 2 * H_A * DQK_A + 2 * H_A * DV_A
AB_D0 = AB_I0 + 4 * H_A
AB_B0 = AB_D0 + 2 * H_B * DK_B + 2 * H_B * DV_B
AB_IN = AB_B0 + 4 * H_B


def _ab_weight_kernel(w_ref, wm_ref, wg_ref):
    n_gate = 2 * H_A
    wm_ref[:AB_I0, :] = w_ref[:AB_I0, :]
    wm_ref[AB_I0:, :] = w_ref[AB_D0:AB_B0, :]
    wg_ref[:2 * n_gate, :] = w_ref[AB_I0:AB_D0, :]
    wg_ref[2 * n_gate:4 * n_gate, :] = w_ref[AB_B0:, :]
    wg_ref[4 * n_gate:, :] = jnp.zeros((LANES - 4 * n_gate, w_ref.shape[1]), f32)


def _ab_weight_call(ab_w_in, layer, tk=256):
    w_t = jnp.swapaxes(ab_w_in, 1, 2)
    return pl.pallas_call(
        _ab_weight_kernel,
        grid=(D_MODEL // tk,),
        in_specs=[pl.BlockSpec((None, AB_IN, tk), lambda i: (layer, 0, i))],
        out_specs=[pl.BlockSpec((AB_MAIN, tk), lambda i: (0, i)), pl.BlockSpec((LANES, tk), lambda i: (0, i))],
        out_shape=[jax.ShapeDtypeStruct((AB_MAIN, D_MODEL), f32), jax.ShapeDtypeStruct((LANES, D_MODEL), f32)],
        compiler_params=pltpu.CompilerParams(vmem_limit_bytes=VMEM_LIMIT),
        name="ab_weights",
    )(w_t)


def _column_tile(n, cap):
    return max(t for t in range(LANES, cap + 1, LANES) if n % t == 0)


def _proj_call(x, mod_l, g_row, w, w_gate, rows, out_dtype, w_transposed, tm=2048):
    n = w.shape[0] if w_transposed else w.shape[1]
    tn = _column_tile(n, 1024)
    with_gates = w_gate is not None
    w_spec = (pl.BlockSpec((tn, D_MODEL), lambda i, j: (j, 0)) if w_transposed
              else pl.BlockSpec((D_MODEL, tn), lambda i, j: (0, j)))
    in_specs = [pl.BlockSpec((tm, D_MODEL), lambda i, j: (i, 0)),
                pl.BlockSpec((None, N_ADA, D_MODEL), lambda i, j: (_mod_row_index(i, tm), 0, 0)),
                pl.BlockSpec((1, D_MODEL), lambda i, j: (0, 0)),
                w_spec]
    out_specs = [pl.BlockSpec((tm, tn), lambda i, j: (i, j))]
    out_shape = [jax.ShapeDtypeStruct((N_TOK, n), out_dtype)]
    args = [x, mod_l, g_row, w]
    if with_gates:
        in_specs.append(pl.BlockSpec(w_gate.shape, lambda i, j: (0, 0)))
        out_specs.append(pl.BlockSpec((tm, LANES), lambda i, j: (i, 0)))
        out_shape.append(jax.ShapeDtypeStruct((N_TOK, LANES), f32))
        args.append(w_gate)
    return pl.pallas_call(
        functools.partial(_proj_kernel, rows=rows, with_gates=with_gates, w_transposed=w_transposed),
        grid=(N_TOK // tm, n // tn),
        in_specs=in_specs, out_specs=out_specs, out_shape=out_shape,
        scratch_shapes=[pltpu.VMEM((tm, D_MODEL), bf16)],
        compiler_params=pltpu.CompilerParams(dimension_semantics=("parallel", "arbitrary"),
                                             vmem_limit_bytes=VMEM_LIMIT),
        name="in_proj",
    )(*args)


def _chunk_masks():
    r = lax.broadcasted_iota(jnp.int32, (CHUNK, CHUNK), 0)
    c = lax.broadcasted_iota(jnp.int32, (CHUNK, CHUNK), 1)
    return r >= c, r <= c, r > c, r < c


def _lane_col(x, lane, j):
    return jnp.sum(jnp.where(lane == j, x, 0.0), axis=1, keepdims=True)


def _head_lane_select(base, pair):
    r = lax.broadcasted_iota(jnp.int32, (8, LANES), 0)
    ln = lax.broadcasted_iota(jnp.int32, (8, LANES), 1)
    return jnp.where((ln == base + 2 * pair + r) & (r < 2), 1.0, 0.0)


def _split3_bf16(x):
    hi = x.astype(bf16)
    r = x - hi.astype(f32)
    mid = r.astype(bf16)
    return hi, mid, (r - mid.astype(f32)).astype(bf16)


def _thrice(a16):
    return jnp.concatenate([a16] * 3, axis=1)


def _mask_matmul_f32(mask16x3, x):
    return jnp.dot(mask16x3, jnp.concatenate(_split3_bf16(x), axis=0), preferred_element_type=f32)


def _select_rows_f32(sel16x3, x):
    return lax.dot_general(sel16x3, jnp.concatenate(_split3_bf16(x), axis=1), NT_DIMS, preferred_element_type=f32)


def _chunks_per_trip(nc):
    return min(MAX_CHUNKS_PER_TRIP, nc)


def _chunk_start(trip, u, d, nc):
    step = trip * _chunks_per_trip(nc) + u
    c = step if d == 0 else nc - 1 - step
    return pl.multiple_of(c * CHUNK, CHUNK)


def _mlstm_kernel(*refs, seq, zero_init):
    if zero_init:
        qp_ref, kp_ref, v0_ref, v1_ref, g_ref, par_ref, h_ref, cf_ref, nf_ref, mf_ref = refs
    else:
        (qp_ref, kp_ref, v0_ref, v1_ref, g_ref, par_ref, c0_ref, n0_ref, m0_ref,
         h_ref, cf_ref, nf_ref, mf_ref) = refs
    nc = seq // CHUNK
    pair = pl.program_id(1)
    lane = lax.broadcasted_iota(jnp.int32, (1, LANES), 1)
    tril, triu, _, _ = _chunk_masks()
    masks = (tril, triu)
    masks16 = (_thrice(tril.astype(bf16)), _thrice(triu.astype(bf16)))
    sel = [_thrice(_head_lane_select(LANE_F + d * H_A, pair).astype(bf16)) for d in range(2)]
    bi_row = par_ref[0:1, :]
    bf_row = par_ref[1:2, :]
    ones_col = jnp.where(lane == 0, 1.0, 0.0) + jnp.zeros((CHUNK, LANES), f32)
    v_refs = (v0_ref, v1_ref)
    streams = [(hh, d) for hh in range(2) for d in range(2)]

    h_ref[...] = jnp.zeros_like(h_ref)

    init = []
    for hh, d in streams:
        if zero_init:
            init.append((jnp.zeros((DQK_A, 2 * LANES), f32), jnp.zeros((1, 1), f32)))
        else:
            n_aug = jnp.where(lane == 0, n0_ref[d, hh], 0.0)
            init.append((jnp.concatenate([c0_ref[d, hh], n_aug], axis=1), m0_ref[d, hh]))

    def body(trip, carry):
        subs = range(_chunks_per_trip(nc))
        shared = {}
        for u in subs:
            for d in range(2):
                r0 = _chunk_start(trip, u, d, nc)
                gates = g_ref[pl.ds(r0, CHUNK), :]
                gi = gates + bi_row
                gf = _log_sigmoid(gates + bf_row)
                cum = _mask_matmul_f32(masks16[d], gf)
                stack = jnp.concatenate([cum, pltpu.roll(gi, LANE_F - LANE_I, axis=1)], axis=0)
                rows = _select_rows_f32(sel[d], stack)
                total = cum[CHUNK - 1:CHUNK, :] if d == 0 else cum[0:1, :]
                shared[u, d] = (r0, gi, cum, rows, total)
        chains = [(u, hh, d) for u in subs for hh, d in streams]
        st = {ch: {} for ch in chains}
        for ch in chains:
            u, hh, d = ch
            r0 = shared[u, d][0]
            s = st[ch]
            s["q"] = (qp_ref[pl.ds(r0, CHUNK), hh * DQK_A:(hh + 1) * DQK_A] * DQK_A ** -0.5).astype(bf16)
            s["k"] = kp_ref[pl.ds(r0, CHUNK), hh * DQK_A:(hh + 1) * DQK_A].astype(bf16)
            s["v_aug"] = jnp.concatenate([v_refs[hh][pl.ds(r0, CHUNK), :].astype(f32), ones_col], axis=1)
            s["qk"] = lax.dot_general(s["q"], s["k"], NT_DIMS, preferred_element_type=f32)
        for ch in chains:
            u, hh, d = ch
            _, gi, cum, rows, total = shared[u, d]
            s = st[ch]
            head = 2 * pair + hh
            jf = LANE_F + d * H_A + head
            ji = LANE_I + d * H_A + head
            bcol = _lane_col(cum, lane, jf)
            icol = _lane_col(gi, lane, ji)
            dmat = jnp.where(masks[d], bcol - rows[hh:hh + 1, :CHUNK] + rows[hh:hh + 1, CHUNK:], -jnp.inf)
            m_loc = jnp.max(dmat, axis=1, keepdims=True)
            s["sw"] = (s["qk"] * jnp.exp(dmat - m_loc)).astype(bf16)
            blast = _lane_col(total, lane, jf)
            gs = blast - bcol + icol
            ms_loc = jnp.max(gs, axis=0, keepdims=True)
            s["wv"] = (jnp.exp(gs - ms_loc) * s["v_aug"]).astype(bf16)
            s.update(bcol=bcol, m_loc=m_loc, blast=blast, ms_loc=ms_loc)
        for ch in chains:
            s = st[ch]
            s["num"] = jnp.dot(s["sw"], s["v_aug"].astype(bf16), preferred_element_type=f32)
            s["kv"] = lax.dot_general(s["k"], s["wv"], TN_DIMS, preferred_element_type=f32)
        state = list(carry)
        for u in subs:
            qc = [jnp.dot(st[u, hh, d]["q"], state[i][0].astype(bf16), preferred_element_type=f32)
                  for i, (hh, d) in enumerate(streams)]
            for i, (hh, d) in enumerate(streams):
                s = st[u, hh, d]
                c_aug, m = state[i]
                m_inter = s["bcol"] + m
                m_t = jnp.maximum(m_inter, s["m_loc"])
                nd = jnp.exp(s["m_loc"] - m_t) * s["num"] + jnp.exp(m_inter - m_t) * qc[i]
                den = nd[:, DV_A:DV_A + 1]
                hval = nd[:, :DV_A] / jnp.maximum(jnp.abs(den), jnp.exp(-m_t))
                r0 = shared[u, d][0]
                h_ref[pl.ds(r0, CHUNK), hh * DV_A:(hh + 1) * DV_A] += hval
                m_new = jnp.maximum(s["blast"] + m, s["ms_loc"])
                c_new = jnp.exp(s["blast"] + m - m_new) * c_aug + jnp.exp(s["ms_loc"] - m_new) * s["kv"]
                state[i] = (c_new, m_new)
        return tuple(state)

    final = lax.fori_loop(0, nc // _chunks_per_trip(nc), body, tuple(init))
    for i, (hh, d) in enumerate(streams):
        c_aug, m = final[i]
        cf_ref[d, hh] = c_aug[:, :DV_A]
        nf_ref[d, hh] = c_aug[:, DV_A:DV_A + 1]
        mf_ref[d, hh] = m


def _mlstm_call(proj, gates, par, nb, seq, off, state):
    zero_init = state is None
    blk = lambda col: pl.BlockSpec((seq, LANES), col)
    in_specs = [blk(lambda b, p: (off + b, p)),
                blk(lambda b, p: (off + b, 2 + p)),
                blk(lambda b, p: (off + b, 4 + 2 * p)),
                blk(lambda b, p: (off + b, 5 + 2 * p)),
                blk(lambda b, p: (off + b, 0)),
                pl.BlockSpec((8, LANES), lambda b, p: (0, 0))]
    st_specs = [pl.BlockSpec((None, 2, 2, DQK_A, DV_A), lambda b, p: (b, 0, p, 0, 0)),
                pl.BlockSpec((None, 2, 2, DQK_A, 1), lambda b, p: (b, 0, p, 0, 0)),
                pl.BlockSpec((None, 2, 2, 1, 1), lambda b, p: (b, 0, p, 0, 0))]
    args = [proj, proj, proj, proj, gates, par]
    if not zero_init:
        in_specs += st_specs
        args += list(state)
    return pl.pallas_call(
        functools.partial(_mlstm_kernel, seq=seq, zero_init=zero_init),
        grid=(nb, H_A // 2),
        in_specs=in_specs,
        out_specs=[pl.BlockSpec((seq, 2 * DV_A), lambda b, p: (b, p))] + st_specs,
        out_shape=[jax.ShapeDtypeStruct((nb * seq, H_A * DV_A), f32),
                   jax.ShapeDtypeStruct((nb, 2, H_A, DQK_A, DV_A), f32),
                   jax.ShapeDtypeStruct((nb, 2, H_A, DQK_A, 1), f32),
                   jax.ShapeDtypeStruct((nb, 2, H_A, 1, 1), f32)],
        compiler_params=pltpu.CompilerParams(vmem_limit_bytes=VMEM_LIMIT),
        name="mlstm",
    )(*args)


CONV_PAD = 8


def _short_conv_silu(x, w_ref, pad_scr, seq):
    pad_scr[:CONV_PAD, :] = jnp.zeros((CONV_PAD, LANES), f32)
    pad_scr[CONV_PAD + seq:, :] = jnp.zeros((CONV_PAD, LANES), f32)
    pad_scr[CONV_PAD:CONV_PAD + seq, :] = x
    acc = x * w_ref[CONV_K // 2:CONV_K // 2 + 1, :]
    for tap in range(CONV_K):
        delta = tap - CONV_K // 2
        if delta != 0:
            acc = acc + pad_scr[CONV_PAD + delta:CONV_PAD + delta + seq, :] * w_ref[tap:tap + 1, :]
    return _silu(acc)


def _l2_unit(y):
    return y * lax.rsqrt(jnp.sum(y * y, axis=-1, keepdims=True) + EPS)


def _split_bf16(x):
    hi = x.astype(bf16)
    return hi, (x - hi.astype(f32)).astype(bf16)


def _matmul_3pass(m, x):
    mh, ml = _split_bf16(m)
    xh, xl = _split_bf16(x)
    return jnp.dot(jnp.concatenate([mh, ml, mh], axis=1), jnp.concatenate([xh, xh, xl], axis=0),
                   preferred_element_type=f32)


def _unit_triangular_solves(ns, xs):
    levels = CHUNK.bit_length() - 1
    r = lax.broadcasted_iota(jnp.int32, (CHUNK, CHUNK), 0)
    c = lax.broadcasted_iota(jnp.int32, (CHUNK, CHUNK), 1)
    eye = jnp.where(r == c, 1.0, 0.0)
    ts = [eye + n for n in ns]
    ms = [_matmul_3pass(n, n) for n in ns]
    for lvl in range(1, levels):
        for i in range(len(ns)):
            if lvl < levels - 1:
                prod = _matmul_3pass(ms[i], jnp.concatenate([ts[i], ms[i]], axis=1))
                ts[i] = ts[i] + prod[:, :CHUNK]
                ms[i] = prod[:, CHUNK:]
            else:
                ts[i] = ts[i] + _matmul_3pass(ms[i], ts[i])
    return [_matmul_3pass(t, x) for t, x in zip(ts, xs)]


def _delta_kernel(*refs, seq, zero_init):
    if zero_init:
        (q0_ref, q1_ref, k0_ref, k1_ref, v0_ref, v1_ref, wq0_ref, wq1_ref, wk0_ref, wk1_ref, wv0_ref, wv1_ref,
         g_ref, par_ref, o_ref, sf_ref, q_scr, k_scr, v_scr, pad_scr) = refs
    else:
        (q0_ref, q1_ref, k0_ref, k1_ref, v0_ref, v1_ref, wq0_ref, wq1_ref, wk0_ref, wk1_ref, wv0_ref, wv1_ref,
         g_ref, par_ref, s0_ref, o_ref, sf_ref, q_scr, k_scr, v_scr, pad_scr) = refs
    nc = seq // CHUNK
    pair = pl.program_id(1)
    lane = lax.broadcasted_iota(jnp.int32, (1, LANES), 1)
    tril, triu, stril, striu = _chunk_masks()
    masks, smasks = (tril, triu), (stril, striu)
    masks16 = (_thrice(tril.astype(bf16)), _thrice(triu.astype(bf16)))
    sel = [_thrice(_head_lane_select(LANE_A + d * H_B, pair).astype(bf16)) for d in range(2)]
    neg_a_row = -jnp.exp(par_ref[0:1, :])
    dt_row = par_ref[1:2, :]
    streams = [(hh, d) for hh in range(2) for d in range(2)]

    for hh, (q_ref, k_ref, v_ref, wq_ref, wk_ref, wv_ref) in enumerate(
            ((q0_ref, k0_ref, v0_ref, wq0_ref, wk0_ref, wv0_ref), (q1_ref, k1_ref, v1_ref, wq1_ref, wk1_ref, wv1_ref))):
        q_scr[hh] = _l2_unit(_short_conv_silu(q_ref[...].astype(f32), wq_ref, pad_scr, seq)) * DK_B ** -0.5
        k_scr[hh] = _l2_unit(_short_conv_silu(k_ref[...].astype(f32), wk_ref, pad_scr, seq))
        v_scr[hh] = _short_conv_silu(v_ref[...].astype(f32), wv_ref, pad_scr, seq)

    o_ref[...] = jnp.zeros_like(o_ref)
    init = tuple(jnp.zeros((DK_B, DV_B), f32) if zero_init else s0_ref[d, hh] for hh, d in streams)

    def body(trip, carry):
        subs = range(_chunks_per_trip(nc))
        shared = {}
        for u in subs:
            for d in range(2):
                r0 = _chunk_start(trip, u, d, nc)
                gates = g_ref[pl.ds(r0, CHUNK), :]
                beta_all = jax.nn.sigmoid(gates)
                glog = neg_a_row * _softplus(gates + dt_row)
                cum = _mask_matmul_f32(masks16[d], glog)
                rows = _select_rows_f32(sel[d], cum)
                shared[u, d] = (r0, beta_all, cum, rows)
        chains = [(u, hh, d) for u in subs for hh, d in streams]
        st = {ch: {} for ch in chains}
        for ch in chains:
            u, hh, d = ch
            r0, beta_all, cum, rows = shared[u, d]
            s = st[ch]
            head = 2 * pair + hh
            beta = _lane_col(beta_all, lane, LANE_BETA + d * H_B + head)
            gcol = _lane_col(cum, lane, LANE_A + d * H_B + head)
            glast = gcol[CHUNK - 1:CHUNK, :] if d == 0 else gcol[0:1, :]
            q = q_scr[hh, pl.ds(r0, CHUNK), :]
            k = k_scr[hh, pl.ds(r0, CHUNK), :]
            v = v_scr[hh, pl.ds(r0, CHUNK), :]
            k16 = k.astype(bf16)
            kbeta = k * beta
            eg = jnp.exp(gcol)
            decay = jnp.exp(jnp.where(masks[d], gcol - rows[hh:hh + 1, :], -jnp.inf))
            kk = lax.dot_general(kbeta.astype(bf16), k16, NT_DIMS, preferred_element_type=f32)
            qk = lax.dot_general(q.astype(bf16), k16, NT_DIMS, preferred_element_type=f32)
            s["n"] = -jnp.where(smasks[d], kk * decay, 0.0)
            s["x"] = jnp.concatenate([v * beta, kbeta * eg], axis=1)
            s["qk"] = (qk * decay).astype(bf16)
            s["qg"] = (q * eg).astype(bf16)
            s["kd_t"] = (k * jnp.exp(glast - gcol)).T.astype(bf16)
            s["gl"] = jnp.exp(glast)
        solved = _unit_triangular_solves([st[ch]["n"] for ch in chains], [st[ch]["x"] for ch in chains])
        for ch, uw in zip(chains, solved):
            s = st[ch]
            s["u"] = uw[:, :DV_B]
            s["w_qg"] = jnp.concatenate([uw[:, DV_B:].astype(bf16), s["qg"]], axis=0)
        state = list(carry)
        for u in subs:
            ws = [jnp.dot(st[u, hh, d]["w_qg"], state[i].astype(bf16), preferred_element_type=f32)
                  for i, (hh, d) in enumerate(streams)]
            v_new = [(st[u, hh, d]["u"] - ws[i][:CHUNK]).astype(bf16) for i, (hh, d) in enumerate(streams)]
            for i, (hh, d) in enumerate(streams):
                s = st[u, hh, d]
                o = ws[i][CHUNK:] + jnp.dot(s["qk"], v_new[i], preferred_element_type=f32)
                r0 = shared[u, d][0]
                o_ref[pl.ds(r0, CHUNK), hh * DV_B:(hh + 1) * DV_B] += o
                state[i] = state[i] * s["gl"] + jnp.dot(s["kd_t"], v_new[i], preferred_element_type=f32)
        return tuple(state)

    final = lax.fori_loop(0, nc // _chunks_per_trip(nc), body, init)
    for i, (hh, d) in enumerate(streams):
        sf_ref[d, hh] = final[i]


def _delta_call(proj, conv_w8, gates, par, nb, seq, off, state):
    zero_init = state is None
    col0 = (2 * H_A * DQK_A + 2 * H_A * DV_A) // LANES
    blk = lambda col: pl.BlockSpec((seq, LANES), col)
    wblk = lambda col: pl.BlockSpec((8, LANES), col)
    in_specs = [blk(lambda b, p: (off + b, col0 + 2 * p)), blk(lambda b, p: (off + b, col0 + 2 * p + 1)),
                blk(lambda b, p: (off + b, col0 + H_B + 2 * p)), blk(lambda b, p: (off + b, col0 + H_B + 2 * p + 1)),
                blk(lambda b, p: (off + b, col0 + 2 * H_B + 2 * p)), blk(lambda b, p: (off + b, col0 + 2 * H_B + 2 * p + 1)),
                wblk(lambda b, p: (0, 2 * p)), wblk(lambda b, p: (0, 2 * p + 1)),
                wblk(lambda b, p: (0, H_B + 2 * p)), wblk(lambda b, p: (0, H_B + 2 * p + 1)),
                wblk(lambda b, p: (0, 2 * H_B + 2 * p)), wblk(lambda b, p: (0, 2 * H_B + 2 * p + 1)),
                blk(lambda b, p: (off + b, 0)),
                pl.BlockSpec((8, LANES), lambda b, p: (0, 0))]
    st_spec = pl.BlockSpec((None, 2, 2, DK_B, DV_B), lambda b, p: (b, 0, p, 0, 0))
    args = [proj] * 6 + [conv_w8] * 6 + [gates, par]
    if not zero_init:
        in_specs.append(st_spec)
        args.append(state)
    return pl.pallas_call(
        functools.partial(_delta_kernel, seq=seq, zero_init=zero_init),
        grid=(nb, H_B // 2),
        in_specs=in_specs,
        out_specs=[pl.BlockSpec((seq, 2 * DV_B), lambda b, p: (b, p)), st_spec],
        out_shape=[jax.ShapeDtypeStruct((nb * seq, H_B * DV_B), f32),
                   jax.ShapeDtypeStruct((nb, 2, H_B, DK_B, DV_B), f32)],
        scratch_shapes=[pltpu.VMEM((2, seq, LANES), f32)] * 3 + [pltpu.VMEM((seq + 2 * CONV_PAD, LANES), f32)],
        compiler_params=pltpu.CompilerParams(vmem_limit_bytes=VMEM_LIMIT),
        name="delta",
    )(*args)


def _head_rms(x):
    return x * lax.rsqrt(jnp.mean(x * x, axis=-1, keepdims=True) + EPS)


def _ab_out_kernel(x_ref, hp_ref, hs_ref, op_ref, os_ref, om_ref, zd_ref, gm_ref, gd_ref, w_ref, mod_ref, o_ref,
                   *, gate_row, tm):
    hsum = _prompt_or_sample(hp_ref, hs_ref, tm)
    osum = _prompt_or_sample(op_ref, os_ref, tm)
    parts = []
    for h in range(H_A):
        sl = slice(h * DV_A, (h + 1) * DV_A)
        parts.append(_head_rms(hsum[:, sl]) * gm_ref[:, sl] * jax.nn.sigmoid(om_ref[:, sl].astype(f32)))
    for h in range(H_B):
        sl = slice(h * DV_B, (h + 1) * DV_B)
        parts.append(_head_rms(osum[:, sl]) * gd_ref[:, sl] * _silu(zd_ref[:, sl].astype(f32)))
    cat = jnp.concatenate(parts, axis=1).astype(bf16)
    y = jnp.dot(cat, w_ref[...].astype(bf16), preferred_element_type=f32)
    o_ref[...] = x_ref[...] + mod_ref[gate_row:gate_row + 1, :] * y


def _ab_out_call(x, hp, hs, op, os_, proj, gm_row, gd_row, w_out, mod_l, gate_row, tm=512):
    wide = H_A * DV_A
    om_blk = (2 * H_A * DQK_A) // wide + 1
    zd_blk = AB_MAIN // wide - 1
    return pl.pallas_call(
        functools.partial(_ab_out_kernel, gate_row=gate_row, tm=tm),
        grid=(N_TOK // tm,),
        in_specs=[pl.BlockSpec((tm, D_MODEL), lambda i: (i, 0))] + _split_specs(tm, wide) + _split_specs(tm, wide) + [
            pl.BlockSpec((tm, wide), lambda i: (i, om_blk)),
            pl.BlockSpec((tm, wide), lambda i: (i, zd_blk)),
            pl.BlockSpec((1, wide), lambda i: (0, 0)),
            pl.BlockSpec((1, wide), lambda i: (0, 0)),
            pl.BlockSpec((2 * wide, D_MODEL), lambda i: (0, 0)),
            pl.BlockSpec((None, N_ADA, D_MODEL), lambda i: (_mod_row_index(i, tm), 0, 0))],
        out_specs=pl.BlockSpec((tm, D_MODEL), lambda i: (i, 0)),
        out_shape=jax.ShapeDtypeStruct((N_TOK, D_MODEL), f32),
        compiler_params=pltpu.CompilerParams(vmem_limit_bytes=VMEM_LIMIT),
        name="ab_out",
    )(x, hp, hs, op, os_, proj, proj, gm_row, gd_row, w_out, mod_l)


def _na_out_kernel(x_ref, ap_ref, as_ref, w_ref, mod_ref, o_ref, *, gate_row, tm):
    a = _prompt_or_sample(ap_ref, as_ref, tm).astype(bf16)
    y = jnp.dot(a, w_ref[...].astype(bf16), preferred_element_type=f32)
    o_ref[...] = x_ref[...] + mod_ref[gate_row:gate_row + 1, :] * y


def _na_out_call(x, attn_p, attn_s, w_out, mod_l, gate_row, tm=512):
    return pl.pallas_call(
        functools.partial(_na_out_kernel, gate_row=gate_row, tm=tm),
        grid=(N_TOK // tm,),
        in_specs=[pl.BlockSpec((tm, D_MODEL), lambda i: (i, 0))] + _split_specs(tm, NA_W) + [
            pl.BlockSpec((NA_W, D_MODEL), lambda i: (0, 0)),
            pl.BlockSpec((None, N_ADA, D_MODEL), lambda i: (_mod_row_index(i, tm), 0, 0))],
        out_specs=pl.BlockSpec((tm, D_MODEL), lambda i: (i, 0)),
        out_shape=jax.ShapeDtypeStruct((N_TOK, D_MODEL), f32),
        compiler_params=pltpu.CompilerParams(vmem_limit_bytes=VMEM_LIMIT),
        name="na_out",
    )(x, attn_p, attn_s, w_out, mod_l)


CTX_HEADS = 8


def _ctx_attn_kernel(q_ref, k_ref, v_ref, o_ref, nk_ref, nv_ref):
    low = lax.broadcasted_iota(jnp.int32, (1, LANES), 1) < DH_C
    for pair in range(CTX_HEADS // 2):
        sl = slice(pair * LANES, (pair + 1) * LANES)
        q = q_ref[:, sl] * DH_C ** -0.5
        k_t = k_ref[:, sl].T
        v_t = v_ref[:, sl].T
        for hh in range(2):
            nk_ref[2 * pair + hh] = k_t[hh * DH_C:(hh + 1) * DH_C, :]
            nv_ref[2 * pair + hh] = v_t[hh * DH_C:(hh + 1) * DH_C, :]
        k = k_ref[:, sl].astype(bf16)
        v = v_ref[:, sl].astype(bf16)
        outs = []
        for hh in range(2):
            qh = jnp.where(low if hh == 0 else jnp.logical_not(low), q, 0.0).astype(bf16)
            s = lax.dot_general(qh, k, NT_DIMS, preferred_element_type=f32)
            p = jnp.exp(s - jnp.max(s, axis=1, keepdims=True))
            o = jnp.dot(p.astype(bf16), v, preferred_element_type=f32)
            outs.append(o / jnp.sum(p, axis=1, keepdims=True))
        o_ref[:, sl] = jnp.where(low, outs[0], outs[1]).astype(o_ref.dtype)


def _ctx_attn_call(proj):
    w = CTX_HEADS * DH_C
    nblk = NA_W // w
    return pl.pallas_call(
        _ctx_attn_kernel,
        grid=(BATCH, nblk),
        in_specs=[pl.BlockSpec((SEQ, w), lambda b, j: (b, j)),
                  pl.BlockSpec((SEQ, w), lambda b, j: (b, nblk + j)),
                  pl.BlockSpec((SEQ, w), lambda b, j: (b, 2 * nblk + j))],
        out_specs=[pl.BlockSpec((SEQ, w), lambda b, j: (b, j)),
                   pl.BlockSpec((None, CTX_HEADS, DH_C, SEQ), lambda b, j: (b, j, 0, 0)),
                   pl.BlockSpec((None, CTX_HEADS, DH_C, SEQ), lambda b, j: (b, j, 0, 0))],
        out_shape=[jax.ShapeDtypeStruct((N_PROMPT, NA_W), bf16),
                   jax.ShapeDtypeStruct((BATCH, H_C, DH_C, SEQ), f32),
                   jax.ShapeDtypeStruct((BATCH, H_C, DH_C, SEQ), f32)],
        compiler_params=pltpu.CompilerParams(vmem_limit_bytes=VMEM_LIMIT),
        name="ctx_attn",
    )(proj, proj, proj)


QROWS = 4
QBLK = QROWS * GRID_W
KROWS_MID = QROWS + WIN_R - 1
N_RIDX = 2 * WIN_R - 1
N_CIDX = 2 * WIN_C - 1
N_QBLK = GRID_ROWS // QROWS


def _nattn_bias_tables(rb_ref, tb_scr, bmid_scr, btop_scr, bbot_scr):
    qc = lax.broadcasted_iota(jnp.int32, (GRID_W, GRID_W), 0)
    kc = lax.broadcasted_iota(jnp.int32, (GRID_W, GRID_W), 1)
    cs = jnp.clip(qc - WIN_C // 2, 0, GRID_W - WIN_C)
    valid = (kc >= cs) & (kc < cs + WIN_C)
    neg = jnp.full((GRID_W, GRID_W), -jnp.inf, f32)
    for hh in range(2):
        for ri in range(N_RIDX):
            row = jnp.broadcast_to(rb_ref[hh, ri:ri + 1, :], (GRID_W, LANES))
            tile = pltpu.roll(row, LANES - (WIN_C - 1), 1, stride=1, stride_axis=0)[:, :GRID_W]
            tb_scr[ri] = jnp.where(valid, tile, -jnp.inf)
        for i in range(QROWS):
            rs = slice(i * GRID_W, (i + 1) * GRID_W)
            for jj in range(KROWS_MID):
                inside = 0 <= jj - i < WIN_R
                bmid_scr[hh, rs, jj * GRID_W:(jj + 1) * GRID_W] = tb_scr[jj - i + WIN_R // 2 - 1] if inside else neg
            for jj in range(WIN_R):
                btop_scr[hh, rs, jj * GRID_W:(jj + 1) * GRID_W] = tb_scr[jj - i + WIN_R - 1]
                bbot_scr[hh, rs, jj * GRID_W:(jj + 1) * GRID_W] = tb_scr[jj - i + WIN_R // 2 - 1]


def _nattn_kernel(rb_ref, q_ref, k_ref, v_ref, ck_ref, cv_ref, o_ref, tb_scr, bmid_scr, btop_scr, bbot_scr):
    scale = DH_C ** -0.5

    @pl.when(pl.program_id(1) == 0)
    def _():
        _nattn_bias_tables(rb_ref, tb_scr, bmid_scr, btop_scr, bbot_scr)

    low = lax.broadcasted_iota(jnp.int32, (1, LANES), 1) < DH_C
    kctx_t = jnp.concatenate([ck_ref[0], ck_ref[1]], axis=0).astype(bf16)
    vctx = jnp.concatenate([cv_ref[0], cv_ref[1]], axis=0).T.astype(bf16)

    def block(q_start, k_start, k_rows, bias_scr):
        nk = k_rows * GRID_W
        q = q_ref[pl.ds(q_start, QBLK), :] * scale
        ku = k_ref[pl.ds(k_start, nk), :].astype(bf16)
        vu = v_ref[pl.ds(k_start, nk), :].astype(bf16)
        outs = []
        for hh in range(2):
            qh = jnp.where(low if hh == 0 else jnp.logical_not(low), q, 0.0).astype(bf16)
            s_loc = lax.dot_general(qh, ku, NT_DIMS, preferred_element_type=f32) + bias_scr[hh]
            s_ctx = jnp.dot(qh, kctx_t, preferred_element_type=f32)
            m = jnp.maximum(jnp.max(s_loc, axis=1, keepdims=True), jnp.max(s_ctx, axis=1, keepdims=True))
            p_loc = jnp.exp(s_loc - m)
            p_ctx = jnp.exp(s_ctx - m)
            denom = jnp.sum(p_loc, axis=1, keepdims=True) + jnp.sum(p_ctx, axis=1, keepdims=True)
            o = (jnp.dot(p_loc.astype(bf16), vu, preferred_element_type=f32)
                 + jnp.dot(p_ctx.astype(bf16), vctx, preferred_element_type=f32))
            outs.append(o / denom)
        o_ref[pl.ds(q_start, QBLK), :] = jnp.where(low, outs[0], outs[1]).astype(o_ref.dtype)

    block(0, 0, WIN_R, btop_scr)

    def mid_body(blk, _):
        q_start = pl.multiple_of(blk * QBLK, QBLK)
        k_start = pl.multiple_of(blk * QBLK - (WIN_R // 2) * GRID_W, QBLK)
        block(q_start, k_start, KROWS_MID, bmid_scr)
        return 0

    lax.fori_loop(1, N_QBLK - 1, mid_body, 0)
    block((N_QBLK - 1) * QBLK, (GRID_ROWS - WIN_R) * GRID_W, WIN_R, bbot_scr)


def _nattn_call(proj, cache_k, cache_v, layer, rel_bias, off):
    npair = H_C // 2
    return pl.pallas_call(
        _nattn_kernel,
        grid=(npair, DEC_BATCH),
        in_specs=[pl.BlockSpec((2, 2 * WIN_R, LANES), lambda p, b: (p, 0, 0)),
                  pl.BlockSpec((DEC_SEQ, LANES), lambda p, b: (off + b, p)),
                  pl.BlockSpec((DEC_SEQ, LANES), lambda p, b: (off + b, npair + p)),
                  pl.BlockSpec((DEC_SEQ, LANES), lambda p, b: (off + b, 2 * npair + p)),
                  pl.BlockSpec((None, None, 2, DH_C, PAST_LEN), lambda p, b: (b, layer, p, 0, 0)),
                  pl.BlockSpec((None, None, 2, DH_C, PAST_LEN), lambda p, b: (b, layer, p, 0, 0))],
        out_specs=pl.BlockSpec((DEC_SEQ, LANES), lambda p, b: (b, p)),
        out_shape=jax.ShapeDtypeStruct((N_SAMPLE, NA_W), bf16),
        scratch_shapes=[pltpu.VMEM((N_RIDX, GRID_W, GRID_W), f32),
                        pltpu.VMEM((2, QBLK, KROWS_MID * GRID_W), f32),
                        pltpu.VMEM((2, QBLK, WIN_R * GRID_W), f32),
                        pltpu.VMEM((2, QBLK, WIN_R * GRID_W), f32)],
        compiler_params=pltpu.CompilerParams(dimension_semantics=("arbitrary", "arbitrary"),
                                             vmem_limit_bytes=VMEM_LIMIT),
        name="nattn",
    )(jnp.pad(rel_bias, ((0, 0), (0, 2 * WIN_R - N_RIDX), (0, LANES - N_CIDX))), proj, proj, proj, cache_k, cache_v)


def _lane_row(pieces):
    row = jnp.zeros((LANES,), f32)
    for off, vals in pieces:
        row = row.at[off:off + vals.shape[0]].set(vals.astype(f32))
    return row


def _param_rows(rows):
    out = jnp.zeros((8, LANES), f32)
    for r, row in enumerate(rows):
        out = out.at[r].set(row)
    return out


def kernel(x_prompt, x_sample, c, state_mlstm_C, state_mlstm_n, state_mlstm_m, state_delta_S, cache_na_k, cache_na_v, c_ctx, ada_w, ada_b, norm_g, ffn_wg, ffn_wu, ffn_wd, ab_w_in, ab_w_out, mlstm_b_i, mlstm_b_f, mlstm_norm_g, delta_conv_w, delta_a_log, delta_dt_bias, delta_norm_g, na_w_in, na_w_out, na_rel_bias, final_norm_g):
    xs = (x_prompt.reshape(N_PROMPT, D_MODEL), x_sample.reshape(N_SAMPLE, D_MODEL))
    mods = _ada_call(jnp.concatenate([c_ctx[None, :], c], axis=0), ada_w, ada_b)
    mods = mods.reshape(DEPTH, 3, N_ADA, D_MODEL)
    gf_row = final_norm_g.reshape(1, D_MODEL)
    s_off = N_PROMPT // DEC_SEQ
    new_c, new_n, new_m, new_s, new_k, new_v = [], [], [], [], [], []

    for l in range(DEPTH):
        mod_l = mods[l]
        a = l // 2
        x = _ffn_call(xs if l == 0 else (x,), mod_l, norm_g[l, 0].reshape(1, D_MODEL), ffn_wg, ffn_wu, ffn_wd, l, 0,
                      gf_row, rows=(0, 1, 2), final=False, split_out=False, tm=FFN_TM)[0]
        g_mix = norm_g[l, 1].reshape(1, D_MODEL)
        if l % 2 == 0:
            w_main, w_gate = _ab_weight_call(ab_w_in, a)
            proj, gates = _proj_call(x, mod_l, g_mix, w_main, w_gate, rows=(3, 4), out_dtype=bf16, w_transposed=True)

            par_m = _param_rows([_lane_row([(LANE_I, mlstm_b_i[a].reshape(-1))]),
                                 _lane_row([(LANE_F, mlstm_b_f[a].reshape(-1))])])
            par_d = _param_rows([_lane_row([(LANE_A, delta_a_log[a].reshape(-1))]),
                                 _lane_row([(LANE_A, delta_dt_bias[a].reshape(-1))])])
            conv_w8 = jnp.concatenate([delta_conv_w[a], jnp.zeros((8 - CONV_K, 3 * H_B * DK_B), f32)], axis=0)

            hp, cn, nn_, mn = _mlstm_call(proj, gates, par_m, BATCH, SEQ, 0, None)
            st = (state_mlstm_C[:, a], state_mlstm_n[:, a][..., None], state_mlstm_m[:, a][..., None, None])
            hs, _, _, _ = _mlstm_call(proj, gates, par_m, DEC_BATCH, DEC_SEQ, s_off, st)
            op, sn = _delta_call(proj, conv_w8, gates, par_d, BATCH, SEQ, 0, None)
            os_, _ = _delta_call(proj, conv_w8, gates, par_d, DEC_BATCH, DEC_SEQ, s_off, state_delta_S[:, a])
            new_c.append(cn)
            new_n.append(nn_[..., 0])
            new_m.append(mn[..., 0, 0])
            new_s.append(sn)
            x = _ab_out_call(x, hp, hs, op, os_, proj, mlstm_norm_g[a].reshape(1, -1),
                             jnp.tile(delta_norm_g[a], H_B).reshape(1, -1), ab_w_out[a], mod_l, gate_row=5)
        else:
            proj = _proj_call(x, mod_l, g_mix, na_w_in[a], None, rows=(3, 4), out_dtype=f32, w_transposed=False)[0]
            attn_p, kp_t, vp_t = _ctx_attn_call(proj)
            attn_s = _nattn_call(proj, jnp.swapaxes(cache_na_k, -1, -2), jnp.swapaxes(cache_na_v, -1, -2), a,
                                 na_rel_bias[a], s_off)
            new_k.append(jnp.swapaxes(kp_t, -1, -2))
            new_v.append(jnp.swapaxes(vp_t, -1, -2))
            x = _na_out_call(x, attn_p, attn_s, na_w_out[a], mod_l, gate_row=5)
        last = l == DEPTH - 1
        outs = _ffn_call((x,), mod_l, norm_g[l, 2].reshape(1, D_MODEL), ffn_wg, ffn_wu, ffn_wd, l, 1,
                         gf_row, rows=(6, 7, 8), final=last, split_out=last, tm=FFN_TM_SPLIT if last else FFN_TM)
        x = outs[0]

    y_prompt = outs[0].reshape(BATCH, SEQ, D_MODEL)
    y_sample = outs[1].reshape(DEC_BATCH, DEC_SEQ, D_MODEL)
    return (y_prompt, y_sample, jnp.stack(new_c, axis=1), jnp.stack(new_n, axis=1), jnp.stack(new_m, axis=1),
            jnp.stack(new_s, axis=1), jnp.stack(new_k, axis=1), jnp.stack(new_v, axis=1))
```

```python
import functools

import jax
import jax.numpy as jnp
from jax import lax
from jax.experimental import pallas as pl
from jax.experimental.pallas import tpu as pltpu

f32 = jnp.float32
bf16 = jnp.bfloat16

D_MODEL = 1024
BATCH = 16
SEQ = 256
DEPTH = 2
DEC_BATCH = 2
DEC_SEQ = 2048
PAST_LEN = 256
GRID_W = 64
GRID_ROWS = DEC_SEQ // GRID_W
D_FF = 2816
N_ADA = 9
EPS = 1e-6
CHUNK = 64
H_A, DQK_A, DV_A = 4, 64, 128
H_B, DK_B, DV_B = 4, 128, 128
CONV_K = 5
H_C, DH_C = 16, 64
WIN_R, WIN_C = 8, 16
NA_W = H_C * DH_C
assert DH_C ** -0.5 == 2.0 ** -3 and DQK_A ** -0.5 == 2.0 ** -3

N_PROMPT = BATCH * SEQ
N_SAMPLE = DEC_BATCH * DEC_SEQ
N_TOK = N_PROMPT + N_SAMPLE
AB_MAIN = 2 * H_A * DQK_A + 2 * H_A * DV_A + 2 * H_B * DK_B + 2 * H_B * DV_B
LANES = 128
VMEM_LIMIT = 56 * 1024 * 1024

LANE_I, LANE_F, LANE_BETA, LANE_A = 0, 8, 16, 24

NT_DIMS = (((1,), (1,)), ((), ()))
TN_DIMS = (((0,), (0,)), ((), ()))

MAX_CHUNKS_PER_TRIP = 4
FFN_TM = 2048
FFN_ROW_BLOCK = 512
PROJ_ROW_BLOCK = 256


def _softplus(x):
    return jnp.maximum(x, 0.0) + jnp.log1p(jnp.exp(-jnp.abs(x)))


def _log_sigmoid(x):
    return -_softplus(-x)


def _silu(x):
    return x * jax.nn.sigmoid(x)


def _mod_row_index(i, tm):
    n_p = N_PROMPT // tm
    per_b = DEC_SEQ // tm
    return jnp.where(i < n_p, 0, 1 + (i - n_p) // per_b)


def _modulated(x, g_row, shift_row, scale_row):
    y = x * lax.rsqrt(jnp.mean(x * x, axis=-1, keepdims=True) + EPS) * g_row
    return y * (1.0 + scale_row) + shift_row


def _ada_kernel(cb_ref, w_ref, b_ref, o_ref, s_scr):
    tn = w_ref.shape[-1]

    @pl.when((pl.program_id(0) == 0) & (pl.program_id(1) == 0))
    def _():
        cb = cb_ref[...]
        s_scr[...] = cb * jax.nn.sigmoid(cb)

    for j in range(tn // LANES):
        cols = slice(j * LANES, (j + 1) * LANES)
        w = w_ref[:, cols]
        for r in range(3):
            o_ref[r:r + 1, cols] = jnp.sum(w * s_scr[r], axis=0, keepdims=True) + b_ref[:, cols]


def _ada_call(cond3, ada_w, ada_b):
    tn = N_ADA * D_MODEL // 4
    cb = jnp.broadcast_to(cond3[:, :, None], (3, D_MODEL, LANES))
    return pl.pallas_call(
        _ada_kernel,
        grid=(DEPTH, N_ADA * D_MODEL // tn),
        in_specs=[pl.BlockSpec((3, D_MODEL, LANES), lambda l, j: (0, 0, 0)),
                  pl.BlockSpec((None, D_MODEL, tn), lambda l, j: (l, 0, j)),
                  pl.BlockSpec((None, 1, tn), lambda l, j: (l, 0, j))],
        out_specs=pl.BlockSpec((None, 3, tn), lambda l, j: (l, 0, j)),
        out_shape=jax.ShapeDtypeStruct((DEPTH, 3, N_ADA * D_MODEL), f32),
        scratch_shapes=[pltpu.VMEM((3, D_MODEL, LANES), f32)],
        compiler_params=pltpu.CompilerParams(dimension_semantics=("arbitrary", "arbitrary"),
                                             vmem_limit_bytes=VMEM_LIMIT),
        name="ada_mod",
    )(cb, ada_w, ada_b.reshape(DEPTH, 1, N_ADA * D_MODEL))


def _prompt_or_sample(p_ref, s_ref, tm):
    return jnp.where(pl.program_id(0) < N_PROMPT // tm, p_ref[...], s_ref[...])


def _split_specs(tm, width, n_grid_axes=1, single_buffered=False):
    n_p = N_PROMPT // tm
    kw = dict(pipeline_mode=pl.Buffered(1)) if single_buffered else {}
    if n_grid_axes == 1:
        return [pl.BlockSpec((tm, width), lambda i: (jnp.minimum(i, n_p - 1), 0), **kw),
                pl.BlockSpec((tm, width), lambda i: (jnp.maximum(i - n_p, 0), 0), **kw)]
    return [pl.BlockSpec((tm, width), lambda i, j: (jnp.minimum(i, n_p - 1), 0), **kw),
            pl.BlockSpec((tm, width), lambda i, j: (jnp.maximum(i - n_p, 0), 0), **kw)]


def _ffn_kernel(*refs, rows, final, split_in, split_out, tm):
    it = iter(refs)
    x_refs = [next(it) for _ in range(2 if split_in else 1)]
    mod_ref, g_ref, wg_ref, wu_ref, wd_ref, gf_ref = [next(it) for _ in range(6)]
    o_refs = [next(it) for _ in range(2 if split_out else 1)]
    h_scr = next(it)
    j = pl.program_id(1)
    last_j = pl.num_programs(1) - 1
    is_prompt = pl.program_id(0) < N_PROMPT // tm

    def x_rows(rs):
        if split_in:
            return jnp.where(is_prompt, x_refs[0][rs, :], x_refs[1][rs, :])
        return x_refs[0][rs, :]

    def step(first, last, acc_ref):
        wg = wg_ref[...].astype(bf16)
        wu = wu_ref[...].astype(bf16)
        wd = wd_ref[...].astype(bf16)
        for r in range(tm // FFN_ROW_BLOCK):
            rs = slice(r * FFN_ROW_BLOCK, (r + 1) * FFN_ROW_BLOCK)
            if first:
                h = _modulated(x_rows(rs), g_ref[...], mod_ref[rows[0]:rows[0] + 1, :],
                               mod_ref[rows[1]:rows[1] + 1, :]).astype(bf16)
                h_scr[rs, :] = h
            else:
                h = h_scr[rs, :]
            g = jnp.dot(h, wg, preferred_element_type=f32)
            u = jnp.dot(h, wu, preferred_element_type=f32)
            part = jnp.dot((_silu(g) * u).astype(bf16), wd, preferred_element_type=f32)
            acc = part if first else acc_ref[rs, :] + part
            if last:
                xn = x_rows(rs) + (0.5 * mod_ref[rows[2]:rows[2] + 1, :]) * acc
                if final:
                    xn = xn * lax.rsqrt(jnp.mean(xn * xn, axis=-1, keepdims=True) + EPS) * gf_ref[...]
                acc_ref[rs, :] = xn
            else:
                acc_ref[rs, :] = acc

    owners = [(is_prompt, o_refs[0]), (jnp.logical_not(is_prompt), o_refs[1])] if split_out else [(True, o_refs[0])]
    for owns, o_ref in owners:
        pl.when((j == 0) & owns)(functools.partial(step, True, False, o_ref))
        pl.when((j > 0) & (j < last_j) & owns)(functools.partial(step, False, False, o_ref))
        pl.when((j == last_j) & owns)(functools.partial(step, False, True, o_ref))


def _ffn_call(xs, mod_l, g_row, ffn_wg, ffn_wu, ffn_wd, layer, half, gf_row, rows, final, split_out, tm, tf=256):
    split_in = len(xs) == 2
    if split_in:
        x_specs = _split_specs(tm, D_MODEL, 2, single_buffered=True)
    elif split_out:
        x_specs = [pl.BlockSpec((tm, D_MODEL), lambda i, j: (i, 0), pipeline_mode=pl.Buffered(1))]
    else:
        x_specs = [pl.BlockSpec((tm, D_MODEL), lambda i, j: (i, 0))]
    if split_out:
        out_specs = _split_specs(tm, D_MODEL, 2)
        out_shape = [jax.ShapeDtypeStruct((N_PROMPT, D_MODEL), f32), jax.ShapeDtypeStruct((N_SAMPLE, D_MODEL), f32)]
    else:
        out_specs = [pl.BlockSpec((tm, D_MODEL), lambda i, j: (i, 0))]
        out_shape = [jax.ShapeDtypeStruct((N_TOK, D_MODEL), f32)]
    scratch = [pltpu.VMEM((tm, D_MODEL), bf16)]
    return pl.pallas_call(
        functools.partial(_ffn_kernel, rows=rows, final=final, split_in=split_in, split_out=split_out, tm=tm),
        grid=(N_TOK // tm, D_FF // tf),
        in_specs=x_specs + [
            pl.BlockSpec((None, N_ADA, D_MODEL), lambda i, j: (_mod_row_index(i, tm), 0, 0)),
            pl.BlockSpec((1, D_MODEL), lambda i, j: (0, 0)),
            pl.BlockSpec((None, None, D_MODEL, tf), lambda i, j: (layer, half, 0, j)),
            pl.BlockSpec((None, None, D_MODEL, tf), lambda i, j: (layer, half, 0, j)),
            pl.BlockSpec((None, None, tf, D_MODEL), lambda i, j: (layer, half, j, 0)),
            pl.BlockSpec((1, D_MODEL), lambda i, j: (0, 0))],
        out_specs=out_specs, out_shape=out_shape, scratch_shapes=scratch,
        compiler_params=pltpu.CompilerParams(dimension_semantics=("parallel", "arbitrary"),
                                             vmem_limit_bytes=VMEM_LIMIT),
        name="ffn",
    )(*xs, mod_l, g_row, ffn_wg, ffn_wu, ffn_wd, gf_row)


def _proj_kernel(*refs, rows, with_gates, w_transposed):
    if with_gates:
        x_ref, mod_ref, g_ref, w_ref, wgate_ref, o_ref, og_ref, h_scr = refs
    else:
        x_ref, mod_ref, g_ref, w_ref, o_ref, h_scr = refs
    j = pl.program_id(1)
    tm = x_ref.shape[0]
    dims = NT_DIMS if w_transposed else (((1,), (0,)), ((), ()))

    def matmul(h, w):
        return lax.dot_general(h, w, dims, preferred_element_type=f32)

    @pl.when(j == 0)
    def _():
        w = w_ref[...].astype(bf16)
        for r in range(tm // PROJ_ROW_BLOCK):
            rs = slice(r * PROJ_ROW_BLOCK, (r + 1) * PROJ_ROW_BLOCK)
            hb = _modulated(x_ref[rs, :], g_ref[...], mod_ref[rows[0]:rows[0] + 1, :],
                            mod_ref[rows[1]:rows[1] + 1, :]).astype(bf16)
            h_scr[rs, :] = hb
            o_ref[rs, :] = matmul(hb, w).astype(o_ref.dtype)
            if with_gates:
                og_ref[rs, :] = matmul(hb, wgate_ref[...].astype(bf16))

    @pl.when(j > 0)
    def _():
        o_ref[...] = matmul(h_scr[...], w_ref[...].astype(bf16)).astype(o_ref.dtype)


AB_I0 = 2 * H_A * DQK_A + 2 * H_A * DV_A
AB_D0 = AB_I0 + 4 * H_A
AB_B0 = AB_D0 + 2 * H_B * DK_B + 2 * H_B * DV_B
AB_IN = AB_B0 + 4 * H_B


def _ab_weight_kernel(w_ref, wm_ref, wg_ref):
    n_gate = 2 * H_A
    wm_ref[:AB_I0, :] = w_ref[:AB_I0, :]
    wm_ref[AB_I0:, :] = w_ref[AB_D0:AB_B0, :]
    wg_ref[:2 * n_gate, :] = w_ref[AB_I0:AB_D0, :]
    wg_ref[2 * n_gate:4 * n_gate, :] = w_ref[AB_B0:, :]
    wg_ref[4 * n_gate:, :] = jnp.zeros((LANES - 4 * n_gate, w_ref.shape[1]), f32)


def _ab_weight_call(ab_w_in, layer, tk=256):
    w_t = jnp.swapaxes(ab_w_in, 1, 2)
    return pl.pallas_call(
        _ab_weight_kernel,
        grid=(D_MODEL // tk,),
        in_specs=[pl.BlockSpec((None, AB_IN, tk), lambda i: (layer, 0, i))],
        out_specs=[pl.BlockSpec((AB_MAIN, tk), lambda i: (0, i)), pl.BlockSpec((LANES, tk), lambda i: (0, i))],
        out_shape=[jax.ShapeDtypeStruct((AB_MAIN, D_MODEL), f32), jax.ShapeDtypeStruct((LANES, D_MODEL), f32)],
        compiler_params=pltpu.CompilerParams(vmem_limit_bytes=VMEM_LIMIT),
        name="ab_weights",
    )(w_t)


def _column_tile(n, cap):
    return max(t for t in range(LANES, cap + 1, LANES) if n % t == 0)


def _proj_call(x, mod_l, g_row, w, w_gate, rows, out_dtype, w_transposed, tm=2048):
    n = w.shape[0] if w_transposed else w.shape[1]
    tn = _column_tile(n, 1024)
    with_gates = w_gate is not None
    w_spec = (pl.BlockSpec((tn, D_MODEL), lambda i, j: (j, 0)) if w_transposed
              else pl.BlockSpec((D_MODEL, tn), lambda i, j: (0, j)))
    in_specs = [pl.BlockSpec((tm, D_MODEL), lambda i, j: (i, 0)),
                pl.BlockSpec((None, N_ADA, D_MODEL), lambda i, j: (_mod_row_index(i, tm), 0, 0)),
                pl.BlockSpec((1, D_MODEL), lambda i, j: (0, 0)),
                w_spec]
    out_specs = [pl.BlockSpec((tm, tn), lambda i, j: (i, j))]
    out_shape = [jax.ShapeDtypeStruct((N_TOK, n), out_dtype)]
    args = [x, mod_l, g_row, w]
    if with_gates:
        in_specs.append(pl.BlockSpec(w_gate.shape, lambda i, j: (0, 0)))
        out_specs.append(pl.BlockSpec((tm, LANES), lambda i, j: (i, 0)))
        out_shape.append(jax.ShapeDtypeStruct((N_TOK, LANES), f32))
        args.append(w_gate)
    return pl.pallas_call(
        functools.partial(_proj_kernel, rows=rows, with_gates=with_gates, w_transposed=w_transposed),
        grid=(N_TOK // tm, n // tn),
        in_specs=in_specs, out_specs=out_specs, out_shape=out_shape,
        scratch_shapes=[pltpu.VMEM((tm, D_MODEL), bf16)],
        compiler_params=pltpu.CompilerParams(dimension_semantics=("parallel", "arbitrary"),
                                             vmem_limit_bytes=VMEM_LIMIT),
        name="in_proj",
    )(*args)


def _chunk_masks():
    r = lax.broadcasted_iota(jnp.int32, (CHUNK, CHUNK), 0)
    c = lax.broadcasted_iota(jnp.int32, (CHUNK, CHUNK), 1)
    return r >= c, r <= c, r > c, r < c


def _lane_col(x, lane, j):
    return jnp.sum(jnp.where(lane == j, x, 0.0), axis=1, keepdims=True)


def _head_lane_select(base, pair):
    r = lax.broadcasted_iota(jnp.int32, (8, LANES), 0)
    ln = lax.broadcasted_iota(jnp.int32, (8, LANES), 1)
    return jnp.where((ln == base + 2 * pair + r) & (r < 2), 1.0, 0.0)


def _split3_bf16(x):
    hi = x.astype(bf16)
    r = x - hi.astype(f32)
    mid = r.astype(bf16)
    return hi, mid, (r - mid.astype(f32)).astype(bf16)


def _thrice(a16):
    return jnp.concatenate([a16] * 3, axis=1)


def _mask_matmul_f32(mask16x3, x):
    return jnp.dot(mask16x3, jnp.concatenate(_split3_bf16(x), axis=0), preferred_element_type=f32)


def _select_rows_f32(sel16x3, x):
    return lax.dot_general(sel16x3, jnp.concatenate(_split3_bf16(x), axis=1), NT_DIMS, preferred_element_type=f32)


def _chunks_per_trip(nc):
    return min(MAX_CHUNKS_PER_TRIP, nc)


def _chunk_start(trip, u, d, nc):
    step = trip * _chunks_per_trip(nc) + u
    c = step if d == 0 else nc - 1 - step
    return pl.multiple_of(c * CHUNK, CHUNK)


def _mlstm_kernel(*refs, seq, zero_init):
    if zero_init:
        qp_ref, kp_ref, v0_ref, v1_ref, g_ref, par_ref, h_ref, cf_ref, nf_ref, mf_ref = refs
    else:
        (qp_ref, kp_ref, v0_ref, v1_ref, g_ref, par_ref, c0_ref, n0_ref, m0_ref,
         h_ref, cf_ref, nf_ref, mf_ref) = refs
    nc = seq // CHUNK
    pair = pl.program_id(1)
    lane = lax.broadcasted_iota(jnp.int32, (1, LANES), 1)
    tril, triu, _, _ = _chunk_masks()
    masks = (tril, triu)
    masks16 = (_thrice(tril.astype(bf16)), _thrice(triu.astype(bf16)))
    sel = [_thrice(_head_lane_select(LANE_F + d * H_A, pair).astype(bf16)) for d in range(2)]
    bi_row = par_ref[0:1, :]
    bf_row = par_ref[1:2, :]
    ones_col = jnp.where(lane == 0, 1.0, 0.0) + jnp.zeros((CHUNK, LANES), f32)
    v_refs = (v0_ref, v1_ref)
    streams = [(hh, d) for hh in range(2) for d in range(2)]

    h_ref[...] = jnp.zeros_like(h_ref)

    init = []
    for hh, d in streams:
        if zero_init:
            init.append((jnp.zeros((DQK_A, 2 * LANES), f32), jnp.zeros((1, 1), f32)))
        else:
            n_aug = jnp.where(lane == 0, n0_ref[d, hh], 0.0)
            init.append((jnp.concatenate([c0_ref[d, hh], n_aug], axis=1), m0_ref[d, hh]))

    def body(trip, carry):
        subs = range(_chunks_per_trip(nc))
        shared = {}
        for u in subs:
            for d in range(2):
                r0 = _chunk_start(trip, u, d, nc)
                gates = g_ref[pl.ds(r0, CHUNK), :]
                gi = gates + bi_row
                gf = _log_sigmoid(gates + bf_row)
                cum = _mask_matmul_f32(masks16[d], gf)
                stack = jnp.concatenate([cum, pltpu.roll(gi, LANE_F - LANE_I, axis=1)], axis=0)
                rows = _select_rows_f32(sel[d], stack)
                total = cum[CHUNK - 1:CHUNK, :] if d == 0 else cum[0:1, :]
                shared[u, d] = (r0, gi, cum, rows, total)
        chains = [(u, hh, d) for u in subs for hh, d in streams]
        st = {ch: {} for ch in chains}
        for ch in chains:
            u, hh, d = ch
            r0 = shared[u, d][0]
            s = st[ch]
            s["q"] = (qp_ref[pl.ds(r0, CHUNK), hh * DQK_A:(hh + 1) * DQK_A] * DQK_A ** -0.5).astype(bf16)
            s["k"] = kp_ref[pl.ds(r0, CHUNK), hh * DQK_A:(hh + 1) * DQK_A].astype(bf16)
            s["v_aug"] = jnp.concatenate([v_refs[hh][pl.ds(r0, CHUNK), :].astype(f32), ones_col], axis=1)
            s["qk"] = lax.dot_general(s["q"], s["k"], NT_DIMS, preferred_element_type=f32)
        for ch in chains:
            u, hh, d = ch
            _, gi, cum, rows, total = shared[u, d]
            s = st[ch]
            head = 2 * pair + hh
            jf = LANE_F + d * H_A + head
            ji = LANE_I + d * H_A + head
            bcol = _lane_col(cum, lane, jf)
            icol = _lane_col(gi, lane, ji)
            dmat = jnp.where(masks[d], bcol - rows[hh:hh + 1, :CHUNK] + rows[hh:hh + 1, CHUNK:], -jnp.inf)
            m_loc = jnp.max(dmat, axis=1, keepdims=True)
            s["sw"] = (s["qk"] * jnp.exp(dmat - m_loc)).astype(bf16)
            blast = _lane_col(total, lane, jf)
            gs = blast - bcol + icol
            ms_loc = jnp.max(gs, axis=0, keepdims=True)
            s["wv"] = (jnp.exp(gs - ms_loc) * s["v_aug"]).astype(bf16)
            s.update(bcol=bcol, m_loc=m_loc, blast=blast, ms_loc=ms_loc)
        for ch in chains:
            s = st[ch]
            s["num"] = jnp.dot(s["sw"], s["v_aug"].astype(bf16), preferred_element_type=f32)
            s["kv"] = lax.dot_general(s["k"], s["wv"], TN_DIMS, preferred_element_type=f32)
        state = list(carry)
        for u in subs:
            qc = [jnp.dot(st[u, hh, d]["q"], state[i][0].astype(bf16), preferred_element_type=f32)
                  for i, (hh, d) in enumerate(streams)]
            for i, (hh, d) in enumerate(streams):
                s = st[u, hh, d]
                c_aug, m = state[i]
                m_inter = s["bcol"] + m
                m_t = jnp.maximum(m_inter, s["m_loc"])
                nd = jnp.exp(s["m_loc"] - m_t) * s["num"] + jnp.exp(m_inter - m_t) * qc[i]
                den = nd[:, DV_A:DV_A + 1]
                hval = nd[:, :DV_A] / jnp.maximum(jnp.abs(den), jnp.exp(-m_t))
                r0 = shared[u, d][0]
                h_ref[pl.ds(r0, CHUNK), hh * DV_A:(hh + 1) * DV_A] += hval
                m_new = jnp.maximum(s["blast"] + m, s["ms_loc"])
                c_new = jnp.exp(s["blast"] + m - m_new) * c_aug + jnp.exp(s["ms_loc"] - m_new) * s["kv"]
                state[i] = (c_new, m_new)
        return tuple(state)

    final = lax.fori_loop(0, nc // _chunks_per_trip(nc), body, tuple(init))
    for i, (hh, d) in enumerate(streams):
        c_aug, m = final[i]
        cf_ref[d, hh] = c_aug[:, :DV_A]
        nf_ref[d, hh] = c_aug[:, DV_A:DV_A + 1]
        mf_ref[d, hh] = m


def _mlstm_call(proj, gates, par, nb, seq, off, state):
    zero_init = state is None
    blk = lambda col: pl.BlockSpec((seq, LANES), col)
    in_specs = [blk(lambda b, p: (off + b, p)),
                blk(lambda b, p: (off + b, 2 + p)),
                blk(lambda b, p: (off + b, 4 + 2 * p)),
                blk(lambda b, p: (off + b, 5 + 2 * p)),
                blk(lambda b, p: (off + b, 0)),
                pl.BlockSpec((8, LANES), lambda b, p: (0, 0))]
    st_specs = [pl.BlockSpec((None, 2, 2, DQK_A, DV_A), lambda b, p: (b, 0, p, 0, 0)),
                pl.BlockSpec((None, 2, 2, DQK_A, 1), lambda b, p: (b, 0, p, 0, 0)),
                pl.BlockSpec((None, 2, 2, 1, 1), lambda b, p: (b, 0, p, 0, 0))]
    args = [proj, proj, proj, proj, gates, par]
    if not zero_init:
        in_specs += st_specs
        args += list(state)
    return pl.pallas_call(
        functools.partial(_mlstm_kernel, seq=seq, zero_init=zero_init),
        grid=(nb, H_A // 2),
        in_specs=in_specs,
        out_specs=[pl.BlockSpec((seq, 2 * DV_A), lambda b, p: (b, p))] + st_specs,
        out_shape=[jax.ShapeDtypeStruct((nb * seq, H_A * DV_A), f32),
                   jax.ShapeDtypeStruct((nb, 2, H_A, DQK_A, DV_A), f32),
                   jax.ShapeDtypeStruct((nb, 2, H_A, DQK_A, 1), f32),
                   jax.ShapeDtypeStruct((nb, 2, H_A, 1, 1), f32)],
        compiler_params=pltpu.CompilerParams(vmem_limit_bytes=VMEM_LIMIT),
        name="mlstm",
    )(*args)


CONV_PAD = 8


def _short_conv_silu(x, w_ref, pad_scr, seq):
    pad_scr[:CONV_PAD, :] = jnp.zeros((CONV_PAD, LANES), f32)
    pad_scr[CONV_PAD + seq:, :] = jnp.zeros((CONV_PAD, LANES), f32)
    pad_scr[CONV_PAD:CONV_PAD + seq, :] = x
    acc = x * w_ref[CONV_K // 2:CONV_K // 2 + 1, :]
    for tap in range(CONV_K):
        delta = tap - CONV_K // 2
        if delta != 0:
            acc = acc + pad_scr[CONV_PAD + delta:CONV_PAD + delta + seq, :] * w_ref[tap:tap + 1, :]
    return _silu(acc)


def _l2_unit(y):
    return y * lax.rsqrt(jnp.sum(y * y, axis=-1, keepdims=True) + EPS)


def _split_bf16(x):
    hi = x.astype(bf16)
    return hi, (x - hi.astype(f32)).astype(bf16)


def _matmul_3pass(m, x):
    mh, ml = _split_bf16(m)
    xh, xl = _split_bf16(x)
    return jnp.dot(jnp.concatenate([mh, ml, mh], axis=1), jnp.concatenate([xh, xh, xl], axis=0),
                   preferred_element_type=f32)


def _unit_triangular_solves(ns, xs):
    levels = CHUNK.bit_length() - 1
    r = lax.broadcasted_iota(jnp.int32, (CHUNK, CHUNK), 0)
    c = lax.broadcasted_iota(jnp.int32, (CHUNK, CHUNK), 1)
    eye = jnp.where(r == c, 1.0, 0.0)
    ts = [eye + n for n in ns]
    ms = [_matmul_3pass(n, n) for n in ns]
    for lvl in range(1, levels):
        for i in range(len(ns)):
            if lvl < levels - 1:
                prod = _matmul_3pass(ms[i], jnp.concatenate([ts[i], ms[i]], axis=1))
                ts[i] = ts[i] + prod[:, :CHUNK]
                ms[i] = prod[:, CHUNK:]
            else:
                ts[i] = ts[i] + _matmul_3pass(ms[i], ts[i])
    return [_matmul_3pass(t, x) for t, x in zip(ts, xs)]


def _delta_kernel(*refs, seq, zero_init):
    if zero_init:
        (q0_ref, q1_ref, k0_ref, k1_ref, v0_ref, v1_ref, wq0_ref, wq1_ref, wk0_ref, wk1_ref, wv0_ref, wv1_ref,
         g_ref, par_ref, o_ref, sf_ref, q_scr, k_scr, v_scr, pad_scr) = refs
    else:
        (q0_ref, q1_ref, k0_ref, k1_ref, v0_ref, v1_ref, wq0_ref, wq1_ref, wk0_ref, wk1_ref, wv0_ref, wv1_ref,
         g_ref, par_ref, s0_ref, o_ref, sf_ref, q_scr, k_scr, v_scr, pad_scr) = refs
    nc = seq // CHUNK
    pair = pl.program_id(1)
    lane = lax.broadcasted_iota(jnp.int32, (1, LANES), 1)
    tril, triu, stril, striu = _chunk_masks()
    masks, smasks = (tril, triu), (stril, striu)
    masks16 = (_thrice(tril.astype(bf16)), _thrice(triu.astype(bf16)))
    sel = [_thrice(_head_lane_select(LANE_A + d * H_B, pair).astype(bf16)) for d in range(2)]
    neg_a_row = -jnp.exp(par_ref[0:1, :])
    dt_row = par_ref[1:2, :]
    streams = [(hh, d) for hh in range(2) for d in range(2)]

    for hh, (q_ref, k_ref, v_ref, wq_ref, wk_ref, wv_ref) in enumerate(
            ((q0_ref, k0_ref, v0_ref, wq0_ref, wk0_ref, wv0_ref), (q1_ref, k1_ref, v1_ref, wq1_ref, wk1_ref, wv1_ref))):
        q_scr[hh] = _l2_unit(_short_conv_silu(q_ref[...].astype(f32), wq_ref, pad_scr, seq)) * DK_B ** -0.5
        k_scr[hh] = _l2_unit(_short_conv_silu(k_ref[...].astype(f32), wk_ref, pad_scr, seq))
        v_scr[hh] = _short_conv_silu(v_ref[...].astype(f32), wv_ref, pad_scr, seq)

    o_ref[...] = jnp.zeros_like(o_ref)
    init = tuple(jnp.zeros((DK_B, DV_B), f32) if zero_init else s0_ref[d, hh] for hh, d in streams)

    def body(trip, carry):
        subs = range(_chunks_per_trip(nc))
        shared = {}
        for u in subs:
            for d in range(2):
                r0 = _chunk_start(trip, u, d, nc)
                gates = g_ref[pl.ds(r0, CHUNK), :]
                beta_all = jax.nn.sigmoid(gates)
                glog = neg_a_row * _softplus(gates + dt_row)
                cum = _mask_matmul_f32(masks16[d], glog)
                rows = _select_rows_f32(sel[d], cum)
                shared[u, d] = (r0, beta_all, cum, rows)
        chains = [(u, hh, d) for u in subs for hh, d in streams]
        st = {ch: {} for ch in chains}
        for ch in chains:
            u, hh, d = ch
            r0, beta_all, cum, rows = shared[u, d]
            s = st[ch]
            head = 2 * pair + hh
            beta = _lane_col(beta_all, lane, LANE_BETA + d * H_B + head)
            gcol = _lane_col(cum, lane, LANE_A + d * H_B + head)
            glast = gcol[CHUNK - 1:CHUNK, :] if d == 0 else gcol[0:1, :]
            q = q_scr[hh, pl.ds(r0, CHUNK), :]
            k = k_scr[hh, pl.ds(r0, CHUNK), :]
            v = v_scr[hh, pl.ds(r0, CHUNK), :]
            k16 = k.astype(bf16)
            kbeta = k * beta
            eg = jnp.exp(gcol)
            decay = jnp.exp(jnp.where(masks[d], gcol - rows[hh:hh + 1, :], -jnp.inf))
            kk = lax.dot_general(kbeta.astype(bf16), k16, NT_DIMS, preferred_element_type=f32)
            qk = lax.dot_general(q.astype(bf16), k16, NT_DIMS, preferred_element_type=f32)
            s["n"] = -jnp.where(smasks[d], kk * decay, 0.0)
            s["x"] = jnp.concatenate([v * beta, kbeta * eg], axis=1)
            s["qk"] = (qk * decay).astype(bf16)
            s["qg"] = (q * eg).astype(bf16)
            s["kd_t"] = (k * jnp.exp(glast - gcol)).T.astype(bf16)
            s["gl"] = jnp.exp(glast)
        solved = _unit_triangular_solves([st[ch]["n"] for ch in chains], [st[ch]["x"] for ch in chains])
        for ch, uw in zip(chains, solved):
            s = st[ch]
            s["u"] = uw[:, :DV_B]
            s["w_qg"] = jnp.concatenate([uw[:, DV_B:].astype(bf16), s["qg"]], axis=0)
        state = list(carry)
        for u in subs:
            ws = [jnp.dot(st[u, hh, d]["w_qg"], state[i].astype(bf16), preferred_element_type=f32)
                  for i, (hh, d) in enumerate(streams)]
            v_new = [(st[u, hh, d]["u"] - ws[i][:CHUNK]).astype(bf16) for i, (hh, d) in enumerate(streams)]
            for i, (hh, d) in enumerate(streams):
                s = st[u, hh, d]
                o = ws[i][CHUNK:] + jnp.dot(s["qk"], v_new[i], preferred_element_type=f32)
                r0 = shared[u, d][0]
                o_ref[pl.ds(r0, CHUNK), hh * DV_B:(hh + 1) * DV_B] += o
                state[i] = state[i] * s["gl"] + jnp.dot(s["kd_t"], v_new[i], preferred_element_type=f32)
        return tuple(state)

    final = lax.fori_loop(0, nc // _chunks_per_trip(nc), body, init)
    for i, (hh, d) in enumerate(streams):
        sf_ref[d, hh] = final[i]


def _delta_call(proj, conv_w8, gates, par, nb, seq, off, state):
    zero_init = state is None
    col0 = (2 * H_A * DQK_A + 2 * H_A * DV_A) // LANES
    blk = lambda col: pl.BlockSpec((seq, LANES), col)
    wblk = lambda col: pl.BlockSpec((8, LANES), col)
    in_specs = [blk(lambda b, p: (off + b, col0 + 2 * p)), blk(lambda b, p: (off + b, col0 + 2 * p + 1)),
                blk(lambda b, p: (off + b, col0 + H_B + 2 * p)), blk(lambda b, p: (off + b, col0 + H_B + 2 * p + 1)),
                blk(lambda b, p: (off + b, col0 + 2 * H_B + 2 * p)), blk(lambda b, p: (off + b, col0 + 2 * H_B + 2 * p + 1)),
                wblk(lambda b, p: (0, 2 * p)), wblk(lambda b, p: (0, 2 * p + 1)),
                wblk(lambda b, p: (0, H_B + 2 * p)), wblk(lambda b, p: (0, H_B + 2 * p + 1)),
                wblk(lambda b, p: (0, 2 * H_B + 2 * p)), wblk(lambda b, p: (0, 2 * H_B + 2 * p + 1)),
                blk(lambda b, p: (off + b, 0)),
                pl.BlockSpec((8, LANES), lambda b, p: (0, 0))]
    st_spec = pl.BlockSpec((None, 2, 2, DK_B, DV_B), lambda b, p: (b, 0, p, 0, 0))
    args = [proj] * 6 + [conv_w8] * 6 + [gates, par]
    if not zero_init:
        in_specs.append(st_spec)
        args.append(state)
    return pl.pallas_call(
        functools.partial(_delta_kernel, seq=seq, zero_init=zero_init),
        grid=(nb, H_B // 2),
        in_specs=in_specs,
        out_specs=[pl.BlockSpec((seq, 2 * DV_B), lambda b, p: (b, p)), st_spec],
        out_shape=[jax.ShapeDtypeStruct((nb * seq, H_B * DV_B), f32),
                   jax.ShapeDtypeStruct((nb, 2, H_B, DK_B, DV_B), f32)],
        scratch_shapes=[pltpu.VMEM((2, seq, LANES), f32)] * 3 + [pltpu.VMEM((seq + 2 * CONV_PAD, LANES), f32)],
        compiler_params=pltpu.CompilerParams(vmem_limit_bytes=VMEM_LIMIT),
        name="delta",
    )(*args)


def _head_rms(x):
    return x * lax.rsqrt(jnp.mean(x * x, axis=-1, keepdims=True) + EPS)


def _ab_out_kernel(x_ref, hp_ref, hs_ref, op_ref, os_ref, om_ref, zd_ref, gm_ref, gd_ref, w_ref, mod_ref, o_ref,
                   *, gate_row, tm):
    hsum = _prompt_or_sample(hp_ref, hs_ref, tm)
    osum = _prompt_or_sample(op_ref, os_ref, tm)
    parts = []
    for h in range(H_A):
        sl = slice(h * DV_A, (h + 1) * DV_A)
        parts.append(_head_rms(hsum[:, sl]) * gm_ref[:, sl] * jax.nn.sigmoid(om_ref[:, sl].astype(f32)))
    for h in range(H_B):
        sl = slice(h * DV_B, (h + 1) * DV_B)
        parts.append(_head_rms(osum[:, sl]) * gd_ref[:, sl] * _silu(zd_ref[:, sl].astype(f32)))
    cat = jnp.concatenate(parts, axis=1).astype(bf16)
    y = jnp.dot(cat, w_ref[...].astype(bf16), preferred_element_type=f32)
    o_ref[...] = x_ref[...] + mod_ref[gate_row:gate_row + 1, :] * y


def _ab_out_call(x, hp, hs, op, os_, proj, gm_row, gd_row, w_out, mod_l, gate_row, tm=1024):
    wide = H_A * DV_A
    om_blk = (2 * H_A * DQK_A) // wide + 1
    zd_blk = AB_MAIN // wide - 1
    return pl.pallas_call(
        functools.partial(_ab_out_kernel, gate_row=gate_row, tm=tm),
        grid=(N_TOK // tm,),
        in_specs=[pl.BlockSpec((tm, D_MODEL), lambda i: (i, 0))] + _split_specs(tm, wide) + _split_specs(tm, wide) + [
            pl.BlockSpec((tm, wide), lambda i: (i, om_blk)),
            pl.BlockSpec((tm, wide), lambda i: (i, zd_blk)),
            pl.BlockSpec((1, wide), lambda i: (0, 0)),
            pl.BlockSpec((1, wide), lambda i: (0, 0)),
            pl.BlockSpec((2 * wide, D_MODEL), lambda i: (0, 0)),
            pl.BlockSpec((None, N_ADA, D_MODEL), lambda i: (_mod_row_index(i, tm), 0, 0))],
        out_specs=pl.BlockSpec((tm, D_MODEL), lambda i: (i, 0)),
        out_shape=jax.ShapeDtypeStruct((N_TOK, D_MODEL), f32),
        compiler_params=pltpu.CompilerParams(vmem_limit_bytes=VMEM_LIMIT),
        name="ab_out",
    )(x, hp, hs, op, os_, proj, proj, gm_row, gd_row, w_out, mod_l)


def _na_out_kernel(x_ref, ap_ref, as_ref, w_ref, mod_ref, o_ref, *, gate_row, tm):
    a = _prompt_or_sample(ap_ref, as_ref, tm).astype(bf16)
    y = jnp.dot(a, w_ref[...].astype(bf16), preferred_element_type=f32)
    o_ref[...] = x_ref[...] + mod_ref[gate_row:gate_row + 1, :] * y


def _na_out_call(x, attn_p, attn_s, w_out, mod_l, gate_row, tm=1024):
    return pl.pallas_call(
        functools.partial(_na_out_kernel, gate_row=gate_row, tm=tm),
        grid=(N_TOK // tm,),
        in_specs=[pl.BlockSpec((tm, D_MODEL), lambda i: (i, 0))] + _split_specs(tm, NA_W) + [
            pl.BlockSpec((NA_W, D_MODEL), lambda i: (0, 0)),
            pl.BlockSpec((None, N_ADA, D_MODEL), lambda i: (_mod_row_index(i, tm), 0, 0))],
        out_specs=pl.BlockSpec((tm, D_MODEL), lambda i: (i, 0)),
        out_shape=jax.ShapeDtypeStruct((N_TOK, D_MODEL), f32),
        compiler_params=pltpu.CompilerParams(vmem_limit_bytes=VMEM_LIMIT),
        name="na_out",
    )(x, attn_p, attn_s, w_out, mod_l)


CTX_HEADS = 8


def _ctx_attn_kernel(q_ref, k_ref, v_ref, o_ref, nk_ref, nv_ref):
    low = lax.broadcasted_iota(jnp.int32, (1, LANES), 1) < DH_C
    for pair in range(CTX_HEADS // 2):
        sl = slice(pair * LANES, (pair + 1) * LANES)
        q = q_ref[:, sl] * DH_C ** -0.5
        k_t = k_ref[:, sl].T
        v_t = v_ref[:, sl].T
        for hh in range(2):
            nk_ref[2 * pair + hh] = k_t[hh * DH_C:(hh + 1) * DH_C, :]
            nv_ref[2 * pair + hh] = v_t[hh * DH_C:(hh + 1) * DH_C, :]
        k = k_ref[:, sl].astype(bf16)
        v = v_ref[:, sl].astype(bf16)
        outs = []
        for hh in range(2):
            qh = jnp.where(low if hh == 0 else jnp.logical_not(low), q, 0.0).astype(bf16)
            s = lax.dot_general(qh, k, NT_DIMS, preferred_element_type=f32)
            p = jnp.exp(s - jnp.max(s, axis=1, keepdims=True))
            o = jnp.dot(p.astype(bf16), v, preferred_element_type=f32)
            outs.append(o / jnp.sum(p, axis=1, keepdims=True))
        o_ref[:, sl] = jnp.where(low, outs[0], outs[1]).astype(o_ref.dtype)


def _ctx_attn_call(proj):
    w = CTX_HEADS * DH_C
    nblk = NA_W // w
    return pl.pallas_call(
        _ctx_attn_kernel,
        grid=(BATCH, nblk),
        in_specs=[pl.BlockSpec((SEQ, w), lambda b, j: (b, j)),
                  pl.BlockSpec((SEQ, w), lambda b, j: (b, nblk + j)),
                  pl.BlockSpec((SEQ, w), lambda b, j: (b, 2 * nblk + j))],
        out_specs=[pl.BlockSpec((SEQ, w), lambda b, j: (b, j)),
                   pl.BlockSpec((None, CTX_HEADS, DH_C, SEQ), lambda b, j: (b, j, 0, 0)),
                   pl.BlockSpec((None, CTX_HEADS, DH_C, SEQ), lambda b, j: (b, j, 0, 0))],
        out_shape=[jax.ShapeDtypeStruct((N_PROMPT, NA_W), bf16),
                   jax.ShapeDtypeStruct((BATCH, H_C, DH_C, SEQ), f32),
                   jax.ShapeDtypeStruct((BATCH, H_C, DH_C, SEQ), f32)],
        compiler_params=pltpu.CompilerParams(vmem_limit_bytes=VMEM_LIMIT),
        name="ctx_attn",
    )(proj, proj, proj)


QROWS = 4
QBLK = QROWS * GRID_W
KROWS_MID = QROWS + WIN_R - 1
N_RIDX = 2 * WIN_R - 1
N_CIDX = 2 * WIN_C - 1
N_QBLK = GRID_ROWS // QROWS


def _nattn_bias_tables(rb_ref, tb_scr, bmid_scr, btop_scr, bbot_scr):
    qc = lax.broadcasted_iota(jnp.int32, (GRID_W, GRID_W), 0)
    kc = lax.broadcasted_iota(jnp.int32, (GRID_W, GRID_W), 1)
    cs = jnp.clip(qc - WIN_C // 2, 0, GRID_W - WIN_C)
    valid = (kc >= cs) & (kc < cs + WIN_C)
    neg = jnp.full((GRID_W, GRID_W), -jnp.inf, f32)
    for hh in range(2):
        for ri in range(N_RIDX):
            row = jnp.broadcast_to(rb_ref[hh, ri:ri + 1, :], (GRID_W, LANES))
            tile = pltpu.roll(row, LANES - (WIN_C - 1), 1, stride=1, stride_axis=0)[:, :GRID_W]
            tb_scr[ri] = jnp.where(valid, tile, -jnp.inf)
        for i in range(QROWS):
            rs = slice(i * GRID_W, (i + 1) * GRID_W)
            for jj in range(KROWS_MID):
                inside = 0 <= jj - i < WIN_R
                bmid_scr[hh, rs, jj * GRID_W:(jj + 1) * GRID_W] = tb_scr[jj - i + WIN_R // 2 - 1] if inside else neg
            for jj in range(WIN_R):
                btop_scr[hh, rs, jj * GRID_W:(jj + 1) * GRID_W] = tb_scr[jj - i + WIN_R - 1]
                bbot_scr[hh, rs, jj * GRID_W:(jj + 1) * GRID_W] = tb_scr[jj - i + WIN_R // 2 - 1]


def _nattn_kernel(rb_ref, q_ref, k_ref, v_ref, ck_ref, cv_ref, o_ref, tb_scr, bmid_scr, btop_scr, bbot_scr):
    scale = DH_C ** -0.5

    @pl.when(pl.program_id(1) == 0)
    def _():
        _nattn_bias_tables(rb_ref, tb_scr, bmid_scr, btop_scr, bbot_scr)

    low = lax.broadcasted_iota(jnp.int32, (1, LANES), 1) < DH_C
    kctx_t = jnp.concatenate([ck_ref[0], ck_ref[1]], axis=0).astype(bf16)
    vctx = jnp.concatenate([cv_ref[0], cv_ref[1]], axis=0).T.astype(bf16)

    def block(q_start, k_start, k_rows, bias_scr):
        nk = k_rows * GRID_W
        q = q_ref[pl.ds(q_start, QBLK), :] * scale
        ku = k_ref[pl.ds(k_start, nk), :].astype(bf16)
        vu = v_ref[pl.ds(k_start, nk), :].astype(bf16)
        outs = []
        for hh in range(2):
            qh = jnp.where(low if hh == 0 else jnp.logical_not(low), q, 0.0).astype(bf16)
            s_loc = lax.dot_general(qh, ku, NT_DIMS, preferred_element_type=f32) + bias_scr[hh]
            s_ctx = jnp.dot(qh, kctx_t, preferred_element_type=f32)
            m = jnp.maximum(jnp.max(s_loc, axis=1, keepdims=True), jnp.max(s_ctx, axis=1, keepdims=True))
            p_loc = jnp.exp(s_loc - m)
            p_ctx = jnp.exp(s_ctx - m)
            denom = jnp.sum(p_loc, axis=1, keepdims=True) + jnp.sum(p_ctx, axis=1, keepdims=True)
            o = (jnp.dot(p_loc.astype(bf16), vu, preferred_element_type=f32)
                 + jnp.dot(p_ctx.astype(bf16), vctx, preferred_element_type=f32))
            outs.append(o / denom)
        o_ref[pl.ds(q_start, QBLK), :] = jnp.where(low, outs[0], outs[1]).astype(o_ref.dtype)

    block(0, 0, WIN_R, btop_scr)

    def mid_body(blk, _):
        q_start = pl.multiple_of(blk * QBLK, QBLK)
        k_start = pl.multiple_of(blk * QBLK - (WIN_R // 2) * GRID_W, QBLK)
        block(q_start, k_start, KROWS_MID, bmid_scr)
        return 0

    lax.fori_loop(1, N_QBLK - 1, mid_body, 0)
    block((N_QBLK - 1) * QBLK, (GRID_ROWS - WIN_R) * GRID_W, WIN_R, bbot_scr)


def _nattn_call(proj, cache_k, cache_v, layer, rel_bias, off):
    npair = H_C // 2
    return pl.pallas_call(
        _nattn_kernel,
        grid=(npair, DEC_BATCH),
        in_specs=[pl.BlockSpec((2, 2 * WIN_R, LANES), lambda p, b: (p, 0, 0)),
                  pl.BlockSpec((DEC_SEQ, LANES), lambda p, b: (off + b, p)),
                  pl.BlockSpec((DEC_SEQ, LANES), lambda p, b: (off + b, npair + p)),
                  pl.BlockSpec((DEC_SEQ, LANES), lambda p, b: (off + b, 2 * npair + p)),
                  pl.BlockSpec((None, None, 2, DH_C, PAST_LEN), lambda p, b: (b, layer, p, 0, 0)),
                  pl.BlockSpec((None, None, 2, DH_C, PAST_LEN), lambda p, b: (b, layer, p, 0, 0))],
        out_specs=pl.BlockSpec((DEC_SEQ, LANES), lambda p, b: (b, p)),
        out_shape=jax.ShapeDtypeStruct((N_SAMPLE, NA_W), bf16),
        scratch_shapes=[pltpu.VMEM((N_RIDX, GRID_W, GRID_W), f32),
                        pltpu.VMEM((2, QBLK, KROWS_MID * GRID_W), f32),
                        pltpu.VMEM((2, QBLK, WIN_R * GRID_W), f32),
                        pltpu.VMEM((2, QBLK, WIN_R * GRID_W), f32)],
        compiler_params=pltpu.CompilerParams(dimension_semantics=("arbitrary", "arbitrary"),
                                             vmem_limit_bytes=VMEM_LIMIT),
        name="nattn",
    )(jnp.pad(rel_bias, ((0, 0), (0, 2 * WIN_R - N_RIDX), (0, LANES - N_CIDX))), proj, proj, proj, cache_k, cache_v)


def _lane_row(pieces):
    row = jnp.zeros((LANES,), f32)
    for off, vals in pieces:
        row = row.at[off:off + vals.shape[0]].set(vals.astype(f32))
    return row


def _param_rows(rows):
    out = jnp.zeros((8, LANES), f32)
    for r, row in enumerate(rows):
        out = out.at[r].set(row)
    return out


def kernel(x_prompt, x_sample, c, state_mlstm_C, state_mlstm_n, state_mlstm_m, state_delta_S, cache_na_k, cache_na_v, c_ctx, ada_w, ada_b, norm_g, ffn_wg, ffn_wu, ffn_wd, ab_w_in, ab_w_out, mlstm_b_i, mlstm_b_f, mlstm_norm_g, delta_conv_w, delta_a_log, delta_dt_bias, delta_norm_g, na_w_in, na_w_out, na_rel_bias, final_norm_g):
    xs = (x_prompt.reshape(N_PROMPT, D_MODEL), x_sample.reshape(N_SAMPLE, D_MODEL))
    mods = _ada_call(jnp.concatenate([c_ctx[None, :], c], axis=0), ada_w, ada_b)
    mods = mods.reshape(DEPTH, 3, N_ADA, D_MODEL)
    gf_row = final_norm_g.reshape(1, D_MODEL)
    s_off = N_PROMPT // DEC_SEQ
    new_c, new_n, new_m, new_s, new_k, new_v = [], [], [], [], [], []

    for l in range(DEPTH):
        mod_l = mods[l]
        a = l // 2
        x = _ffn_call(xs if l == 0 else (x,), mod_l, norm_g[l, 0].reshape(1, D_MODEL), ffn_wg, ffn_wu, ffn_wd, l, 0,
                      gf_row, rows=(0, 1, 2), final=False, split_out=False, tm=FFN_TM)[0]
        g_mix = norm_g[l, 1].reshape(1, D_MODEL)
        if l % 2 == 0:
            w_main, w_gate = _ab_weight_call(ab_w_in, a)
            proj, gates = _proj_call(x, mod_l, g_mix, w_main, w_gate, rows=(3, 4), out_dtype=bf16, w_transposed=True)

            par_m = _param_rows([_lane_row([(LANE_I, mlstm_b_i[a].reshape(-1))]),
                                 _lane_row([(LANE_F, mlstm_b_f[a].reshape(-1))])])
            par_d = _param_rows([_lane_row([(LANE_A, delta_a_log[a].reshape(-1))]),
                                 _lane_row([(LANE_A, delta_dt_bias[a].reshape(-1))])])
            conv_w8 = jnp.concatenate([delta_conv_w[a], jnp.zeros((8 - CONV_K, 3 * H_B * DK_B), f32)], axis=0)

            hp, cn, nn_, mn = _mlstm_call(proj, gates, par_m, BATCH, SEQ, 0, None)
            st = (state_mlstm_C[:, a], state_mlstm_n[:, a][..., None], state_mlstm_m[:, a][..., None, None])
            hs, _, _, _ = _mlstm_call(proj, gates, par_m, DEC_BATCH, DEC_SEQ, s_off, st)
            op, sn = _delta_call(proj, conv_w8, gates, par_d, BATCH, SEQ, 0, None)
            os_, _ = _delta_call(proj, conv_w8, gates, par_d, DEC_BATCH, DEC_SEQ, s_off, state_delta_S[:, a])
            new_c.append(cn)
            new_n.append(nn_[..., 0])
            new_m.append(mn[..., 0, 0])
            new_s.append(sn)
            x = _ab_out_call(x, hp, hs, op, os_, proj, mlstm_norm_g[a].reshape(1, -1),
                             jnp.tile(delta_norm_g[a], H_B).reshape(1, -1), ab_w_out[a], mod_l, gate_row=5)
        else:
            proj = _proj_call(x, mod_l, g_mix, na_w_in[a], None, rows=(3, 4), out_dtype=f32, w_transposed=False)[0]
            attn_p, kp_t, vp_t = _ctx_attn_call(proj)
            attn_s = _nattn_call(proj, jnp.swapaxes(cache_na_k, -1, -2), jnp.swapaxes(cache_na_v, -1, -2), a,
                                 na_rel_bias[a], s_off)
            new_k.append(jnp.swapaxes(kp_t, -1, -2))
            new_v.append(jnp.swapaxes(vp_t, -1, -2))
            x = _na_out_call(x, attn_p, attn_s, na_w_out[a], mod_l, gate_row=5)
        last = l == DEPTH - 1
        outs = _ffn_call((x,), mod_l, norm_g[l, 2].reshape(1, D_MODEL), ffn_wg, ffn_wu, ffn_wd, l, 1,
                         gf_row, rows=(6, 7, 8), final=last, split_out=last, tm=FFN_TM)
        x = outs[0]

    y_prompt = outs[0].reshape(BATCH, SEQ, D_MODEL)
    y_sample = outs[1].reshape(DEC_BATCH, DEC_SEQ, D_MODEL)
    return (y_prompt, y_sample, jnp.stack(new_c, axis=1), jnp.stack(new_n, axis=1), jnp.stack(new_m, axis=1),
            jnp.stack(new_s, axis=1), jnp.stack(new_k, axis=1), jnp.stack(new_v, axis=1))
```

```python
import functools

import jax
import jax.numpy as jnp
from jax import lax
from jax.experimental import pallas as pl
from jax.experimental.pallas import tpu as pltpu

f32 = jnp.float32
bf16 = jnp.bfloat16

D_MODEL = 1024
BATCH = 16
SEQ = 256
DEPTH = 2
DEC_BATCH = 2
DEC_SEQ = 2048
PAST_LEN = 256
GRID_W = 64
GRID_ROWS = DEC_SEQ // GRID_W
D_FF = 2816
N_ADA = 9
EPS = 1e-6
CHUNK = 64
H_A, DQK_A, DV_A = 4, 64, 128
H_B, DK_B, DV_B = 4, 128, 128
CONV_K = 5
H_C, DH_C = 16, 64
WIN_R, WIN_C = 8, 16
NA_W = H_C * DH_C
assert DH_C ** -0.5 == 2.0 ** -3 and DQK_A ** -0.5 == 2.0 ** -3

N_PROMPT = BATCH * SEQ
N_SAMPLE = DEC_BATCH * DEC_SEQ
N_TOK = N_PROMPT + N_SAMPLE
AB_MAIN = 2 * H_A * DQK_A + 2 * H_A * DV_A + 2 * H_B * DK_B + 2 * H_B * DV_B
LANES = 128
VMEM_LIMIT = 56 * 1024 * 1024
FFN_VMEM_SLACK = 4 * 1024 * 1024
SUBLANES = 8

LANE_I, LANE_F, LANE_BETA, LANE_A = 0, 8, 16, 24

NT_DIMS = (((1,), (1,)), ((), ()))
TN_DIMS = (((0,), (0,)), ((), ()))

MAX_CHUNKS_PER_TRIP = 4
FFN_TM = 2048
FFN_ROW_BLOCK = 512
PROJ_ROW_BLOCK = 256


def _softplus(x):
    return jnp.maximum(x, 0.0) + jnp.log1p(jnp.exp(-jnp.abs(x)))


def _log_sigmoid(x):
    return -_softplus(-x)


def _silu(x):
    return x * jax.nn.sigmoid(x)


def _mod_row_index(i, tm):
    n_p = N_PROMPT // tm
    per_b = DEC_SEQ // tm
    return jnp.where(i < n_p, 0, 1 + (i - n_p) // per_b)


def _modulated(x, g_row, shift_row, scale_row):
    y = x * lax.rsqrt(jnp.mean(x * x, axis=-1, keepdims=True) + EPS) * g_row
    return y * (1.0 + scale_row) + shift_row


def _ada_kernel(cb_ref, w_ref, b_ref, o_ref, s_scr):
    tn = w_ref.shape[-1]

    @pl.when((pl.program_id(0) == 0) & (pl.program_id(1) == 0))
    def _():
        cb = cb_ref[...]
        s_scr[...] = cb * jax.nn.sigmoid(cb)

    for j in range(tn // LANES):
        cols = slice(j * LANES, (j + 1) * LANES)
        w = w_ref[:, cols]
        for r in range(3):
            o_ref[r:r + 1, cols] = jnp.sum(w * s_scr[r], axis=0, keepdims=True) + b_ref[:, cols]


def _ada_call(cond3, ada_w, ada_b):
    tn = N_ADA * D_MODEL // 4
    cb = jnp.broadcast_to(cond3[:, :, None], (3, D_MODEL, LANES))
    return pl.pallas_call(
        _ada_kernel,
        grid=(DEPTH, N_ADA * D_MODEL // tn),
        in_specs=[pl.BlockSpec((3, D_MODEL, LANES), lambda l, j: (0, 0, 0)),
                  pl.BlockSpec((None, D_MODEL, tn), lambda l, j: (l, 0, j)),
                  pl.BlockSpec((None, 1, tn), lambda l, j: (l, 0, j))],
        out_specs=pl.BlockSpec((None, 3, tn), lambda l, j: (l, 0, j)),
        out_shape=jax.ShapeDtypeStruct((DEPTH, 3, N_ADA * D_MODEL), f32),
        scratch_shapes=[pltpu.VMEM((3, D_MODEL, LANES), f32)],
        compiler_params=pltpu.CompilerParams(dimension_semantics=("arbitrary", "arbitrary"),
                                             vmem_limit_bytes=VMEM_LIMIT),
        name="ada_mod",
    )(cb, ada_w, ada_b.reshape(DEPTH, 1, N_ADA * D_MODEL))


def _prompt_or_sample(p_ref, s_ref, tm):
    return jnp.where(pl.program_id(0) < N_PROMPT // tm, p_ref[...], s_ref[...])


def _split_specs(tm, width, n_grid_axes=1):
    n_p = N_PROMPT // tm
    if n_grid_axes == 1:
        return [pl.BlockSpec((tm, width), lambda i: (jnp.minimum(i, n_p - 1), 0)),
                pl.BlockSpec((tm, width), lambda i: (jnp.maximum(i - n_p, 0), 0))]
    return [pl.BlockSpec((tm, width), lambda i, j: (jnp.minimum(i, n_p - 1), 0)),
            pl.BlockSpec((tm, width), lambda i, j: (jnp.maximum(i - n_p, 0), 0))]


def _ffn_kernel(*refs, rows, final, split_in, split_out, tm):
    it = iter(refs)
    x_refs = [next(it) for _ in range(2 if split_in else 1)]
    mod_ref, g_ref, wg_ref, wu_ref, wd_ref, gf_ref = [next(it) for _ in range(6)]
    o_refs = [next(it) for _ in range(2 if split_out else 1)]
    h_scr = next(it)
    j = pl.program_id(1)
    last_j = pl.num_programs(1) - 1
    is_prompt = pl.program_id(0) < N_PROMPT // tm

    def x_rows(rs):
        if split_in:
            return jnp.where(is_prompt, x_refs[0][rs, :], x_refs[1][rs, :])
        return x_refs[0][rs, :]

    def step(first, last, acc_ref):
        wg = wg_ref[...].astype(bf16)
        wu = wu_ref[...].astype(bf16)
        wd = wd_ref[...].astype(bf16)
        for r in range(tm // FFN_ROW_BLOCK):
            rs = slice(r * FFN_ROW_BLOCK, (r + 1) * FFN_ROW_BLOCK)
            if first:
                h = _modulated(x_rows(rs), g_ref[...], mod_ref[rows[0]:rows[0] + 1, :],
                               mod_ref[rows[1]:rows[1] + 1, :]).astype(bf16)
                h_scr[rs, :] = h
            else:
                h = h_scr[rs, :]
            g = jnp.dot(h, wg, preferred_element_type=f32)
            u = jnp.dot(h, wu, preferred_element_type=f32)
            part = jnp.dot((_silu(g) * u).astype(bf16), wd, preferred_element_type=f32)
            acc = part if first else acc_ref[rs, :] + part
            if last:
                xn = x_rows(rs) + (0.5 * mod_ref[rows[2]:rows[2] + 1, :]) * acc
                if final:
                    xn = xn * lax.rsqrt(jnp.mean(xn * xn, axis=-1, keepdims=True) + EPS) * gf_ref[...]
                acc_ref[rs, :] = xn
            else:
                acc_ref[rs, :] = acc

    owners = [(is_prompt, o_refs[0]), (jnp.logical_not(is_prompt), o_refs[1])] if split_out else [(True, o_refs[0])]
    for owns, o_ref in owners:
        pl.when((j == 0) & owns)(functools.partial(step, True, False, o_ref))
        pl.when((j > 0) & (j < last_j) & owns)(functools.partial(step, False, False, o_ref))
        pl.when((j == last_j) & owns)(functools.partial(step, False, True, o_ref))


def _ffn_call(xs, mod_l, g_row, ffn_wg, ffn_wu, ffn_wd, layer, half, gf_row, rows, final, split_out, tm, tf=256):
    split_in = len(xs) == 2
    x_specs = _split_specs(tm, D_MODEL, 2) if split_in else [pl.BlockSpec((tm, D_MODEL), lambda i, j: (i, 0))]
    n_token_windows = 2 + split_in + split_out
    vmem_bytes = (2 * n_token_windows * tm * D_MODEL * 4 + tm * D_MODEL * 2 + 2 * 3 * D_MODEL * tf * 4
                  + FFN_VMEM_SLACK)
    if split_out:
        out_specs = _split_specs(tm, D_MODEL, 2)
        out_shape = [jax.ShapeDtypeStruct((N_PROMPT, D_MODEL), f32), jax.ShapeDtypeStruct((N_SAMPLE, D_MODEL), f32)]
    else:
        out_specs = [pl.BlockSpec((tm, D_MODEL), lambda i, j: (i, 0))]
        out_shape = [jax.ShapeDtypeStruct((N_TOK, D_MODEL), f32)]
    scratch = [pltpu.VMEM((tm, D_MODEL), bf16)]
    return pl.pallas_call(
        functools.partial(_ffn_kernel, rows=rows, final=final, split_in=split_in, split_out=split_out, tm=tm),
        grid=(N_TOK // tm, D_FF // tf),
        in_specs=x_specs + [
            pl.BlockSpec((None, N_ADA, D_MODEL), lambda i, j: (_mod_row_index(i, tm), 0, 0)),
            pl.BlockSpec((1, D_MODEL), lambda i, j: (0, 0)),
            pl.BlockSpec((None, None, D_MODEL, tf), lambda i, j: (layer, half, 0, j)),
            pl.BlockSpec((None, None, D_MODEL, tf), lambda i, j: (layer, half, 0, j)),
            pl.BlockSpec((None, None, tf, D_MODEL), lambda i, j: (layer, half, j, 0)),
            pl.BlockSpec((1, D_MODEL), lambda i, j: (0, 0))],
        out_specs=out_specs, out_shape=out_shape, scratch_shapes=scratch,
        compiler_params=pltpu.CompilerParams(dimension_semantics=("parallel", "arbitrary"),
                                             vmem_limit_bytes=vmem_bytes),
        name="ffn",
    )(*xs, mod_l, g_row, ffn_wg, ffn_wu, ffn_wd, gf_row)


def _proj_kernel(*refs, rows, with_gates, w_transposed):
    if with_gates:
        x_ref, mod_ref, g_ref, w_ref, wgate_ref, o_ref, og_ref, h_scr = refs
    else:
        x_ref, mod_ref, g_ref, w_ref, o_ref, h_scr = refs
    j = pl.program_id(1)
    tm = x_ref.shape[0]
    dims = NT_DIMS if w_transposed else (((1,), (0,)), ((), ()))

    def matmul(h, w):
        return lax.dot_general(h, w, dims, preferred_element_type=f32)

    @pl.when(j == 0)
    def _():
        w = w_ref[...].astype(bf16)
        for r in range(tm // PROJ_ROW_BLOCK):
            rs = slice(r * PROJ_ROW_BLOCK, (r + 1) * PROJ_ROW_BLOCK)
            hb = _modulated(x_ref[rs, :], g_ref[...], mod_ref[rows[0]:rows[0] + 1, :],
                            mod_ref[rows[1]:rows[1] + 1, :]).astype(bf16)
            h_scr[rs, :] = hb
            o_ref[rs, :] = matmul(hb, w).astype(o_ref.dtype)
            if with_gates:
                og_ref[rs, :] = matmul(hb, wgate_ref[...].astype(bf16))

    @pl.when(j > 0)
    def _():
        o_ref[...] = matmul(h_scr[...], w_ref[...].astype(bf16)).astype(o_ref.dtype)


AB_I0 = 2 * H_A * DQK_A + 2 * H_A * DV_A
AB_D0 = AB_I0 + 4 * H_A
AB_B0 = AB_D0 + 2 * H_B * DK_B + 2 * H_B * DV_B
AB_IN = AB_B0 + 4 * H_B


def _ab_weight_kernel(w_ref, wm_ref, wg_ref):
    n_gate = 2 * H_A
    wm_ref[:AB_I0, :] = w_ref[:AB_I0, :]
    wm_ref[AB_I0:, :] = w_ref[AB_D0:AB_B0, :]
    wg_ref[:2 * n_gate, :] = w_ref[AB_I0:AB_D0, :]
    wg_ref[2 * n_gate:4 * n_gate, :] = w_ref[AB_B0:, :]
    wg_ref[4 * n_gate:, :] = jnp.zeros((LANES - 4 * n_gate, w_ref.shape[1]), f32)


def _ab_weight_call(ab_w_in, layer, tk=256):
    w_t = jnp.swapaxes(ab_w_in, 1, 2)
    return pl.pallas_call(
        _ab_weight_kernel,
        grid=(D_MODEL // tk,),
        in_specs=[pl.BlockSpec((None, AB_IN, tk), lambda i: (layer, 0, i))],
        out_specs=[pl.BlockSpec((AB_MAIN, tk), lambda i: (0, i)), pl.BlockSpec((LANES, tk), lambda i: (0, i))],
        out_shape=[jax.ShapeDtypeStruct((AB_MAIN, D_MODEL), f32), jax.ShapeDtypeStruct((LANES, D_MODEL), f32)],
        compiler_params=pltpu.CompilerParams(vmem_limit_bytes=VMEM_LIMIT),
        name="ab_weights",
    )(w_t)


def _column_tile(n, cap):
    return max(t for t in range(LANES, cap + 1, LANES) if n % t == 0)


def _proj_call(x, mod_l, g_row, w, w_gate, rows, out_dtype, w_transposed, tm=2048):
    n = w.shape[0] if w_transposed else w.shape[1]
    tn = _column_tile(n, 1024)
    with_gates = w_gate is not None
    w_spec = (pl.BlockSpec((tn, D_MODEL), lambda i, j: (j, 0)) if w_transposed
              else pl.BlockSpec((D_MODEL, tn), lambda i, j: (0, j)))
    in_specs = [pl.BlockSpec((tm, D_MODEL), lambda i, j: (i, 0)),
                pl.BlockSpec((None, N_ADA, D_MODEL), lambda i, j: (_mod_row_index(i, tm), 0, 0)),
                pl.BlockSpec((1, D_MODEL), lambda i, j: (0, 0)),
                w_spec]
    out_specs = [pl.BlockSpec((tm, tn), lambda i, j: (i, j))]
    out_shape = [jax.ShapeDtypeStruct((N_TOK, n), out_dtype)]
    args = [x, mod_l, g_row, w]
    if with_gates:
        in_specs.append(pl.BlockSpec(w_gate.shape, lambda i, j: (0, 0)))
        out_specs.append(pl.BlockSpec((tm, LANES), lambda i, j: (i, 0)))
        out_shape.append(jax.ShapeDtypeStruct((N_TOK, LANES), f32))
        args.append(w_gate)
    return pl.pallas_call(
        functools.partial(_proj_kernel, rows=rows, with_gates=with_gates, w_transposed=w_transposed),
        grid=(N_TOK // tm, n // tn),
        in_specs=in_specs, out_specs=out_specs, out_shape=out_shape,
        scratch_shapes=[pltpu.VMEM((tm, D_MODEL), bf16)],
        compiler_params=pltpu.CompilerParams(dimension_semantics=("parallel", "arbitrary"),
                                             vmem_limit_bytes=VMEM_LIMIT),
        name="in_proj",
    )(*args)


def _chunk_masks():
    r = lax.broadcasted_iota(jnp.int32, (CHUNK, CHUNK), 0)
    c = lax.broadcasted_iota(jnp.int32, (CHUNK, CHUNK), 1)
    return r >= c, r <= c, r > c, r < c


def _lane_col(x, lane, j):
    return jnp.sum(jnp.where(lane == j, x, 0.0), axis=1, keepdims=True)


def _head_lane_select(base, pair):
    r = lax.broadcasted_iota(jnp.int32, (SUBLANES, LANES), 0)
    ln = lax.broadcasted_iota(jnp.int32, (SUBLANES, LANES), 1)
    return jnp.where((ln == base + 2 * pair + r) & (r < 2), 1.0, 0.0)


def _split3_bf16(x):
    hi = x.astype(bf16)
    r = x - hi.astype(f32)
    mid = r.astype(bf16)
    return hi, mid, (r - mid.astype(f32)).astype(bf16)


def _thrice(a16):
    return jnp.concatenate([a16] * 3, axis=1)


def _mask_matmul_f32(mask16x3, x):
    return jnp.dot(mask16x3, jnp.concatenate(_split3_bf16(x), axis=0), preferred_element_type=f32)


def _select_rows_f32(sel16x3, x):
    return lax.dot_general(sel16x3, jnp.concatenate(_split3_bf16(x), axis=1), NT_DIMS, preferred_element_type=f32)


def _chunks_per_trip(nc):
    return min(MAX_CHUNKS_PER_TRIP, nc)


def _chunk_start(trip, u, d, nc):
    step = trip * _chunks_per_trip(nc) + u
    c = step if d == 0 else nc - 1 - step
    return pl.multiple_of(c * CHUNK, CHUNK)


def _mlstm_kernel(*refs, seq, zero_init):
    if zero_init:
        qp_ref, kp_ref, v0_ref, v1_ref, g_ref, par_ref, h_ref, cf_ref, nf_ref, mf_ref = refs
    else:
        (qp_ref, kp_ref, v0_ref, v1_ref, g_ref, par_ref, c0_ref, n0_ref, m0_ref,
         h_ref, cf_ref, nf_ref, mf_ref) = refs
    nc = seq // CHUNK
    pair = pl.program_id(1)
    lane = lax.broadcasted_iota(jnp.int32, (1, LANES), 1)
    tril, triu, _, _ = _chunk_masks()
    masks = (tril, triu)
    masks16 = (_thrice(tril.astype(bf16)), _thrice(triu.astype(bf16)))
    sel = [_thrice(_head_lane_select(LANE_F + d * H_A, pair).astype(bf16)) for d in range(2)]
    bi_row = par_ref[0:1, :]
    bf_row = par_ref[1:2, :]
    ones_col = jnp.where(lane == 0, 1.0, 0.0) + jnp.zeros((CHUNK, LANES), f32)
    v_refs = (v0_ref, v1_ref)
    streams = [(hh, d) for hh in range(2) for d in range(2)]

    h_ref[...] = jnp.zeros_like(h_ref)

    init = []
    for hh, d in streams:
        if zero_init:
            init.append((jnp.zeros((DQK_A, 2 * LANES), f32), jnp.zeros((1, 1), f32)))
        else:
            n_aug = jnp.where(lane == 0, n0_ref[d, hh], 0.0)
            init.append((jnp.concatenate([c0_ref[d, hh], n_aug], axis=1), m0_ref[d, hh]))

    def body(trip, carry):
        subs = range(_chunks_per_trip(nc))
        shared = {}
        for u in subs:
            for d in range(2):
                r0 = _chunk_start(trip, u, d, nc)
                gates = g_ref[pl.ds(r0, CHUNK), :]
                gi = gates + bi_row
                gf = _log_sigmoid(gates + bf_row)
                cum = _mask_matmul_f32(masks16[d], gf)
                stack = jnp.concatenate([cum, pltpu.roll(gi, LANE_F - LANE_I, axis=1)], axis=0)
                rows = _select_rows_f32(sel[d], stack)
                total = cum[CHUNK - 1:CHUNK, :] if d == 0 else cum[0:1, :]
                shared[u, d] = (r0, gi, cum, rows, total)
        chains = [(u, hh, d) for u in subs for hh, d in streams]
        st = {ch: {} for ch in chains}
        for ch in chains:
            u, hh, d = ch
            r0 = shared[u, d][0]
            s = st[ch]
            s["q"] = (qp_ref[pl.ds(r0, CHUNK), hh * DQK_A:(hh + 1) * DQK_A] * DQK_A ** -0.5).astype(bf16)
            s["k"] = kp_ref[pl.ds(r0, CHUNK), hh * DQK_A:(hh + 1) * DQK_A].astype(bf16)
            s["v_aug"] = jnp.concatenate([v_refs[hh][pl.ds(r0, CHUNK), :].astype(f32), ones_col], axis=1)
            s["qk"] = lax.dot_general(s["q"], s["k"], NT_DIMS, preferred_element_type=f32)
        for ch in chains:
            u, hh, d = ch
            _, gi, cum, rows, total = shared[u, d]
            s = st[ch]
            head = 2 * pair + hh
            jf = LANE_F + d * H_A + head
            ji = LANE_I + d * H_A + head
            bcol = _lane_col(cum, lane, jf)
            icol = _lane_col(gi, lane, ji)
            dmat = jnp.where(masks[d], bcol - rows[hh:hh + 1, :CHUNK] + rows[hh:hh + 1, CHUNK:], -jnp.inf)
            m_loc = jnp.max(dmat, axis=1, keepdims=True)
            s["sw"] = (s["qk"] * jnp.exp(dmat - m_loc)).astype(bf16)
            blast = _lane_col(total, lane, jf)
            gs = blast - bcol + icol
            ms_loc = jnp.max(gs, axis=0, keepdims=True)
            s["wv"] = (jnp.exp(gs - ms_loc) * s["v_aug"]).astype(bf16)
            s.update(bcol=bcol, m_loc=m_loc, blast=blast, ms_loc=ms_loc)
        for ch in chains:
            s = st[ch]
            s["num"] = jnp.dot(s["sw"], s["v_aug"].astype(bf16), preferred_element_type=f32)
            s["kv"] = lax.dot_general(s["k"], s["wv"], TN_DIMS, preferred_element_type=f32)
        state = list(carry)
        for u in subs:
            qc = [jnp.dot(st[u, hh, d]["q"], state[i][0].astype(bf16), preferred_element_type=f32)
                  for i, (hh, d) in enumerate(streams)]
            for i, (hh, d) in enumerate(streams):
                s = st[u, hh, d]
                c_aug, m = state[i]
                m_inter = s["bcol"] + m
                m_t = jnp.maximum(m_inter, s["m_loc"])
                nd = jnp.exp(s["m_loc"] - m_t) * s["num"] + jnp.exp(m_inter - m_t) * qc[i]
                den = nd[:, DV_A:DV_A + 1]
                hval = nd[:, :DV_A] / jnp.maximum(jnp.abs(den), jnp.exp(-m_t))
                r0 = shared[u, d][0]
                h_ref[pl.ds(r0, CHUNK), hh * DV_A:(hh + 1) * DV_A] += hval
                m_new = jnp.maximum(s["blast"] + m, s["ms_loc"])
                c_new = jnp.exp(s["blast"] + m - m_new) * c_aug + jnp.exp(s["ms_loc"] - m_new) * s["kv"]
                state[i] = (c_new, m_new)
        return tuple(state)

    final = lax.fori_loop(0, nc // _chunks_per_trip(nc), body, tuple(init))
    for i, (hh, d) in enumerate(streams):
        c_aug, m = final[i]
        cf_ref[d, hh] = c_aug[:, :DV_A]
        nf_ref[d, hh] = c_aug[:, DV_A:DV_A + 1]
        mf_ref[d, hh] = m


def _mlstm_call(proj, gates, par, nb, seq, off, state):
    zero_init = state is None
    blk = lambda col: pl.BlockSpec((seq, LANES), col)
    in_specs = [blk(lambda b, p: (off + b, p)),
                blk(lambda b, p: (off + b, 2 + p)),
                blk(lambda b, p: (off + b, 4 + 2 * p)),
                blk(lambda b, p: (off + b, 5 + 2 * p)),
                blk(lambda b, p: (off + b, 0)),
                pl.BlockSpec((SUBLANES, LANES), lambda b, p: (0, 0))]
    st_specs = [pl.BlockSpec((None, 2, 2, DQK_A, DV_A), lambda b, p: (b, 0, p, 0, 0)),
                pl.BlockSpec((None, 2, 2, DQK_A, 1), lambda b, p: (b, 0, p, 0, 0)),
                pl.BlockSpec((None, 2, 2, 1, 1), lambda b, p: (b, 0, p, 0, 0))]
    args = [proj, proj, proj, proj, gates, par]
    if not zero_init:
        in_specs += st_specs
        args += list(state)
    return pl.pallas_call(
        functools.partial(_mlstm_kernel, seq=seq, zero_init=zero_init),
        grid=(nb, H_A // 2),
        in_specs=in_specs,
        out_specs=[pl.BlockSpec((seq, 2 * DV_A), lambda b, p: (b, p))] + st_specs,
        out_shape=[jax.ShapeDtypeStruct((nb * seq, H_A * DV_A), f32),
                   jax.ShapeDtypeStruct((nb, 2, H_A, DQK_A, DV_A), f32),
                   jax.ShapeDtypeStruct((nb, 2, H_A, DQK_A, 1), f32),
                   jax.ShapeDtypeStruct((nb, 2, H_A, 1, 1), f32)],
        compiler_params=pltpu.CompilerParams(vmem_limit_bytes=VMEM_LIMIT),
        name="mlstm",
    )(*args)


CONV_PAD = SUBLANES


def _short_conv_silu(x, w_ref, pad_scr, seq):
    pad_scr[:CONV_PAD, :] = jnp.zeros((CONV_PAD, LANES), f32)
    pad_scr[CONV_PAD + seq:, :] = jnp.zeros((CONV_PAD, LANES), f32)
    pad_scr[CONV_PAD:CONV_PAD + seq, :] = x
    acc = x * w_ref[CONV_K // 2:CONV_K // 2 + 1, :]
    for tap in range(CONV_K):
        delta = tap - CONV_K // 2
        if delta != 0:
            acc = acc + pad_scr[CONV_PAD + delta:CONV_PAD + delta + seq, :] * w_ref[tap:tap + 1, :]
    return _silu(acc)


def _l2_unit(y):
    return y * lax.rsqrt(jnp.sum(y * y, axis=-1, keepdims=True) + EPS)


def _split_bf16(x):
    hi = x.astype(bf16)
    return hi, (x - hi.astype(f32)).astype(bf16)


def _matmul_3pass(m, x):
    mh, ml = _split_bf16(m)
    xh, xl = _split_bf16(x)
    return jnp.dot(jnp.concatenate([mh, ml, mh], axis=1), jnp.concatenate([xh, xh, xl], axis=0),
                   preferred_element_type=f32)


def _unit_triangular_solves(ns, xs):
    levels = CHUNK.bit_length() - 1
    r = lax.broadcasted_iota(jnp.int32, (CHUNK, CHUNK), 0)
    c = lax.broadcasted_iota(jnp.int32, (CHUNK, CHUNK), 1)
    eye = jnp.where(r == c, 1.0, 0.0)
    ts = [eye + n for n in ns]
    ms = [_matmul_3pass(n, n) for n in ns]
    for lvl in range(1, levels):
        for i in range(len(ns)):
            if lvl < levels - 1:
                prod = _matmul_3pass(ms[i], jnp.concatenate([ts[i], ms[i]], axis=1))
                ts[i] = ts[i] + prod[:, :CHUNK]
                ms[i] = prod[:, CHUNK:]
            else:
                ts[i] = ts[i] + _matmul_3pass(ms[i], ts[i])
    return [_matmul_3pass(t, x) for t, x in zip(ts, xs)]


def _delta_kernel(*refs, seq, zero_init):
    if zero_init:
        (q0_ref, q1_ref, k0_ref, k1_ref, v0_ref, v1_ref, wq0_ref, wq1_ref, wk0_ref, wk1_ref, wv0_ref, wv1_ref,
         g_ref, par_ref, o_ref, sf_ref, q_scr, k_scr, v_scr, pad_scr) = refs
    else:
        (q0_ref, q1_ref, k0_ref, k1_ref, v0_ref, v1_ref, wq0_ref, wq1_ref, wk0_ref, wk1_ref, wv0_ref, wv1_ref,
         g_ref, par_ref, s0_ref, o_ref, sf_ref, q_scr, k_scr, v_scr, pad_scr) = refs
    nc = seq // CHUNK
    pair = pl.program_id(1)
    lane = lax.broadcasted_iota(jnp.int32, (1, LANES), 1)
    tril, triu, stril, striu = _chunk_masks()
    masks, smasks = (tril, triu), (stril, striu)
    masks16 = (_thrice(tril.astype(bf16)), _thrice(triu.astype(bf16)))
    sel = [_thrice(_head_lane_select(LANE_A + d * H_B, pair).astype(bf16)) for d in range(2)]
    neg_a_row = -jnp.exp(par_ref[0:1, :])
    dt_row = par_ref[1:2, :]
    streams = [(hh, d) for hh in range(2) for d in range(2)]

    for hh, (q_ref, k_ref, v_ref, wq_ref, wk_ref, wv_ref) in enumerate(
            ((q0_ref, k0_ref, v0_ref, wq0_ref, wk0_ref, wv0_ref), (q1_ref, k1_ref, v1_ref, wq1_ref, wk1_ref, wv1_ref))):
        q_scr[hh] = _l2_unit(_short_conv_silu(q_ref[...].astype(f32), wq_ref, pad_scr, seq)) * DK_B ** -0.5
        k_scr[hh] = _l2_unit(_short_conv_silu(k_ref[...].astype(f32), wk_ref, pad_scr, seq))
        v_scr[hh] = _short_conv_silu(v_ref[...].astype(f32), wv_ref, pad_scr, seq)

    o_ref[...] = jnp.zeros_like(o_ref)
    init = tuple(jnp.zeros((DK_B, DV_B), f32) if zero_init else s0_ref[d, hh] for hh, d in streams)

    def body(trip, carry):
        subs = range(_chunks_per_trip(nc))
        shared = {}
        for u in subs:
            for d in range(2):
                r0 = _chunk_start(trip, u, d, nc)
                gates = g_ref[pl.ds(r0, CHUNK), :]
                beta_all = jax.nn.sigmoid(gates)
                glog = neg_a_row * _softplus(gates + dt_row)
                cum = _mask_matmul_f32(masks16[d], glog)
                rows = _select_rows_f32(sel[d], cum)
                shared[u, d] = (r0, beta_all, cum, rows)
        chains = [(u, hh, d) for u in subs for hh, d in streams]
        st = {ch: {} for ch in chains}
        for ch in chains:
            u, hh, d = ch
            r0, beta_all, cum, rows = shared[u, d]
            s = st[ch]
            head = 2 * pair + hh
            beta = _lane_col(beta_all, lane, LANE_BETA + d * H_B + head)
            gcol = _lane_col(cum, lane, LANE_A + d * H_B + head)
            glast = gcol[CHUNK - 1:CHUNK, :] if d == 0 else gcol[0:1, :]
            q = q_scr[hh, pl.ds(r0, CHUNK), :]
            k = k_scr[hh, pl.ds(r0, CHUNK), :]
            v = v_scr[hh, pl.ds(r0, CHUNK), :]
            k16 = k.astype(bf16)
            kbeta = k * beta
            eg = jnp.exp(gcol)
            decay = jnp.exp(jnp.where(masks[d], gcol - rows[hh:hh + 1, :], -jnp.inf))
            kk = lax.dot_general(kbeta.astype(bf16), k16, NT_DIMS, preferred_element_type=f32)
            qk = lax.dot_general(q.astype(bf16), k16, NT_DIMS, preferred_element_type=f32)
            s["n"] = -jnp.where(smasks[d], kk * decay, 0.0)
            s["x"] = jnp.concatenate([v * beta, kbeta * eg], axis=1)
            s["qk"] = (qk * decay).astype(bf16)
            s["qg"] = (q * eg).astype(bf16)
            s["kd_t"] = (k * jnp.exp(glast - gcol)).T.astype(bf16)
            s["gl"] = jnp.exp(glast)
        solved = _unit_triangular_solves([st[ch]["n"] for ch in chains], [st[ch]["x"] for ch in chains])
        for ch, uw in zip(chains, solved):
            s = st[ch]
            s["u"] = uw[:, :DV_B]
            s["w_qg"] = jnp.concatenate([uw[:, DV_B:].astype(bf16), s["qg"]], axis=0)
        state = list(carry)
        for u in subs:
            ws = [jnp.dot(st[u, hh, d]["w_qg"], state[i].astype(bf16), preferred_element_type=f32)
                  for i, (hh, d) in enumerate(streams)]
            v_new = [(st[u, hh, d]["u"] - ws[i][:CHUNK]).astype(bf16) for i, (hh, d) in enumerate(streams)]
            for i, (hh, d) in enumerate(streams):
                s = st[u, hh, d]
                o = ws[i][CHUNK:] + jnp.dot(s["qk"], v_new[i], preferred_element_type=f32)
                r0 = shared[u, d][0]
                o_ref[pl.ds(r0, CHUNK), hh * DV_B:(hh + 1) * DV_B] += o
                state[i] = state[i] * s["gl"] + jnp.dot(s["kd_t"], v_new[i], preferred_element_type=f32)
        return tuple(state)

    final = lax.fori_loop(0, nc // _chunks_per_trip(nc), body, init)
    for i, (hh, d) in enumerate(streams):
        sf_ref[d, hh] = final[i]


def _delta_call(proj, conv_w8, gates, par, nb, seq, off, state):
    zero_init = state is None
    col0 = (2 * H_A * DQK_A + 2 * H_A * DV_A) // LANES
    blk = lambda col: pl.BlockSpec((seq, LANES), col)
    wblk = lambda col: pl.BlockSpec((SUBLANES, LANES), col)
    in_specs = [blk(lambda b, p: (off + b, col0 + 2 * p)), blk(lambda b, p: (off + b, col0 + 2 * p + 1)),
                blk(lambda b, p: (off + b, col0 + H_B + 2 * p)), blk(lambda b, p: (off + b, col0 + H_B + 2 * p + 1)),
                blk(lambda b, p: (off + b, col0 + 2 * H_B + 2 * p)), blk(lambda b, p: (off + b, col0 + 2 * H_B + 2 * p + 1)),
                wblk(lambda b, p: (0, 2 * p)), wblk(lambda b, p: (0, 2 * p + 1)),
                wblk(lambda b, p: (0, H_B + 2 * p)), wblk(lambda b, p: (0, H_B + 2 * p + 1)),
                wblk(lambda b, p: (0, 2 * H_B + 2 * p)), wblk(lambda b, p: (0, 2 * H_B + 2 * p + 1)),
                blk(lambda b, p: (off + b, 0)),
                pl.BlockSpec((SUBLANES, LANES), lambda b, p: (0, 0))]
    st_spec = pl.BlockSpec((None, 2, 2, DK_B, DV_B), lambda b, p: (b, 0, p, 0, 0))
    args = [proj] * 6 + [conv_w8] * 6 + [gates, par]
    if not zero_init:
        in_specs.append(st_spec)
        args.append(state)
    return pl.pallas_call(
        functools.partial(_delta_kernel, seq=seq, zero_init=zero_init),
        grid=(nb, H_B // 2),
        in_specs=in_specs,
        out_specs=[pl.BlockSpec((seq, 2 * DV_B), lambda b, p: (b, p)), st_spec],
        out_shape=[jax.ShapeDtypeStruct((nb * seq, H_B * DV_B), f32),
                   jax.ShapeDtypeStruct((nb, 2, H_B, DK_B, DV_B), f32)],
        scratch_shapes=[pltpu.VMEM((2, seq, LANES), f32)] * 3 + [pltpu.VMEM((seq + 2 * CONV_PAD, LANES), f32)],
        compiler_params=pltpu.CompilerParams(vmem_limit_bytes=VMEM_LIMIT),
        name="delta",
    )(*args)


def _head_rms(x):
    return x * lax.rsqrt(jnp.mean(x * x, axis=-1, keepdims=True) + EPS)


def _ab_out_kernel(x_ref, hp_ref, hs_ref, op_ref, os_ref, om_ref, zd_ref, gm_ref, gd_ref, w_ref, mod_ref, o_ref,
                   *, gate_row, tm):
    hsum = _prompt_or_sample(hp_ref, hs_ref, tm)
    osum = _prompt_or_sample(op_ref, os_ref, tm)
    parts = []
    for h in range(H_A):
        sl = slice(h * DV_A, (h + 1) * DV_A)
        parts.append(_head_rms(hsum[:, sl]) * gm_ref[:, sl] * jax.nn.sigmoid(om_ref[:, sl].astype(f32)))
    for h in range(H_B):
        sl = slice(h * DV_B, (h + 1) * DV_B)
        parts.append(_head_rms(osum[:, sl]) * gd_ref[:, sl] * _silu(zd_ref[:, sl].astype(f32)))
    cat = jnp.concatenate(parts, axis=1).astype(bf16)
    y = jnp.dot(cat, w_ref[...].astype(bf16), preferred_element_type=f32)
    o_ref[...] = x_ref[...] + mod_ref[gate_row:gate_row + 1, :] * y


def _ab_out_call(x, hp, hs, op, os_, proj, gm_row, gd_row, w_out, mod_l, gate_row, tm=1024):
    wide = H_A * DV_A
    om_blk = (2 * H_A * DQK_A) // wide + 1
    zd_blk = AB_MAIN // wide - 1
    return pl.pallas_call(
        functools.partial(_ab_out_kernel, gate_row=gate_row, tm=tm),
        grid=(N_TOK // tm,),
        in_specs=[pl.BlockSpec((tm, D_MODEL), lambda i: (i, 0))] + _split_specs(tm, wide) + _split_specs(tm, wide) + [
            pl.BlockSpec((tm, wide), lambda i: (i, om_blk)),
            pl.BlockSpec((tm, wide), lambda i: (i, zd_blk)),
            pl.BlockSpec((1, wide), lambda i: (0, 0)),
            pl.BlockSpec((1, wide), lambda i: (0, 0)),
            pl.BlockSpec((2 * wide, D_MODEL), lambda i: (0, 0)),
            pl.BlockSpec((None, N_ADA, D_MODEL), lambda i: (_mod_row_index(i, tm), 0, 0))],
        out_specs=pl.BlockSpec((tm, D_MODEL), lambda i: (i, 0)),
        out_shape=jax.ShapeDtypeStruct((N_TOK, D_MODEL), f32),
        compiler_params=pltpu.CompilerParams(vmem_limit_bytes=VMEM_LIMIT),
        name="ab_out",
    )(x, hp, hs, op, os_, proj, proj, gm_row, gd_row, w_out, mod_l)


def _na_out_kernel(x_ref, ap_ref, as_ref, w_ref, mod_ref, o_ref, *, gate_row, tm):
    a = _prompt_or_sample(ap_ref, as_ref, tm).astype(bf16)
    y = jnp.dot(a, w_ref[...].astype(bf16), preferred_element_type=f32)
    o_ref[...] = x_ref[...] + mod_ref[gate_row:gate_row + 1, :] * y


def _na_out_call(x, attn_p, attn_s, w_out, mod_l, gate_row, tm=1024):
    return pl.pallas_call(
        functools.partial(_na_out_kernel, gate_row=gate_row, tm=tm),
        grid=(N_TOK // tm,),
        in_specs=[pl.BlockSpec((tm, D_MODEL), lambda i: (i, 0))] + _split_specs(tm, NA_W) + [
            pl.BlockSpec((NA_W, D_MODEL), lambda i: (0, 0)),
            pl.BlockSpec((None, N_ADA, D_MODEL), lambda i: (_mod_row_index(i, tm), 0, 0))],
        out_specs=pl.BlockSpec((tm, D_MODEL), lambda i: (i, 0)),
        out_shape=jax.ShapeDtypeStruct((N_TOK, D_MODEL), f32),
        compiler_params=pltpu.CompilerParams(vmem_limit_bytes=VMEM_LIMIT),
        name="na_out",
    )(x, attn_p, attn_s, w_out, mod_l)


CTX_HEADS = 8


def _ctx_attn_kernel(q_ref, k_ref, v_ref, o_ref, nk_ref, nv_ref):
    low = lax.broadcasted_iota(jnp.int32, (1, LANES), 1) < DH_C
    for pair in range(CTX_HEADS // 2):
        sl = slice(pair * LANES, (pair + 1) * LANES)
        q = q_ref[:, sl] * DH_C ** -0.5
        k_t = k_ref[:, sl].T
        v_t = v_ref[:, sl].T
        for hh in range(2):
            nk_ref[2 * pair + hh] = k_t[hh * DH_C:(hh + 1) * DH_C, :]
            nv_ref[2 * pair + hh] = v_t[hh * DH_C:(hh + 1) * DH_C, :]
        k = k_ref[:, sl].astype(bf16)
        v = v_ref[:, sl].astype(bf16)
        outs = []
        for hh in range(2):
            qh = jnp.where(low if hh == 0 else jnp.logical_not(low), q, 0.0).astype(bf16)
            s = lax.dot_general(qh, k, NT_DIMS, preferred_element_type=f32)
            p = jnp.exp(s - jnp.max(s, axis=1, keepdims=True))
            o = jnp.dot(p.astype(bf16), v, preferred_element_type=f32)
            outs.append(o / jnp.sum(p, axis=1, keepdims=True))
        o_ref[:, sl] = jnp.where(low, outs[0], outs[1]).astype(o_ref.dtype)


def _ctx_attn_call(proj):
    w = CTX_HEADS * DH_C
    nblk = NA_W // w
    return pl.pallas_call(
        _ctx_attn_kernel,
        grid=(BATCH, nblk),
        in_specs=[pl.BlockSpec((SEQ, w), lambda b, j: (b, j)),
                  pl.BlockSpec((SEQ, w), lambda b, j: (b, nblk + j)),
                  pl.BlockSpec((SEQ, w), lambda b, j: (b, 2 * nblk + j))],
        out_specs=[pl.BlockSpec((SEQ, w), lambda b, j: (b, j)),
                   pl.BlockSpec((None, CTX_HEADS, DH_C, SEQ), lambda b, j: (b, j, 0, 0)),
                   pl.BlockSpec((None, CTX_HEADS, DH_C, SEQ), lambda b, j: (b, j, 0, 0))],
        out_shape=[jax.ShapeDtypeStruct((N_PROMPT, NA_W), bf16),
                   jax.ShapeDtypeStruct((BATCH, H_C, DH_C, SEQ), f32),
                   jax.ShapeDtypeStruct((BATCH, H_C, DH_C, SEQ), f32)],
        compiler_params=pltpu.CompilerParams(vmem_limit_bytes=VMEM_LIMIT),
        name="ctx_attn",
    )(proj, proj, proj)


QROWS = 4
QBLK = QROWS * GRID_W
KROWS_MID = QROWS + WIN_R - 1
N_RIDX = 2 * WIN_R - 1
N_CIDX = 2 * WIN_C - 1
N_QBLK = GRID_ROWS // QROWS


def _nattn_bias_tables(rb_ref, tb_scr, bmid_scr, btop_scr, bbot_scr):
    qc = lax.broadcasted_iota(jnp.int32, (GRID_W, GRID_W), 0)
    kc = lax.broadcasted_iota(jnp.int32, (GRID_W, GRID_W), 1)
    cs = jnp.clip(qc - WIN_C // 2, 0, GRID_W - WIN_C)
    valid = (kc >= cs) & (kc < cs + WIN_C)
    neg = jnp.full((GRID_W, GRID_W), -jnp.inf, f32)
    for hh in range(2):
        for ri in range(N_RIDX):
            row = jnp.broadcast_to(rb_ref[hh, ri:ri + 1, :], (GRID_W, LANES))
            tile = pltpu.roll(row, LANES - (WIN_C - 1), 1, stride=1, stride_axis=0)[:, :GRID_W]
            tb_scr[ri] = jnp.where(valid, tile, -jnp.inf)
        for i in range(QROWS):
            rs = slice(i * GRID_W, (i + 1) * GRID_W)
            for jj in range(KROWS_MID):
                inside = 0 <= jj - i < WIN_R
                bmid_scr[hh, rs, jj * GRID_W:(jj + 1) * GRID_W] = tb_scr[jj - i + WIN_R // 2 - 1] if inside else neg
            for jj in range(WIN_R):
                btop_scr[hh, rs, jj * GRID_W:(jj + 1) * GRID_W] = tb_scr[jj - i + WIN_R - 1]
                bbot_scr[hh, rs, jj * GRID_W:(jj + 1) * GRID_W] = tb_scr[jj - i + WIN_R // 2 - 1]


def _nattn_kernel(rb_ref, q_ref, k_ref, v_ref, ck_ref, cv_ref, o_ref, tb_scr, bmid_scr, btop_scr, bbot_scr):
    scale = DH_C ** -0.5

    @pl.when(pl.program_id(1) == 0)
    def _():
        _nattn_bias_tables(rb_ref, tb_scr, bmid_scr, btop_scr, bbot_scr)

    low = lax.broadcasted_iota(jnp.int32, (1, LANES), 1) < DH_C
    kctx_t = jnp.concatenate([ck_ref[0], ck_ref[1]], axis=0).astype(bf16)
    vctx = jnp.concatenate([cv_ref[0], cv_ref[1]], axis=0).T.astype(bf16)

    def block(q_start, k_start, k_rows, bias_scr):
        nk = k_rows * GRID_W
        q = q_ref[pl.ds(q_start, QBLK), :] * scale
        ku = k_ref[pl.ds(k_start, nk), :].astype(bf16)
        vu = v_ref[pl.ds(k_start, nk), :].astype(bf16)
        outs = []
        for hh in range(2):
            qh = jnp.where(low if hh == 0 else jnp.logical_not(low), q, 0.0).astype(bf16)
            s_loc = lax.dot_general(qh, ku, NT_DIMS, preferred_element_type=f32) + bias_scr[hh]
            s_ctx = jnp.dot(qh, kctx_t, preferred_element_type=f32)
            m = jnp.maximum(jnp.max(s_loc, axis=1, keepdims=True), jnp.max(s_ctx, axis=1, keepdims=True))
            p_loc = jnp.exp(s_loc - m)
            p_ctx = jnp.exp(s_ctx - m)
            denom = jnp.sum(p_loc, axis=1, keepdims=True) + jnp.sum(p_ctx, axis=1, keepdims=True)
            o = (jnp.dot(p_loc.astype(bf16), vu, preferred_element_type=f32)
                 + jnp.dot(p_ctx.astype(bf16), vctx, preferred_element_type=f32))
            outs.append(o / denom)
        o_ref[pl.ds(q_start, QBLK), :] = jnp.where(low, outs[0], outs[1]).astype(o_ref.dtype)

    block(0, 0, WIN_R, btop_scr)

    def mid_body(blk, _):
        q_start = pl.multiple_of(blk * QBLK, QBLK)
        k_start = pl.multiple_of(blk * QBLK - (WIN_R // 2) * GRID_W, QBLK)
        block(q_start, k_start, KROWS_MID, bmid_scr)
        return 0

    lax.fori_loop(1, N_QBLK - 1, mid_body, 0)
    block((N_QBLK - 1) * QBLK, (GRID_ROWS - WIN_R) * GRID_W, WIN_R, bbot_scr)


def _nattn_call(proj, cache_k, cache_v, layer, rel_bias, off):
    npair = H_C // 2
    return pl.pallas_call(
        _nattn_kernel,
        grid=(npair, DEC_BATCH),
        in_specs=[pl.BlockSpec((2, 2 * WIN_R, LANES), lambda p, b: (p, 0, 0)),
                  pl.BlockSpec((DEC_SEQ, LANES), lambda p, b: (off + b, p)),
                  pl.BlockSpec((DEC_SEQ, LANES), lambda p, b: (off + b, npair + p)),
                  pl.BlockSpec((DEC_SEQ, LANES), lambda p, b: (off + b, 2 * npair + p)),
                  pl.BlockSpec((None, None, 2, DH_C, PAST_LEN), lambda p, b: (b, layer, p, 0, 0)),
                  pl.BlockSpec((None, None, 2, DH_C, PAST_LEN), lambda p, b: (b, layer, p, 0, 0))],
        out_specs=pl.BlockSpec((DEC_SEQ, LANES), lambda p, b: (b, p)),
        out_shape=jax.ShapeDtypeStruct((N_SAMPLE, NA_W), bf16),
        scratch_shapes=[pltpu.VMEM((N_RIDX, GRID_W, GRID_W), f32),
                        pltpu.VMEM((2, QBLK, KROWS_MID * GRID_W), f32),
                        pltpu.VMEM((2, QBLK, WIN_R * GRID_W), f32),
                        pltpu.VMEM((2, QBLK, WIN_R * GRID_W), f32)],
        compiler_params=pltpu.CompilerParams(dimension_semantics=("arbitrary", "arbitrary"),
                                             vmem_limit_bytes=VMEM_LIMIT),
        name="nattn",
    )(jnp.pad(rel_bias, ((0, 0), (0, 2 * WIN_R - N_RIDX), (0, LANES - N_CIDX))), proj, proj, proj, cache_k, cache_v)


def _lane_row(pieces):
    row = jnp.zeros((LANES,), f32)
    for off, vals in pieces:
        row = row.at[off:off + vals.shape[0]].set(vals.astype(f32))
    return row


def _param_rows(rows):
    out = jnp.zeros((SUBLANES, LANES), f32)
    for r, row in enumerate(rows):
        out = out.at[r].set(row)
    return out


def kernel(x_prompt, x_sample, c, state_mlstm_C, state_mlstm_n, state_mlstm_m, state_delta_S, cache_na_k, cache_na_v, c_ctx, ada_w, ada_b, norm_g, ffn_wg, ffn_wu, ffn_wd, ab_w_in, ab_w_out, mlstm_b_i, mlstm_b_f, mlstm_norm_g, delta_conv_w, delta_a_log, delta_dt_bias, delta_norm_g, na_w_in, na_w_out, na_rel_bias, final_norm_g):
    xs = (x_prompt.reshape(N_PROMPT, D_MODEL), x_sample.reshape(N_SAMPLE, D_MODEL))
    mods = _ada_call(jnp.concatenate([c_ctx[None, :], c], axis=0), ada_w, ada_b)
    mods = mods.reshape(DEPTH, 3, N_ADA, D_MODEL)
    gf_row = final_norm_g.reshape(1, D_MODEL)
    s_off = N_PROMPT // DEC_SEQ
    new_c, new_n, new_m, new_s, new_k, new_v = [], [], [], [], [], []

    for l in range(DEPTH):
        mod_l = mods[l]
        a = l // 2
        x = _ffn_call(xs if l == 0 else (x,), mod_l, norm_g[l, 0].reshape(1, D_MODEL), ffn_wg, ffn_wu, ffn_wd, l, 0,
                      gf_row, rows=(0, 1, 2), final=False, split_out=False, tm=FFN_TM)[0]
        g_mix = norm_g[l, 1].reshape(1, D_MODEL)
        if l % 2 == 0:
            w_main, w_gate = _ab_weight_call(ab_w_in, a)
            proj, gates = _proj_call(x, mod_l, g_mix, w_main, w_gate, rows=(3, 4), out_dtype=bf16, w_transposed=True)

            par_m = _param_rows([_lane_row([(LANE_I, mlstm_b_i[a].reshape(-1))]),
                                 _lane_row([(LANE_F, mlstm_b_f[a].reshape(-1))])])
            par_d = _param_rows([_lane_row([(LANE_A, delta_a_log[a].reshape(-1))]),
                                 _lane_row([(LANE_A, delta_dt_bias[a].reshape(-1))])])
            conv_w8 = jnp.concatenate([delta_conv_w[a], jnp.zeros((SUBLANES - CONV_K, 3 * H_B * DK_B), f32)], axis=0)

            hp, cn, nn_, mn = _mlstm_call(proj, gates, par_m, BATCH, SEQ, 0, None)
            st = (state_mlstm_C[:, a], state_mlstm_n[:, a][..., None], state_mlstm_m[:, a][..., None, None])
            hs, _, _, _ = _mlstm_call(proj, gates, par_m, DEC_BATCH, DEC_SEQ, s_off, st)
            op, sn = _delta_call(proj, conv_w8, gates, par_d, BATCH, SEQ, 0, None)
            os_, _ = _delta_call(proj, conv_w8, gates, par_d, DEC_BATCH, DEC_SEQ, s_off, state_delta_S[:, a])
            new_c.append(cn)
            new_n.append(nn_[..., 0])
            new_m.append(mn[..., 0, 0])
            new_s.append(sn)
            x = _ab_out_call(x, hp, hs, op, os_, proj, mlstm_norm_g[a].reshape(1, -1),
                             jnp.tile(delta_norm_g[a], H_B).reshape(1, -1), ab_w_out[a], mod_l, gate_row=5)
        else:
            proj = _proj_call(x, mod_l, g_mix, na_w_in[a], None, rows=(3, 4), out_dtype=f32, w_transposed=False)[0]
            attn_p, kp_t, vp_t = _ctx_attn_call(proj)
            attn_s = _nattn_call(proj, jnp.swapaxes(cache_na_k, -1, -2), jnp.swapaxes(cache_na_v, -1, -2), a,
                                 na_rel_bias[a], s_off)
            new_k.append(jnp.swapaxes(kp_t, -1, -2))
            new_v.append(jnp.swapaxes(vp_t, -1, -2))
            x = _na_out_call(x, attn_p, attn_s, na_w_out[a], mod_l, gate_row=5)
        last = l == DEPTH - 1
        outs = _ffn_call((x,), mod_l, norm_g[l, 2].reshape(1, D_MODEL), ffn_wg, ffn_wu, ffn_wd, l, 1,
                         gf_row, rows=(6, 7, 8), final=last, split_out=last, tm=FFN_TM)
        x = outs[0]

    y_prompt = outs[0].reshape(BATCH, SEQ, D_MODEL)
    y_sample = outs[1].reshape(DEC_BATCH, DEC_SEQ, D_MODEL)
    return (y_prompt, y_sample, jnp.stack(new_c, axis=1), jnp.stack(new_n, axis=1), jnp.stack(new_m, axis=1),
            jnp.stack(new_s, axis=1), jnp.stack(new_k, axis=1), jnp.stack(new_v, axis=1))
```

```python
import functools

import jax
import jax.numpy as jnp
from jax import lax
from jax.experimental import pallas as pl
from jax.experimental.pallas import tpu as pltpu

f32 = jnp.float32
bf16 = jnp.bfloat16

D_MODEL = 1024
BATCH = 16
SEQ = 256
DEPTH = 2
DEC_BATCH = 2
DEC_SEQ = 2048
PAST_LEN = 256
GRID_W = 64
GRID_ROWS = DEC_SEQ // GRID_W
D_FF = 2816
N_ADA = 9
EPS = 1e-6
CHUNK = 64
H_A, DQK_A, DV_A = 4, 64, 128
H_B, DK_B, DV_B = 4, 128, 128
CONV_K = 5
H_C, DH_C = 16, 64
WIN_R, WIN_C = 8, 16
NA_W = H_C * DH_C
assert DH_C ** -0.5 == 2.0 ** -3 and DQK_A ** -0.5 == 2.0 ** -3

N_PROMPT = BATCH * SEQ
N_SAMPLE = DEC_BATCH * DEC_SEQ
N_TOK = N_PROMPT + N_SAMPLE
AB_MAIN = 2 * H_A * DQK_A + 2 * H_A * DV_A + 2 * H_B * DK_B + 2 * H_B * DV_B
LANES = 128
VMEM_LIMIT = 56 * 1024 * 1024
FFN_VMEM_SLACK = 4 * 1024 * 1024
SUBLANES = 8

LANE_I, LANE_F, LANE_BETA, LANE_A = 0, 8, 16, 24

NT_DIMS = (((1,), (1,)), ((), ()))
TN_DIMS = (((0,), (0,)), ((), ()))

MAX_CHUNKS_PER_TRIP = 4
FFN_TM = 2048
FFN_ROW_BLOCK = 512
PROJ_ROW_BLOCK = 256


def _softplus(x):
    return jnp.maximum(x, 0.0) + jnp.log1p(jnp.exp(-jnp.abs(x)))


def _log_sigmoid(x):
    return -_softplus(-x)


def _silu(x):
    return x * jax.nn.sigmoid(x)


def _mod_row_index(i, tm):
    n_p = N_PROMPT // tm
    per_b = DEC_SEQ // tm
    return jnp.where(i < n_p, 0, 1 + (i - n_p) // per_b)


def _modulated(x, g_row, shift_row, scale_row):
    y = x * lax.rsqrt(jnp.mean(x * x, axis=-1, keepdims=True) + EPS) * g_row
    return y * (1.0 + scale_row) + shift_row


def _ada_kernel(cb_ref, w_ref, b_ref, o_ref, s_scr):
    tn = w_ref.shape[-1]

    @pl.when((pl.program_id(0) == 0) & (pl.program_id(1) == 0))
    def _():
        cb = cb_ref[...]
        s_scr[...] = cb * jax.nn.sigmoid(cb)

    for j in range(tn // LANES):
        cols = slice(j * LANES, (j + 1) * LANES)
        w = w_ref[:, cols]
        for r in range(3):
            o_ref[r:r + 1, cols] = jnp.sum(w * s_scr[r], axis=0, keepdims=True) + b_ref[:, cols]


def _ada_call(cond3, ada_w, ada_b):
    tn = N_ADA * D_MODEL // 4
    cb = jnp.broadcast_to(cond3[:, :, None], (3, D_MODEL, LANES))
    return pl.pallas_call(
        _ada_kernel,
        grid=(DEPTH, N_ADA * D_MODEL // tn),
        in_specs=[pl.BlockSpec((3, D_MODEL, LANES), lambda l, j: (0, 0, 0)),
                  pl.BlockSpec((None, D_MODEL, tn), lambda l, j: (l, 0, j)),
                  pl.BlockSpec((None, 1, tn), lambda l, j: (l, 0, j))],
        out_specs=pl.BlockSpec((None, 3, tn), lambda l, j: (l, 0, j)),
        out_shape=jax.ShapeDtypeStruct((DEPTH, 3, N_ADA * D_MODEL), f32),
        scratch_shapes=[pltpu.VMEM((3, D_MODEL, LANES), f32)],
        compiler_params=pltpu.CompilerParams(dimension_semantics=("arbitrary", "arbitrary"),
                                             vmem_limit_bytes=VMEM_LIMIT),
        name="ada_mod",
    )(cb, ada_w, ada_b.reshape(DEPTH, 1, N_ADA * D_MODEL))


def _prompt_or_sample(p_ref, s_ref, tm):
    return jnp.where(pl.program_id(0) < N_PROMPT // tm, p_ref[...], s_ref[...])


def _split_specs(tm, width, n_grid_axes=1):
    n_p = N_PROMPT // tm
    if n_grid_axes == 1:
        return [pl.BlockSpec((tm, width), lambda i: (jnp.minimum(i, n_p - 1), 0)),
                pl.BlockSpec((tm, width), lambda i: (jnp.maximum(i - n_p, 0), 0))]
    return [pl.BlockSpec((tm, width), lambda i, j: (jnp.minimum(i, n_p - 1), 0)),
            pl.BlockSpec((tm, width), lambda i, j: (jnp.maximum(i - n_p, 0), 0))]


def _ffn_kernel(*refs, rows, final, split_in, split_out, tm):
    it = iter(refs)
    x_refs = [next(it) for _ in range(2 if split_in else 1)]
    mod_ref, g_ref, wg_ref, wu_ref, wd_ref, gf_ref = [next(it) for _ in range(6)]
    o_refs = [next(it) for _ in range(2 if split_out else 1)]
    h_scr = next(it)
    j = pl.program_id(1)
    last_j = pl.num_programs(1) - 1
    is_prompt = pl.program_id(0) < N_PROMPT // tm

    def x_rows(rs):
        if split_in:
            return jnp.where(is_prompt, x_refs[0][rs, :], x_refs[1][rs, :])
        return x_refs[0][rs, :]

    def step(first, last, acc_ref):
        wg = wg_ref[...].astype(bf16)
        wu = wu_ref[...].astype(bf16)
        wd = wd_ref[...].astype(bf16)
        for r in range(tm // FFN_ROW_BLOCK):
            rs = slice(r * FFN_ROW_BLOCK, (r + 1) * FFN_ROW_BLOCK)
            if first:
                h = _modulated(x_rows(rs), g_ref[...], mod_ref[rows[0]:rows[0] + 1, :],
                               mod_ref[rows[1]:rows[1] + 1, :]).astype(bf16)
                h_scr[rs, :] = h
            else:
                h = h_scr[rs, :]
            g = jnp.dot(h, wg, preferred_element_type=f32)
            u = jnp.dot(h, wu, preferred_element_type=f32)
            part = jnp.dot((_silu(g) * u).astype(bf16), wd, preferred_element_type=f32)
            acc = part if first else acc_ref[rs, :] + part
            if last:
                xn = x_rows(rs) + (0.5 * mod_ref[rows[2]:rows[2] + 1, :]) * acc
                if final:
                    xn = xn * lax.rsqrt(jnp.mean(xn * xn, axis=-1, keepdims=True) + EPS) * gf_ref[...]
                acc_ref[rs, :] = xn
            else:
                acc_ref[rs, :] = acc

    owners = [(is_prompt, o_refs[0]), (jnp.logical_not(is_prompt), o_refs[1])] if split_out else [(True, o_refs[0])]
    for owns, o_ref in owners:
        pl.when((j == 0) & owns)(functools.partial(step, True, False, o_ref))
        pl.when((j > 0) & (j < last_j) & owns)(functools.partial(step, False, False, o_ref))
        pl.when((j == last_j) & owns)(functools.partial(step, False, True, o_ref))


def _ffn_call(xs, mod_l, g_row, ffn_wg, ffn_wu, ffn_wd, layer, half, gf_row, rows, final, split_out, tm, tf=256):
    split_in = len(xs) == 2
    x_specs = _split_specs(tm, D_MODEL, 2) if split_in else [pl.BlockSpec((tm, D_MODEL), lambda i, j: (i, 0))]
    n_token_windows = 2 + split_in + split_out
    vmem_bytes = (2 * n_token_windows * tm * D_MODEL * 4 + tm * D_MODEL * 2 + 2 * 3 * D_MODEL * tf * 4
                  + FFN_VMEM_SLACK)
    if split_out:
        out_specs = _split_specs(tm, D_MODEL, 2)
        out_shape = [jax.ShapeDtypeStruct((N_PROMPT, D_MODEL), f32), jax.ShapeDtypeStruct((N_SAMPLE, D_MODEL), f32)]
    else:
        out_specs = [pl.BlockSpec((tm, D_MODEL), lambda i, j: (i, 0))]
        out_shape = [jax.ShapeDtypeStruct((N_TOK, D_MODEL), f32)]
    scratch = [pltpu.VMEM((tm, D_MODEL), bf16)]
    return pl.pallas_call(
        functools.partial(_ffn_kernel, rows=rows, final=final, split_in=split_in, split_out=split_out, tm=tm),
        grid=(N_TOK // tm, D_FF // tf),
        in_specs=x_specs + [
            pl.BlockSpec((None, N_ADA, D_MODEL), lambda i, j: (_mod_row_index(i, tm), 0, 0)),
            pl.BlockSpec((1, D_MODEL), lambda i, j: (0, 0)),
            pl.BlockSpec((None, None, D_MODEL, tf), lambda i, j: (layer, half, 0, j)),
            pl.BlockSpec((None, None, D_MODEL, tf), lambda i, j: (layer, half, 0, j)),
            pl.BlockSpec((None, None, tf, D_MODEL), lambda i, j: (layer, half, j, 0)),
            pl.BlockSpec((1, D_MODEL), lambda i, j: (0, 0))],
        out_specs=out_specs, out_shape=out_shape, scratch_shapes=scratch,
        compiler_params=pltpu.CompilerParams(dimension_semantics=("parallel", "arbitrary"),
                                             vmem_limit_bytes=vmem_bytes),
        name="ffn",
    )(*xs, mod_l, g_row, ffn_wg, ffn_wu, ffn_wd, gf_row)


def _proj_kernel(*refs, rows, with_gates, w_transposed):
    if with_gates:
        x_ref, mod_ref, g_ref, w_ref, wgate_ref, o_ref, og_ref, h_scr = refs
    else:
        x_ref, mod_ref, g_ref, w_ref, o_ref, h_scr = refs
    j = pl.program_id(1)
    tm = x_ref.shape[0]
    dims = NT_DIMS if w_transposed else (((1,), (0,)), ((), ()))

    def matmul(h, w):
        return lax.dot_general(h, w, dims, preferred_element_type=f32)

    @pl.when(j == 0)
    def _():
        w = w_ref[...].astype(bf16)
        for r in range(tm // PROJ_ROW_BLOCK):
            rs = slice(r * PROJ_ROW_BLOCK, (r + 1) * PROJ_ROW_BLOCK)
            hb = _modulated(x_ref[rs, :], g_ref[...], mod_ref[rows[0]:rows[0] + 1, :],
                            mod_ref[rows[1]:rows[1] + 1, :]).astype(bf16)
            h_scr[rs, :] = hb
            o_ref[rs, :] = matmul(hb, w).astype(o_ref.dtype)
            if with_gates:
                og_ref[rs, :] = matmul(hb, wgate_ref[...].astype(bf16))

    @pl.when(j > 0)
    def _():
        o_ref[...] = matmul(h_scr[...], w_ref[...].astype(bf16)).astype(o_ref.dtype)


AB_I0 = 2 * H_A * DQK_A + 2 * H_A * DV_A
AB_D0 = AB_I0 + 4 * H_A
AB_B0 = AB_D0 + 2 * H_B * DK_B + 2 * H_B * DV_B
AB_IN = AB_B0 + 4 * H_B


def _ab_weight_kernel(w_ref, wm_ref, wg_ref):
    n_gate = 2 * H_A
    wm_ref[:AB_I0, :] = w_ref[:AB_I0, :]
    wm_ref[AB_I0:, :] = w_ref[AB_D0:AB_B0, :]
    wg_ref[:2 * n_gate, :] = w_ref[AB_I0:AB_D0, :]
    wg_ref[2 * n_gate:4 * n_gate, :] = w_ref[AB_B0:, :]
    wg_ref[4 * n_gate:, :] = jnp.zeros((LANES - 4 * n_gate, w_ref.shape[1]), f32)


def _ab_weight_call(ab_w_in, layer, tk=256):
    w_t = jnp.swapaxes(ab_w_in, 1, 2)
    return pl.pallas_call(
        _ab_weight_kernel,
        grid=(D_MODEL // tk,),
        in_specs=[pl.BlockSpec((None, AB_IN, tk), lambda i: (layer, 0, i))],
        out_specs=[pl.BlockSpec((AB_MAIN, tk), lambda i: (0, i)), pl.BlockSpec((LANES, tk), lambda i: (0, i))],
        out_shape=[jax.ShapeDtypeStruct((AB_MAIN, D_MODEL), f32), jax.ShapeDtypeStruct((LANES, D_MODEL), f32)],
        compiler_params=pltpu.CompilerParams(vmem_limit_bytes=VMEM_LIMIT),
        name="ab_weights",
    )(w_t)


def _column_tile(n, cap):
    return max(t for t in range(LANES, cap + 1, LANES) if n % t == 0)


def _proj_call(x, mod_l, g_row, w, w_gate, rows, out_dtype, w_transposed, tm=2048):
    n = w.shape[0] if w_transposed else w.shape[1]
    tn = _column_tile(n, 1024)
    with_gates = w_gate is not None
    w_spec = (pl.BlockSpec((tn, D_MODEL), lambda i, j: (j, 0)) if w_transposed
              else pl.BlockSpec((D_MODEL, tn), lambda i, j: (0, j)))
    in_specs = [pl.BlockSpec((tm, D_MODEL), lambda i, j: (i, 0)),
                pl.BlockSpec((None, N_ADA, D_MODEL), lambda i, j: (_mod_row_index(i, tm), 0, 0)),
                pl.BlockSpec((1, D_MODEL), lambda i, j: (0, 0)),
                w_spec]
    out_specs = [pl.BlockSpec((tm, tn), lambda i, j: (i, j))]
    out_shape = [jax.ShapeDtypeStruct((N_TOK, n), out_dtype)]
    args = [x, mod_l, g_row, w]
    if with_gates:
        in_specs.append(pl.BlockSpec(w_gate.shape, lambda i, j: (0, 0)))
        out_specs.append(pl.BlockSpec((tm, LANES), lambda i, j: (i, 0)))
        out_shape.append(jax.ShapeDtypeStruct((N_TOK, LANES), f32))
        args.append(w_gate)
    return pl.pallas_call(
        functools.partial(_proj_kernel, rows=rows, with_gates=with_gates, w_transposed=w_transposed),
        grid=(N_TOK // tm, n // tn),
        in_specs=in_specs, out_specs=out_specs, out_shape=out_shape,
        scratch_shapes=[pltpu.VMEM((tm, D_MODEL), bf16)],
        compiler_params=pltpu.CompilerParams(dimension_semantics=("parallel", "arbitrary"),
                                             vmem_limit_bytes=VMEM_LIMIT),
        name="in_proj",
    )(*args)


def _chunk_masks():
    r = lax.broadcasted_iota(jnp.int32, (CHUNK, CHUNK), 0)
    c = lax.broadcasted_iota(jnp.int32, (CHUNK, CHUNK), 1)
    return r >= c, r <= c, r > c, r < c


def _lane_col(x, lane, j):
    return jnp.sum(jnp.where(lane == j, x, 0.0), axis=1, keepdims=True)


def _head_lane_select(base, pair):
    r = lax.broadcasted_iota(jnp.int32, (SUBLANES, LANES), 0)
    ln = lax.broadcasted_iota(jnp.int32, (SUBLANES, LANES), 1)
    return jnp.where((ln == base + 2 * pair + r) & (r < 2), 1.0, 0.0)


def _split3_bf16(x):
    hi = x.astype(bf16)
    r = x - hi.astype(f32)
    mid = r.astype(bf16)
    return hi, mid, (r - mid.astype(f32)).astype(bf16)


def _thrice(a16):
    return jnp.concatenate([a16] * 3, axis=1)


def _mask_matmul_f32(mask16x3, x):
    return jnp.dot(mask16x3, jnp.concatenate(_split3_bf16(x), axis=0), preferred_element_type=f32)


def _select_rows_f32(sel16x3, x):
    return lax.dot_general(sel16x3, jnp.concatenate(_split3_bf16(x), axis=1), NT_DIMS, preferred_element_type=f32)


def _chunks_per_trip(nc):
    return min(MAX_CHUNKS_PER_TRIP, nc)


def _chunk_start(trip, u, d, nc):
    step = trip * _chunks_per_trip(nc) + u
    c = step if d == 0 else nc - 1 - step
    return pl.multiple_of(c * CHUNK, CHUNK)


def _mlstm_kernel(*refs, seq, zero_init):
    if zero_init:
        qp_ref, kp_ref, v0_ref, v1_ref, g_ref, par_ref, h_ref, cf_ref, nf_ref, mf_ref = refs
    else:
        (qp_ref, kp_ref, v0_ref, v1_ref, g_ref, par_ref, c0_ref, n0_ref, m0_ref,
         h_ref, cf_ref, nf_ref, mf_ref) = refs
    nc = seq // CHUNK
    pair = pl.program_id(1)
    lane = lax.broadcasted_iota(jnp.int32, (1, LANES), 1)
    tril, triu, _, _ = _chunk_masks()
    masks = (tril, triu)
    masks16 = (_thrice(tril.astype(bf16)), _thrice(triu.astype(bf16)))
    sel = [_thrice(_head_lane_select(LANE_F + d * H_A, pair).astype(bf16)) for d in range(2)]
    bi_row = par_ref[0:1, :]
    bf_row = par_ref[1:2, :]
    ones_col = jnp.where(lane == 0, 1.0, 0.0) + jnp.zeros((CHUNK, LANES), f32)
    v_refs = (v0_ref, v1_ref)
    streams = [(hh, d) for hh in range(2) for d in range(2)]

    h_ref[...] = jnp.zeros_like(h_ref)

    init = []
    for hh, d in streams:
        if zero_init:
            init.append((jnp.zeros((DQK_A, 2 * LANES), f32), jnp.zeros((1, 1), f32)))
        else:
            n_aug = jnp.where(lane == 0, n0_ref[d, hh], 0.0)
            init.append((jnp.concatenate([c0_ref[d, hh], n_aug], axis=1), m0_ref[d, hh]))

    def body(trip, carry):
        subs = range(_chunks_per_trip(nc))
        shared = {}
        for u in subs:
            for d in range(2):
                r0 = _chunk_start(trip, u, d, nc)
                gates = g_ref[pl.ds(r0, CHUNK), :]
                gi = gates + bi_row
                gf = _log_sigmoid(gates + bf_row)
                cum = _mask_matmul_f32(masks16[d], gf)
                stack = jnp.concatenate([cum, pltpu.roll(gi, LANE_F - LANE_I, axis=1)], axis=0)
                rows = _select_rows_f32(sel[d], stack)
                total = cum[CHUNK - 1:CHUNK, :] if d == 0 else cum[0:1, :]
                shared[u, d] = (r0, gi, cum, rows, total)
        chains = [(u, hh, d) for u in subs for hh, d in streams]
        st = {ch: {} for ch in chains}
        for ch in chains:
            u, hh, d = ch
            r0 = shared[u, d][0]
            s = st[ch]
            s["q"] = (qp_ref[pl.ds(r0, CHUNK), hh * DQK_A:(hh + 1) * DQK_A] * DQK_A ** -0.5).astype(bf16)
            s["k"] = kp_ref[pl.ds(r0, CHUNK), hh * DQK_A:(hh + 1) * DQK_A].astype(bf16)
            s["v_aug"] = jnp.concatenate([v_refs[hh][pl.ds(r0, CHUNK), :].astype(f32), ones_col], axis=1)
            s["qk"] = lax.dot_general(s["q"], s["k"], NT_DIMS, preferred_element_type=f32)
        for ch in chains:
            u, hh, d = ch
            _, gi, cum, rows, total = shared[u, d]
            s = st[ch]
            head = 2 * pair + hh
            jf = LANE_F + d * H_A + head
            ji = LANE_I + d * H_A + head
            bcol = _lane_col(cum, lane, jf)
            icol = _lane_col(gi, lane, ji)
            dmat = jnp.where(masks[d], bcol - rows[hh:hh + 1, :CHUNK] + rows[hh:hh + 1, CHUNK:], -jnp.inf)
            m_loc = jnp.max(dmat, axis=1, keepdims=True)
            s["sw"] = (s["qk"] * jnp.exp(dmat - m_loc)).astype(bf16)
            blast = _lane_col(total, lane, jf)
            gs = blast - bcol + icol
            ms_loc = jnp.max(gs, axis=0, keepdims=True)
            s["wv"] = (jnp.exp(gs - ms_loc) * s["v_aug"]).astype(bf16)
            s.update(bcol=bcol, m_loc=m_loc, blast=blast, ms_loc=ms_loc)
        for ch in chains:
            s = st[ch]
            s["num"] = jnp.dot(s["sw"], s["v_aug"].astype(bf16), preferred_element_type=f32)
            s["kv"] = lax.dot_general(s["k"], s["wv"], TN_DIMS, preferred_element_type=f32)
        state = list(carry)
        for u in subs:
            qc = [jnp.dot(st[u, hh, d]["q"], state[i][0].astype(bf16), preferred_element_type=f32)
                  for i, (hh, d) in enumerate(streams)]
            for i, (hh, d) in enumerate(streams):
                s = st[u, hh, d]
                c_aug, m = state[i]
                m_inter = s["bcol"] + m
                m_t = jnp.maximum(m_inter, s["m_loc"])
                nd = jnp.exp(s["m_loc"] - m_t) * s["num"] + jnp.exp(m_inter - m_t) * qc[i]
                den = nd[:, DV_A:DV_A + 1]
                hval = nd[:, :DV_A] / jnp.maximum(jnp.abs(den), jnp.exp(-m_t))
                r0 = shared[u, d][0]
                h_ref[pl.ds(r0, CHUNK), hh * DV_A:(hh + 1) * DV_A] += hval
                m_new = jnp.maximum(s["blast"] + m, s["ms_loc"])
                c_new = jnp.exp(s["blast"] + m - m_new) * c_aug + jnp.exp(s["ms_loc"] - m_new) * s["kv"]
                state[i] = (c_new, m_new)
        return tuple(state)

    final = lax.fori_loop(0, nc // _chunks_per_trip(nc), body, tuple(init))
    for i, (hh, d) in enumerate(streams):
        c_aug, m = final[i]
        cf_ref[d, hh] = c_aug[:, :DV_A]
        nf_ref[d, hh] = c_aug[:, DV_A:DV_A + 1]
        mf_ref[d, hh] = m


def _mlstm_call(proj, gates, par, nb, seq, off, state):
    zero_init = state is None
    blk = lambda col: pl.BlockSpec((seq, LANES), col)
    in_specs = [blk(lambda b, p: (off + b, p)),
                blk(lambda b, p: (off + b, 2 + p)),
                blk(lambda b, p: (off + b, 4 + 2 * p)),
                blk(lambda b, p: (off + b, 5 + 2 * p)),
                blk(lambda b, p: (off + b, 0)),
                pl.BlockSpec((SUBLANES, LANES), lambda b, p: (0, 0))]
    st_specs = [pl.BlockSpec((None, 2, 2, DQK_A, DV_A), lambda b, p: (b, 0, p, 0, 0)),
                pl.BlockSpec((None, 2, 2, DQK_A, 1), lambda b, p: (b, 0, p, 0, 0)),
                pl.BlockSpec((None, 2, 2, 1, 1), lambda b, p: (b, 0, p, 0, 0))]
    args = [proj, proj, proj, proj, gates, par]
    if not zero_init:
        in_specs += st_specs
        args += list(state)
    return pl.pallas_call(
        functools.partial(_mlstm_kernel, seq=seq, zero_init=zero_init),
        grid=(nb, H_A // 2),
        in_specs=in_specs,
        out_specs=[pl.BlockSpec((seq, 2 * DV_A), lambda b, p: (b, p))] + st_specs,
        out_shape=[jax.ShapeDtypeStruct((nb * seq, H_A * DV_A), f32),
                   jax.ShapeDtypeStruct((nb, 2, H_A, DQK_A, DV_A), f32),
                   jax.ShapeDtypeStruct((nb, 2, H_A, DQK_A, 1), f32),
                   jax.ShapeDtypeStruct((nb, 2, H_A, 1, 1), f32)],
        compiler_params=pltpu.CompilerParams(vmem_limit_bytes=VMEM_LIMIT),
        name="mlstm",
    )(*args)


CONV_PAD = SUBLANES


def _short_conv_silu(x, w_ref, pad_scr, seq):
    pad_scr[:CONV_PAD, :] = jnp.zeros((CONV_PAD, LANES), f32)
    pad_scr[CONV_PAD + seq:, :] = jnp.zeros((CONV_PAD, LANES), f32)
    pad_scr[CONV_PAD:CONV_PAD + seq, :] = x
    acc = x * w_ref[CONV_K // 2:CONV_K // 2 + 1, :]
    for tap in range(CONV_K):
        delta = tap - CONV_K // 2
        if delta != 0:
            acc = acc + pad_scr[CONV_PAD + delta:CONV_PAD + delta + seq, :] * w_ref[tap:tap + 1, :]
    return _silu(acc)


def _l2_unit(y):
    return y * lax.rsqrt(jnp.sum(y * y, axis=-1, keepdims=True) + EPS)


def _split_bf16(x):
    hi = x.astype(bf16)
    return hi, (x - hi.astype(f32)).astype(bf16)


def _matmul_3pass(m, x):
    mh, ml = _split_bf16(m)
    xh, xl = _split_bf16(x)
    return jnp.dot(jnp.concatenate([mh, ml, mh], axis=1), jnp.concatenate([xh, xh, xl], axis=0),
                   preferred_element_type=f32)


def _unit_triangular_solves(ns, xs):
    levels = CHUNK.bit_length() - 1
    r = lax.broadcasted_iota(jnp.int32, (CHUNK, CHUNK), 0)
    c = lax.broadcasted_iota(jnp.int32, (CHUNK, CHUNK), 1)
    eye = jnp.where(r == c, 1.0, 0.0)
    ts = [eye + n for n in ns]
    ms = [_matmul_3pass(n, n) for n in ns]
    for lvl in range(1, levels):
        for i in range(len(ns)):
            if lvl < levels - 1:
                prod = _matmul_3pass(ms[i], jnp.concatenate([ts[i], ms[i]], axis=1))
                ts[i] = ts[i] + prod[:, :CHUNK]
                ms[i] = prod[:, CHUNK:]
            else:
                ts[i] = ts[i] + _matmul_3pass(ms[i], ts[i])
    return [_matmul_3pass(t, x) for t, x in zip(ts, xs)]


def _delta_kernel(*refs, seq, zero_init):
    if zero_init:
        (q0_ref, q1_ref, k0_ref, k1_ref, v0_ref, v1_ref, wq0_ref, wq1_ref, wk0_ref, wk1_ref, wv0_ref, wv1_ref,
         g_ref, par_ref, o_ref, sf_ref, q_scr, k_scr, v_scr, pad_scr) = refs
    else:
        (q0_ref, q1_ref, k0_ref, k1_ref, v0_ref, v1_ref, wq0_ref, wq1_ref, wk0_ref, wk1_ref, wv0_ref, wv1_ref,
         g_ref, par_ref, s0_ref, o_ref, sf_ref, q_scr, k_scr, v_scr, pad_scr) = refs
    nc = seq // CHUNK
    pair = pl.program_id(1)
    lane = lax.broadcasted_iota(jnp.int32, (1, LANES), 1)
    tril, triu, stril, striu = _chunk_masks()
    masks, smasks = (tril, triu), (stril, striu)
    masks16 = (_thrice(tril.astype(bf16)), _thrice(triu.astype(bf16)))
    sel = [_thrice(_head_lane_select(LANE_A + d * H_B, pair).astype(bf16)) for d in range(2)]
    neg_a_row = -jnp.exp(par_ref[0:1, :])
    dt_row = par_ref[1:2, :]
    streams = [(hh, d) for hh in range(2) for d in range(2)]

    for hh, (q_ref, k_ref, v_ref, wq_ref, wk_ref, wv_ref) in enumerate(
            ((q0_ref, k0_ref, v0_ref, wq0_ref, wk0_ref, wv0_ref), (q1_ref, k1_ref, v1_ref, wq1_ref, wk1_ref, wv1_ref))):
        q_scr[hh] = _l2_unit(_short_conv_silu(q_ref[...].astype(f32), wq_ref, pad_scr, seq)) * DK_B ** -0.5
        k_scr[hh] = _l2_unit(_short_conv_silu(k_ref[...].astype(f32), wk_ref, pad_scr, seq))
        v_scr[hh] = _short_conv_silu(v_ref[...].astype(f32), wv_ref, pad_scr, seq)

    o_ref[...] = jnp.zeros_like(o_ref)
    init = tuple(jnp.zeros((DK_B, DV_B), f32) if zero_init else s0_ref[d, hh] for hh, d in streams)

    def body(trip, carry):
        subs = range(_chunks_per_trip(nc))
        shared = {}
        for u in subs:
            for d in range(2):
                r0 = _chunk_start(trip, u, d, nc)
                gates = g_ref[pl.ds(r0, CHUNK), :]
                beta_all = jax.nn.sigmoid(gates)
                glog = neg_a_row * _softplus(gates + dt_row)
                cum = _mask_matmul_f32(masks16[d], glog)
                rows = _select_rows_f32(sel[d], cum)
                shared[u, d] = (r0, beta_all, cum, rows)
        chains = [(u, hh, d) for u in subs for hh, d in streams]
        st = {ch: {} for ch in chains}
        for ch in chains:
            u, hh, d = ch
            r0, beta_all, cum, rows = shared[u, d]
            s = st[ch]
            head = 2 * pair + hh
            beta = _lane_col(beta_all, lane, LANE_BETA + d * H_B + head)
            gcol = _lane_col(cum, lane, LANE_A + d * H_B + head)
            glast = gcol[CHUNK - 1:CHUNK, :] if d == 0 else gcol[0:1, :]
            q = q_scr[hh, pl.ds(r0, CHUNK), :]
            k = k_scr[hh, pl.ds(r0, CHUNK), :]
            v = v_scr[hh, pl.ds(r0, CHUNK), :]
            k16 = k.astype(bf16)
            kbeta = k * beta
            eg = jnp.exp(gcol)
            decay = jnp.exp(jnp.where(masks[d], gcol - rows[hh:hh + 1, :], -jnp.inf))
            kk = lax.dot_general(kbeta.astype(bf16), k16, NT_DIMS, preferred_element_type=f32)
            qk = lax.dot_general(q.astype(bf16), k16, NT_DIMS, preferred_element_type=f32)
            s["n"] = -jnp.where(smasks[d], kk * decay, 0.0)
            s["x"] = jnp.concatenate([v * beta, kbeta * eg], axis=1)
            s["qk"] = (qk * decay).astype(bf16)
            s["qg"] = (q * eg).astype(bf16)
            s["kd_t"] = (k * jnp.exp(glast - gcol)).T.astype(bf16)
            s["gl"] = jnp.exp(glast)
        solved = _unit_triangular_solves([st[ch]["n"] for ch in chains], [st[ch]["x"] for ch in chains])
        for ch, uw in zip(chains, solved):
            s = st[ch]
            s["u"] = uw[:, :DV_B]
            s["w_qg"] = jnp.concatenate([uw[:, DV_B:].astype(bf16), s["qg"]], axis=0)
        state = list(carry)
        for u in subs:
            ws = [jnp.dot(st[u, hh, d]["w_qg"], state[i].astype(bf16), preferred_element_type=f32)
                  for i, (hh, d) in enumerate(streams)]
            v_new = [(st[u, hh, d]["u"] - ws[i][:CHUNK]).astype(bf16) for i, (hh, d) in enumerate(streams)]
            for i, (hh, d) in enumerate(streams):
                s = st[u, hh, d]
                o = ws[i][CHUNK:] + jnp.dot(s["qk"], v_new[i], preferred_element_type=f32)
                r0 = shared[u, d][0]
                o_ref[pl.ds(r0, CHUNK), hh * DV_B:(hh + 1) * DV_B] += o
                state[i] = state[i] * s["gl"] + jnp.dot(s["kd_t"], v_new[i], preferred_element_type=f32)
        return tuple(state)

    final = lax.fori_loop(0, nc // _chunks_per_trip(nc), body, init)
    for i, (hh, d) in enumerate(streams):
        sf_ref[d, hh] = final[i]


def _delta_call(proj, conv_w8, gates, par, nb, seq, off, state):
    zero_init = state is None
    col0 = (2 * H_A * DQK_A + 2 * H_A * DV_A) // LANES
    blk = lambda col: pl.BlockSpec((seq, LANES), col)
    wblk = lambda col: pl.BlockSpec((SUBLANES, LANES), col)
    in_specs = [blk(lambda b, p: (off + b, col0 + 2 * p)), blk(lambda b, p: (off + b, col0 + 2 * p + 1)),
                blk(lambda b, p: (off + b, col0 + H_B + 2 * p)), blk(lambda b, p: (off + b, col0 + H_B + 2 * p + 1)),
                blk(lambda b, p: (off + b, col0 + 2 * H_B + 2 * p)), blk(lambda b, p: (off + b, col0 + 2 * H_B + 2 * p + 1)),
                wblk(lambda b, p: (0, 2 * p)), wblk(lambda b, p: (0, 2 * p + 1)),
                wblk(lambda b, p: (0, H_B + 2 * p)), wblk(lambda b, p: (0, H_B + 2 * p + 1)),
                wblk(lambda b, p: (0, 2 * H_B + 2 * p)), wblk(lambda b, p: (0, 2 * H_B + 2 * p + 1)),
                blk(lambda b, p: (off + b, 0)),
                pl.BlockSpec((SUBLANES, LANES), lambda b, p: (0, 0))]
    st_spec = pl.BlockSpec((None, 2, 2, DK_B, DV_B), lambda b, p: (b, 0, p, 0, 0))
    args = [proj] * 6 + [conv_w8] * 6 + [gates, par]
    if not zero_init:
        in_specs.append(st_spec)
        args.append(state)
    return pl.pallas_call(
        functools.partial(_delta_kernel, seq=seq, zero_init=zero_init),
        grid=(nb, H_B // 2),
        in_specs=in_specs,
        out_specs=[pl.BlockSpec((seq, 2 * DV_B), lambda b, p: (b, p)), st_spec],
        out_shape=[jax.ShapeDtypeStruct((nb * seq, H_B * DV_B), f32),
                   jax.ShapeDtypeStruct((nb, 2, H_B, DK_B, DV_B), f32)],
        scratch_shapes=[pltpu.VMEM((2, seq, LANES), f32)] * 3 + [pltpu.VMEM((seq + 2 * CONV_PAD, LANES), f32)],
        compiler_params=pltpu.CompilerParams(vmem_limit_bytes=VMEM_LIMIT),
        name="delta",
    )(*args)


def _head_rms(x):
    return x * lax.rsqrt(jnp.mean(x * x, axis=-1, keepdims=True) + EPS)


def _ab_out_kernel(x_ref, hp_ref, hs_ref, op_ref, os_ref, om_ref, zd_ref, gm_ref, gd_ref, w_ref, mod_ref, o_ref,
                   *, gate_row, tm):
    hsum = _prompt_or_sample(hp_ref, hs_ref, tm)
    osum = _prompt_or_sample(op_ref, os_ref, tm)
    parts = []
    for h in range(H_A):
        sl = slice(h * DV_A, (h + 1) * DV_A)
        parts.append(_head_rms(hsum[:, sl]) * gm_ref[:, sl] * jax.nn.sigmoid(om_ref[:, sl].astype(f32)))
    for h in range(H_B):
        sl = slice(h * DV_B, (h + 1) * DV_B)
        parts.append(_head_rms(osum[:, sl]) * gd_ref[:, sl] * _silu(zd_ref[:, sl].astype(f32)))
    cat = jnp.concatenate(parts, axis=1).astype(bf16)
    y = jnp.dot(cat, w_ref[...].astype(bf16), preferred_element_type=f32)
    o_ref[...] = x_ref[...] + mod_ref[gate_row:gate_row + 1, :] * y


def _ab_out_call(x, hp, hs, op, os_, proj, gm_row, gd_row, w_out, mod_l, gate_row, tm=1024):
    wide = H_A * DV_A
    om_blk = (2 * H_A * DQK_A) // wide + 1
    zd_blk = AB_MAIN // wide - 1
    return pl.pallas_call(
        functools.partial(_ab_out_kernel, gate_row=gate_row, tm=tm),
        grid=(N_TOK // tm,),
        in_specs=[pl.BlockSpec((tm, D_MODEL), lambda i: (i, 0))] + _split_specs(tm, wide) + _split_specs(tm, wide) + [
            pl.BlockSpec((tm, wide), lambda i: (i, om_blk)),
            pl.BlockSpec((tm, wide), lambda i: (i, zd_blk)),
            pl.BlockSpec((1, wide), lambda i: (0, 0)),
            pl.BlockSpec((1, wide), lambda i: (0, 0)),
            pl.BlockSpec((2 * wide, D_MODEL), lambda i: (0, 0)),
            pl.BlockSpec((None, N_ADA, D_MODEL), lambda i: (_mod_row_index(i, tm), 0, 0))],
        out_specs=pl.BlockSpec((tm, D_MODEL), lambda i: (i, 0)),
        out_shape=jax.ShapeDtypeStruct((N_TOK, D_MODEL), f32),
        compiler_params=pltpu.CompilerParams(vmem_limit_bytes=VMEM_LIMIT),
        name="ab_out",
    )(x, hp, hs, op, os_, proj, proj, gm_row, gd_row, w_out, mod_l)


def _na_out_kernel(x_ref, ap_ref, as_ref, w_ref, mod_ref, o_ref, *, gate_row, tm):
    a = _prompt_or_sample(ap_ref, as_ref, tm).astype(bf16)
    y = jnp.dot(a, w_ref[...].astype(bf16), preferred_element_type=f32)
    o_ref[...] = x_ref[...] + mod_ref[gate_row:gate_row + 1, :] * y


def _na_out_call(x, attn_p, attn_s, w_out, mod_l, gate_row, tm=1024):
    return pl.pallas_call(
        functools.partial(_na_out_kernel, gate_row=gate_row, tm=tm),
        grid=(N_TOK // tm,),
        in_specs=[pl.BlockSpec((tm, D_MODEL), lambda i: (i, 0))] + _split_specs(tm, NA_W) + [
            pl.BlockSpec((NA_W, D_MODEL), lambda i: (0, 0)),
            pl.BlockSpec((None, N_ADA, D_MODEL), lambda i: (_mod_row_index(i, tm), 0, 0))],
        out_specs=pl.BlockSpec((tm, D_MODEL), lambda i: (i, 0)),
        out_shape=jax.ShapeDtypeStruct((N_TOK, D_MODEL), f32),
        compiler_params=pltpu.CompilerParams(vmem_limit_bytes=VMEM_LIMIT),
        name="na_out",
    )(x, attn_p, attn_s, w_out, mod_l)


CTX_HEADS = 8


def _ctx_attn_kernel(q_ref, k_ref, v_ref, o_ref, nk_ref, nv_ref):
    low = lax.broadcasted_iota(jnp.int32, (1, LANES), 1) < DH_C
    for pair in range(CTX_HEADS // 2):
        sl = slice(pair * LANES, (pair + 1) * LANES)
        q = q_ref[:, sl] * DH_C ** -0.5
        k_t = k_ref[:, sl].T
        v_t = v_ref[:, sl].T
        for hh in range(2):
            nk_ref[2 * pair + hh] = k_t[hh * DH_C:(hh + 1) * DH_C, :]
            nv_ref[2 * pair + hh] = v_t[hh * DH_C:(hh + 1) * DH_C, :]
        k = k_ref[:, sl].astype(bf16)
        v = v_ref[:, sl].astype(bf16)
        scores = [lax.dot_general(jnp.where(low if hh == 0 else jnp.logical_not(low), q, 0.0).astype(bf16), k,
                                  NT_DIMS, preferred_element_type=f32) for hh in range(2)]
        probs = [jnp.exp(s - jnp.max(s, axis=1, keepdims=True)) for s in scores]
        outs = [jnp.dot(p.astype(bf16), v, preferred_element_type=f32) / jnp.sum(p, axis=1, keepdims=True)
                for p in probs]
        o_ref[:, sl] = jnp.where(low, outs[0], outs[1]).astype(o_ref.dtype)


def _ctx_attn_call(proj):
    w = CTX_HEADS * DH_C
    nblk = NA_W // w
    return pl.pallas_call(
        _ctx_attn_kernel,
        grid=(BATCH, nblk),
        in_specs=[pl.BlockSpec((SEQ, w), lambda b, j: (b, j)),
                  pl.BlockSpec((SEQ, w), lambda b, j: (b, nblk + j)),
                  pl.BlockSpec((SEQ, w), lambda b, j: (b, 2 * nblk + j))],
        out_specs=[pl.BlockSpec((SEQ, w), lambda b, j: (b, j)),
                   pl.BlockSpec((None, CTX_HEADS, DH_C, SEQ), lambda b, j: (b, j, 0, 0)),
                   pl.BlockSpec((None, CTX_HEADS, DH_C, SEQ), lambda b, j: (b, j, 0, 0))],
        out_shape=[jax.ShapeDtypeStruct((N_PROMPT, NA_W), bf16),
                   jax.ShapeDtypeStruct((BATCH, H_C, DH_C, SEQ), f32),
                   jax.ShapeDtypeStruct((BATCH, H_C, DH_C, SEQ), f32)],
        compiler_params=pltpu.CompilerParams(vmem_limit_bytes=VMEM_LIMIT),
        name="ctx_attn",
    )(proj, proj, proj)


QROWS = 4
QBLK = QROWS * GRID_W
KROWS_MID = QROWS + WIN_R - 1
N_RIDX = 2 * WIN_R - 1
N_CIDX = 2 * WIN_C - 1
N_QBLK = GRID_ROWS // QROWS


def _nattn_bias_tables(rb_ref, tb_scr, bmid_scr, btop_scr, bbot_scr):
    qc = lax.broadcasted_iota(jnp.int32, (GRID_W, GRID_W), 0)
    kc = lax.broadcasted_iota(jnp.int32, (GRID_W, GRID_W), 1)
    cs = jnp.clip(qc - WIN_C // 2, 0, GRID_W - WIN_C)
    valid = (kc >= cs) & (kc < cs + WIN_C)
    neg = jnp.full((GRID_W, GRID_W), -jnp.inf, f32)
    for hh in range(2):
        for ri in range(N_RIDX):
            row = jnp.broadcast_to(rb_ref[hh, ri:ri + 1, :], (GRID_W, LANES))
            tile = pltpu.roll(row, LANES - (WIN_C - 1), 1, stride=1, stride_axis=0)[:, :GRID_W]
            tb_scr[ri] = jnp.where(valid, tile, -jnp.inf)
        for i in range(QROWS):
            rs = slice(i * GRID_W, (i + 1) * GRID_W)
            for jj in range(KROWS_MID):
                inside = 0 <= jj - i < WIN_R
                bmid_scr[hh, rs, jj * GRID_W:(jj + 1) * GRID_W] = tb_scr[jj - i + WIN_R // 2 - 1] if inside else neg
            for jj in range(WIN_R):
                btop_scr[hh, rs, jj * GRID_W:(jj + 1) * GRID_W] = tb_scr[jj - i + WIN_R - 1]
                bbot_scr[hh, rs, jj * GRID_W:(jj + 1) * GRID_W] = tb_scr[jj - i + WIN_R // 2 - 1]


def _nattn_kernel(rb_ref, q_ref, k_ref, v_ref, ck_ref, cv_ref, o_ref, tb_scr, bmid_scr, btop_scr, bbot_scr):
    scale = DH_C ** -0.5

    @pl.when(pl.program_id(1) == 0)
    def _():
        _nattn_bias_tables(rb_ref, tb_scr, bmid_scr, btop_scr, bbot_scr)

    low = lax.broadcasted_iota(jnp.int32, (1, LANES), 1) < DH_C
    kctx_t = jnp.concatenate([ck_ref[0], ck_ref[1]], axis=0).astype(bf16)
    vctx = jnp.concatenate([cv_ref[0], cv_ref[1]], axis=0).T.astype(bf16)

    def block(q_start, k_start, k_rows, bias_scr):
        nk = k_rows * GRID_W
        q = q_ref[pl.ds(q_start, QBLK), :] * scale
        ku = k_ref[pl.ds(k_start, nk), :].astype(bf16)
        vu = v_ref[pl.ds(k_start, nk), :].astype(bf16)
        scores = []
        for hh in range(2):
            qh = jnp.where(low if hh == 0 else jnp.logical_not(low), q, 0.0).astype(bf16)
            scores.append((lax.dot_general(qh, ku, NT_DIMS, preferred_element_type=f32) + bias_scr[hh],
                           jnp.dot(qh, kctx_t, preferred_element_type=f32)))
        probs = []
        for s_loc, s_ctx in scores:
            m = jnp.maximum(jnp.max(s_loc, axis=1, keepdims=True), jnp.max(s_ctx, axis=1, keepdims=True))
            p_loc = jnp.exp(s_loc - m)
            p_ctx = jnp.exp(s_ctx - m)
            denom = jnp.sum(p_loc, axis=1, keepdims=True) + jnp.sum(p_ctx, axis=1, keepdims=True)
            probs.append((p_loc.astype(bf16), p_ctx.astype(bf16), denom))
        outs = [(jnp.dot(p_loc, vu, preferred_element_type=f32) + jnp.dot(p_ctx, vctx, preferred_element_type=f32)) / denom
                for p_loc, p_ctx, denom in probs]
        o_ref[pl.ds(q_start, QBLK), :] = jnp.where(low, outs[0], outs[1]).astype(o_ref.dtype)

    block(0, 0, WIN_R, btop_scr)

    def mid_body(blk, _):
        q_start = pl.multiple_of(blk * QBLK, QBLK)
        k_start = pl.multiple_of(blk * QBLK - (WIN_R // 2) * GRID_W, QBLK)
        block(q_start, k_start, KROWS_MID, bmid_scr)
        return 0

    lax.fori_loop(1, N_QBLK - 1, mid_body, 0, unroll=3)
    block((N_QBLK - 1) * QBLK, (GRID_ROWS - WIN_R) * GRID_W, WIN_R, bbot_scr)


def _nattn_call(proj, cache_k, cache_v, layer, rel_bias, off):
    npair = H_C // 2
    return pl.pallas_call(
        _nattn_kernel,
        grid=(npair, DEC_BATCH),
        in_specs=[pl.BlockSpec((2, 2 * WIN_R, LANES), lambda p, b: (p, 0, 0)),
                  pl.BlockSpec((DEC_SEQ, LANES), lambda p, b: (off + b, p)),
                  pl.BlockSpec((DEC_SEQ, LANES), lambda p, b: (off + b, npair + p)),
                  pl.BlockSpec((DEC_SEQ, LANES), lambda p, b: (off + b, 2 * npair + p)),
                  pl.BlockSpec((None, None, 2, DH_C, PAST_LEN), lambda p, b: (b, layer, p, 0, 0)),
                  pl.BlockSpec((None, None, 2, DH_C, PAST_LEN), lambda p, b: (b, layer, p, 0, 0))],
        out_specs=pl.BlockSpec((DEC_SEQ, LANES), lambda p, b: (b, p)),
        out_shape=jax.ShapeDtypeStruct((N_SAMPLE, NA_W), bf16),
        scratch_shapes=[pltpu.VMEM((N_RIDX, GRID_W, GRID_W), f32),
                        pltpu.VMEM((2, QBLK, KROWS_MID * GRID_W), f32),
                        pltpu.VMEM((2, QBLK, WIN_R * GRID_W), f32),
                        pltpu.VMEM((2, QBLK, WIN_R * GRID_W), f32)],
        compiler_params=pltpu.CompilerParams(dimension_semantics=("arbitrary", "arbitrary"),
                                             vmem_limit_bytes=VMEM_LIMIT),
        name="nattn",
    )(jnp.pad(rel_bias, ((0, 0), (0, 2 * WIN_R - N_RIDX), (0, LANES - N_CIDX))), proj, proj, proj, cache_k, cache_v)


def _lane_row(pieces):
    row = jnp.zeros((LANES,), f32)
    for off, vals in pieces:
        row = row.at[off:off + vals.shape[0]].set(vals.astype(f32))
    return row


def _param_rows(rows):
    out = jnp.zeros((SUBLANES, LANES), f32)
    for r, row in enumerate(rows):
        out = out.at[r].set(row)
    return out


def kernel(x_prompt, x_sample, c, state_mlstm_C, state_mlstm_n, state_mlstm_m, state_delta_S, cache_na_k, cache_na_v, c_ctx, ada_w, ada_b, norm_g, ffn_wg, ffn_wu, ffn_wd, ab_w_in, ab_w_out, mlstm_b_i, mlstm_b_f, mlstm_norm_g, delta_conv_w, delta_a_log, delta_dt_bias, delta_norm_g, na_w_in, na_w_out, na_rel_bias, final_norm_g):
    xs = (x_prompt.reshape(N_PROMPT, D_MODEL), x_sample.reshape(N_SAMPLE, D_MODEL))
    mods = _ada_call(jnp.concatenate([c_ctx[None, :], c], axis=0), ada_w, ada_b)
    mods = mods.reshape(DEPTH, 3, N_ADA, D_MODEL)
    gf_row = final_norm_g.reshape(1, D_MODEL)
    s_off = N_PROMPT // DEC_SEQ
    new_c, new_n, new_m, new_s, new_k, new_v = [], [], [], [], [], []

    for l in range(DEPTH):
        mod_l = mods[l]
        a = l // 2
        x = _ffn_call(xs if l == 0 else (x,), mod_l, norm_g[l, 0].reshape(1, D_MODEL), ffn_wg, ffn_wu, ffn_wd, l, 0,
                      gf_row, rows=(0, 1, 2), final=False, split_out=False, tm=FFN_TM)[0]
        g_mix = norm_g[l, 1].reshape(1, D_MODEL)
        if l % 2 == 0:
            w_main, w_gate = _ab_weight_call(ab_w_in, a)
            proj, gates = _proj_call(x, mod_l, g_mix, w_main, w_gate, rows=(3, 4), out_dtype=bf16, w_transposed=True)

            par_m = _param_rows([_lane_row([(LANE_I, mlstm_b_i[a].reshape(-1))]),
                                 _lane_row([(LANE_F, mlstm_b_f[a].reshape(-1))])])
            par_d = _param_rows([_lane_row([(LANE_A, delta_a_log[a].reshape(-1))]),
                                 _lane_row([(LANE_A, delta_dt_bias[a].reshape(-1))])])
            conv_w8 = jnp.concatenate([delta_conv_w[a], jnp.zeros((SUBLANES - CONV_K, 3 * H_B * DK_B), f32)], axis=0)

            hp, cn, nn_, mn = _mlstm_call(proj, gates, par_m, BATCH, SEQ, 0, None)
            st = (state_mlstm_C[:, a], state_mlstm_n[:, a][..., None], state_mlstm_m[:, a][..., None, None])
            hs, _, _, _ = _mlstm_call(proj, gates, par_m, DEC_BATCH, DEC_SEQ, s_off, st)
            op, sn = _delta_call(proj, conv_w8, gates, par_d, BATCH, SEQ, 0, None)
            os_, _ = _delta_call(proj, conv_w8, gates, par_d, DEC_BATCH, DEC_SEQ, s_off, state_delta_S[:, a])
            new_c.append(cn)
            new_n.append(nn_[..., 0])
            new_m.append(mn[..., 0, 0])
            new_s.append(sn)
            x = _ab_out_call(x, hp, hs, op, os_, proj, mlstm_norm_g[a].reshape(1, -1),
                             jnp.tile(delta_norm_g[a], H_B).reshape(1, -1), ab_w_out[a], mod_l, gate_row=5)
        else:
            proj = _proj_call(x, mod_l, g_mix, na_w_in[a], None, rows=(3, 4), out_dtype=f32, w_transposed=False)[0]
            attn_p, kp_t, vp_t = _ctx_attn_call(proj)
            attn_s = _nattn_call(proj, jnp.swapaxes(cache_na_k, -1, -2), jnp.swapaxes(cache_na_v, -1, -2), a,
                                 na_rel_bias[a], s_off)
            new_k.append(jnp.swapaxes(kp_t, -1, -2))
            new_v.append(jnp.swapaxes(vp_t, -1, -2))
            x = _na_out_call(x, attn_p, attn_s, na_w_out[a], mod_l, gate_row=5)
        last = l == DEPTH - 1
        outs = _ffn_call((x,), mod_l, norm_g[l, 2].reshape(1, D_MODEL), ffn_wg, ffn_wu, ffn_wd, l, 1,
                         gf_row, rows=(6, 7, 8), final=last, split_out=last, tm=FFN_TM)
        x = outs[0]

    y_prompt = outs[0].reshape(BATCH, SEQ, D_MODEL)
    y_sample = outs[1].reshape(DEC_BATCH, DEC_SEQ, D_MODEL)
    return (y_prompt, y_sample, jnp.stack(new_c, axis=1), jnp.stack(new_n, axis=1), jnp.stack(new_m, axis=1),
            jnp.stack(new_s, axis=1), jnp.stack(new_k, axis=1), jnp.stack(new_v, axis=1))
```

```python
import functools

import jax
import jax.numpy as jnp
from jax import lax
from jax.experimental import pallas as pl
from jax.experimental.pallas import tpu as pltpu

f32 = jnp.float32
bf16 = jnp.bfloat16

D_MODEL = 1024
BATCH = 16
SEQ = 256
DEPTH = 2
DEC_BATCH = 2
DEC_SEQ = 2048
PAST_LEN = 256
GRID_W = 64
GRID_ROWS = DEC_SEQ // GRID_W
D_FF = 2816
N_ADA = 9
EPS = 1e-6
CHUNK = 64
H_A, DQK_A, DV_A = 4, 64, 128
H_B, DK_B, DV_B = 4, 128, 128
CONV_K = 5
H_C, DH_C = 16, 64
WIN_R, WIN_C = 8, 16
NA_W = H_C * DH_C
assert DH_C ** -0.5 == 2.0 ** -3 and DQK_A ** -0.5 == 2.0 ** -3

N_PROMPT = BATCH * SEQ
N_SAMPLE = DEC_BATCH * DEC_SEQ
N_TOK = N_PROMPT + N_SAMPLE
AB_MAIN = 2 * H_A * DQK_A + 2 * H_A * DV_A + 2 * H_B * DK_B + 2 * H_B * DV_B
LANES = 128
VMEM_LIMIT = 56 * 1024 * 1024
FFN_VMEM_SLACK = 4 * 1024 * 1024
SUBLANES = 8

LANE_I, LANE_F, LANE_BETA, LANE_A = 0, 8, 16, 24

NT_DIMS = (((1,), (1,)), ((), ()))
TN_DIMS = (((0,), (0,)), ((), ()))

MAX_CHUNKS_PER_TRIP = 4
FFN_TM = 2048
FFN_ROW_BLOCK = 512
PROJ_ROW_BLOCK = 256


def _softplus(x):
    return jnp.maximum(x, 0.0) + jnp.log1p(jnp.exp(-jnp.abs(x)))


def _log_sigmoid(x):
    return -_softplus(-x)


def _silu(x):
    return x * jax.nn.sigmoid(x)


def _mod_row_index(i, tm):
    n_p = N_PROMPT // tm
    per_b = DEC_SEQ // tm
    return jnp.where(i < n_p, 0, 1 + (i - n_p) // per_b)


def _modulated(x, g_row, shift_row, scale_row):
    y = x * lax.rsqrt(jnp.mean(x * x, axis=-1, keepdims=True) + EPS) * g_row
    return y * (1.0 + scale_row) + shift_row


def _ada_kernel(cb_ref, w_ref, b_ref, o_ref, s_scr):
    tn = w_ref.shape[-1]

    @pl.when((pl.program_id(0) == 0) & (pl.program_id(1) == 0))
    def _():
        cb = cb_ref[...]
        s_scr[...] = cb * jax.nn.sigmoid(cb)

    for j in range(tn // LANES):
        cols = slice(j * LANES, (j + 1) * LANES)
        w = w_ref[:, cols]
        for r in range(3):
            o_ref[r:r + 1, cols] = jnp.sum(w * s_scr[r], axis=0, keepdims=True) + b_ref[:, cols]


def _ada_call(cond3, ada_w, ada_b):
    tn = N_ADA * D_MODEL // 4
    cb = jnp.broadcast_to(cond3[:, :, None], (3, D_MODEL, LANES))
    return pl.pallas_call(
        _ada_kernel,
        grid=(DEPTH, N_ADA * D_MODEL // tn),
        in_specs=[pl.BlockSpec((3, D_MODEL, LANES), lambda l, j: (0, 0, 0)),
                  pl.BlockSpec((None, D_MODEL, tn), lambda l, j: (l, 0, j)),
                  pl.BlockSpec((None, 1, tn), lambda l, j: (l, 0, j))],
        out_specs=pl.BlockSpec((None, 3, tn), lambda l, j: (l, 0, j)),
        out_shape=jax.ShapeDtypeStruct((DEPTH, 3, N_ADA * D_MODEL), f32),
        scratch_shapes=[pltpu.VMEM((3, D_MODEL, LANES), f32)],
        compiler_params=pltpu.CompilerParams(dimension_semantics=("arbitrary", "arbitrary"),
                                             vmem_limit_bytes=VMEM_LIMIT),
        name="ada_mod",
    )(cb, ada_w, ada_b.reshape(DEPTH, 1, N_ADA * D_MODEL))


def _prompt_or_sample(p_ref, s_ref, tm):
    return jnp.where(pl.program_id(0) < N_PROMPT // tm, p_ref[...], s_ref[...])


def _split_specs(tm, width, n_grid_axes=1):
    n_p = N_PROMPT // tm
    if n_grid_axes == 1:
        return [pl.BlockSpec((tm, width), lambda i: (jnp.minimum(i, n_p - 1), 0)),
                pl.BlockSpec((tm, width), lambda i: (jnp.maximum(i - n_p, 0), 0))]
    return [pl.BlockSpec((tm, width), lambda i, j: (jnp.minimum(i, n_p - 1), 0)),
            pl.BlockSpec((tm, width), lambda i, j: (jnp.maximum(i - n_p, 0), 0))]


def _ffn_kernel(*refs, rows, final, split_in, split_out, tm):
    it = iter(refs)
    x_refs = [next(it) for _ in range(2 if split_in else 1)]
    mod_ref, g_ref, wg_ref, wu_ref, wd_ref, gf_ref = [next(it) for _ in range(6)]
    o_refs = [next(it) for _ in range(2 if split_out else 1)]
    h_scr = next(it)
    j = pl.program_id(1)
    last_j = pl.num_programs(1) - 1
    is_prompt = pl.program_id(0) < N_PROMPT // tm

    def x_rows(rs):
        if split_in:
            return jnp.where(is_prompt, x_refs[0][rs, :], x_refs[1][rs, :])
        return x_refs[0][rs, :]

    def step(first, last, acc_ref):
        wg = wg_ref[...].astype(bf16)
        wu = wu_ref[...].astype(bf16)
        wd = wd_ref[...].astype(bf16)
        for r in range(tm // FFN_ROW_BLOCK):
            rs = slice(r * FFN_ROW_BLOCK, (r + 1) * FFN_ROW_BLOCK)
            if first:
                h = _modulated(x_rows(rs), g_ref[...], mod_ref[rows[0]:rows[0] + 1, :],
                               mod_ref[rows[1]:rows[1] + 1, :]).astype(bf16)
                h_scr[rs, :] = h
            else:
                h = h_scr[rs, :]
            g = jnp.dot(h, wg, preferred_element_type=f32)
            u = jnp.dot(h, wu, preferred_element_type=f32)
            part = jnp.dot((_silu(g) * u).astype(bf16), wd, preferred_element_type=f32)
            acc = part if first else acc_ref[rs, :] + part
            if last:
                xn = x_rows(rs) + (0.5 * mod_ref[rows[2]:rows[2] + 1, :]) * acc
                if final:
                    xn = xn * lax.rsqrt(jnp.mean(xn * xn, axis=-1, keepdims=True) + EPS) * gf_ref[...]
                acc_ref[rs, :] = xn
            else:
                acc_ref[rs, :] = acc

    owners = [(is_prompt, o_refs[0]), (jnp.logical_not(is_prompt), o_refs[1])] if split_out else [(True, o_refs[0])]
    for owns, o_ref in owners:
        pl.when((j == 0) & owns)(functools.partial(step, True, False, o_ref))
        pl.when((j > 0) & (j < last_j) & owns)(functools.partial(step, False, False, o_ref))
        pl.when((j == last_j) & owns)(functools.partial(step, False, True, o_ref))


def _ffn_call(xs, mod_l, g_row, ffn_wg, ffn_wu, ffn_wd, layer, half, gf_row, rows, final, split_out, tm, tf=256):
    split_in = len(xs) == 2
    x_specs = _split_specs(tm, D_MODEL, 2) if split_in else [pl.BlockSpec((tm, D_MODEL), lambda i, j: (i, 0))]
    n_token_windows = 2 + split_in + split_out
    vmem_bytes = (2 * n_token_windows * tm * D_MODEL * 4 + tm * D_MODEL * 2 + 2 * 3 * D_MODEL * tf * 4
                  + FFN_VMEM_SLACK)
    if split_out:
        out_specs = _split_specs(tm, D_MODEL, 2)
        out_shape = [jax.ShapeDtypeStruct((N_PROMPT, D_MODEL), f32), jax.ShapeDtypeStruct((N_SAMPLE, D_MODEL), f32)]
    else:
        out_specs = [pl.BlockSpec((tm, D_MODEL), lambda i, j: (i, 0))]
        out_shape = [jax.ShapeDtypeStruct((N_TOK, D_MODEL), f32)]
    scratch = [pltpu.VMEM((tm, D_MODEL), bf16)]
    return pl.pallas_call(
        functools.partial(_ffn_kernel, rows=rows, final=final, split_in=split_in, split_out=split_out, tm=tm),
        grid=(N_TOK // tm, D_FF // tf),
        in_specs=x_specs + [
            pl.BlockSpec((None, N_ADA, D_MODEL), lambda i, j: (_mod_row_index(i, tm), 0, 0)),
            pl.BlockSpec((1, D_MODEL), lambda i, j: (0, 0)),
            pl.BlockSpec((None, None, D_MODEL, tf), lambda i, j: (layer, half, 0, j)),
            pl.BlockSpec((None, None, D_MODEL, tf), lambda i, j: (layer, half, 0, j)),
            pl.BlockSpec((None, None, tf, D_MODEL), lambda i, j: (layer, half, j, 0)),
            pl.BlockSpec((1, D_MODEL), lambda i, j: (0, 0))],
        out_specs=out_specs, out_shape=out_shape, scratch_shapes=scratch,
        compiler_params=pltpu.CompilerParams(dimension_semantics=("parallel", "arbitrary"),
                                             vmem_limit_bytes=vmem_bytes),
        name="ffn",
    )(*xs, mod_l, g_row, ffn_wg, ffn_wu, ffn_wd, gf_row)


def _proj_kernel(*refs, rows, with_gates, w_transposed):
    if with_gates:
        x_ref, mod_ref, g_ref, w_ref, wgate_ref, o_ref, og_ref, h_scr = refs
    else:
        x_ref, mod_ref, g_ref, w_ref, o_ref, h_scr = refs
    j = pl.program_id(1)
    tm = x_ref.shape[0]
    dims = NT_DIMS if w_transposed else (((1,), (0,)), ((), ()))

    def matmul(h, w):
        return lax.dot_general(h, w, dims, preferred_element_type=f32)

    @pl.when(j == 0)
    def _():
        w = w_ref[...].astype(bf16)
        for r in range(tm // PROJ_ROW_BLOCK):
            rs = slice(r * PROJ_ROW_BLOCK, (r + 1) * PROJ_ROW_BLOCK)
            hb = _modulated(x_ref[rs, :], g_ref[...], mod_ref[rows[0]:rows[0] + 1, :],
                            mod_ref[rows[1]:rows[1] + 1, :]).astype(bf16)
            h_scr[rs, :] = hb
            o_ref[rs, :] = matmul(hb, w).astype(o_ref.dtype)
            if with_gates:
                og_ref[rs, :] = matmul(hb, wgate_ref[...].astype(bf16))

    @pl.when(j > 0)
    def _():
        o_ref[...] = matmul(h_scr[...], w_ref[...].astype(bf16)).astype(o_ref.dtype)


AB_I0 = 2 * H_A * DQK_A + 2 * H_A * DV_A
AB_D0 = AB_I0 + 4 * H_A
AB_B0 = AB_D0 + 2 * H_B * DK_B + 2 * H_B * DV_B
AB_IN = AB_B0 + 4 * H_B


def _ab_weight_kernel(w_ref, wm_ref, wg_ref):
    n_gate = 2 * H_A
    wm_ref[:AB_I0, :] = w_ref[:AB_I0, :]
    wm_ref[AB_I0:, :] = w_ref[AB_D0:AB_B0, :]
    wg_ref[:2 * n_gate, :] = w_ref[AB_I0:AB_D0, :]
    wg_ref[2 * n_gate:4 * n_gate, :] = w_ref[AB_B0:, :]
    wg_ref[4 * n_gate:, :] = jnp.zeros((LANES - 4 * n_gate, w_ref.shape[1]), f32)


def _ab_weight_call(ab_w_in, layer, tk=256):
    w_t = jnp.swapaxes(ab_w_in, 1, 2)
    return pl.pallas_call(
        _ab_weight_kernel,
        grid=(D_MODEL // tk,),
        in_specs=[pl.BlockSpec((None, AB_IN, tk), lambda i: (layer, 0, i))],
        out_specs=[pl.BlockSpec((AB_MAIN, tk), lambda i: (0, i)), pl.BlockSpec((LANES, tk), lambda i: (0, i))],
        out_shape=[jax.ShapeDtypeStruct((AB_MAIN, D_MODEL), f32), jax.ShapeDtypeStruct((LANES, D_MODEL), f32)],
        compiler_params=pltpu.CompilerParams(vmem_limit_bytes=VMEM_LIMIT),
        name="ab_weights",
    )(w_t)


def _column_tile(n, cap):
    return max(t for t in range(LANES, cap + 1, LANES) if n % t == 0)


def _proj_call(x, mod_l, g_row, w, w_gate, rows, out_dtype, w_transposed, tm=2048):
    n = w.shape[0] if w_transposed else w.shape[1]
    tn = _column_tile(n, 1024)
    with_gates = w_gate is not None
    w_spec = (pl.BlockSpec((tn, D_MODEL), lambda i, j: (j, 0)) if w_transposed
              else pl.BlockSpec((D_MODEL, tn), lambda i, j: (0, j)))
    in_specs = [pl.BlockSpec((tm, D_MODEL), lambda i, j: (i, 0)),
                pl.BlockSpec((None, N_ADA, D_MODEL), lambda i, j: (_mod_row_index(i, tm), 0, 0)),
                pl.BlockSpec((1, D_MODEL), lambda i, j: (0, 0)),
                w_spec]
    out_specs = [pl.BlockSpec((tm, tn), lambda i, j: (i, j))]
    out_shape = [jax.ShapeDtypeStruct((N_TOK, n), out_dtype)]
    args = [x, mod_l, g_row, w]
    if with_gates:
        in_specs.append(pl.BlockSpec(w_gate.shape, lambda i, j: (0, 0)))
        out_specs.append(pl.BlockSpec((tm, LANES), lambda i, j: (i, 0)))
        out_shape.append(jax.ShapeDtypeStruct((N_TOK, LANES), f32))
        args.append(w_gate)
    return pl.pallas_call(
        functools.partial(_proj_kernel, rows=rows, with_gates=with_gates, w_transposed=w_transposed),
        grid=(N_TOK // tm, n // tn),
        in_specs=in_specs, out_specs=out_specs, out_shape=out_shape,
        scratch_shapes=[pltpu.VMEM((tm, D_MODEL), bf16)],
        compiler_params=pltpu.CompilerParams(dimension_semantics=("parallel", "arbitrary"),
                                             vmem_limit_bytes=VMEM_LIMIT),
        name="in_proj",
    )(*args)


def _chunk_masks():
    r = lax.broadcasted_iota(jnp.int32, (CHUNK, CHUNK), 0)
    c = lax.broadcasted_iota(jnp.int32, (CHUNK, CHUNK), 1)
    return r >= c, r <= c, r > c, r < c


def _lane_col(x, lane, j):
    return jnp.sum(jnp.where(lane == j, x, 0.0), axis=1, keepdims=True)


def _head_lane_select(base, pair):
    r = lax.broadcasted_iota(jnp.int32, (SUBLANES, LANES), 0)
    ln = lax.broadcasted_iota(jnp.int32, (SUBLANES, LANES), 1)
    return jnp.where((ln == base + 2 * pair + r) & (r < 2), 1.0, 0.0)


def _split3_bf16(x):
    hi = x.astype(bf16)
    r = x - hi.astype(f32)
    mid = r.astype(bf16)
    return hi, mid, (r - mid.astype(f32)).astype(bf16)


def _thrice(a16):
    return jnp.concatenate([a16] * 3, axis=1)


def _mask_matmul_f32(mask16x3, x):
    return jnp.dot(mask16x3, jnp.concatenate(_split3_bf16(x), axis=0), preferred_element_type=f32)


def _select_rows_f32(sel16x3, x):
    return lax.dot_general(sel16x3, jnp.concatenate(_split3_bf16(x), axis=1), NT_DIMS, preferred_element_type=f32)


def _chunks_per_trip(nc):
    return min(MAX_CHUNKS_PER_TRIP, nc)


def _chunk_start(trip, u, d, nc):
    step = trip * _chunks_per_trip(nc) + u
    c = step if d == 0 else nc - 1 - step
    return pl.multiple_of(c * CHUNK, CHUNK)


def _mlstm_kernel(*refs, seq, zero_init):
    if zero_init:
        qp_ref, kp_ref, v0_ref, v1_ref, g_ref, par_ref, h_ref, cf_ref, nf_ref, mf_ref = refs
    else:
        (qp_ref, kp_ref, v0_ref, v1_ref, g_ref, par_ref, c0_ref, n0_ref, m0_ref,
         h_ref, cf_ref, nf_ref, mf_ref) = refs
    nc = seq // CHUNK
    pair = pl.program_id(1)
    lane = lax.broadcasted_iota(jnp.int32, (1, LANES), 1)
    tril, triu, _, _ = _chunk_masks()
    masks = (tril, triu)
    masks16 = (_thrice(tril.astype(bf16)), _thrice(triu.astype(bf16)))
    sel = [_thrice(_head_lane_select(LANE_F + d * H_A, pair).astype(bf16)) for d in range(2)]
    bi_row = par_ref[0:1, :]
    bf_row = par_ref[1:2, :]
    ones_col = jnp.where(lane == 0, 1.0, 0.0) + jnp.zeros((CHUNK, LANES), f32)
    v_refs = (v0_ref, v1_ref)
    streams = [(hh, d) for hh in range(2) for d in range(2)]

    h_ref[...] = jnp.zeros_like(h_ref)

    init = []
    for hh, d in streams:
        if zero_init:
            init.append((jnp.zeros((DQK_A, 2 * LANES), f32), jnp.zeros((1, 1), f32)))
        else:
            n_aug = jnp.where(lane == 0, n0_ref[d, hh], 0.0)
            init.append((jnp.concatenate([c0_ref[d, hh], n_aug], axis=1), m0_ref[d, hh]))

    def body(trip, carry):
        subs = range(_chunks_per_trip(nc))
        shared = {}
        for u in subs:
            for d in range(2):
                r0 = _chunk_start(trip, u, d, nc)
                gates = g_ref[pl.ds(r0, CHUNK), :]
                gi = gates + bi_row
                gf = _log_sigmoid(gates + bf_row)
                cum = _mask_matmul_f32(masks16[d], gf)
                stack = jnp.concatenate([cum, pltpu.roll(gi, LANE_F - LANE_I, axis=1)], axis=0)
                rows = _select_rows_f32(sel[d], stack)
                total = cum[CHUNK - 1:CHUNK, :] if d == 0 else cum[0:1, :]
                shared[u, d] = (r0, gi, cum, rows, total)
        chains = [(u, hh, d) for u in subs for hh, d in streams]
        st = {ch: {} for ch in chains}
        for ch in chains:
            u, hh, d = ch
            r0 = shared[u, d][0]
            s = st[ch]
            s["q"] = (qp_ref[pl.ds(r0, CHUNK), hh * DQK_A:(hh + 1) * DQK_A] * DQK_A ** -0.5).astype(bf16)
            s["k"] = kp_ref[pl.ds(r0, CHUNK), hh * DQK_A:(hh + 1) * DQK_A].astype(bf16)
            s["v_aug"] = jnp.concatenate([v_refs[hh][pl.ds(r0, CHUNK), :].astype(f32), ones_col], axis=1)
            s["qk"] = lax.dot_general(s["q"], s["k"], NT_DIMS, preferred_element_type=f32)
        for ch in chains:
            u, hh, d = ch
            _, gi, cum, rows, total = shared[u, d]
            s = st[ch]
            head = 2 * pair + hh
            jf = LANE_F + d * H_A + head
            ji = LANE_I + d * H_A + head
            bcol = _lane_col(cum, lane, jf)
            icol = _lane_col(gi, lane, ji)
            dmat = jnp.where(masks[d], bcol - rows[hh:hh + 1, :CHUNK] + rows[hh:hh + 1, CHUNK:], -jnp.inf)
            m_loc = jnp.max(dmat, axis=1, keepdims=True)
            s["sw"] = (s["qk"] * jnp.exp(dmat - m_loc)).astype(bf16)
            blast = _lane_col(total, lane, jf)
            gs = blast - bcol + icol
            ms_loc = jnp.max(gs, axis=0, keepdims=True)
            s["wv"] = (jnp.exp(gs - ms_loc) * s["v_aug"]).astype(bf16)
            s.update(bcol=bcol, m_loc=m_loc, blast=blast, ms_loc=ms_loc)
        for ch in chains:
            s = st[ch]
            s["num"] = jnp.dot(s["sw"], s["v_aug"].astype(bf16), preferred_element_type=f32)
            s["kv"] = lax.dot_general(s["k"], s["wv"], TN_DIMS, preferred_element_type=f32)
        state = list(carry)
        for u in subs:
            qc = [jnp.dot(st[u, hh, d]["q"], state[i][0].astype(bf16), preferred_element_type=f32)
                  for i, (hh, d) in enumerate(streams)]
            for i, (hh, d) in enumerate(streams):
                s = st[u, hh, d]
                c_aug, m = state[i]
                m_inter = s["bcol"] + m
                m_t = jnp.maximum(m_inter, s["m_loc"])
                nd = jnp.exp(s["m_loc"] - m_t) * s["num"] + jnp.exp(m_inter - m_t) * qc[i]
                den = nd[:, DV_A:DV_A + 1]
                hval = nd[:, :DV_A] / jnp.maximum(jnp.abs(den), jnp.exp(-m_t))
                r0 = shared[u, d][0]
                h_ref[pl.ds(r0, CHUNK), hh * DV_A:(hh + 1) * DV_A] += hval
                m_new = jnp.maximum(s["blast"] + m, s["ms_loc"])
                c_new = jnp.exp(s["blast"] + m - m_new) * c_aug + jnp.exp(s["ms_loc"] - m_new) * s["kv"]
                state[i] = (c_new, m_new)
        return tuple(state)

    final = lax.fori_loop(0, nc // _chunks_per_trip(nc), body, tuple(init))
    for i, (hh, d) in enumerate(streams):
        c_aug, m = final[i]
        cf_ref[d, hh] = c_aug[:, :DV_A]
        nf_ref[d, hh] = c_aug[:, DV_A:DV_A + 1]
        mf_ref[d, hh] = m


def _mlstm_call(proj, gates, par, nb, seq, off, state):
    zero_init = state is None
    blk = lambda col: pl.BlockSpec((seq, LANES), col)
    in_specs = [blk(lambda b, p: (off + b, p)),
                blk(lambda b, p: (off + b, 2 + p)),
                blk(lambda b, p: (off + b, 4 + 2 * p)),
                blk(lambda b, p: (off + b, 5 + 2 * p)),
                blk(lambda b, p: (off + b, 0)),
                pl.BlockSpec((SUBLANES, LANES), lambda b, p: (0, 0))]
    st_specs = [pl.BlockSpec((None, 2, 2, DQK_A, DV_A), lambda b, p: (b, 0, p, 0, 0)),
                pl.BlockSpec((None, 2, 2, DQK_A, 1), lambda b, p: (b, 0, p, 0, 0)),
                pl.BlockSpec((None, 2, 2, 1, 1), lambda b, p: (b, 0, p, 0, 0))]
    args = [proj, proj, proj, proj, gates, par]
    if not zero_init:
        in_specs += st_specs
        args += list(state)
    return pl.pallas_call(
        functools.partial(_mlstm_kernel, seq=seq, zero_init=zero_init),
        grid=(nb, H_A // 2),
        in_specs=in_specs,
        out_specs=[pl.BlockSpec((seq, 2 * DV_A), lambda b, p: (b, p))] + st_specs,
        out_shape=[jax.ShapeDtypeStruct((nb * seq, H_A * DV_A), f32),
                   jax.ShapeDtypeStruct((nb, 2, H_A, DQK_A, DV_A), f32),
                   jax.ShapeDtypeStruct((nb, 2, H_A, DQK_A, 1), f32),
                   jax.ShapeDtypeStruct((nb, 2, H_A, 1, 1), f32)],
        compiler_params=pltpu.CompilerParams(vmem_limit_bytes=VMEM_LIMIT),
        name="mlstm",
    )(*args)


CONV_PAD = SUBLANES


def _short_conv_silu(x, w_ref, pad_scr, seq):
    pad_scr[:CONV_PAD, :] = jnp.zeros((CONV_PAD, LANES), f32)
    pad_scr[CONV_PAD + seq:, :] = jnp.zeros((CONV_PAD, LANES), f32)
    pad_scr[CONV_PAD:CONV_PAD + seq, :] = x
    acc = x * w_ref[CONV_K // 2:CONV_K // 2 + 1, :]
    for tap in range(CONV_K):
        delta = tap - CONV_K // 2
        if delta != 0:
            acc = acc + pad_scr[CONV_PAD + delta:CONV_PAD + delta + seq, :] * w_ref[tap:tap + 1, :]
    return _silu(acc)


def _l2_unit(y):
    return y * lax.rsqrt(jnp.sum(y * y, axis=-1, keepdims=True) + EPS)


def _split_bf16(x):
    hi = x.astype(bf16)
    return hi, (x - hi.astype(f32)).astype(bf16)


def _matmul_3pass(m, x):
    mh, ml = _split_bf16(m)
    xh, xl = _split_bf16(x)
    return jnp.dot(jnp.concatenate([mh, ml, mh], axis=1), jnp.concatenate([xh, xh, xl], axis=0),
                   preferred_element_type=f32)


def _unit_triangular_solves(ns, xs):
    levels = CHUNK.bit_length() - 1
    r = lax.broadcasted_iota(jnp.int32, (CHUNK, CHUNK), 0)
    c = lax.broadcasted_iota(jnp.int32, (CHUNK, CHUNK), 1)
    eye = jnp.where(r == c, 1.0, 0.0)
    ts = [eye + n for n in ns]
    ms = [_matmul_3pass(n, n) for n in ns]
    for lvl in range(1, levels):
        for i in range(len(ns)):
            if lvl < levels - 1:
                prod = _matmul_3pass(ms[i], jnp.concatenate([ts[i], ms[i]], axis=1))
                ts[i] = ts[i] + prod[:, :CHUNK]
                ms[i] = prod[:, CHUNK:]
            else:
                ts[i] = ts[i] + _matmul_3pass(ms[i], ts[i])
    return [_matmul_3pass(t, x) for t, x in zip(ts, xs)]


def _delta_kernel(*refs, seq, zero_init):
    if zero_init:
        (q0_ref, q1_ref, k0_ref, k1_ref, v0_ref, v1_ref, wq0_ref, wq1_ref, wk0_ref, wk1_ref, wv0_ref, wv1_ref,
         g_ref, par_ref, o_ref, sf_ref, q_scr, k_scr, v_scr, pad_scr) = refs
    else:
        (q0_ref, q1_ref, k0_ref, k1_ref, v0_ref, v1_ref, wq0_ref, wq1_ref, wk0_ref, wk1_ref, wv0_ref, wv1_ref,
         g_ref, par_ref, s0_ref, o_ref, sf_ref, q_scr, k_scr, v_scr, pad_scr) = refs
    nc = seq // CHUNK
    pair = pl.program_id(1)
    lane = lax.broadcasted_iota(jnp.int32, (1, LANES), 1)
    tril, triu, stril, striu = _chunk_masks()
    masks, smasks = (tril, triu), (stril, striu)
    masks16 = (_thrice(tril.astype(bf16)), _thrice(triu.astype(bf16)))
    sel = [_thrice(_head_lane_select(LANE_A + d * H_B, pair).astype(bf16)) for d in range(2)]
    neg_a_row = -jnp.exp(par_ref[0:1, :])
    dt_row = par_ref[1:2, :]
    streams = [(hh, d) for hh in range(2) for d in range(2)]

    for hh, (q_ref, k_ref, v_ref, wq_ref, wk_ref, wv_ref) in enumerate(
            ((q0_ref, k0_ref, v0_ref, wq0_ref, wk0_ref, wv0_ref), (q1_ref, k1_ref, v1_ref, wq1_ref, wk1_ref, wv1_ref))):
        q_scr[hh] = _l2_unit(_short_conv_silu(q_ref[...].astype(f32), wq_ref, pad_scr, seq)) * DK_B ** -0.5
        k_scr[hh] = _l2_unit(_short_conv_silu(k_ref[...].astype(f32), wk_ref, pad_scr, seq))
        v_scr[hh] = _short_conv_silu(v_ref[...].astype(f32), wv_ref, pad_scr, seq)

    o_ref[...] = jnp.zeros_like(o_ref)
    init = tuple(jnp.zeros((DK_B, DV_B), f32) if zero_init else s0_ref[d, hh] for hh, d in streams)

    def body(trip, carry):
        subs = range(_chunks_per_trip(nc))
        shared = {}
        for u in subs:
            for d in range(2):
                r0 = _chunk_start(trip, u, d, nc)
                gates = g_ref[pl.ds(r0, CHUNK), :]
                beta_all = jax.nn.sigmoid(gates)
                glog = neg_a_row * _softplus(gates + dt_row)
                cum = _mask_matmul_f32(masks16[d], glog)
                rows = _select_rows_f32(sel[d], cum)
                shared[u, d] = (r0, beta_all, cum, rows)
        chains = [(u, hh, d) for u in subs for hh, d in streams]
        st = {ch: {} for ch in chains}
        for ch in chains:
            u, hh, d = ch
            r0, beta_all, cum, rows = shared[u, d]
            s = st[ch]
            head = 2 * pair + hh
            beta = _lane_col(beta_all, lane, LANE_BETA + d * H_B + head)
            gcol = _lane_col(cum, lane, LANE_A + d * H_B + head)
            glast = gcol[CHUNK - 1:CHUNK, :] if d == 0 else gcol[0:1, :]
            q = q_scr[hh, pl.ds(r0, CHUNK), :]
            k = k_scr[hh, pl.ds(r0, CHUNK), :]
            v = v_scr[hh, pl.ds(r0, CHUNK), :]
            k16 = k.astype(bf16)
            kbeta = k * beta
            eg = jnp.exp(gcol)
            decay = jnp.exp(jnp.where(masks[d], gcol - rows[hh:hh + 1, :], -jnp.inf))
            kk = lax.dot_general(kbeta.astype(bf16), k16, NT_DIMS, preferred_element_type=f32)
            qk = lax.dot_general(q.astype(bf16), k16, NT_DIMS, preferred_element_type=f32)
            s["n"] = -jnp.where(smasks[d], kk * decay, 0.0)
            s["x"] = jnp.concatenate([v * beta, kbeta * eg], axis=1)
            s["qk"] = (qk * decay).astype(bf16)
            s["qg"] = (q * eg).astype(bf16)
            s["kd_t"] = (k * jnp.exp(glast - gcol)).T.astype(bf16)
            s["gl"] = jnp.exp(glast)
        solved = _unit_triangular_solves([st[ch]["n"] for ch in chains], [st[ch]["x"] for ch in chains])
        for ch, uw in zip(chains, solved):
            s = st[ch]
            s["u"] = uw[:, :DV_B]
            s["w_qg"] = jnp.concatenate([uw[:, DV_B:].astype(bf16), s["qg"]], axis=0)
        state = list(carry)
        for u in subs:
            ws = [jnp.dot(st[u, hh, d]["w_qg"], state[i].astype(bf16), preferred_element_type=f32)
                  for i, (hh, d) in enumerate(streams)]
            v_new = [(st[u, hh, d]["u"] - ws[i][:CHUNK]).astype(bf16) for i, (hh, d) in enumerate(streams)]
            for i, (hh, d) in enumerate(streams):
                s = st[u, hh, d]
                o = ws[i][CHUNK:] + jnp.dot(s["qk"], v_new[i], preferred_element_type=f32)
                r0 = shared[u, d][0]
                o_ref[pl.ds(r0, CHUNK), hh * DV_B:(hh + 1) * DV_B] += o
                state[i] = state[i] * s["gl"] + jnp.dot(s["kd_t"], v_new[i], preferred_element_type=f32)
        return tuple(state)

    final = lax.fori_loop(0, nc // _chunks_per_trip(nc), body, init)
    for i, (hh, d) in enumerate(streams):
        sf_ref[d, hh] = final[i]


def _delta_call(proj, conv_w8, gates, par, nb, seq, off, state):
    zero_init = state is None
    col0 = (2 * H_A * DQK_A + 2 * H_A * DV_A) // LANES
    blk = lambda col: pl.BlockSpec((seq, LANES), col)
    wblk = lambda col: pl.BlockSpec((SUBLANES, LANES), col)
    in_specs = [blk(lambda b, p: (off + b, col0 + 2 * p)), blk(lambda b, p: (off + b, col0 + 2 * p + 1)),
                blk(lambda b, p: (off + b, col0 + H_B + 2 * p)), blk(lambda b, p: (off + b, col0 + H_B + 2 * p + 1)),
                blk(lambda b, p: (off + b, col0 + 2 * H_B + 2 * p)), blk(lambda b, p: (off + b, col0 + 2 * H_B + 2 * p + 1)),
                wblk(lambda b, p: (0, 2 * p)), wblk(lambda b, p: (0, 2 * p + 1)),
                wblk(lambda b, p: (0, H_B + 2 * p)), wblk(lambda b, p: (0, H_B + 2 * p + 1)),
                wblk(lambda b, p: (0, 2 * H_B + 2 * p)), wblk(lambda b, p: (0, 2 * H_B + 2 * p + 1)),
                blk(lambda b, p: (off + b, 0)),
                pl.BlockSpec((SUBLANES, LANES), lambda b, p: (0, 0))]
    st_spec = pl.BlockSpec((None, 2, 2, DK_B, DV_B), lambda b, p: (b, 0, p, 0, 0))
    args = [proj] * 6 + [conv_w8] * 6 + [gates, par]
    if not zero_init:
        in_specs.append(st_spec)
        args.append(state)
    return pl.pallas_call(
        functools.partial(_delta_kernel, seq=seq, zero_init=zero_init),
        grid=(nb, H_B // 2),
        in_specs=in_specs,
        out_specs=[pl.BlockSpec((seq, 2 * DV_B), lambda b, p: (b, p)), st_spec],
        out_shape=[jax.ShapeDtypeStruct((nb * seq, H_B * DV_B), f32),
                   jax.ShapeDtypeStruct((nb, 2, H_B, DK_B, DV_B), f32)],
        scratch_shapes=[pltpu.VMEM((2, seq, LANES), f32)] * 3 + [pltpu.VMEM((seq + 2 * CONV_PAD, LANES), f32)],
        compiler_params=pltpu.CompilerParams(vmem_limit_bytes=VMEM_LIMIT),
        name="delta",
    )(*args)


def _head_rms(x):
    return x * lax.rsqrt(jnp.mean(x * x, axis=-1, keepdims=True) + EPS)


def _ab_out_kernel(x_ref, hp_ref, hs_ref, op_ref, os_ref, om_ref, zd_ref, gm_ref, gd_ref, w_ref, mod_ref, o_ref,
                   *, gate_row, tm):
    hsum = _prompt_or_sample(hp_ref, hs_ref, tm)
    osum = _prompt_or_sample(op_ref, os_ref, tm)
    parts = []
    for h in range(H_A):
        sl = slice(h * DV_A, (h + 1) * DV_A)
        parts.append(_head_rms(hsum[:, sl]) * gm_ref[:, sl] * jax.nn.sigmoid(om_ref[:, sl].astype(f32)))
    for h in range(H_B):
        sl = slice(h * DV_B, (h + 1) * DV_B)
        parts.append(_head_rms(osum[:, sl]) * gd_ref[:, sl] * _silu(zd_ref[:, sl].astype(f32)))
    cat = jnp.concatenate(parts, axis=1).astype(bf16)
    y = jnp.dot(cat, w_ref[...].astype(bf16), preferred_element_type=f32)
    o_ref[...] = x_ref[...] + mod_ref[gate_row:gate_row + 1, :] * y


def _ab_out_call(x, hp, hs, op, os_, proj, gm_row, gd_row, w_out, mod_l, gate_row, tm=1024):
    wide = H_A * DV_A
    om_blk = (2 * H_A * DQK_A) // wide + 1
    zd_blk = AB_MAIN // wide - 1
    return pl.pallas_call(
        functools.partial(_ab_out_kernel, gate_row=gate_row, tm=tm),
        grid=(N_TOK // tm,),
        in_specs=[pl.BlockSpec((tm, D_MODEL), lambda i: (i, 0))] + _split_specs(tm, wide) + _split_specs(tm, wide) + [
            pl.BlockSpec((tm, wide), lambda i: (i, om_blk)),
            pl.BlockSpec((tm, wide), lambda i: (i, zd_blk)),
            pl.BlockSpec((1, wide), lambda i: (0, 0)),
            pl.BlockSpec((1, wide), lambda i: (0, 0)),
            pl.BlockSpec((2 * wide, D_MODEL), lambda i: (0, 0)),
            pl.BlockSpec((None, N_ADA, D_MODEL), lambda i: (_mod_row_index(i, tm), 0, 0))],
        out_specs=pl.BlockSpec((tm, D_MODEL), lambda i: (i, 0)),
        out_shape=jax.ShapeDtypeStruct((N_TOK, D_MODEL), f32),
        compiler_params=pltpu.CompilerParams(vmem_limit_bytes=VMEM_LIMIT),
        name="ab_out",
    )(x, hp, hs, op, os_, proj, proj, gm_row, gd_row, w_out, mod_l)


def _na_out_kernel(x_ref, ap_ref, as_ref, w_ref, mod_ref, o_ref, *, gate_row, tm):
    a = _prompt_or_sample(ap_ref, as_ref, tm).astype(bf16)
    y = jnp.dot(a, w_ref[...].astype(bf16), preferred_element_type=f32)
    o_ref[...] = x_ref[...] + mod_ref[gate_row:gate_row + 1, :] * y


def _na_out_call(x, attn_p, attn_s, w_out, mod_l, gate_row, tm=1024):
    return pl.pallas_call(
        functools.partial(_na_out_kernel, gate_row=gate_row, tm=tm),
        grid=(N_TOK // tm,),
        in_specs=[pl.BlockSpec((tm, D_MODEL), lambda i: (i, 0))] + _split_specs(tm, NA_W) + [
            pl.BlockSpec((NA_W, D_MODEL), lambda i: (0, 0)),
            pl.BlockSpec((None, N_ADA, D_MODEL), lambda i: (_mod_row_index(i, tm), 0, 0))],
        out_specs=pl.BlockSpec((tm, D_MODEL), lambda i: (i, 0)),
        out_shape=jax.ShapeDtypeStruct((N_TOK, D_MODEL), f32),
        compiler_params=pltpu.CompilerParams(vmem_limit_bytes=VMEM_LIMIT),
        name="na_out",
    )(x, attn_p, attn_s, w_out, mod_l)


CTX_HEADS = 8


def _ctx_attn_kernel(q_ref, k_ref, v_ref, o_ref, nk_ref, nv_ref):
    low = lax.broadcasted_iota(jnp.int32, (1, LANES), 1) < DH_C
    def pair_scores(pair):
        sl = slice(pair * LANES, (pair + 1) * LANES)
        q = q_ref[:, sl] * DH_C ** -0.5
        k_t = k_ref[:, sl].T
        v_t = v_ref[:, sl].T
        for hh in range(2):
            nk_ref[2 * pair + hh] = k_t[hh * DH_C:(hh + 1) * DH_C, :]
            nv_ref[2 * pair + hh] = v_t[hh * DH_C:(hh + 1) * DH_C, :]
        k = k_ref[:, sl].astype(bf16)
        scores = [lax.dot_general(jnp.where(low if hh == 0 else jnp.logical_not(low), q, 0.0).astype(bf16), k,
                                  NT_DIMS, preferred_element_type=f32) for hh in range(2)]
        return sl, scores

    def pair_finish(sl, scores):
        v = v_ref[:, sl].astype(bf16)
        probs = [jnp.exp(s - jnp.max(s, axis=1, keepdims=True)) for s in scores]
        outs = [jnp.dot(p.astype(bf16), v, preferred_element_type=f32) / jnp.sum(p, axis=1, keepdims=True)
                for p in probs]
        o_ref[:, sl] = jnp.where(low, outs[0], outs[1]).astype(o_ref.dtype)

    n_pairs = CTX_HEADS // 2
    pending = pair_scores(0)
    for pair in range(n_pairs):
        upcoming = pair_scores(pair + 1) if pair + 1 < n_pairs else None
        pair_finish(*pending)
        pending = upcoming


def _ctx_attn_call(proj):
    w = CTX_HEADS * DH_C
    nblk = NA_W // w
    return pl.pallas_call(
        _ctx_attn_kernel,
        grid=(BATCH, nblk),
        in_specs=[pl.BlockSpec((SEQ, w), lambda b, j: (b, j)),
                  pl.BlockSpec((SEQ, w), lambda b, j: (b, nblk + j)),
                  pl.BlockSpec((SEQ, w), lambda b, j: (b, 2 * nblk + j))],
        out_specs=[pl.BlockSpec((SEQ, w), lambda b, j: (b, j)),
                   pl.BlockSpec((None, CTX_HEADS, DH_C, SEQ), lambda b, j: (b, j, 0, 0)),
                   pl.BlockSpec((None, CTX_HEADS, DH_C, SEQ), lambda b, j: (b, j, 0, 0))],
        out_shape=[jax.ShapeDtypeStruct((N_PROMPT, NA_W), bf16),
                   jax.ShapeDtypeStruct((BATCH, H_C, DH_C, SEQ), f32),
                   jax.ShapeDtypeStruct((BATCH, H_C, DH_C, SEQ), f32)],
        compiler_params=pltpu.CompilerParams(vmem_limit_bytes=VMEM_LIMIT),
        name="ctx_attn",
    )(proj, proj, proj)


QROWS = 4
QBLK = QROWS * GRID_W
KROWS_MID = QROWS + WIN_R - 1
N_RIDX = 2 * WIN_R - 1
N_CIDX = 2 * WIN_C - 1
N_QBLK = GRID_ROWS // QROWS


def _nattn_bias_tables(rb_ref, tb_scr, bmid_scr, btop_scr, bbot_scr):
    qc = lax.broadcasted_iota(jnp.int32, (GRID_W, GRID_W), 0)
    kc = lax.broadcasted_iota(jnp.int32, (GRID_W, GRID_W), 1)
    cs = jnp.clip(qc - WIN_C // 2, 0, GRID_W - WIN_C)
    valid = (kc >= cs) & (kc < cs + WIN_C)
    neg = jnp.full((GRID_W, GRID_W), -jnp.inf, f32)
    for hh in range(2):
        for ri in range(N_RIDX):
            row = jnp.broadcast_to(rb_ref[hh, ri:ri + 1, :], (GRID_W, LANES))
            tile = pltpu.roll(row, LANES - (WIN_C - 1), 1, stride=1, stride_axis=0)[:, :GRID_W]
            tb_scr[ri] = jnp.where(valid, tile, -jnp.inf)
        for i in range(QROWS):
            rs = slice(i * GRID_W, (i + 1) * GRID_W)
            for jj in range(KROWS_MID):
                inside = 0 <= jj - i < WIN_R
                bmid_scr[hh, rs, jj * GRID_W:(jj + 1) * GRID_W] = tb_scr[jj - i + WIN_R // 2 - 1] if inside else neg
            for jj in range(WIN_R):
                btop_scr[hh, rs, jj * GRID_W:(jj + 1) * GRID_W] = tb_scr[jj - i + WIN_R - 1]
                bbot_scr[hh, rs, jj * GRID_W:(jj + 1) * GRID_W] = tb_scr[jj - i + WIN_R // 2 - 1]


def _nattn_kernel(rb_ref, q_ref, k_ref, v_ref, ck_ref, cv_ref, o_ref, tb_scr, bmid_scr, btop_scr, bbot_scr):
    scale = DH_C ** -0.5

    @pl.when(pl.program_id(1) == 0)
    def _():
        _nattn_bias_tables(rb_ref, tb_scr, bmid_scr, btop_scr, bbot_scr)

    low = lax.broadcasted_iota(jnp.int32, (1, LANES), 1) < DH_C
    kctx_t = jnp.concatenate([ck_ref[0], ck_ref[1]], axis=0).astype(bf16)
    vctx = jnp.concatenate([cv_ref[0], cv_ref[1]], axis=0).T.astype(bf16)

    def block_scores(q_start, k_start, k_rows, bias_scr):
        nk = k_rows * GRID_W
        q = q_ref[pl.ds(q_start, QBLK), :] * scale
        ku = k_ref[pl.ds(k_start, nk), :].astype(bf16)
        vu = v_ref[pl.ds(k_start, nk), :].astype(bf16)
        scores = []
        for hh in range(2):
            qh = jnp.where(low if hh == 0 else jnp.logical_not(low), q, 0.0).astype(bf16)
            scores.append((lax.dot_general(qh, ku, NT_DIMS, preferred_element_type=f32) + bias_scr[hh],
                           jnp.dot(qh, kctx_t, preferred_element_type=f32)))
        return q_start, vu, scores

    def block_finish(q_start, vu, scores):
        probs = []
        for s_loc, s_ctx in scores:
            m = jnp.maximum(jnp.max(s_loc, axis=1, keepdims=True), jnp.max(s_ctx, axis=1, keepdims=True))
            p_loc = jnp.exp(s_loc - m)
            p_ctx = jnp.exp(s_ctx - m)
            denom = jnp.sum(p_loc, axis=1, keepdims=True) + jnp.sum(p_ctx, axis=1, keepdims=True)
            probs.append((p_loc.astype(bf16), p_ctx.astype(bf16), denom))
        outs = [(jnp.dot(p_loc, vu, preferred_element_type=f32) + jnp.dot(p_ctx, vctx, preferred_element_type=f32)) / denom
                for p_loc, p_ctx, denom in probs]
        o_ref[pl.ds(q_start, QBLK), :] = jnp.where(low, outs[0], outs[1]).astype(o_ref.dtype)

    blocks = ([(0, 0, WIN_R, btop_scr)]
              + [(b * QBLK, b * QBLK - (WIN_R // 2) * GRID_W, KROWS_MID, bmid_scr) for b in range(1, N_QBLK - 1)]
              + [((N_QBLK - 1) * QBLK, (GRID_ROWS - WIN_R) * GRID_W, WIN_R, bbot_scr)])
    pending = block_scores(*blocks[0])
    for nxt in blocks[1:] + [None]:
        upcoming = block_scores(*nxt) if nxt is not None else None
        block_finish(*pending)
        pending = upcoming


def _nattn_call(proj, cache_k, cache_v, layer, rel_bias, off):
    npair = H_C // 2
    return pl.pallas_call(
        _nattn_kernel,
        grid=(npair, DEC_BATCH),
        in_specs=[pl.BlockSpec((2, 2 * WIN_R, LANES), lambda p, b: (p, 0, 0)),
                  pl.BlockSpec((DEC_SEQ, LANES), lambda p, b: (off + b, p)),
                  pl.BlockSpec((DEC_SEQ, LANES), lambda p, b: (off + b, npair + p)),
                  pl.BlockSpec((DEC_SEQ, LANES), lambda p, b: (off + b, 2 * npair + p)),
                  pl.BlockSpec((None, None, 2, DH_C, PAST_LEN), lambda p, b: (b, layer, p, 0, 0)),
                  pl.BlockSpec((None, None, 2, DH_C, PAST_LEN), lambda p, b: (b, layer, p, 0, 0))],
        out_specs=pl.BlockSpec((DEC_SEQ, LANES), lambda p, b: (b, p)),
        out_shape=jax.ShapeDtypeStruct((N_SAMPLE, NA_W), bf16),
        scratch_shapes=[pltpu.VMEM((N_RIDX, GRID_W, GRID_W), f32),
                        pltpu.VMEM((2, QBLK, KROWS_MID * GRID_W), f32),
                        pltpu.VMEM((2, QBLK, WIN_R * GRID_W), f32),
                        pltpu.VMEM((2, QBLK, WIN_R * GRID_W), f32)],
        compiler_params=pltpu.CompilerParams(dimension_semantics=("arbitrary", "arbitrary"),
                                             vmem_limit_bytes=VMEM_LIMIT),
        name="nattn",
    )(jnp.pad(rel_bias, ((0, 0), (0, 2 * WIN_R - N_RIDX), (0, LANES - N_CIDX))), proj, proj, proj, cache_k, cache_v)


def _lane_row(pieces):
    row = jnp.zeros((LANES,), f32)
    for off, vals in pieces:
        row = row.at[off:off + vals.shape[0]].set(vals.astype(f32))
    return row


def _param_rows(rows):
    out = jnp.zeros((SUBLANES, LANES), f32)
    for r, row in enumerate(rows):
        out = out.at[r].set(row)
    return out


def kernel(x_prompt, x_sample, c, state_mlstm_C, state_mlstm_n, state_mlstm_m, state_delta_S, cache_na_k, cache_na_v, c_ctx, ada_w, ada_b, norm_g, ffn_wg, ffn_wu, ffn_wd, ab_w_in, ab_w_out, mlstm_b_i, mlstm_b_f, mlstm_norm_g, delta_conv_w, delta_a_log, delta_dt_bias, delta_norm_g, na_w_in, na_w_out, na_rel_bias, final_norm_g):
    xs = (x_prompt.reshape(N_PROMPT, D_MODEL), x_sample.reshape(N_SAMPLE, D_MODEL))
    mods = _ada_call(jnp.concatenate([c_ctx[None, :], c], axis=0), ada_w, ada_b)
    mods = mods.reshape(DEPTH, 3, N_ADA, D_MODEL)
    gf_row = final_norm_g.reshape(1, D_MODEL)
    s_off = N_PROMPT // DEC_SEQ
    new_c, new_n, new_m, new_s, new_k, new_v = [], [], [], [], [], []

    for l in range(DEPTH):
        mod_l = mods[l]
        a = l // 2
        x = _ffn_call(xs if l == 0 else (x,), mod_l, norm_g[l, 0].reshape(1, D_MODEL), ffn_wg, ffn_wu, ffn_wd, l, 0,
                      gf_row, rows=(0, 1, 2), final=False, split_out=False, tm=FFN_TM)[0]
        g_mix = norm_g[l, 1].reshape(1, D_MODEL)
        if l % 2 == 0:
            w_main, w_gate = _ab_weight_call(ab_w_in, a)
            proj, gates = _proj_call(x, mod_l, g_mix, w_main, w_gate, rows=(3, 4), out_dtype=bf16, w_transposed=True)

            par_m = _param_rows([_lane_row([(LANE_I, mlstm_b_i[a].reshape(-1))]),
                                 _lane_row([(LANE_F, mlstm_b_f[a].reshape(-1))])])
            par_d = _param_rows([_lane_row([(LANE_A, delta_a_log[a].reshape(-1))]),
                                 _lane_row([(LANE_A, delta_dt_bias[a].reshape(-1))])])
            conv_w8 = jnp.concatenate([delta_conv_w[a], jnp.zeros((SUBLANES - CONV_K, 3 * H_B * DK_B), f32)], axis=0)

            hp, cn, nn_, mn = _mlstm_call(proj, gates, par_m, BATCH, SEQ, 0, None)
            st = (state_mlstm_C[:, a], state_mlstm_n[:, a][..., None], state_mlstm_m[:, a][..., None, None])
            hs, _, _, _ = _mlstm_call(proj, gates, par_m, DEC_BATCH, DEC_SEQ, s_off, st)
            op, sn = _delta_call(proj, conv_w8, gates, par_d, BATCH, SEQ, 0, None)
            os_, _ = _delta_call(proj, conv_w8, gates, par_d, DEC_BATCH, DEC_SEQ, s_off, state_delta_S[:, a])
            new_c.append(cn)
            new_n.append(nn_[..., 0])
            new_m.append(mn[..., 0, 0])
            new_s.append(sn)
            x = _ab_out_call(x, hp, hs, op, os_, proj, mlstm_norm_g[a].reshape(1, -1),
                             jnp.tile(delta_norm_g[a], H_B).reshape(1, -1), ab_w_out[a], mod_l, gate_row=5)
        else:
            proj = _proj_call(x, mod_l, g_mix, na_w_in[a], None, rows=(3, 4), out_dtype=f32, w_transposed=False)[0]
            attn_p, kp_t, vp_t = _ctx_attn_call(proj)
            attn_s = _nattn_call(proj, jnp.swapaxes(cache_na_k, -1, -2), jnp.swapaxes(cache_na_v, -1, -2), a,
                                 na_rel_bias[a], s_off)
            new_k.append(jnp.swapaxes(kp_t, -1, -2))
            new_v.append(jnp.swapaxes(vp_t, -1, -2))
            x = _na_out_call(x, attn_p, attn_s, na_w_out[a], mod_l, gate_row=5)
        last = l == DEPTH - 1
        outs = _ffn_call((x,), mod_l, norm_g[l, 2].reshape(1, D_MODEL), ffn_wg, ffn_wu, ffn_wd, l, 1,
                         gf_row, rows=(6, 7, 8), final=last, split_out=last, tm=FFN_TM)
        x = outs[0]

    y_prompt = outs[0].reshape(BATCH, SEQ, D_MODEL)
    y_sample = outs[1].reshape(DEC_BATCH, DEC_SEQ, D_MODEL)
    return (y_prompt, y_sample, jnp.stack(new_c, axis=1), jnp.stack(new_n, axis=1), jnp.stack(new_m, axis=1),
            jnp.stack(new_s, axis=1), jnp.stack(new_k, axis=1), jnp.stack(new_v, axis=1))
```

```python
import functools

import jax
import jax.numpy as jnp
from jax import lax
from jax.experimental import pallas as pl
from jax.experimental.pallas import tpu as pltpu

f32 = jnp.float32
bf16 = jnp.bfloat16

D_MODEL = 1024
BATCH = 16
SEQ = 256
DEPTH = 2
DEC_BATCH = 2
DEC_SEQ = 2048
PAST_LEN = 256
GRID_W = 64
GRID_ROWS = DEC_SEQ // GRID_W
D_FF = 2816
N_ADA = 9
EPS = 1e-6
CHUNK = 64
H_A, DQK_A, DV_A = 4, 64, 128
H_B, DK_B, DV_B = 4, 128, 128
CONV_K = 5
H_C, DH_C = 16, 64
WIN_R, WIN_C = 8, 16
NA_W = H_C * DH_C
assert DH_C ** -0.5 == 2.0 ** -3 and DQK_A ** -0.5 == 2.0 ** -3

N_PROMPT = BATCH * SEQ
N_SAMPLE = DEC_BATCH * DEC_SEQ
N_TOK = N_PROMPT + N_SAMPLE
AB_MAIN = 2 * H_A * DQK_A + 2 * H_A * DV_A + 2 * H_B * DK_B + 2 * H_B * DV_B
LANES = 128
VMEM_LIMIT = 56 * 1024 * 1024
FFN_VMEM_SLACK = 4 * 1024 * 1024
SUBLANES = 8

LANE_I, LANE_F, LANE_BETA, LANE_A = 0, 8, 16, 24

NT_DIMS = (((1,), (1,)), ((), ()))
TN_DIMS = (((0,), (0,)), ((), ()))

MAX_CHUNKS_PER_TRIP = 4
FFN_TM = 2048
FFN_ROW_BLOCK = 512
PROJ_ROW_BLOCK = 256


def _softplus(x):
    return jnp.maximum(x, 0.0) + jnp.log1p(jnp.exp(-jnp.abs(x)))


def _log_sigmoid(x):
    return -_softplus(-x)


def _silu(x):
    return x * jax.nn.sigmoid(x)


def _mod_row_index(i, tm):
    n_p = N_PROMPT // tm
    per_b = DEC_SEQ // tm
    return jnp.where(i < n_p, 0, 1 + (i - n_p) // per_b)


def _modulated(x, g_row, shift_row, scale_row):
    y = x * lax.rsqrt(jnp.mean(x * x, axis=-1, keepdims=True) + EPS) * g_row
    return y * (1.0 + scale_row) + shift_row


def _ada_kernel(cb_ref, w_ref, b_ref, o_ref, s_scr):
    tn = w_ref.shape[-1]

    @pl.when((pl.program_id(0) == 0) & (pl.program_id(1) == 0))
    def _():
        cb = cb_ref[...]
        s_scr[...] = cb * jax.nn.sigmoid(cb)

    for j in range(tn // LANES):
        cols = slice(j * LANES, (j + 1) * LANES)
        w = w_ref[:, cols]
        for r in range(3):
            o_ref[r:r + 1, cols] = jnp.sum(w * s_scr[r], axis=0, keepdims=True) + b_ref[:, cols]


def _ada_call(cond3, ada_w, ada_b):
    tn = N_ADA * D_MODEL // 4
    cb = jnp.broadcast_to(cond3[:, :, None], (3, D_MODEL, LANES))
    return pl.pallas_call(
        _ada_kernel,
        grid=(DEPTH, N_ADA * D_MODEL // tn),
        in_specs=[pl.BlockSpec((3, D_MODEL, LANES), lambda l, j: (0, 0, 0)),
                  pl.BlockSpec((None, D_MODEL, tn), lambda l, j: (l, 0, j)),
                  pl.BlockSpec((None, 1, tn), lambda l, j: (l, 0, j))],
        out_specs=pl.BlockSpec((None, 3, tn), lambda l, j: (l, 0, j)),
        out_shape=jax.ShapeDtypeStruct((DEPTH, 3, N_ADA * D_MODEL), f32),
        scratch_shapes=[pltpu.VMEM((3, D_MODEL, LANES), f32)],
        compiler_params=pltpu.CompilerParams(dimension_semantics=("arbitrary", "arbitrary"),
                                             vmem_limit_bytes=VMEM_LIMIT),
        name="ada_mod",
    )(cb, ada_w, ada_b.reshape(DEPTH, 1, N_ADA * D_MODEL))


def _prompt_or_sample(p_ref, s_ref, tm):
    return jnp.where(pl.program_id(0) < N_PROMPT // tm, p_ref[...], s_ref[...])


def _split_specs(tm, width, n_grid_axes=1):
    n_p = N_PROMPT // tm
    if n_grid_axes == 1:
        return [pl.BlockSpec((tm, width), lambda i: (jnp.minimum(i, n_p - 1), 0)),
                pl.BlockSpec((tm, width), lambda i: (jnp.maximum(i - n_p, 0), 0))]
    return [pl.BlockSpec((tm, width), lambda i, j: (jnp.minimum(i, n_p - 1), 0)),
            pl.BlockSpec((tm, width), lambda i, j: (jnp.maximum(i - n_p, 0), 0))]


def _ffn_kernel(*refs, rows, final, split_in, split_out, tm):
    it = iter(refs)
    x_refs = [next(it) for _ in range(2 if split_in else 1)]
    mod_ref, g_ref, wg_ref, wu_ref, wd_ref, gf_ref = [next(it) for _ in range(6)]
    o_refs = [next(it) for _ in range(2 if split_out else 1)]
    h_scr = next(it)
    j = pl.program_id(1)
    last_j = pl.num_programs(1) - 1
    is_prompt = pl.program_id(0) < N_PROMPT // tm

    def x_rows(rs):
        if split_in:
            return jnp.where(is_prompt, x_refs[0][rs, :], x_refs[1][rs, :])
        return x_refs[0][rs, :]

    def step(first, last, acc_ref):
        wg = wg_ref[...].astype(bf16)
        wu = wu_ref[...].astype(bf16)
        wd = wd_ref[...].astype(bf16)
        for r in range(tm // FFN_ROW_BLOCK):
            rs = slice(r * FFN_ROW_BLOCK, (r + 1) * FFN_ROW_BLOCK)
            if first:
                h = _modulated(x_rows(rs), g_ref[...], mod_ref[rows[0]:rows[0] + 1, :],
                               mod_ref[rows[1]:rows[1] + 1, :]).astype(bf16)
                h_scr[rs, :] = h
            else:
                h = h_scr[rs, :]
            g = jnp.dot(h, wg, preferred_element_type=f32)
            u = jnp.dot(h, wu, preferred_element_type=f32)
            part = jnp.dot((_silu(g) * u).astype(bf16), wd, preferred_element_type=f32)
            acc = part if first else acc_ref[rs, :] + part
            if last:
                xn = x_rows(rs) + (0.5 * mod_ref[rows[2]:rows[2] + 1, :]) * acc
                if final:
                    xn = xn * lax.rsqrt(jnp.mean(xn * xn, axis=-1, keepdims=True) + EPS) * gf_ref[...]
                acc_ref[rs, :] = xn
            else:
                acc_ref[rs, :] = acc

    owners = [(is_prompt, o_refs[0]), (jnp.logical_not(is_prompt), o_refs[1])] if split_out else [(True, o_refs[0])]
    for owns, o_ref in owners:
        pl.when((j == 0) & owns)(functools.partial(step, True, False, o_ref))
        pl.when((j > 0) & (j < last_j) & owns)(functools.partial(step, False, False, o_ref))
        pl.when((j == last_j) & owns)(functools.partial(step, False, True, o_ref))


def _ffn_call(xs, mod_l, g_row, ffn_wg, ffn_wu, ffn_wd, layer, half, gf_row, rows, final, split_out, tm, tf=256):
    split_in = len(xs) == 2
    x_specs = _split_specs(tm, D_MODEL, 2) if split_in else [pl.BlockSpec((tm, D_MODEL), lambda i, j: (i, 0))]
    n_token_windows = 2 + split_in + split_out
    vmem_bytes = (2 * n_token_windows * tm * D_MODEL * 4 + tm * D_MODEL * 2 + 2 * 3 * D_MODEL * tf * 4
                  + FFN_VMEM_SLACK)
    if split_out:
        out_specs = _split_specs(tm, D_MODEL, 2)
        out_shape = [jax.ShapeDtypeStruct((N_PROMPT, D_MODEL), f32), jax.ShapeDtypeStruct((N_SAMPLE, D_MODEL), f32)]
    else:
        out_specs = [pl.BlockSpec((tm, D_MODEL), lambda i, j: (i, 0))]
        out_shape = [jax.ShapeDtypeStruct((N_TOK, D_MODEL), f32)]
    scratch = [pltpu.VMEM((tm, D_MODEL), bf16)]
    return pl.pallas_call(
        functools.partial(_ffn_kernel, rows=rows, final=final, split_in=split_in, split_out=split_out, tm=tm),
        grid=(N_TOK // tm, D_FF // tf),
        in_specs=x_specs + [
            pl.BlockSpec((None, N_ADA, D_MODEL), lambda i, j: (_mod_row_index(i, tm), 0, 0)),
            pl.BlockSpec((1, D_MODEL), lambda i, j: (0, 0)),
            pl.BlockSpec((None, None, D_MODEL, tf), lambda i, j: (layer, half, 0, j)),
            pl.BlockSpec((None, None, D_MODEL, tf), lambda i, j: (layer, half, 0, j)),
            pl.BlockSpec((None, None, tf, D_MODEL), lambda i, j: (layer, half, j, 0)),
            pl.BlockSpec((1, D_MODEL), lambda i, j: (0, 0))],
        out_specs=out_specs, out_shape=out_shape, scratch_shapes=scratch,
        compiler_params=pltpu.CompilerParams(dimension_semantics=("parallel", "arbitrary"),
                                             vmem_limit_bytes=vmem_bytes),
        name="ffn",
    )(*xs, mod_l, g_row, ffn_wg, ffn_wu, ffn_wd, gf_row)


def _proj_kernel(*refs, rows, with_gates, w_transposed):
    if with_gates:
        x_ref, mod_ref, g_ref, w_ref, wgate_ref, o_ref, og_ref, h_scr = refs
    else:
        x_ref, mod_ref, g_ref, w_ref, o_ref, h_scr = refs
    j = pl.program_id(1)
    tm, tn = x_ref.shape[0], o_ref.shape[1]
    dims = NT_DIMS if w_transposed else (((1,), (0,)), ((), ()))

    def matmul(h, w):
        return lax.dot_general(h, w, dims, preferred_element_type=f32)

    def w_tile(t):
        return (w_ref[t * tn:(t + 1) * tn, :] if w_transposed else w_ref[:, t * tn:(t + 1) * tn]).astype(bf16)

    @pl.when(j == 0)
    def _():
        w = w_tile(0)
        for r in range(tm // PROJ_ROW_BLOCK):
            rs = slice(r * PROJ_ROW_BLOCK, (r + 1) * PROJ_ROW_BLOCK)
            hb = _modulated(x_ref[rs, :], g_ref[...], mod_ref[rows[0]:rows[0] + 1, :],
                            mod_ref[rows[1]:rows[1] + 1, :]).astype(bf16)
            h_scr[rs, :] = hb
            o_ref[rs, :] = matmul(hb, w).astype(o_ref.dtype)
            if with_gates:
                og_ref[rs, :] = matmul(hb, wgate_ref[...].astype(bf16))

    n_tiles = (w_ref.shape[0] if w_transposed else w_ref.shape[1]) // tn
    for t in range(1, n_tiles):
        @pl.when(j == t)
        def _(t=t):
            o_ref[...] = matmul(h_scr[...], w_tile(t)).astype(o_ref.dtype)


AB_I0 = 2 * H_A * DQK_A + 2 * H_A * DV_A
AB_D0 = AB_I0 + 4 * H_A
AB_B0 = AB_D0 + 2 * H_B * DK_B + 2 * H_B * DV_B
AB_IN = AB_B0 + 4 * H_B


def _ab_weight_kernel(w_ref, wm_ref, wg_ref):
    n_gate = 2 * H_A
    wm_ref[:AB_I0, :] = w_ref[:AB_I0, :]
    wm_ref[AB_I0:, :] = w_ref[AB_D0:AB_B0, :]
    wg_ref[:2 * n_gate, :] = w_ref[AB_I0:AB_D0, :]
    wg_ref[2 * n_gate:4 * n_gate, :] = w_ref[AB_B0:, :]
    wg_ref[4 * n_gate:, :] = jnp.zeros((LANES - 4 * n_gate, w_ref.shape[1]), f32)


def _ab_weight_call(ab_w_in, layer, tk=256):
    w_t = jnp.swapaxes(ab_w_in, 1, 2)
    return pl.pallas_call(
        _ab_weight_kernel,
        grid=(D_MODEL // tk,),
        in_specs=[pl.BlockSpec((None, AB_IN, tk), lambda i: (layer, 0, i))],
        out_specs=[pl.BlockSpec((AB_MAIN, tk), lambda i: (0, i)), pl.BlockSpec((LANES, tk), lambda i: (0, i))],
        out_shape=[jax.ShapeDtypeStruct((AB_MAIN, D_MODEL), f32), jax.ShapeDtypeStruct((LANES, D_MODEL), f32)],
        compiler_params=pltpu.CompilerParams(vmem_limit_bytes=VMEM_LIMIT),
        name="ab_weights",
    )(w_t)


def _column_tile(n, cap):
    return max(t for t in range(LANES, cap + 1, LANES) if n % t == 0)


def _proj_call(x, mod_l, g_row, w, w_gate, rows, out_dtype, w_transposed, tm=2048):
    n = w.shape[0] if w_transposed else w.shape[1]
    tn = _column_tile(n, 1024)
    with_gates = w_gate is not None
    w_spec = pl.BlockSpec(w.shape, lambda i, j: (0, 0), pipeline_mode=pl.Buffered(1))
    in_specs = [pl.BlockSpec((tm, D_MODEL), lambda i, j: (i, 0)),
                pl.BlockSpec((None, N_ADA, D_MODEL), lambda i, j: (_mod_row_index(i, tm), 0, 0)),
                pl.BlockSpec((1, D_MODEL), lambda i, j: (0, 0)),
                w_spec]
    out_specs = [pl.BlockSpec((tm, tn), lambda i, j: (i, j))]
    out_shape = [jax.ShapeDtypeStruct((N_TOK, n), out_dtype)]
    args = [x, mod_l, g_row, w]
    if with_gates:
        in_specs.append(pl.BlockSpec(w_gate.shape, lambda i, j: (0, 0)))
        out_specs.append(pl.BlockSpec((tm, LANES), lambda i, j: (i, 0)))
        out_shape.append(jax.ShapeDtypeStruct((N_TOK, LANES), f32))
        args.append(w_gate)
    return pl.pallas_call(
        functools.partial(_proj_kernel, rows=rows, with_gates=with_gates, w_transposed=w_transposed),
        grid=(N_TOK // tm, n // tn),
        in_specs=in_specs, out_specs=out_specs, out_shape=out_shape,
        scratch_shapes=[pltpu.VMEM((tm, D_MODEL), bf16)],
        compiler_params=pltpu.CompilerParams(dimension_semantics=("parallel", "arbitrary"),
                                             vmem_limit_bytes=VMEM_LIMIT),
        name="in_proj",
    )(*args)


def _chunk_masks():
    r = lax.broadcasted_iota(jnp.int32, (CHUNK, CHUNK), 0)
    c = lax.broadcasted_iota(jnp.int32, (CHUNK, CHUNK), 1)
    return r >= c, r <= c, r > c, r < c


def _lane_col(x, lane, j):
    return jnp.sum(jnp.where(lane == j, x, 0.0), axis=1, keepdims=True)


def _head_lane_select(base, pair):
    r = lax.broadcasted_iota(jnp.int32, (SUBLANES, LANES), 0)
    ln = lax.broadcasted_iota(jnp.int32, (SUBLANES, LANES), 1)
    return jnp.where((ln == base + 2 * pair + r) & (r < 2), 1.0, 0.0)


def _split3_bf16(x):
    hi = x.astype(bf16)
    r = x - hi.astype(f32)
    mid = r.astype(bf16)
    return hi, mid, (r - mid.astype(f32)).astype(bf16)


def _thrice(a16):
    return jnp.concatenate([a16] * 3, axis=1)


def _mask_matmul_f32(mask16x3, x):
    return jnp.dot(mask16x3, jnp.concatenate(_split3_bf16(x), axis=0), preferred_element_type=f32)


def _select_rows_f32(sel16x3, x):
    return lax.dot_general(sel16x3, jnp.concatenate(_split3_bf16(x), axis=1), NT_DIMS, preferred_element_type=f32)


def _chunks_per_trip(nc):
    return min(MAX_CHUNKS_PER_TRIP, nc)


def _chunk_start(trip, u, d, nc):
    step = trip * _chunks_per_trip(nc) + u
    c = step if d == 0 else nc - 1 - step
    return pl.multiple_of(c * CHUNK, CHUNK)


def _mlstm_kernel(*refs, seq, zero_init):
    if zero_init:
        qp_ref, kp_ref, v0_ref, v1_ref, g_ref, par_ref, h_ref, cf_ref, nf_ref, mf_ref = refs
    else:
        (qp_ref, kp_ref, v0_ref, v1_ref, g_ref, par_ref, c0_ref, n0_ref, m0_ref,
         h_ref, cf_ref, nf_ref, mf_ref) = refs
    nc = seq // CHUNK
    pair = pl.program_id(1)
    lane = lax.broadcasted_iota(jnp.int32, (1, LANES), 1)
    tril, triu, _, _ = _chunk_masks()
    masks = (tril, triu)
    masks16 = (_thrice(tril.astype(bf16)), _thrice(triu.astype(bf16)))
    sel = [_thrice(_head_lane_select(LANE_F + d * H_A, pair).astype(bf16)) for d in range(2)]
    bi_row = par_ref[0:1, :]
    bf_row = par_ref[1:2, :]
    ones_col = jnp.where(lane == 0, 1.0, 0.0) + jnp.zeros((CHUNK, LANES), f32)
    v_refs = (v0_ref, v1_ref)
    streams = [(hh, d) for hh in range(2) for d in range(2)]

    h_ref[...] = jnp.zeros_like(h_ref)

    init = []
    for hh, d in streams:
        if zero_init:
            init.append((jnp.zeros((DQK_A, 2 * LANES), f32), jnp.zeros((1, 1), f32)))
        else:
            n_aug = jnp.where(lane == 0, n0_ref[d, hh], 0.0)
            init.append((jnp.concatenate([c0_ref[d, hh], n_aug], axis=1), m0_ref[d, hh]))

    def body(trip, carry):
        subs = range(_chunks_per_trip(nc))
        shared = {}
        for u in subs:
            for d in range(2):
                r0 = _chunk_start(trip, u, d, nc)
                gates = g_ref[pl.ds(r0, CHUNK), :]
                gi = gates + bi_row
                gf = _log_sigmoid(gates + bf_row)
                cum = _mask_matmul_f32(masks16[d], gf)
                stack = jnp.concatenate([cum, pltpu.roll(gi, LANE_F - LANE_I, axis=1)], axis=0)
                rows = _select_rows_f32(sel[d], stack)
                total = cum[CHUNK - 1:CHUNK, :] if d == 0 else cum[0:1, :]
                shared[u, d] = (r0, gi, cum, rows, total)
        chains = [(u, hh, d) for u in subs for hh, d in streams]
        st = {ch: {} for ch in chains}
        for ch in chains:
            u, hh, d = ch
            r0 = shared[u, d][0]
            s = st[ch]
            s["q"] = (qp_ref[pl.ds(r0, CHUNK), hh * DQK_A:(hh + 1) * DQK_A] * DQK_A ** -0.5).astype(bf16)
            s["k"] = kp_ref[pl.ds(r0, CHUNK), hh * DQK_A:(hh + 1) * DQK_A].astype(bf16)
            s["v_aug"] = jnp.concatenate([v_refs[hh][pl.ds(r0, CHUNK), :].astype(f32), ones_col], axis=1)
            s["qk"] = lax.dot_general(s["q"], s["k"], NT_DIMS, preferred_element_type=f32)
        for ch in chains:
            u, hh, d = ch
            _, gi, cum, rows, total = shared[u, d]
            s = st[ch]
            head = 2 * pair + hh
            jf = LANE_F + d * H_A + head
            ji = LANE_I + d * H_A + head
            bcol = _lane_col(cum, lane, jf)
            icol = _lane_col(gi, lane, ji)
            dmat = jnp.where(masks[d], bcol - rows[hh:hh + 1, :CHUNK] + rows[hh:hh + 1, CHUNK:], -jnp.inf)
            m_loc = jnp.max(dmat, axis=1, keepdims=True)
            s["sw"] = (s["qk"] * jnp.exp(dmat - m_loc)).astype(bf16)
            blast = _lane_col(total, lane, jf)
            gs = blast - bcol + icol
            ms_loc = jnp.max(gs, axis=0, keepdims=True)
            s["wv"] = (jnp.exp(gs - ms_loc) * s["v_aug"]).astype(bf16)
            s.update(bcol=bcol, m_loc=m_loc, blast=blast, ms_loc=ms_loc)
        for ch in chains:
            s = st[ch]
            s["num"] = jnp.dot(s["sw"], s["v_aug"].astype(bf16), preferred_element_type=f32)
            s["kv"] = lax.dot_general(s["k"], s["wv"], TN_DIMS, preferred_element_type=f32)
        state = list(carry)
        for u in subs:
            qc = [jnp.dot(st[u, hh, d]["q"], state[i][0].astype(bf16), preferred_element_type=f32)
                  for i, (hh, d) in enumerate(streams)]
            for i, (hh, d) in enumerate(streams):
                s = st[u, hh, d]
                c_aug, m = state[i]
                m_inter = s["bcol"] + m
                m_t = jnp.maximum(m_inter, s["m_loc"])
                nd = jnp.exp(s["m_loc"] - m_t) * s["num"] + jnp.exp(m_inter - m_t) * qc[i]
                den = nd[:, DV_A:DV_A + 1]
                hval = nd[:, :DV_A] / jnp.maximum(jnp.abs(den), jnp.exp(-m_t))
                r0 = shared[u, d][0]
                h_ref[pl.ds(r0, CHUNK), hh * DV_A:(hh + 1) * DV_A] += hval
                m_new = jnp.maximum(s["blast"] + m, s["ms_loc"])
                c_new = jnp.exp(s["blast"] + m - m_new) * c_aug + jnp.exp(s["ms_loc"] - m_new) * s["kv"]
                state[i] = (c_new, m_new)
        return tuple(state)

    final = lax.fori_loop(0, nc // _chunks_per_trip(nc), body, tuple(init))
    for i, (hh, d) in enumerate(streams):
        c_aug, m = final[i]
        cf_ref[d, hh] = c_aug[:, :DV_A]
        nf_ref[d, hh] = c_aug[:, DV_A:DV_A + 1]
        mf_ref[d, hh] = m


def _mlstm_call(proj, gates, par, nb, seq, off, state):
    zero_init = state is None
    blk = lambda col: pl.BlockSpec((seq, LANES), col)
    in_specs = [blk(lambda b, p: (off + b, p)),
                blk(lambda b, p: (off + b, 2 + p)),
                blk(lambda b, p: (off + b, 4 + 2 * p)),
                blk(lambda b, p: (off + b, 5 + 2 * p)),
                blk(lambda b, p: (off + b, 0)),
                pl.BlockSpec((SUBLANES, LANES), lambda b, p: (0, 0))]
    st_specs = [pl.BlockSpec((None, 2, 2, DQK_A, DV_A), lambda b, p: (b, 0, p, 0, 0)),
                pl.BlockSpec((None, 2, 2, DQK_A, 1), lambda b, p: (b, 0, p, 0, 0)),
                pl.BlockSpec((None, 2, 2, 1, 1), lambda b, p: (b, 0, p, 0, 0))]
    args = [proj, proj, proj, proj, gates, par]
    if not zero_init:
        in_specs += st_specs
        args += list(state)
    return pl.pallas_call(
        functools.partial(_mlstm_kernel, seq=seq, zero_init=zero_init),
        grid=(nb, H_A // 2),
        in_specs=in_specs,
        out_specs=[pl.BlockSpec((seq, 2 * DV_A), lambda b, p: (b, p))] + st_specs,
        out_shape=[jax.ShapeDtypeStruct((nb * seq, H_A * DV_A), f32),
                   jax.ShapeDtypeStruct((nb, 2, H_A, DQK_A, DV_A), f32),
                   jax.ShapeDtypeStruct((nb, 2, H_A, DQK_A, 1), f32),
                   jax.ShapeDtypeStruct((nb, 2, H_A, 1, 1), f32)],
        compiler_params=pltpu.CompilerParams(vmem_limit_bytes=VMEM_LIMIT),
        name="mlstm",
    )(*args)


CONV_PAD = SUBLANES


def _short_conv_silu(x, w_ref, pad_scr, seq):
    pad_scr[:CONV_PAD, :] = jnp.zeros((CONV_PAD, LANES), f32)
    pad_scr[CONV_PAD + seq:, :] = jnp.zeros((CONV_PAD, LANES), f32)
    pad_scr[CONV_PAD:CONV_PAD + seq, :] = x
    acc = x * w_ref[CONV_K // 2:CONV_K // 2 + 1, :]
    for tap in range(CONV_K):
        delta = tap - CONV_K // 2
        if delta != 0:
            acc = acc + pad_scr[CONV_PAD + delta:CONV_PAD + delta + seq, :] * w_ref[tap:tap + 1, :]
    return _silu(acc)


def _l2_unit(y):
    return y * lax.rsqrt(jnp.sum(y * y, axis=-1, keepdims=True) + EPS)


def _split_bf16(x):
    hi = x.astype(bf16)
    return hi, (x - hi.astype(f32)).astype(bf16)


def _matmul_3pass(m, x):
    mh, ml = _split_bf16(m)
    xh, xl = _split_bf16(x)
    return jnp.dot(jnp.concatenate([mh, ml, mh], axis=1), jnp.concatenate([xh, xh, xl], axis=0),
                   preferred_element_type=f32)


def _unit_triangular_solves(ns, xs):
    levels = CHUNK.bit_length() - 1
    r = lax.broadcasted_iota(jnp.int32, (CHUNK, CHUNK), 0)
    c = lax.broadcasted_iota(jnp.int32, (CHUNK, CHUNK), 1)
    eye = jnp.where(r == c, 1.0, 0.0)
    ts = [eye + n for n in ns]
    ms = [_matmul_3pass(n, n) for n in ns]
    for lvl in range(1, levels):
        for i in range(len(ns)):
            if lvl < levels - 1:
                prod = _matmul_3pass(ms[i], jnp.concatenate([ts[i], ms[i]], axis=1))
                ts[i] = ts[i] + prod[:, :CHUNK]
                ms[i] = prod[:, CHUNK:]
            else:
                ts[i] = ts[i] + _matmul_3pass(ms[i], ts[i])
    return [_matmul_3pass(t, x) for t, x in zip(ts, xs)]


def _delta_kernel(*refs, seq, zero_init):
    if zero_init:
        (q0_ref, q1_ref, k0_ref, k1_ref, v0_ref, v1_ref, wq0_ref, wq1_ref, wk0_ref, wk1_ref, wv0_ref, wv1_ref,
         g_ref, par_ref, o_ref, sf_ref, q_scr, k_scr, v_scr, pad_scr) = refs
    else:
        (q0_ref, q1_ref, k0_ref, k1_ref, v0_ref, v1_ref, wq0_ref, wq1_ref, wk0_ref, wk1_ref, wv0_ref, wv1_ref,
         g_ref, par_ref, s0_ref, o_ref, sf_ref, q_scr, k_scr, v_scr, pad_scr) = refs
    nc = seq // CHUNK
    pair = pl.program_id(1)
    lane = lax.broadcasted_iota(jnp.int32, (1, LANES), 1)
    tril, triu, stril, striu = _chunk_masks()
    masks, smasks = (tril, triu), (stril, striu)
    masks16 = (_thrice(tril.astype(bf16)), _thrice(triu.astype(bf16)))
    sel = [_thrice(_head_lane_select(LANE_A + d * H_B, pair).astype(bf16)) for d in range(2)]
    neg_a_row = -jnp.exp(par_ref[0:1, :])
    dt_row = par_ref[1:2, :]
    streams = [(hh, d) for hh in range(2) for d in range(2)]

    for hh, (q_ref, k_ref, v_ref, wq_ref, wk_ref, wv_ref) in enumerate(
            ((q0_ref, k0_ref, v0_ref, wq0_ref, wk0_ref, wv0_ref), (q1_ref, k1_ref, v1_ref, wq1_ref, wk1_ref, wv1_ref))):
        q_scr[hh] = _l2_unit(_short_conv_silu(q_ref[...].astype(f32), wq_ref, pad_scr, seq)) * DK_B ** -0.5
        k_scr[hh] = _l2_unit(_short_conv_silu(k_ref[...].astype(f32), wk_ref, pad_scr, seq))
        v_scr[hh] = _short_conv_silu(v_ref[...].astype(f32), wv_ref, pad_scr, seq)

    o_ref[...] = jnp.zeros_like(o_ref)
    init = tuple(jnp.zeros((DK_B, DV_B), f32) if zero_init else s0_ref[d, hh] for hh, d in streams)

    def body(trip, carry):
        subs = range(_chunks_per_trip(nc))
        shared = {}
        for u in subs:
            for d in range(2):
                r0 = _chunk_start(trip, u, d, nc)
                gates = g_ref[pl.ds(r0, CHUNK), :]
                beta_all = jax.nn.sigmoid(gates)
                glog = neg_a_row * _softplus(gates + dt_row)
                cum = _mask_matmul_f32(masks16[d], glog)
                rows = _select_rows_f32(sel[d], cum)
                shared[u, d] = (r0, beta_all, cum, rows)
        chains = [(u, hh, d) for u in subs for hh, d in streams]
        st = {ch: {} for ch in chains}
        for ch in chains:
            u, hh, d = ch
            r0, beta_all, cum, rows = shared[u, d]
            s = st[ch]
            head = 2 * pair + hh
            beta = _lane_col(beta_all, lane, LANE_BETA + d * H_B + head)
            gcol = _lane_col(cum, lane, LANE_A + d * H_B + head)
            glast = gcol[CHUNK - 1:CHUNK, :] if d == 0 else gcol[0:1, :]
            q = q_scr[hh, pl.ds(r0, CHUNK), :]
            k = k_scr[hh, pl.ds(r0, CHUNK), :]
            v = v_scr[hh, pl.ds(r0, CHUNK), :]
            k16 = k.astype(bf16)
            kbeta = k * beta
            eg = jnp.exp(gcol)
            decay = jnp.exp(jnp.where(masks[d], gcol - rows[hh:hh + 1, :], -jnp.inf))
            kk = lax.dot_general(kbeta.astype(bf16), k16, NT_DIMS, preferred_element_type=f32)
            qk = lax.dot_general(q.astype(bf16), k16, NT_DIMS, preferred_element_type=f32)
            s["n"] = -jnp.where(smasks[d], kk * decay, 0.0)
            s["x"] = jnp.concatenate([v * beta, kbeta * eg], axis=1)
            s["qk"] = (qk * decay).astype(bf16)
            s["qg"] = (q * eg).astype(bf16)
            s["kd_t"] = (k * jnp.exp(glast - gcol)).T.astype(bf16)
            s["gl"] = jnp.exp(glast)
        solved = _unit_triangular_solves([st[ch]["n"] for ch in chains], [st[ch]["x"] for ch in chains])
        for ch, uw in zip(chains, solved):
            s = st[ch]
            s["u"] = uw[:, :DV_B]
            s["w_qg"] = jnp.concatenate([uw[:, DV_B:].astype(bf16), s["qg"]], axis=0)
        state = list(carry)
        for u in subs:
            ws = [jnp.dot(st[u, hh, d]["w_qg"], state[i].astype(bf16), preferred_element_type=f32)
                  for i, (hh, d) in enumerate(streams)]
            v_new = [(st[u, hh, d]["u"] - ws[i][:CHUNK]).astype(bf16) for i, (hh, d) in enumerate(streams)]
            for i, (hh, d) in enumerate(streams):
                s = st[u, hh, d]
                o = ws[i][CHUNK:] + jnp.dot(s["qk"], v_new[i], preferred_element_type=f32)
                r0 = shared[u, d][0]
                o_ref[pl.ds(r0, CHUNK), hh * DV_B:(hh + 1) * DV_B] += o
                state[i] = state[i] * s["gl"] + jnp.dot(s["kd_t"], v_new[i], preferred_element_type=f32)
        return tuple(state)

    final = lax.fori_loop(0, nc // _chunks_per_trip(nc), body, init)
    for i, (hh, d) in enumerate(streams):
        sf_ref[d, hh] = final[i]


def _delta_call(proj, conv_w8, gates, par, nb, seq, off, state):
    zero_init = state is None
    col0 = (2 * H_A * DQK_A + 2 * H_A * DV_A) // LANES
    blk = lambda col: pl.BlockSpec((seq, LANES), col)
    wblk = lambda col: pl.BlockSpec((SUBLANES, LANES), col)
    in_specs = [blk(lambda b, p: (off + b, col0 + 2 * p)), blk(lambda b, p: (off + b, col0 + 2 * p + 1)),
                blk(lambda b, p: (off + b, col0 + H_B + 2 * p)), blk(lambda b, p: (off + b, col0 + H_B + 2 * p + 1)),
                blk(lambda b, p: (off + b, col0 + 2 * H_B + 2 * p)), blk(lambda b, p: (off + b, col0 + 2 * H_B + 2 * p + 1)),
                wblk(lambda b, p: (0, 2 * p)), wblk(lambda b, p: (0, 2 * p + 1)),
                wblk(lambda b, p: (0, H_B + 2 * p)), wblk(lambda b, p: (0, H_B + 2 * p + 1)),
                wblk(lambda b, p: (0, 2 * H_B + 2 * p)), wblk(lambda b, p: (0, 2 * H_B + 2 * p + 1)),
                blk(lambda b, p: (off + b, 0)),
                pl.BlockSpec((SUBLANES, LANES), lambda b, p: (0, 0))]
    st_spec = pl.BlockSpec((None, 2, 2, DK_B, DV_B), lambda b, p: (b, 0, p, 0, 0))
    args = [proj] * 6 + [conv_w8] * 6 + [gates, par]
    if not zero_init:
        in_specs.append(st_spec)
        args.append(state)
    return pl.pallas_call(
        functools.partial(_delta_kernel, seq=seq, zero_init=zero_init),
        grid=(nb, H_B // 2),
        in_specs=in_specs,
        out_specs=[pl.BlockSpec((seq, 2 * DV_B), lambda b, p: (b, p)), st_spec],
        out_shape=[jax.ShapeDtypeStruct((nb * seq, H_B * DV_B), f32),
                   jax.ShapeDtypeStruct((nb, 2, H_B, DK_B, DV_B), f32)],
        scratch_shapes=[pltpu.VMEM((2, seq, LANES), f32)] * 3 + [pltpu.VMEM((seq + 2 * CONV_PAD, LANES), f32)],
        compiler_params=pltpu.CompilerParams(vmem_limit_bytes=VMEM_LIMIT),
        name="delta",
    )(*args)


def _head_rms(x):
    return x * lax.rsqrt(jnp.mean(x * x, axis=-1, keepdims=True) + EPS)


def _ab_out_kernel(x_ref, hp_ref, hs_ref, op_ref, os_ref, om_ref, zd_ref, gm_ref, gd_ref, w_ref, mod_ref, o_ref,
                   *, gate_row, tm):
    hsum = _prompt_or_sample(hp_ref, hs_ref, tm)
    osum = _prompt_or_sample(op_ref, os_ref, tm)
    parts = []
    for h in range(H_A):
        sl = slice(h * DV_A, (h + 1) * DV_A)
        parts.append(_head_rms(hsum[:, sl]) * gm_ref[:, sl] * jax.nn.sigmoid(om_ref[:, sl].astype(f32)))
    for h in range(H_B):
        sl = slice(h * DV_B, (h + 1) * DV_B)
        parts.append(_head_rms(osum[:, sl]) * gd_ref[:, sl] * _silu(zd_ref[:, sl].astype(f32)))
    cat = jnp.concatenate(parts, axis=1).astype(bf16)
    y = jnp.dot(cat, w_ref[...].astype(bf16), preferred_element_type=f32)
    o_ref[...] = x_ref[...] + mod_ref[gate_row:gate_row + 1, :] * y


def _ab_out_call(x, hp, hs, op, os_, proj, gm_row, gd_row, w_out, mod_l, gate_row, tm=1024):
    wide = H_A * DV_A
    om_blk = (2 * H_A * DQK_A) // wide + 1
    zd_blk = AB_MAIN // wide - 1
    return pl.pallas_call(
        functools.partial(_ab_out_kernel, gate_row=gate_row, tm=tm),
        grid=(N_TOK // tm,),
        in_specs=[pl.BlockSpec((tm, D_MODEL), lambda i: (i, 0))] + _split_specs(tm, wide) + _split_specs(tm, wide) + [
            pl.BlockSpec((tm, wide), lambda i: (i, om_blk)),
            pl.BlockSpec((tm, wide), lambda i: (i, zd_blk)),
            pl.BlockSpec((1, wide), lambda i: (0, 0)),
            pl.BlockSpec((1, wide), lambda i: (0, 0)),
            pl.BlockSpec((2 * wide, D_MODEL), lambda i: (0, 0)),
            pl.BlockSpec((None, N_ADA, D_MODEL), lambda i: (_mod_row_index(i, tm), 0, 0))],
        out_specs=pl.BlockSpec((tm, D_MODEL), lambda i: (i, 0)),
        out_shape=jax.ShapeDtypeStruct((N_TOK, D_MODEL), f32),
        compiler_params=pltpu.CompilerParams(vmem_limit_bytes=VMEM_LIMIT),
        name="ab_out",
    )(x, hp, hs, op, os_, proj, proj, gm_row, gd_row, w_out, mod_l)


def _na_out_kernel(x_ref, ap_ref, as_ref, w_ref, mod_ref, o_ref, *, gate_row, tm):
    a = _prompt_or_sample(ap_ref, as_ref, tm).astype(bf16)
    y = jnp.dot(a, w_ref[...].astype(bf16), preferred_element_type=f32)
    o_ref[...] = x_ref[...] + mod_ref[gate_row:gate_row + 1, :] * y


def _na_out_call(x, attn_p, attn_s, w_out, mod_l, gate_row, tm=1024):
    return pl.pallas_call(
        functools.partial(_na_out_kernel, gate_row=gate_row, tm=tm),
        grid=(N_TOK // tm,),
        in_specs=[pl.BlockSpec((tm, D_MODEL), lambda i: (i, 0))] + _split_specs(tm, NA_W) + [
            pl.BlockSpec((NA_W, D_MODEL), lambda i: (0, 0)),
            pl.BlockSpec((None, N_ADA, D_MODEL), lambda i: (_mod_row_index(i, tm), 0, 0))],
        out_specs=pl.BlockSpec((tm, D_MODEL), lambda i: (i, 0)),
        out_shape=jax.ShapeDtypeStruct((N_TOK, D_MODEL), f32),
        compiler_params=pltpu.CompilerParams(vmem_limit_bytes=VMEM_LIMIT),
        name="na_out",
    )(x, attn_p, attn_s, w_out, mod_l)


CTX_HEADS = 8


def _ctx_attn_kernel(q_ref, k_ref, v_ref, o_ref, nk_ref, nv_ref):
    low = lax.broadcasted_iota(jnp.int32, (1, LANES), 1) < DH_C
    def pair_scores(pair):
        sl = slice(pair * LANES, (pair + 1) * LANES)
        q = q_ref[:, sl] * DH_C ** -0.5
        k_t = k_ref[:, sl].T
        v_t = v_ref[:, sl].T
        for hh in range(2):
            nk_ref[2 * pair + hh] = k_t[hh * DH_C:(hh + 1) * DH_C, :]
            nv_ref[2 * pair + hh] = v_t[hh * DH_C:(hh + 1) * DH_C, :]
        k = k_ref[:, sl].astype(bf16)
        scores = [lax.dot_general(jnp.where(low if hh == 0 else jnp.logical_not(low), q, 0.0).astype(bf16), k,
                                  NT_DIMS, preferred_element_type=f32) for hh in range(2)]
        return sl, scores

    def pair_finish(sl, scores):
        v = v_ref[:, sl].astype(bf16)
        probs = [jnp.exp(s - jnp.max(s, axis=1, keepdims=True)) for s in scores]
        outs = [jnp.dot(p.astype(bf16), v, preferred_element_type=f32) / jnp.sum(p, axis=1, keepdims=True)
                for p in probs]
        o_ref[:, sl] = jnp.where(low, outs[0], outs[1]).astype(o_ref.dtype)

    n_pairs = CTX_HEADS // 2
    pending = pair_scores(0)
    for pair in range(n_pairs):
        upcoming = pair_scores(pair + 1) if pair + 1 < n_pairs else None
        pair_finish(*pending)
        pending = upcoming


def _ctx_attn_call(proj):
    w = CTX_HEADS * DH_C
    nblk = NA_W // w
    return pl.pallas_call(
        _ctx_attn_kernel,
        grid=(BATCH, nblk),
        in_specs=[pl.BlockSpec((SEQ, w), lambda b, j: (b, j)),
                  pl.BlockSpec((SEQ, w), lambda b, j: (b, nblk + j)),
                  pl.BlockSpec((SEQ, w), lambda b, j: (b, 2 * nblk + j))],
        out_specs=[pl.BlockSpec((SEQ, w), lambda b, j: (b, j)),
                   pl.BlockSpec((None, CTX_HEADS, DH_C, SEQ), lambda b, j: (b, j, 0, 0)),
                   pl.BlockSpec((None, CTX_HEADS, DH_C, SEQ), lambda b, j: (b, j, 0, 0))],
        out_shape=[jax.ShapeDtypeStruct((N_PROMPT, NA_W), bf16),
                   jax.ShapeDtypeStruct((BATCH, H_C, DH_C, SEQ), f32),
                   jax.ShapeDtypeStruct((BATCH, H_C, DH_C, SEQ), f32)],
        compiler_params=pltpu.CompilerParams(vmem_limit_bytes=VMEM_LIMIT),
        name="ctx_attn",
    )(proj, proj, proj)


QROWS = 4
QBLK = QROWS * GRID_W
KROWS_MID = QROWS + WIN_R - 1
N_RIDX = 2 * WIN_R - 1
N_CIDX = 2 * WIN_C - 1
N_QBLK = GRID_ROWS // QROWS


def _nattn_bias_tables(rb_ref, tb_scr, bmid_scr, btop_scr, bbot_scr):
    qc = lax.broadcasted_iota(jnp.int32, (GRID_W, GRID_W), 0)
    kc = lax.broadcasted_iota(jnp.int32, (GRID_W, GRID_W), 1)
    cs = jnp.clip(qc - WIN_C // 2, 0, GRID_W - WIN_C)
    valid = (kc >= cs) & (kc < cs + WIN_C)
    neg = jnp.full((GRID_W, GRID_W), -jnp.inf, f32)
    for hh in range(2):
        for ri in range(N_RIDX):
            row = jnp.broadcast_to(rb_ref[hh, ri:ri + 1, :], (GRID_W, LANES))
            tile = pltpu.roll(row, LANES - (WIN_C - 1), 1, stride=1, stride_axis=0)[:, :GRID_W]
            tb_scr[ri] = jnp.where(valid, tile, -jnp.inf)
        for i in range(QROWS):
            rs = slice(i * GRID_W, (i + 1) * GRID_W)
            for jj in range(KROWS_MID):
                inside = 0 <= jj - i < WIN_R
                bmid_scr[hh, rs, jj * GRID_W:(jj + 1) * GRID_W] = tb_scr[jj - i + WIN_R // 2 - 1] if inside else neg
            for jj in range(WIN_R):
                btop_scr[hh, rs, jj * GRID_W:(jj + 1) * GRID_W] = tb_scr[jj - i + WIN_R - 1]
                bbot_scr[hh, rs, jj * GRID_W:(jj + 1) * GRID_W] = tb_scr[jj - i + WIN_R // 2 - 1]


def _nattn_kernel(rb_ref, q_ref, k_ref, v_ref, ck_ref, cv_ref, o_ref, tb_scr, bmid_scr, btop_scr, bbot_scr):
    scale = DH_C ** -0.5

    @pl.when(pl.program_id(1) == 0)
    def _():
        _nattn_bias_tables(rb_ref, tb_scr, bmid_scr, btop_scr, bbot_scr)

    low = lax.broadcasted_iota(jnp.int32, (1, LANES), 1) < DH_C
    kctx_t = jnp.concatenate([ck_ref[0], ck_ref[1]], axis=0).astype(bf16)
    vctx = jnp.concatenate([cv_ref[0], cv_ref[1]], axis=0).T.astype(bf16)

    def block_scores(q_start, k_start, k_rows, bias_scr):
        nk = k_rows * GRID_W
        q = q_ref[pl.ds(q_start, QBLK), :] * scale
        ku = k_ref[pl.ds(k_start, nk), :].astype(bf16)
        vu = v_ref[pl.ds(k_start, nk), :].astype(bf16)
        scores = []
        for hh in range(2):
            qh = jnp.where(low if hh == 0 else jnp.logical_not(low), q, 0.0).astype(bf16)
            scores.append((lax.dot_general(qh, ku, NT_DIMS, preferred_element_type=f32) + bias_scr[hh],
                           jnp.dot(qh, kctx_t, preferred_element_type=f32)))
        return q_start, vu, scores

    def block_finish(q_start, vu, scores):
        probs = []
        for s_loc, s_ctx in scores:
            m = jnp.maximum(jnp.max(s_loc, axis=1, keepdims=True), jnp.max(s_ctx, axis=1, keepdims=True))
            p_loc = jnp.exp(s_loc - m)
            p_ctx = jnp.exp(s_ctx - m)
            denom = jnp.sum(p_loc, axis=1, keepdims=True) + jnp.sum(p_ctx, axis=1, keepdims=True)
            probs.append((p_loc.astype(bf16), p_ctx.astype(bf16), denom))
        outs = [(jnp.dot(p_loc, vu, preferred_element_type=f32) + jnp.dot(p_ctx, vctx, preferred_element_type=f32)) / denom
                for p_loc, p_ctx, denom in probs]
        o_ref[pl.ds(q_start, QBLK), :] = jnp.where(low, outs[0], outs[1]).astype(o_ref.dtype)

    blocks = ([(0, 0, WIN_R, btop_scr)]
              + [(b * QBLK, b * QBLK - (WIN_R // 2) * GRID_W, KROWS_MID, bmid_scr) for b in range(1, N_QBLK - 1)]
              + [((N_QBLK - 1) * QBLK, (GRID_ROWS - WIN_R) * GRID_W, WIN_R, bbot_scr)])
    pending = block_scores(*blocks[0])
    for nxt in blocks[1:] + [None]:
        upcoming = block_scores(*nxt) if nxt is not None else None
        block_finish(*pending)
        pending = upcoming


def _nattn_call(proj, cache_k, cache_v, layer, rel_bias, off):
    npair = H_C // 2
    return pl.pallas_call(
        _nattn_kernel,
        grid=(npair, DEC_BATCH),
        in_specs=[pl.BlockSpec((2, 2 * WIN_R, LANES), lambda p, b: (p, 0, 0)),
                  pl.BlockSpec((DEC_SEQ, LANES), lambda p, b: (off + b, p)),
                  pl.BlockSpec((DEC_SEQ, LANES), lambda p, b: (off + b, npair + p)),
                  pl.BlockSpec((DEC_SEQ, LANES), lambda p, b: (off + b, 2 * npair + p)),
                  pl.BlockSpec((None, None, 2, DH_C, PAST_LEN), lambda p, b: (b, layer, p, 0, 0)),
                  pl.BlockSpec((None, None, 2, DH_C, PAST_LEN), lambda p, b: (b, layer, p, 0, 0))],
        out_specs=pl.BlockSpec((DEC_SEQ, LANES), lambda p, b: (b, p)),
        out_shape=jax.ShapeDtypeStruct((N_SAMPLE, NA_W), bf16),
        scratch_shapes=[pltpu.VMEM((N_RIDX, GRID_W, GRID_W), f32),
                        pltpu.VMEM((2, QBLK, KROWS_MID * GRID_W), f32),
                        pltpu.VMEM((2, QBLK, WIN_R * GRID_W), f32),
                        pltpu.VMEM((2, QBLK, WIN_R * GRID_W), f32)],
        compiler_params=pltpu.CompilerParams(dimension_semantics=("arbitrary", "arbitrary"),
                                             vmem_limit_bytes=VMEM_LIMIT),
        name="nattn",
    )(jnp.pad(rel_bias, ((0, 0), (0, 2 * WIN_R - N_RIDX), (0, LANES - N_CIDX))), proj, proj, proj, cache_k, cache_v)


def _lane_row(pieces):
    row = jnp.zeros((LANES,), f32)
    for off, vals in pieces:
        row = row.at[off:off + vals.shape[0]].set(vals.astype(f32))
    return row


def _param_rows(rows):
    out = jnp.zeros((SUBLANES, LANES), f32)
    for r, row in enumerate(rows):
        out = out.at[r].set(row)
    return out


def kernel(x_prompt, x_sample, c, state_mlstm_C, state_mlstm_n, state_mlstm_m, state_delta_S, cache_na_k, cache_na_v, c_ctx, ada_w, ada_b, norm_g, ffn_wg, ffn_wu, ffn_wd, ab_w_in, ab_w_out, mlstm_b_i, mlstm_b_f, mlstm_norm_g, delta_conv_w, delta_a_log, delta_dt_bias, delta_norm_g, na_w_in, na_w_out, na_rel_bias, final_norm_g):
    xs = (x_prompt.reshape(N_PROMPT, D_MODEL), x_sample.reshape(N_SAMPLE, D_MODEL))
    mods = _ada_call(jnp.concatenate([c_ctx[None, :], c], axis=0), ada_w, ada_b)
    mods = mods.reshape(DEPTH, 3, N_ADA, D_MODEL)
    gf_row = final_norm_g.reshape(1, D_MODEL)
    s_off = N_PROMPT // DEC_SEQ
    new_c, new_n, new_m, new_s, new_k, new_v = [], [], [], [], [], []

    for l in range(DEPTH):
        mod_l = mods[l]
        a = l // 2
        x = _ffn_call(xs if l == 0 else (x,), mod_l, norm_g[l, 0].reshape(1, D_MODEL), ffn_wg, ffn_wu, ffn_wd, l, 0,
                      gf_row, rows=(0, 1, 2), final=False, split_out=False, tm=FFN_TM)[0]
        g_mix = norm_g[l, 1].reshape(1, D_MODEL)
        if l % 2 == 0:
            w_main, w_gate = _ab_weight_call(ab_w_in, a)
            proj, gates = _proj_call(x, mod_l, g_mix, w_main, w_gate, rows=(3, 4), out_dtype=bf16, w_transposed=True)

            par_m = _param_rows([_lane_row([(LANE_I, mlstm_b_i[a].reshape(-1))]),
                                 _lane_row([(LANE_F, mlstm_b_f[a].reshape(-1))])])
            par_d = _param_rows([_lane_row([(LANE_A, delta_a_log[a].reshape(-1))]),
                                 _lane_row([(LANE_A, delta_dt_bias[a].reshape(-1))])])
            conv_w8 = jnp.concatenate([delta_conv_w[a], jnp.zeros((SUBLANES - CONV_K, 3 * H_B * DK_B), f32)], axis=0)

            hp, cn, nn_, mn = _mlstm_call(proj, gates, par_m, BATCH, SEQ, 0, None)
            st = (state_mlstm_C[:, a], state_mlstm_n[:, a][..., None], state_mlstm_m[:, a][..., None, None])
            hs, _, _, _ = _mlstm_call(proj, gates, par_m, DEC_BATCH, DEC_SEQ, s_off, st)
            op, sn = _delta_call(proj, conv_w8, gates, par_d, BATCH, SEQ, 0, None)
            os_, _ = _delta_call(proj, conv_w8, gates, par_d, DEC_BATCH, DEC_SEQ, s_off, state_delta_S[:, a])
            new_c.append(cn)
            new_n.append(nn_[..., 0])
            new_m.append(mn[..., 0, 0])
            new_s.append(sn)
            x = _ab_out_call(x, hp, hs, op, os_, proj, mlstm_norm_g[a].reshape(1, -1),
                             jnp.tile(delta_norm_g[a], H_B).reshape(1, -1), ab_w_out[a], mod_l, gate_row=5)
        else:
            proj = _proj_call(x, mod_l, g_mix, na_w_in[a], None, rows=(3, 4), out_dtype=f32, w_transposed=False)[0]
            attn_p, kp_t, vp_t = _ctx_attn_call(proj)
            attn_s = _nattn_call(proj, jnp.swapaxes(cache_na_k, -1, -2), jnp.swapaxes(cache_na_v, -1, -2), a,
                                 na_rel_bias[a], s_off)
            new_k.append(jnp.swapaxes(kp_t, -1, -2))
            new_v.append(jnp.swapaxes(vp_t, -1, -2))
            x = _na_out_call(x, attn_p, attn_s, na_w_out[a], mod_l, gate_row=5)
        last = l == DEPTH - 1
        outs = _ffn_call((x,), mod_l, norm_g[l, 2].reshape(1, D_MODEL), ffn_wg, ffn_wu, ffn_wd, l, 1,
                         gf_row, rows=(6, 7, 8), final=last, split_out=last, tm=FFN_TM)
        x = outs[0]

    y_prompt = outs[0].reshape(BATCH, SEQ, D_MODEL)
    y_sample = outs[1].reshape(DEC_BATCH, DEC_SEQ, D_MODEL)
    return (y_prompt, y_sample, jnp.stack(new_c, axis=1), jnp.stack(new_n, axis=1), jnp.stack(new_m, axis=1),
            jnp.stack(new_s, axis=1), jnp.stack(new_k, axis=1), jnp.stack(new_v, axis=1))
```

```python
import functools

import jax
import jax.numpy as jnp
from jax import lax
from jax.experimental import pallas as pl
from jax.experimental.pallas import tpu as pltpu

f32 = jnp.float32
bf16 = jnp.bfloat16

D_MODEL = 1024
BATCH = 16
SEQ = 256
DEPTH = 2
DEC_BATCH = 2
DEC_SEQ = 2048
PAST_LEN = 256
GRID_W = 64
GRID_ROWS = DEC_SEQ // GRID_W
D_FF = 2816
N_ADA = 9
EPS = 1e-6
CHUNK = 64
H_A, DQK_A, DV_A = 4, 64, 128
H_B, DK_B, DV_B = 4, 128, 128
CONV_K = 5
H_C, DH_C = 16, 64
WIN_R, WIN_C = 8, 16
NA_W = H_C * DH_C
assert DH_C ** -0.5 == 2.0 ** -3 and DQK_A ** -0.5 == 2.0 ** -3

N_PROMPT = BATCH * SEQ
N_SAMPLE = DEC_BATCH * DEC_SEQ
N_TOK = N_PROMPT + N_SAMPLE
AB_MAIN = 2 * H_A * DQK_A + 2 * H_A * DV_A + 2 * H_B * DK_B + 2 * H_B * DV_B
LANES = 128
VMEM_LIMIT = 56 * 1024 * 1024
FFN_VMEM_SLACK = 4 * 1024 * 1024
SUBLANES = 8

LANE_I, LANE_F, LANE_BETA, LANE_A = 0, 8, 16, 24

NT_DIMS = (((1,), (1,)), ((), ()))
TN_DIMS = (((0,), (0,)), ((), ()))

MAX_CHUNKS_PER_TRIP = 4
FFN_TM = 2048
FFN_ROW_BLOCK = 512
PROJ_ROW_BLOCK = 256


def _softplus(x):
    return jnp.maximum(x, 0.0) + jnp.log1p(jnp.exp(-jnp.abs(x)))


def _log_sigmoid(x):
    return -_softplus(-x)


def _silu(x):
    return x * jax.nn.sigmoid(x)


def _mod_row_index(i, tm):
    n_p = N_PROMPT // tm
    per_b = DEC_SEQ // tm
    return jnp.where(i < n_p, 0, 1 + (i - n_p) // per_b)


def _modulated(x, g_row, shift_row, scale_row):
    y = x * lax.rsqrt(jnp.mean(x * x, axis=-1, keepdims=True) + EPS) * g_row
    return y * (1.0 + scale_row) + shift_row


def _ada_kernel(cb_ref, w_ref, b_ref, o_ref, s_scr):
    tn = w_ref.shape[-1]

    @pl.when((pl.program_id(0) == 0) & (pl.program_id(1) == 0))
    def _():
        cb = cb_ref[...]
        s_scr[...] = cb * jax.nn.sigmoid(cb)

    for j in range(tn // LANES):
        cols = slice(j * LANES, (j + 1) * LANES)
        w = w_ref[:, cols]
        for r in range(3):
            o_ref[r:r + 1, cols] = jnp.sum(w * s_scr[r], axis=0, keepdims=True) + b_ref[:, cols]


def _ada_call(cond3, ada_w, ada_b):
    tn = N_ADA * D_MODEL // 4
    cb = jnp.broadcast_to(cond3[:, :, None], (3, D_MODEL, LANES))
    return pl.pallas_call(
        _ada_kernel,
        grid=(DEPTH, N_ADA * D_MODEL // tn),
        in_specs=[pl.BlockSpec((3, D_MODEL, LANES), lambda l, j: (0, 0, 0)),
                  pl.BlockSpec((None, D_MODEL, tn), lambda l, j: (l, 0, j)),
                  pl.BlockSpec((None, 1, tn), lambda l, j: (l, 0, j))],
        out_specs=pl.BlockSpec((None, 3, tn), lambda l, j: (l, 0, j)),
        out_shape=jax.ShapeDtypeStruct((DEPTH, 3, N_ADA * D_MODEL), f32),
        scratch_shapes=[pltpu.VMEM((3, D_MODEL, LANES), f32)],
        compiler_params=pltpu.CompilerParams(dimension_semantics=("arbitrary", "arbitrary"),
                                             vmem_limit_bytes=VMEM_LIMIT),
        name="ada_mod",
    )(cb, ada_w, ada_b.reshape(DEPTH, 1, N_ADA * D_MODEL))


def _prompt_or_sample(p_ref, s_ref, tm):
    return jnp.where(pl.program_id(0) < N_PROMPT // tm, p_ref[...], s_ref[...])


def _split_specs(tm, width, n_grid_axes=1):
    n_p = N_PROMPT // tm
    if n_grid_axes == 1:
        return [pl.BlockSpec((tm, width), lambda i: (jnp.minimum(i, n_p - 1), 0)),
                pl.BlockSpec((tm, width), lambda i: (jnp.maximum(i - n_p, 0), 0))]
    return [pl.BlockSpec((tm, width), lambda i, j: (jnp.minimum(i, n_p - 1), 0)),
            pl.BlockSpec((tm, width), lambda i, j: (jnp.maximum(i - n_p, 0), 0))]


def _ffn_kernel(*refs, rows, final, split_in, split_out, tm):
    it = iter(refs)
    x_refs = [next(it) for _ in range(2 if split_in else 1)]
    mod_ref, g_ref, wg_ref, wu_ref, wd_ref, gf_ref = [next(it) for _ in range(6)]
    o_refs = [next(it) for _ in range(2 if split_out else 1)]
    h_scr = next(it)
    j = pl.program_id(1)
    last_j = pl.num_programs(1) - 1
    is_prompt = pl.program_id(0) < N_PROMPT // tm

    def x_rows(rs):
        if split_in:
            return jnp.where(is_prompt, x_refs[0][rs, :], x_refs[1][rs, :])
        return x_refs[0][rs, :]

    def step(first, last, acc_ref):
        wg = wg_ref[...].astype(bf16)
        wu = wu_ref[...].astype(bf16)
        wd = wd_ref[...].astype(bf16)
        for r in range(tm // FFN_ROW_BLOCK):
            rs = slice(r * FFN_ROW_BLOCK, (r + 1) * FFN_ROW_BLOCK)
            if first:
                h = _modulated(x_rows(rs), g_ref[...], mod_ref[rows[0]:rows[0] + 1, :],
                               mod_ref[rows[1]:rows[1] + 1, :]).astype(bf16)
                h_scr[rs, :] = h
            else:
                h = h_scr[rs, :]
            g = jnp.dot(h, wg, preferred_element_type=f32)
            u = jnp.dot(h, wu, preferred_element_type=f32)
            part = jnp.dot((_silu(g) * u).astype(bf16), wd, preferred_element_type=f32)
            acc = part if first else acc_ref[rs, :] + part
            if last:
                xn = x_rows(rs) + (0.5 * mod_ref[rows[2]:rows[2] + 1, :]) * acc
                if final:
                    xn = xn * lax.rsqrt(jnp.mean(xn * xn, axis=-1, keepdims=True) + EPS) * gf_ref[...]
                acc_ref[rs, :] = xn
            else:
                acc_ref[rs, :] = acc

    owners = [(is_prompt, o_refs[0]), (jnp.logical_not(is_prompt), o_refs[1])] if split_out else [(True, o_refs[0])]
    for owns, o_ref in owners:
        pl.when((j == 0) & owns)(functools.partial(step, True, False, o_ref))
        pl.when((j > 0) & (j < last_j) & owns)(functools.partial(step, False, False, o_ref))
        pl.when((j == last_j) & owns)(functools.partial(step, False, True, o_ref))


def _ffn_call(xs, mod_l, g_row, ffn_wg, ffn_wu, ffn_wd, layer, half, gf_row, rows, final, split_out, tm, tf=256):
    split_in = len(xs) == 2
    x_specs = _split_specs(tm, D_MODEL, 2) if split_in else [pl.BlockSpec((tm, D_MODEL), lambda i, j: (i, 0))]
    n_token_windows = 2 + split_in + split_out
    vmem_bytes = (2 * n_token_windows * tm * D_MODEL * 4 + tm * D_MODEL * 2 + 2 * 3 * D_MODEL * tf * 4
                  + FFN_VMEM_SLACK)
    if split_out:
        out_specs = _split_specs(tm, D_MODEL, 2)
        out_shape = [jax.ShapeDtypeStruct((N_PROMPT, D_MODEL), f32), jax.ShapeDtypeStruct((N_SAMPLE, D_MODEL), f32)]
    else:
        out_specs = [pl.BlockSpec((tm, D_MODEL), lambda i, j: (i, 0))]
        out_shape = [jax.ShapeDtypeStruct((N_TOK, D_MODEL), f32)]
    scratch = [pltpu.VMEM((tm, D_MODEL), bf16)]
    return pl.pallas_call(
        functools.partial(_ffn_kernel, rows=rows, final=final, split_in=split_in, split_out=split_out, tm=tm),
        grid=(N_TOK // tm, D_FF // tf),
        in_specs=x_specs + [
            pl.BlockSpec((None, N_ADA, D_MODEL), lambda i, j: (_mod_row_index(i, tm), 0, 0)),
            pl.BlockSpec((1, D_MODEL), lambda i, j: (0, 0)),
            pl.BlockSpec((None, None, D_MODEL, tf), lambda i, j: (layer, half, 0, j)),
            pl.BlockSpec((None, None, D_MODEL, tf), lambda i, j: (layer, half, 0, j)),
            pl.BlockSpec((None, None, tf, D_MODEL), lambda i, j: (layer, half, j, 0)),
            pl.BlockSpec((1, D_MODEL), lambda i, j: (0, 0))],
        out_specs=out_specs, out_shape=out_shape, scratch_shapes=scratch,
        compiler_params=pltpu.CompilerParams(dimension_semantics=("parallel", "arbitrary"),
                                             vmem_limit_bytes=vmem_bytes),
        name="ffn",
    )(*xs, mod_l, g_row, ffn_wg, ffn_wu, ffn_wd, gf_row)


def _proj_kernel(*refs, rows, with_gates, w_transposed):
    if with_gates:
        x_ref, mod_ref, g_ref, w_ref, wgate_ref, o_ref, og_ref, h_scr = refs
    else:
        x_ref, mod_ref, g_ref, w_ref, o_ref, h_scr = refs
    j = pl.program_id(1)
    tm, tn = x_ref.shape[0], o_ref.shape[1]
    dims = NT_DIMS if w_transposed else (((1,), (0,)), ((), ()))

    def matmul(h, w):
        return lax.dot_general(h, w, dims, preferred_element_type=f32)

    def w_tile(t):
        return (w_ref[t * tn:(t + 1) * tn, :] if w_transposed else w_ref[:, t * tn:(t + 1) * tn]).astype(bf16)

    @pl.when(j == 0)
    def _():
        w = w_tile(0)
        for r in range(tm // PROJ_ROW_BLOCK):
            rs = slice(r * PROJ_ROW_BLOCK, (r + 1) * PROJ_ROW_BLOCK)
            hb = _modulated(x_ref[rs, :], g_ref[...], mod_ref[rows[0]:rows[0] + 1, :],
                            mod_ref[rows[1]:rows[1] + 1, :]).astype(bf16)
            h_scr[rs, :] = hb
            o_ref[rs, :] = matmul(hb, w).astype(o_ref.dtype)
            if with_gates:
                og_ref[rs, :] = matmul(hb, wgate_ref[...].astype(bf16))

    n_tiles = (w_ref.shape[0] if w_transposed else w_ref.shape[1]) // tn
    for t in range(1, n_tiles):
        @pl.when(j == t)
        def _(t=t):
            o_ref[...] = matmul(h_scr[...], w_tile(t)).astype(o_ref.dtype)


AB_I0 = 2 * H_A * DQK_A + 2 * H_A * DV_A
AB_D0 = AB_I0 + 4 * H_A
AB_B0 = AB_D0 + 2 * H_B * DK_B + 2 * H_B * DV_B
AB_IN = AB_B0 + 4 * H_B


def _ab_weight_kernel(w_ref, wm_ref, wg_ref):
    n_gate = 2 * H_A
    wm_ref[:AB_I0, :] = w_ref[:AB_I0, :]
    wm_ref[AB_I0:, :] = w_ref[AB_D0:AB_B0, :]
    wg_ref[:2 * n_gate, :] = w_ref[AB_I0:AB_D0, :]
    wg_ref[2 * n_gate:4 * n_gate, :] = w_ref[AB_B0:, :]
    wg_ref[4 * n_gate:, :] = jnp.zeros((LANES - 4 * n_gate, w_ref.shape[1]), f32)


def _ab_weight_call(ab_w_in, layer, tk=256):
    w_t = jnp.swapaxes(ab_w_in, 1, 2)
    return pl.pallas_call(
        _ab_weight_kernel,
        grid=(D_MODEL // tk,),
        in_specs=[pl.BlockSpec((None, AB_IN, tk), lambda i: (layer, 0, i))],
        out_specs=[pl.BlockSpec((AB_MAIN, tk), lambda i: (0, i)), pl.BlockSpec((LANES, tk), lambda i: (0, i))],
        out_shape=[jax.ShapeDtypeStruct((AB_MAIN, D_MODEL), f32), jax.ShapeDtypeStruct((LANES, D_MODEL), f32)],
        compiler_params=pltpu.CompilerParams(vmem_limit_bytes=VMEM_LIMIT),
        name="ab_weights",
    )(w_t)


def _column_tile(n, cap):
    return max(t for t in range(LANES, cap + 1, LANES) if n % t == 0)


def _proj_call(x, mod_l, g_row, w, w_gate, rows, out_dtype, w_transposed, tm=1024):
    n = w.shape[0] if w_transposed else w.shape[1]
    tn = _column_tile(n, n)
    with_gates = w_gate is not None
    w_spec = pl.BlockSpec(w.shape, lambda i, j: (0, 0), pipeline_mode=pl.Buffered(1))
    in_specs = [pl.BlockSpec((tm, D_MODEL), lambda i, j: (i, 0)),
                pl.BlockSpec((None, N_ADA, D_MODEL), lambda i, j: (_mod_row_index(i, tm), 0, 0)),
                pl.BlockSpec((1, D_MODEL), lambda i, j: (0, 0)),
                w_spec]
    out_specs = [pl.BlockSpec((tm, tn), lambda i, j: (i, j))]
    out_shape = [jax.ShapeDtypeStruct((N_TOK, n), out_dtype)]
    args = [x, mod_l, g_row, w]
    if with_gates:
        in_specs.append(pl.BlockSpec(w_gate.shape, lambda i, j: (0, 0)))
        out_specs.append(pl.BlockSpec((tm, LANES), lambda i, j: (i, 0)))
        out_shape.append(jax.ShapeDtypeStruct((N_TOK, LANES), f32))
        args.append(w_gate)
    return pl.pallas_call(
        functools.partial(_proj_kernel, rows=rows, with_gates=with_gates, w_transposed=w_transposed),
        grid=(N_TOK // tm, n // tn),
        in_specs=in_specs, out_specs=out_specs, out_shape=out_shape,
        scratch_shapes=[pltpu.VMEM((tm, D_MODEL), bf16)],
        compiler_params=pltpu.CompilerParams(dimension_semantics=("parallel", "arbitrary"),
                                             vmem_limit_bytes=VMEM_LIMIT),
        name="in_proj",
    )(*args)


def _chunk_masks():
    r = lax.broadcasted_iota(jnp.int32, (CHUNK, CHUNK), 0)
    c = lax.broadcasted_iota(jnp.int32, (CHUNK, CHUNK), 1)
    return r >= c, r <= c, r > c, r < c


def _lane_col(x, lane, j):
    return jnp.sum(jnp.where(lane == j, x, 0.0), axis=1, keepdims=True)


def _head_lane_select(base, pair):
    r = lax.broadcasted_iota(jnp.int32, (SUBLANES, LANES), 0)
    ln = lax.broadcasted_iota(jnp.int32, (SUBLANES, LANES), 1)
    return jnp.where((ln == base + 2 * pair + r) & (r < 2), 1.0, 0.0)


def _split3_bf16(x):
    hi = x.astype(bf16)
    r = x - hi.astype(f32)
    mid = r.astype(bf16)
    return hi, mid, (r - mid.astype(f32)).astype(bf16)


def _thrice(a16):
    return jnp.concatenate([a16] * 3, axis=1)


def _mask_matmul_f32(mask16x3, x):
    return jnp.dot(mask16x3, jnp.concatenate(_split3_bf16(x), axis=0), preferred_element_type=f32)


def _select_rows_f32(sel16x3, x):
    return lax.dot_general(sel16x3, jnp.concatenate(_split3_bf16(x), axis=1), NT_DIMS, preferred_element_type=f32)


def _chunks_per_trip(nc):
    return min(MAX_CHUNKS_PER_TRIP, nc)


def _chunk_start(trip, u, d, nc):
    step = trip * _chunks_per_trip(nc) + u
    c = step if d == 0 else nc - 1 - step
    return pl.multiple_of(c * CHUNK, CHUNK)


def _mlstm_kernel(*refs, seq, zero_init):
    if zero_init:
        qp_ref, kp_ref, v0_ref, v1_ref, g_ref, par_ref, h_ref, cf_ref, nf_ref, mf_ref = refs
    else:
        (qp_ref, kp_ref, v0_ref, v1_ref, g_ref, par_ref, c0_ref, n0_ref, m0_ref,
         h_ref, cf_ref, nf_ref, mf_ref) = refs
    nc = seq // CHUNK
    pair = pl.program_id(1)
    lane = lax.broadcasted_iota(jnp.int32, (1, LANES), 1)
    tril, triu, _, _ = _chunk_masks()
    masks = (tril, triu)
    masks16 = (_thrice(tril.astype(bf16)), _thrice(triu.astype(bf16)))
    sel = [_thrice(_head_lane_select(LANE_F + d * H_A, pair).astype(bf16)) for d in range(2)]
    bi_row = par_ref[0:1, :]
    bf_row = par_ref[1:2, :]
    ones_col = jnp.where(lane == 0, 1.0, 0.0) + jnp.zeros((CHUNK, LANES), f32)
    v_refs = (v0_ref, v1_ref)
    streams = [(hh, d) for hh in range(2) for d in range(2)]

    h_ref[...] = jnp.zeros_like(h_ref)

    init = []
    for hh, d in streams:
        if zero_init:
            init.append((jnp.zeros((DQK_A, 2 * LANES), f32), jnp.zeros((1, 1), f32)))
        else:
            n_aug = jnp.where(lane == 0, n0_ref[d, hh], 0.0)
            init.append((jnp.concatenate([c0_ref[d, hh], n_aug], axis=1), m0_ref[d, hh]))

    def body(trip, carry):
        subs = range(_chunks_per_trip(nc))
        shared = {}
        for u in subs:
            for d in range(2):
                r0 = _chunk_start(trip, u, d, nc)
                gates = g_ref[pl.ds(r0, CHUNK), :]
                gi = gates + bi_row
                gf = _log_sigmoid(gates + bf_row)
                cum = _mask_matmul_f32(masks16[d], gf)
                stack = jnp.concatenate([cum, pltpu.roll(gi, LANE_F - LANE_I, axis=1)], axis=0)
                rows = _select_rows_f32(sel[d], stack)
                total = cum[CHUNK - 1:CHUNK, :] if d == 0 else cum[0:1, :]
                shared[u, d] = (r0, gi, cum, rows, total)
        chains = [(u, hh, d) for u in subs for hh, d in streams]
        st = {ch: {} for ch in chains}
        for ch in chains:
            u, hh, d = ch
            r0 = shared[u, d][0]
            s = st[ch]
            s["q"] = (qp_ref[pl.ds(r0, CHUNK), hh * DQK_A:(hh + 1) * DQK_A] * DQK_A ** -0.5).astype(bf16)
            s["k"] = kp_ref[pl.ds(r0, CHUNK), hh * DQK_A:(hh + 1) * DQK_A].astype(bf16)
            s["v_aug"] = jnp.concatenate([v_refs[hh][pl.ds(r0, CHUNK), :].astype(f32), ones_col], axis=1)
            s["qk"] = lax.dot_general(s["q"], s["k"], NT_DIMS, preferred_element_type=f32)
        for ch in chains:
            u, hh, d = ch
            _, gi, cum, rows, total = shared[u, d]
            s = st[ch]
            head = 2 * pair + hh
            jf = LANE_F + d * H_A + head
            ji = LANE_I + d * H_A + head
            bcol = _lane_col(cum, lane, jf)
            icol = _lane_col(gi, lane, ji)
            dmat = jnp.where(masks[d], bcol - rows[hh:hh + 1, :CHUNK] + rows[hh:hh + 1, CHUNK:], -jnp.inf)
            m_loc = jnp.max(dmat, axis=1, keepdims=True)
            s["sw"] = (s["qk"] * jnp.exp(dmat - m_loc)).astype(bf16)
            blast = _lane_col(total, lane, jf)
            gs = blast - bcol + icol
            ms_loc = jnp.max(gs, axis=0, keepdims=True)
            s["wv"] = (jnp.exp(gs - ms_loc) * s["v_aug"]).astype(bf16)
            s.update(bcol=bcol, m_loc=m_loc, blast=blast, ms_loc=ms_loc)
        for ch in chains:
            s = st[ch]
            s["num"] = jnp.dot(s["sw"], s["v_aug"].astype(bf16), preferred_element_type=f32)
            s["kv"] = lax.dot_general(s["k"], s["wv"], TN_DIMS, preferred_element_type=f32)
        state = list(carry)
        for u in subs:
            qc = [jnp.dot(st[u, hh, d]["q"], state[i][0].astype(bf16), preferred_element_type=f32)
                  for i, (hh, d) in enumerate(streams)]
            for i, (hh, d) in enumerate(streams):
                s = st[u, hh, d]
                c_aug, m = state[i]
                m_inter = s["bcol"] + m
                m_t = jnp.maximum(m_inter, s["m_loc"])
                nd = jnp.exp(s["m_loc"] - m_t) * s["num"] + jnp.exp(m_inter - m_t) * qc[i]
                den = nd[:, DV_A:DV_A + 1]
                hval = nd[:, :DV_A] / jnp.maximum(jnp.abs(den), jnp.exp(-m_t))
                r0 = shared[u, d][0]
                h_ref[pl.ds(r0, CHUNK), hh * DV_A:(hh + 1) * DV_A] += hval
                m_new = jnp.maximum(s["blast"] + m, s["ms_loc"])
                c_new = jnp.exp(s["blast"] + m - m_new) * c_aug + jnp.exp(s["ms_loc"] - m_new) * s["kv"]
                state[i] = (c_new, m_new)
        return tuple(state)

    final = lax.fori_loop(0, nc // _chunks_per_trip(nc), body, tuple(init))
    for i, (hh, d) in enumerate(streams):
        c_aug, m = final[i]
        cf_ref[d, hh] = c_aug[:, :DV_A]
        nf_ref[d, hh] = c_aug[:, DV_A:DV_A + 1]
        mf_ref[d, hh] = m


def _mlstm_call(proj, gates, par, nb, seq, off, state):
    zero_init = state is None
    blk = lambda col: pl.BlockSpec((seq, LANES), col)
    in_specs = [blk(lambda b, p: (off + b, p)),
                blk(lambda b, p: (off + b, 2 + p)),
                blk(lambda b, p: (off + b, 4 + 2 * p)),
                blk(lambda b, p: (off + b, 5 + 2 * p)),
                blk(lambda b, p: (off + b, 0)),
                pl.BlockSpec((SUBLANES, LANES), lambda b, p: (0, 0))]
    st_specs = [pl.BlockSpec((None, 2, 2, DQK_A, DV_A), lambda b, p: (b, 0, p, 0, 0)),
                pl.BlockSpec((None, 2, 2, DQK_A, 1), lambda b, p: (b, 0, p, 0, 0)),
                pl.BlockSpec((None, 2, 2, 1, 1), lambda b, p: (b, 0, p, 0, 0))]
    args = [proj, proj, proj, proj, gates, par]
    if not zero_init:
        in_specs += st_specs
        args += list(state)
    return pl.pallas_call(
        functools.partial(_mlstm_kernel, seq=seq, zero_init=zero_init),
        grid=(nb, H_A // 2),
        in_specs=in_specs,
        out_specs=[pl.BlockSpec((seq, 2 * DV_A), lambda b, p: (b, p))] + st_specs,
        out_shape=[jax.ShapeDtypeStruct((nb * seq, H_A * DV_A), f32),
                   jax.ShapeDtypeStruct((nb, 2, H_A, DQK_A, DV_A), f32),
                   jax.ShapeDtypeStruct((nb, 2, H_A, DQK_A, 1), f32),
                   jax.ShapeDtypeStruct((nb, 2, H_A, 1, 1), f32)],
        compiler_params=pltpu.CompilerParams(vmem_limit_bytes=VMEM_LIMIT),
        name="mlstm",
    )(*args)


CONV_PAD = SUBLANES


def _short_conv_silu(x, w_ref, pad_scr, seq):
    pad_scr[:CONV_PAD, :] = jnp.zeros((CONV_PAD, LANES), f32)
    pad_scr[CONV_PAD + seq:, :] = jnp.zeros((CONV_PAD, LANES), f32)
    pad_scr[CONV_PAD:CONV_PAD + seq, :] = x
    acc = x * w_ref[CONV_K // 2:CONV_K // 2 + 1, :]
    for tap in range(CONV_K):
        delta = tap - CONV_K // 2
        if delta != 0:
            acc = acc + pad_scr[CONV_PAD + delta:CONV_PAD + delta + seq, :] * w_ref[tap:tap + 1, :]
    return _silu(acc)


def _l2_unit(y):
    return y * lax.rsqrt(jnp.sum(y * y, axis=-1, keepdims=True) + EPS)


def _split_bf16(x):
    hi = x.astype(bf16)
    return hi, (x - hi.astype(f32)).astype(bf16)


def _matmul_3pass(m, x):
    mh, ml = _split_bf16(m)
    xh, xl = _split_bf16(x)
    return jnp.dot(jnp.concatenate([mh, ml, mh], axis=1), jnp.concatenate([xh, xh, xl], axis=0),
                   preferred_element_type=f32)


def _unit_triangular_solves(ns, xs):
    levels = CHUNK.bit_length() - 1
    r = lax.broadcasted_iota(jnp.int32, (CHUNK, CHUNK), 0)
    c = lax.broadcasted_iota(jnp.int32, (CHUNK, CHUNK), 1)
    eye = jnp.where(r == c, 1.0, 0.0)
    ts = [eye + n for n in ns]
    ms = [_matmul_3pass(n, n) for n in ns]
    for lvl in range(1, levels):
        for i in range(len(ns)):
            if lvl < levels - 1:
                prod = _matmul_3pass(ms[i], jnp.concatenate([ts[i], ms[i]], axis=1))
                ts[i] = ts[i] + prod[:, :CHUNK]
                ms[i] = prod[:, CHUNK:]
            else:
                ts[i] = ts[i] + _matmul_3pass(ms[i], ts[i])
    return [_matmul_3pass(t, x) for t, x in zip(ts, xs)]


def _delta_kernel(*refs, seq, zero_init):
    if zero_init:
        (q0_ref, q1_ref, k0_ref, k1_ref, v0_ref, v1_ref, wq0_ref, wq1_ref, wk0_ref, wk1_ref, wv0_ref, wv1_ref,
         g_ref, par_ref, o_ref, sf_ref, q_scr, k_scr, v_scr, pad_scr) = refs
    else:
        (q0_ref, q1_ref, k0_ref, k1_ref, v0_ref, v1_ref, wq0_ref, wq1_ref, wk0_ref, wk1_ref, wv0_ref, wv1_ref,
         g_ref, par_ref, s0_ref, o_ref, sf_ref, q_scr, k_scr, v_scr, pad_scr) = refs
    nc = seq // CHUNK
    pair = pl.program_id(1)
    lane = lax.broadcasted_iota(jnp.int32, (1, LANES), 1)
    tril, triu, stril, striu = _chunk_masks()
    masks, smasks = (tril, triu), (stril, striu)
    masks16 = (_thrice(tril.astype(bf16)), _thrice(triu.astype(bf16)))
    sel = [_thrice(_head_lane_select(LANE_A + d * H_B, pair).astype(bf16)) for d in range(2)]
    neg_a_row = -jnp.exp(par_ref[0:1, :])
    dt_row = par_ref[1:2, :]
    streams = [(hh, d) for hh in range(2) for d in range(2)]

    for hh, (q_ref, k_ref, v_ref, wq_ref, wk_ref, wv_ref) in enumerate(
            ((q0_ref, k0_ref, v0_ref, wq0_ref, wk0_ref, wv0_ref), (q1_ref, k1_ref, v1_ref, wq1_ref, wk1_ref, wv1_ref))):
        q_scr[hh] = _l2_unit(_short_conv_silu(q_ref[...].astype(f32), wq_ref, pad_scr, seq)) * DK_B ** -0.5
        k_scr[hh] = _l2_unit(_short_conv_silu(k_ref[...].astype(f32), wk_ref, pad_scr, seq))
        v_scr[hh] = _short_conv_silu(v_ref[...].astype(f32), wv_ref, pad_scr, seq)

    o_ref[...] = jnp.zeros_like(o_ref)
    init = tuple(jnp.zeros((DK_B, DV_B), f32) if zero_init else s0_ref[d, hh] for hh, d in streams)

    def body(trip, carry):
        subs = range(_chunks_per_trip(nc))
        shared = {}
        for u in subs:
            for d in range(2):
                r0 = _chunk_start(trip, u, d, nc)
                gates = g_ref[pl.ds(r0, CHUNK), :]
                beta_all = jax.nn.sigmoid(gates)
                glog = neg_a_row * _softplus(gates + dt_row)
                cum = _mask_matmul_f32(masks16[d], glog)
                rows = _select_rows_f32(sel[d], cum)
                shared[u, d] = (r0, beta_all, cum, rows)
        chains = [(u, hh, d) for u in subs for hh, d in streams]
        st = {ch: {} for ch in chains}
        for ch in chains:
            u, hh, d = ch
            r0, beta_all, cum, rows = shared[u, d]
            s = st[ch]
            head = 2 * pair + hh
            beta = _lane_col(beta_all, lane, LANE_BETA + d * H_B + head)
            gcol = _lane_col(cum, lane, LANE_A + d * H_B + head)
            glast = gcol[CHUNK - 1:CHUNK, :] if d == 0 else gcol[0:1, :]
            q = q_scr[hh, pl.ds(r0, CHUNK), :]
            k = k_scr[hh, pl.ds(r0, CHUNK), :]
            v = v_scr[hh, pl.ds(r0, CHUNK), :]
            k16 = k.astype(bf16)
            kbeta = k * beta
            eg = jnp.exp(gcol)
            decay = jnp.exp(jnp.where(masks[d], gcol - rows[hh:hh + 1, :], -jnp.inf))
            kk = lax.dot_general(kbeta.astype(bf16), k16, NT_DIMS, preferred_element_type=f32)
            qk = lax.dot_general(q.astype(bf16), k16, NT_DIMS, preferred_element_type=f32)
            s["n"] = -jnp.where(smasks[d], kk * decay, 0.0)
            s["x"] = jnp.concatenate([v * beta, kbeta * eg], axis=1)
            s["qk"] = (qk * decay).astype(bf16)
            s["qg"] = (q * eg).astype(bf16)
            s["kd_t"] = (k * jnp.exp(glast - gcol)).T.astype(bf16)
            s["gl"] = jnp.exp(glast)
        solved = _unit_triangular_solves([st[ch]["n"] for ch in chains], [st[ch]["x"] for ch in chains])
        for ch, uw in zip(chains, solved):
            s = st[ch]
            s["u"] = uw[:, :DV_B]
            s["w_qg"] = jnp.concatenate([uw[:, DV_B:].astype(bf16), s["qg"]], axis=0)
        state = list(carry)
        for u in subs:
            ws = [jnp.dot(st[u, hh, d]["w_qg"], state[i].astype(bf16), preferred_element_type=f32)
                  for i, (hh, d) in enumerate(streams)]
            v_new = [(st[u, hh, d]["u"] - ws[i][:CHUNK]).astype(bf16) for i, (hh, d) in enumerate(streams)]
            for i, (hh, d) in enumerate(streams):
                s = st[u, hh, d]
                o = ws[i][CHUNK:] + jnp.dot(s["qk"], v_new[i], preferred_element_type=f32)
                r0 = shared[u, d][0]
                o_ref[pl.ds(r0, CHUNK), hh * DV_B:(hh + 1) * DV_B] += o
                state[i] = state[i] * s["gl"] + jnp.dot(s["kd_t"], v_new[i], preferred_element_type=f32)
        return tuple(state)

    final = lax.fori_loop(0, nc // _chunks_per_trip(nc), body, init)
    for i, (hh, d) in enumerate(streams):
        sf_ref[d, hh] = final[i]


def _delta_call(proj, conv_w8, gates, par, nb, seq, off, state):
    zero_init = state is None
    col0 = (2 * H_A * DQK_A + 2 * H_A * DV_A) // LANES
    blk = lambda col: pl.BlockSpec((seq, LANES), col)
    wblk = lambda col: pl.BlockSpec((SUBLANES, LANES), col)
    in_specs = [blk(lambda b, p: (off + b, col0 + 2 * p)), blk(lambda b, p: (off + b, col0 + 2 * p + 1)),
                blk(lambda b, p: (off + b, col0 + H_B + 2 * p)), blk(lambda b, p: (off + b, col0 + H_B + 2 * p + 1)),
                blk(lambda b, p: (off + b, col0 + 2 * H_B + 2 * p)), blk(lambda b, p: (off + b, col0 + 2 * H_B + 2 * p + 1)),
                wblk(lambda b, p: (0, 2 * p)), wblk(lambda b, p: (0, 2 * p + 1)),
                wblk(lambda b, p: (0, H_B + 2 * p)), wblk(lambda b, p: (0, H_B + 2 * p + 1)),
                wblk(lambda b, p: (0, 2 * H_B + 2 * p)), wblk(lambda b, p: (0, 2 * H_B + 2 * p + 1)),
                blk(lambda b, p: (off + b, 0)),
                pl.BlockSpec((SUBLANES, LANES), lambda b, p: (0, 0))]
    st_spec = pl.BlockSpec((None, 2, 2, DK_B, DV_B), lambda b, p: (b, 0, p, 0, 0))
    args = [proj] * 6 + [conv_w8] * 6 + [gates, par]
    if not zero_init:
        in_specs.append(st_spec)
        args.append(state)
    return pl.pallas_call(
        functools.partial(_delta_kernel, seq=seq, zero_init=zero_init),
        grid=(nb, H_B // 2),
        in_specs=in_specs,
        out_specs=[pl.BlockSpec((seq, 2 * DV_B), lambda b, p: (b, p)), st_spec],
        out_shape=[jax.ShapeDtypeStruct((nb * seq, H_B * DV_B), f32),
                   jax.ShapeDtypeStruct((nb, 2, H_B, DK_B, DV_B), f32)],
        scratch_shapes=[pltpu.VMEM((2, seq, LANES), f32)] * 3 + [pltpu.VMEM((seq + 2 * CONV_PAD, LANES), f32)],
        compiler_params=pltpu.CompilerParams(vmem_limit_bytes=VMEM_LIMIT),
        name="delta",
    )(*args)


def _head_rms(x):
    return x * lax.rsqrt(jnp.mean(x * x, axis=-1, keepdims=True) + EPS)


def _ab_out_kernel(x_ref, hp_ref, hs_ref, op_ref, os_ref, om_ref, zd_ref, gm_ref, gd_ref, w_ref, mod_ref, o_ref,
                   *, gate_row, tm):
    hsum = _prompt_or_sample(hp_ref, hs_ref, tm)
    osum = _prompt_or_sample(op_ref, os_ref, tm)
    parts = []
    for h in range(H_A):
        sl = slice(h * DV_A, (h + 1) * DV_A)
        parts.append(_head_rms(hsum[:, sl]) * gm_ref[:, sl] * jax.nn.sigmoid(om_ref[:, sl].astype(f32)))
    for h in range(H_B):
        sl = slice(h * DV_B, (h + 1) * DV_B)
        parts.append(_head_rms(osum[:, sl]) * gd_ref[:, sl] * _silu(zd_ref[:, sl].astype(f32)))
    cat = jnp.concatenate(parts, axis=1).astype(bf16)
    y = jnp.dot(cat, w_ref[...].astype(bf16), preferred_element_type=f32)
    o_ref[...] = x_ref[...] + mod_ref[gate_row:gate_row + 1, :] * y


def _ab_out_call(x, hp, hs, op, os_, proj, gm_row, gd_row, w_out, mod_l, gate_row, tm=1024):
    wide = H_A * DV_A
    om_blk = (2 * H_A * DQK_A) // wide + 1
    zd_blk = AB_MAIN // wide - 1
    return pl.pallas_call(
        functools.partial(_ab_out_kernel, gate_row=gate_row, tm=tm),
        grid=(N_TOK // tm,),
        in_specs=[pl.BlockSpec((tm, D_MODEL), lambda i: (i, 0))] + _split_specs(tm, wide) + _split_specs(tm, wide) + [
            pl.BlockSpec((tm, wide), lambda i: (i, om_blk)),
            pl.BlockSpec((tm, wide), lambda i: (i, zd_blk)),
            pl.BlockSpec((1, wide), lambda i: (0, 0)),
            pl.BlockSpec((1, wide), lambda i: (0, 0)),
            pl.BlockSpec((2 * wide, D_MODEL), lambda i: (0, 0)),
            pl.BlockSpec((None, N_ADA, D_MODEL), lambda i: (_mod_row_index(i, tm), 0, 0))],
        out_specs=pl.BlockSpec((tm, D_MODEL), lambda i: (i, 0)),
        out_shape=jax.ShapeDtypeStruct((N_TOK, D_MODEL), f32),
        compiler_params=pltpu.CompilerParams(vmem_limit_bytes=VMEM_LIMIT),
        name="ab_out",
    )(x, hp, hs, op, os_, proj, proj, gm_row, gd_row, w_out, mod_l)


def _na_out_kernel(x_ref, ap_ref, as_ref, w_ref, mod_ref, o_ref, *, gate_row, tm):
    a = _prompt_or_sample(ap_ref, as_ref, tm).astype(bf16)
    y = jnp.dot(a, w_ref[...].astype(bf16), preferred_element_type=f32)
    o_ref[...] = x_ref[...] + mod_ref[gate_row:gate_row + 1, :] * y


def _na_out_call(x, attn_p, attn_s, w_out, mod_l, gate_row, tm=1024):
    return pl.pallas_call(
        functools.partial(_na_out_kernel, gate_row=gate_row, tm=tm),
        grid=(N_TOK // tm,),
        in_specs=[pl.BlockSpec((tm, D_MODEL), lambda i: (i, 0))] + _split_specs(tm, NA_W) + [
            pl.BlockSpec((NA_W, D_MODEL), lambda i: (0, 0)),
            pl.BlockSpec((None, N_ADA, D_MODEL), lambda i: (_mod_row_index(i, tm), 0, 0))],
        out_specs=pl.BlockSpec((tm, D_MODEL), lambda i: (i, 0)),
        out_shape=jax.ShapeDtypeStruct((N_TOK, D_MODEL), f32),
        compiler_params=pltpu.CompilerParams(vmem_limit_bytes=VMEM_LIMIT),
        name="na_out",
    )(x, attn_p, attn_s, w_out, mod_l)


CTX_HEADS = 8


def _ctx_attn_kernel(q_ref, k_ref, v_ref, o_ref, nk_ref, nv_ref):
    low = lax.broadcasted_iota(jnp.int32, (1, LANES), 1) < DH_C
    def pair_scores(pair):
        sl = slice(pair * LANES, (pair + 1) * LANES)
        q = q_ref[:, sl] * DH_C ** -0.5
        k_t = k_ref[:, sl].T
        v_t = v_ref[:, sl].T
        for hh in range(2):
            nk_ref[2 * pair + hh] = k_t[hh * DH_C:(hh + 1) * DH_C, :]
            nv_ref[2 * pair + hh] = v_t[hh * DH_C:(hh + 1) * DH_C, :]
        k = k_ref[:, sl].astype(bf16)
        scores = [lax.dot_general(jnp.where(low if hh == 0 else jnp.logical_not(low), q, 0.0).astype(bf16), k,
                                  NT_DIMS, preferred_element_type=f32) for hh in range(2)]
        return sl, scores

    def pair_finish(sl, scores):
        v = v_ref[:, sl].astype(bf16)
        probs = [jnp.exp(s - jnp.max(s, axis=1, keepdims=True)) for s in scores]
        outs = [jnp.dot(p.astype(bf16), v, preferred_element_type=f32) / jnp.sum(p, axis=1, keepdims=True)
                for p in probs]
        o_ref[:, sl] = jnp.where(low, outs[0], outs[1]).astype(o_ref.dtype)

    n_pairs = CTX_HEADS // 2
    pending = pair_scores(0)
    for pair in range(n_pairs):
        upcoming = pair_scores(pair + 1) if pair + 1 < n_pairs else None
        pair_finish(*pending)
        pending = upcoming


def _ctx_attn_call(proj):
    w = CTX_HEADS * DH_C
    nblk = NA_W // w
    return pl.pallas_call(
        _ctx_attn_kernel,
        grid=(BATCH, nblk),
        in_specs=[pl.BlockSpec((SEQ, w), lambda b, j: (b, j)),
                  pl.BlockSpec((SEQ, w), lambda b, j: (b, nblk + j)),
                  pl.BlockSpec((SEQ, w), lambda b, j: (b, 2 * nblk + j))],
        out_specs=[pl.BlockSpec((SEQ, w), lambda b, j: (b, j)),
                   pl.BlockSpec((None, CTX_HEADS, DH_C, SEQ), lambda b, j: (b, j, 0, 0)),
                   pl.BlockSpec((None, CTX_HEADS, DH_C, SEQ), lambda b, j: (b, j, 0, 0))],
        out_shape=[jax.ShapeDtypeStruct((N_PROMPT, NA_W), bf16),
                   jax.ShapeDtypeStruct((BATCH, H_C, DH_C, SEQ), f32),
                   jax.ShapeDtypeStruct((BATCH, H_C, DH_C, SEQ), f32)],
        compiler_params=pltpu.CompilerParams(vmem_limit_bytes=VMEM_LIMIT),
        name="ctx_attn",
    )(proj, proj, proj)


QROWS = 4
QBLK = QROWS * GRID_W
KROWS_MID = QROWS + WIN_R - 1
N_RIDX = 2 * WIN_R - 1
N_CIDX = 2 * WIN_C - 1
N_QBLK = GRID_ROWS // QROWS


def _nattn_bias_tables(rb_ref, tb_scr, bmid_scr, btop_scr, bbot_scr):
    qc = lax.broadcasted_iota(jnp.int32, (GRID_W, GRID_W), 0)
    kc = lax.broadcasted_iota(jnp.int32, (GRID_W, GRID_W), 1)
    cs = jnp.clip(qc - WIN_C // 2, 0, GRID_W - WIN_C)
    valid = (kc >= cs) & (kc < cs + WIN_C)
    neg = jnp.full((GRID_W, GRID_W), -jnp.inf, f32)
    for hh in range(2):
        for ri in range(N_RIDX):
            row = jnp.broadcast_to(rb_ref[hh, ri:ri + 1, :], (GRID_W, LANES))
            tile = pltpu.roll(row, LANES - (WIN_C - 1), 1, stride=1, stride_axis=0)[:, :GRID_W]
            tb_scr[ri] = jnp.where(valid, tile, -jnp.inf)
        for i in range(QROWS):
            rs = slice(i * GRID_W, (i + 1) * GRID_W)
            for jj in range(KROWS_MID):
                inside = 0 <= jj - i < WIN_R
                bmid_scr[hh, rs, jj * GRID_W:(jj + 1) * GRID_W] = tb_scr[jj - i + WIN_R // 2 - 1] if inside else neg
            for jj in range(WIN_R):
                btop_scr[hh, rs, jj * GRID_W:(jj + 1) * GRID_W] = tb_scr[jj - i + WIN_R - 1]
                bbot_scr[hh, rs, jj * GRID_W:(jj + 1) * GRID_W] = tb_scr[jj - i + WIN_R // 2 - 1]


def _nattn_kernel(rb_ref, q_ref, k_ref, v_ref, ck_ref, cv_ref, o_ref, tb_scr, bmid_scr, btop_scr, bbot_scr):
    scale = DH_C ** -0.5

    @pl.when(pl.program_id(1) == 0)
    def _():
        _nattn_bias_tables(rb_ref, tb_scr, bmid_scr, btop_scr, bbot_scr)

    low = lax.broadcasted_iota(jnp.int32, (1, LANES), 1) < DH_C
    kctx_t = jnp.concatenate([ck_ref[0], ck_ref[1]], axis=0).astype(bf16)
    vctx = jnp.concatenate([cv_ref[0], cv_ref[1]], axis=0).T.astype(bf16)

    def block_scores(q_start, k_start, k_rows, bias_scr):
        nk = k_rows * GRID_W
        q = q_ref[pl.ds(q_start, QBLK), :] * scale
        ku = k_ref[pl.ds(k_start, nk), :].astype(bf16)
        vu = v_ref[pl.ds(k_start, nk), :].astype(bf16)
        scores = []
        for hh in range(2):
            qh = jnp.where(low if hh == 0 else jnp.logical_not(low), q, 0.0).astype(bf16)
            scores.append((lax.dot_general(qh, ku, NT_DIMS, preferred_element_type=f32) + bias_scr[hh],
                           jnp.dot(qh, kctx_t, preferred_element_type=f32)))
        return q_start, vu, scores

    def block_finish(q_start, vu, scores):
        probs = []
        for s_loc, s_ctx in scores:
            m = jnp.maximum(jnp.max(s_loc, axis=1, keepdims=True), jnp.max(s_ctx, axis=1, keepdims=True))
            p_loc = jnp.exp(s_loc - m)
            p_ctx = jnp.exp(s_ctx - m)
            denom = jnp.sum(p_loc, axis=1, keepdims=True) + jnp.sum(p_ctx, axis=1, keepdims=True)
            probs.append((p_loc.astype(bf16), p_ctx.astype(bf16), denom))
        outs = [(jnp.dot(p_loc, vu, preferred_element_type=f32) + jnp.dot(p_ctx, vctx, preferred_element_type=f32)) / denom
                for p_loc, p_ctx, denom in probs]
        o_ref[pl.ds(q_start, QBLK), :] = jnp.where(low, outs[0], outs[1]).astype(o_ref.dtype)

    blocks = ([(0, 0, WIN_R, btop_scr)]
              + [(b * QBLK, b * QBLK - (WIN_R // 2) * GRID_W, KROWS_MID, bmid_scr) for b in range(1, N_QBLK - 1)]
              + [((N_QBLK - 1) * QBLK, (GRID_ROWS - WIN_R) * GRID_W, WIN_R, bbot_scr)])
    pending = block_scores(*blocks[0])
    for nxt in blocks[1:] + [None]:
        upcoming = block_scores(*nxt) if nxt is not None else None
        block_finish(*pending)
        pending = upcoming


def _nattn_call(proj, cache_k, cache_v, layer, rel_bias, off):
    npair = H_C // 2
    return pl.pallas_call(
        _nattn_kernel,
        grid=(npair, DEC_BATCH),
        in_specs=[pl.BlockSpec((2, 2 * WIN_R, LANES), lambda p, b: (p, 0, 0)),
                  pl.BlockSpec((DEC_SEQ, LANES), lambda p, b: (off + b, p)),
                  pl.BlockSpec((DEC_SEQ, LANES), lambda p, b: (off + b, npair + p)),
                  pl.BlockSpec((DEC_SEQ, LANES), lambda p, b: (off + b, 2 * npair + p)),
                  pl.BlockSpec((None, None, 2, DH_C, PAST_LEN), lambda p, b: (b, layer, p, 0, 0)),
                  pl.BlockSpec((None, None, 2, DH_C, PAST_LEN), lambda p, b: (b, layer, p, 0, 0))],
        out_specs=pl.BlockSpec((DEC_SEQ, LANES), lambda p, b: (b, p)),
        out_shape=jax.ShapeDtypeStruct((N_SAMPLE, NA_W), bf16),
        scratch_shapes=[pltpu.VMEM((N_RIDX, GRID_W, GRID_W), f32),
                        pltpu.VMEM((2, QBLK, KROWS_MID * GRID_W), f32),
                        pltpu.VMEM((2, QBLK, WIN_R * GRID_W), f32),
                        pltpu.VMEM((2, QBLK, WIN_R * GRID_W), f32)],
        compiler_params=pltpu.CompilerParams(dimension_semantics=("arbitrary", "arbitrary"),
                                             vmem_limit_bytes=VMEM_LIMIT),
        name="nattn",
    )(jnp.pad(rel_bias, ((0, 0), (0, 2 * WIN_R - N_RIDX), (0, LANES - N_CIDX))), proj, proj, proj, cache_k, cache_v)


def _lane_row(pieces):
    row = jnp.zeros((LANES,), f32)
    for off, vals in pieces:
        row = row.at[off:off + vals.shape[0]].set(vals.astype(f32))
    return row


def _param_rows(rows):
    out = jnp.zeros((SUBLANES, LANES), f32)
    for r, row in enumerate(rows):
        out = out.at[r].set(row)
    return out


def kernel(x_prompt, x_sample, c, state_mlstm_C, state_mlstm_n, state_mlstm_m, state_delta_S, cache_na_k, cache_na_v, c_ctx, ada_w, ada_b, norm_g, ffn_wg, ffn_wu, ffn_wd, ab_w_in, ab_w_out, mlstm_b_i, mlstm_b_f, mlstm_norm_g, delta_conv_w, delta_a_log, delta_dt_bias, delta_norm_g, na_w_in, na_w_out, na_rel_bias, final_norm_g):
    xs = (x_prompt.reshape(N_PROMPT, D_MODEL), x_sample.reshape(N_SAMPLE, D_MODEL))
    mods = _ada_call(jnp.concatenate([c_ctx[None, :], c], axis=0), ada_w, ada_b)
    mods = mods.reshape(DEPTH, 3, N_ADA, D_MODEL)
    gf_row = final_norm_g.reshape(1, D_MODEL)
    s_off = N_PROMPT // DEC_SEQ
    new_c, new_n, new_m, new_s, new_k, new_v = [], [], [], [], [], []

    for l in range(DEPTH):
        mod_l = mods[l]
        a = l // 2
        x = _ffn_call(xs if l == 0 else (x,), mod_l, norm_g[l, 0].reshape(1, D_MODEL), ffn_wg, ffn_wu, ffn_wd, l, 0,
                      gf_row, rows=(0, 1, 2), final=False, split_out=False, tm=FFN_TM)[0]
        g_mix = norm_g[l, 1].reshape(1, D_MODEL)
        if l % 2 == 0:
            w_main, w_gate = _ab_weight_call(ab_w_in, a)
            proj, gates = _proj_call(x, mod_l, g_mix, w_main, w_gate, rows=(3, 4), out_dtype=bf16, w_transposed=True)

            par_m = _param_rows([_lane_row([(LANE_I, mlstm_b_i[a].reshape(-1))]),
                                 _lane_row([(LANE_F, mlstm_b_f[a].reshape(-1))])])
            par_d = _param_rows([_lane_row([(LANE_A, delta_a_log[a].reshape(-1))]),
                                 _lane_row([(LANE_A, delta_dt_bias[a].reshape(-1))])])
            conv_w8 = jnp.concatenate([delta_conv_w[a], jnp.zeros((SUBLANES - CONV_K, 3 * H_B * DK_B), f32)], axis=0)

            hp, cn, nn_, mn = _mlstm_call(proj, gates, par_m, BATCH, SEQ, 0, None)
            st = (state_mlstm_C[:, a], state_mlstm_n[:, a][..., None], state_mlstm_m[:, a][..., None, None])
            hs, _, _, _ = _mlstm_call(proj, gates, par_m, DEC_BATCH, DEC_SEQ, s_off, st)
            op, sn = _delta_call(proj, conv_w8, gates, par_d, BATCH, SEQ, 0, None)
            os_, _ = _delta_call(proj, conv_w8, gates, par_d, DEC_BATCH, DEC_SEQ, s_off, state_delta_S[:, a])
            new_c.append(cn)
            new_n.append(nn_[..., 0])
            new_m.append(mn[..., 0, 0])
            new_s.append(sn)
            x = _ab_out_call(x, hp, hs, op, os_, proj, mlstm_norm_g[a].reshape(1, -1),
                             jnp.tile(delta_norm_g[a], H_B).reshape(1, -1), ab_w_out[a], mod_l, gate_row=5)
        else:
            proj = _proj_call(x, mod_l, g_mix, na_w_in[a], None, rows=(3, 4), out_dtype=f32, w_transposed=False)[0]
            attn_p, kp_t, vp_t = _ctx_attn_call(proj)
            attn_s = _nattn_call(proj, jnp.swapaxes(cache_na_k, -1, -2), jnp.swapaxes(cache_na_v, -1, -2), a,
                                 na_rel_bias[a], s_off)
            new_k.append(jnp.swapaxes(kp_t, -1, -2))
            new_v.append(jnp.swapaxes(vp_t, -1, -2))
            x = _na_out_call(x, attn_p, attn_s, na_w_out[a], mod_l, gate_row=5)
        last = l == DEPTH - 1
        outs = _ffn_call((x,), mod_l, norm_g[l, 2].reshape(1, D_MODEL), ffn_wg, ffn_wu, ffn_wd, l, 1,
                         gf_row, rows=(6, 7, 8), final=last, split_out=last, tm=FFN_TM)
        x = outs[0]

    y_prompt = outs[0].reshape(BATCH, SEQ, D_MODEL)
    y_sample = outs[1].reshape(DEC_BATCH, DEC_SEQ, D_MODEL)
    return (y_prompt, y_sample, jnp.stack(new_c, axis=1), jnp.stack(new_n, axis=1), jnp.stack(new_m, axis=1),
            jnp.stack(new_s, axis=1), jnp.stack(new_k, axis=1), jnp.stack(new_v, axis=1))
```

```python
import functools

import jax
import jax.numpy as jnp
from jax import lax
from jax.experimental import pallas as pl
from jax.experimental.pallas import tpu as pltpu

f32 = jnp.float32
bf16 = jnp.bfloat16

D_MODEL = 1024
BATCH = 16
SEQ = 256
DEPTH = 2
DEC_BATCH = 2
DEC_SEQ = 2048
PAST_LEN = 256
GRID_W = 64
GRID_ROWS = DEC_SEQ // GRID_W
D_FF = 2816
N_ADA = 9
EPS = 1e-6
CHUNK = 64
H_A, DQK_A, DV_A = 4, 64, 128
H_B, DK_B, DV_B = 4, 128, 128
CONV_K = 5
H_C, DH_C = 16, 64
WIN_R, WIN_C = 8, 16
NA_W = H_C * DH_C
assert DH_C ** -0.5 == 2.0 ** -3 and DQK_A ** -0.5 == 2.0 ** -3

N_PROMPT = BATCH * SEQ
N_SAMPLE = DEC_BATCH * DEC_SEQ
N_TOK = N_PROMPT + N_SAMPLE
AB_MAIN = 2 * H_A * DQK_A + 2 * H_A * DV_A + 2 * H_B * DK_B + 2 * H_B * DV_B
LANES = 128
VMEM_LIMIT = 56 * 1024 * 1024
FFN_VMEM_SLACK = 4 * 1024 * 1024
SUBLANES = 8

LANE_I, LANE_F, LANE_BETA, LANE_A = 0, 8, 16, 24

NT_DIMS = (((1,), (1,)), ((), ()))
TN_DIMS = (((0,), (0,)), ((), ()))

MAX_CHUNKS_PER_TRIP = 4
FFN_TM = 2048
FFN_ROW_BLOCK = 512
PROJ_ROW_BLOCK = 256


def _softplus(x):
    return jnp.maximum(x, 0.0) + jnp.log1p(jnp.exp(-jnp.abs(x)))


def _log_sigmoid(x):
    return -_softplus(-x)


def _silu(x):
    return x * jax.nn.sigmoid(x)


def _mod_row_index(i, tm):
    n_p = N_PROMPT // tm
    per_b = DEC_SEQ // tm
    return jnp.where(i < n_p, 0, 1 + (i - n_p) // per_b)


def _modulated(x, g_row, shift_row, scale_row):
    y = x * lax.rsqrt(jnp.mean(x * x, axis=-1, keepdims=True) + EPS) * g_row
    return y * (1.0 + scale_row) + shift_row


def _ada_kernel(cb_ref, w_ref, b_ref, o_ref, s_scr):
    tn = w_ref.shape[-1]

    @pl.when((pl.program_id(0) == 0) & (pl.program_id(1) == 0))
    def _():
        cb = cb_ref[...]
        s_scr[...] = cb * jax.nn.sigmoid(cb)

    for j in range(tn // LANES):
        cols = slice(j * LANES, (j + 1) * LANES)
        w = w_ref[:, cols]
        for r in range(3):
            o_ref[r:r + 1, cols] = jnp.sum(w * s_scr[r], axis=0, keepdims=True) + b_ref[:, cols]


def _ada_call(cond3, ada_w, ada_b):
    tn = N_ADA * D_MODEL // 4
    cb = jnp.broadcast_to(cond3[:, :, None], (3, D_MODEL, LANES))
    return pl.pallas_call(
        _ada_kernel,
        grid=(DEPTH, N_ADA * D_MODEL // tn),
        in_specs=[pl.BlockSpec((3, D_MODEL, LANES), lambda l, j: (0, 0, 0)),
                  pl.BlockSpec((None, D_MODEL, tn), lambda l, j: (l, 0, j)),
                  pl.BlockSpec((None, 1, tn), lambda l, j: (l, 0, j))],
        out_specs=pl.BlockSpec((None, 3, tn), lambda l, j: (l, 0, j)),
        out_shape=jax.ShapeDtypeStruct((DEPTH, 3, N_ADA * D_MODEL), f32),
        scratch_shapes=[pltpu.VMEM((3, D_MODEL, LANES), f32)],
        compiler_params=pltpu.CompilerParams(dimension_semantics=("arbitrary", "arbitrary"),
                                             vmem_limit_bytes=VMEM_LIMIT),
        name="ada_mod",
    )(cb, ada_w, ada_b.reshape(DEPTH, 1, N_ADA * D_MODEL))


def _prompt_or_sample(p_ref, s_ref, tm):
    return jnp.where(pl.program_id(0) < N_PROMPT // tm, p_ref[...], s_ref[...])


def _split_specs(tm, width, n_grid_axes=1):
    n_p = N_PROMPT // tm
    if n_grid_axes == 1:
        return [pl.BlockSpec((tm, width), lambda i: (jnp.minimum(i, n_p - 1), 0)),
                pl.BlockSpec((tm, width), lambda i: (jnp.maximum(i - n_p, 0), 0))]
    return [pl.BlockSpec((tm, width), lambda i, j: (jnp.minimum(i, n_p - 1), 0)),
            pl.BlockSpec((tm, width), lambda i, j: (jnp.maximum(i - n_p, 0), 0))]


def _ffn_kernel(*refs, rows, final, split_in, split_out, tm):
    it = iter(refs)
    x_refs = [next(it) for _ in range(2 if split_in else 1)]
    mod_ref, g_ref, wg_ref, wu_ref, wd_ref, gf_ref = [next(it) for _ in range(6)]
    o_refs = [next(it) for _ in range(2 if split_out else 1)]
    h_scr = next(it)
    j = pl.program_id(1)
    last_j = pl.num_programs(1) - 1
    is_prompt = pl.program_id(0) < N_PROMPT // tm

    def x_rows(rs):
        if split_in:
            return jnp.where(is_prompt, x_refs[0][rs, :], x_refs[1][rs, :])
        return x_refs[0][rs, :]

    def step(first, last, acc_ref):
        wg = wg_ref[...].astype(bf16)
        wu = wu_ref[...].astype(bf16)
        wd = wd_ref[...].astype(bf16)
        for r in range(tm // FFN_ROW_BLOCK):
            rs = slice(r * FFN_ROW_BLOCK, (r + 1) * FFN_ROW_BLOCK)
            if first:
                h = _modulated(x_rows(rs), g_ref[...], mod_ref[rows[0]:rows[0] + 1, :],
                               mod_ref[rows[1]:rows[1] + 1, :]).astype(bf16)
                h_scr[rs, :] = h
            else:
                h = h_scr[rs, :]
            g = jnp.dot(h, wg, preferred_element_type=f32)
            u = jnp.dot(h, wu, preferred_element_type=f32)
            part = jnp.dot((_silu(g) * u).astype(bf16), wd, preferred_element_type=f32)
            acc = part if first else acc_ref[rs, :] + part
            if last:
                xn = x_rows(rs) + (0.5 * mod_ref[rows[2]:rows[2] + 1, :]) * acc
                if final:
                    xn = xn * lax.rsqrt(jnp.mean(xn * xn, axis=-1, keepdims=True) + EPS) * gf_ref[...]
                acc_ref[rs, :] = xn
            else:
                acc_ref[rs, :] = acc

    owners = [(is_prompt, o_refs[0]), (jnp.logical_not(is_prompt), o_refs[1])] if split_out else [(True, o_refs[0])]
    for owns, o_ref in owners:
        pl.when((j == 0) & owns)(functools.partial(step, True, False, o_ref))
        pl.when((j > 0) & (j < last_j) & owns)(functools.partial(step, False, False, o_ref))
        pl.when((j == last_j) & owns)(functools.partial(step, False, True, o_ref))


def _ffn_call(xs, mod_l, g_row, ffn_wg, ffn_wu, ffn_wd, layer, half, gf_row, rows, final, split_out, tm, tf=256):
    split_in = len(xs) == 2
    x_specs = _split_specs(tm, D_MODEL, 2) if split_in else [pl.BlockSpec((tm, D_MODEL), lambda i, j: (i, 0))]
    n_token_windows = 2 + split_in + split_out
    vmem_bytes = (2 * n_token_windows * tm * D_MODEL * 4 + tm * D_MODEL * 2 + 2 * 3 * D_MODEL * tf * 4
                  + FFN_VMEM_SLACK)
    if split_out:
        out_specs = _split_specs(tm, D_MODEL, 2)
        out_shape = [jax.ShapeDtypeStruct((N_PROMPT, D_MODEL), f32), jax.ShapeDtypeStruct((N_SAMPLE, D_MODEL), f32)]
    else:
        out_specs = [pl.BlockSpec((tm, D_MODEL), lambda i, j: (i, 0))]
        out_shape = [jax.ShapeDtypeStruct((N_TOK, D_MODEL), f32)]
    scratch = [pltpu.VMEM((tm, D_MODEL), bf16)]
    return pl.pallas_call(
        functools.partial(_ffn_kernel, rows=rows, final=final, split_in=split_in, split_out=split_out, tm=tm),
        grid=(N_TOK // tm, D_FF // tf),
        in_specs=x_specs + [
            pl.BlockSpec((None, N_ADA, D_MODEL), lambda i, j: (_mod_row_index(i, tm), 0, 0)),
            pl.BlockSpec((1, D_MODEL), lambda i, j: (0, 0)),
            pl.BlockSpec((None, None, D_MODEL, tf), lambda i, j: (layer, half, 0, j)),
            pl.BlockSpec((None, None, D_MODEL, tf), lambda i, j: (layer, half, 0, j)),
            pl.BlockSpec((None, None, tf, D_MODEL), lambda i, j: (layer, half, j, 0)),
            pl.BlockSpec((1, D_MODEL), lambda i, j: (0, 0))],
        out_specs=out_specs, out_shape=out_shape, scratch_shapes=scratch,
        compiler_params=pltpu.CompilerParams(dimension_semantics=("parallel", "arbitrary"),
                                             vmem_limit_bytes=vmem_bytes),
        name="ffn",
    )(*xs, mod_l, g_row, ffn_wg, ffn_wu, ffn_wd, gf_row)


def _proj_kernel(*refs, rows, with_gates, w_transposed):
    if with_gates:
        x_ref, mod_ref, g_ref, w_ref, wgate_ref, o_ref, og_ref, h_scr = refs
    else:
        x_ref, mod_ref, g_ref, w_ref, o_ref, h_scr = refs
    j = pl.program_id(1)
    tm, tn = x_ref.shape[0], o_ref.shape[1]
    dims = NT_DIMS if w_transposed else (((1,), (0,)), ((), ()))

    def matmul(h, w):
        return lax.dot_general(h, w, dims, preferred_element_type=f32)

    def w_tile(t):
        return (w_ref[t * tn:(t + 1) * tn, :] if w_transposed else w_ref[:, t * tn:(t + 1) * tn]).astype(bf16)

    @pl.when(j == 0)
    def _():
        w = w_tile(0)
        for r in range(tm // PROJ_ROW_BLOCK):
            rs = slice(r * PROJ_ROW_BLOCK, (r + 1) * PROJ_ROW_BLOCK)
            hb = _modulated(x_ref[rs, :], g_ref[...], mod_ref[rows[0]:rows[0] + 1, :],
                            mod_ref[rows[1]:rows[1] + 1, :]).astype(bf16)
            h_scr[rs, :] = hb
            o_ref[rs, :] = matmul(hb, w).astype(o_ref.dtype)
            if with_gates:
                og_ref[rs, :] = matmul(hb, wgate_ref[...].astype(bf16))

    n_tiles = (w_ref.shape[0] if w_transposed else w_ref.shape[1]) // tn
    for t in range(1, n_tiles):
        @pl.when(j == t)
        def _(t=t):
            o_ref[...] = matmul(h_scr[...], w_tile(t)).astype(o_ref.dtype)


AB_I0 = 2 * H_A * DQK_A + 2 * H_A * DV_A
AB_D0 = AB_I0 + 4 * H_A
AB_B0 = AB_D0 + 2 * H_B * DK_B + 2 * H_B * DV_B
AB_IN = AB_B0 + 4 * H_B


def _ab_weight_kernel(w_ref, wm_ref, wg_ref):
    n_gate = 2 * H_A
    wm_ref[:AB_I0, :] = w_ref[:AB_I0, :]
    wm_ref[AB_I0:, :] = w_ref[AB_D0:AB_B0, :]
    wg_ref[:2 * n_gate, :] = w_ref[AB_I0:AB_D0, :]
    wg_ref[2 * n_gate:4 * n_gate, :] = w_ref[AB_B0:, :]
    wg_ref[4 * n_gate:, :] = jnp.zeros((LANES - 4 * n_gate, w_ref.shape[1]), f32)


def _ab_weight_call(ab_w_in, layer, tk=256):
    w_t = jnp.swapaxes(ab_w_in, 1, 2)
    return pl.pallas_call(
        _ab_weight_kernel,
        grid=(D_MODEL // tk,),
        in_specs=[pl.BlockSpec((None, AB_IN, tk), lambda i: (layer, 0, i))],
        out_specs=[pl.BlockSpec((AB_MAIN, tk), lambda i: (0, i)), pl.BlockSpec((LANES, tk), lambda i: (0, i))],
        out_shape=[jax.ShapeDtypeStruct((AB_MAIN, D_MODEL), f32), jax.ShapeDtypeStruct((LANES, D_MODEL), f32)],
        compiler_params=pltpu.CompilerParams(vmem_limit_bytes=VMEM_LIMIT),
        name="ab_weights",
    )(w_t)


def _column_tile(n, cap):
    return max(t for t in range(LANES, cap + 1, LANES) if n % t == 0)


def _proj_call(x, mod_l, g_row, w, w_gate, rows, out_dtype, w_transposed, tm=1024):
    n = w.shape[0] if w_transposed else w.shape[1]
    tn = _column_tile(n, n)
    with_gates = w_gate is not None
    w_spec = pl.BlockSpec(w.shape, lambda i, j: (0, 0), pipeline_mode=pl.Buffered(1))
    in_specs = [pl.BlockSpec((tm, D_MODEL), lambda i, j: (i, 0)),
                pl.BlockSpec((None, N_ADA, D_MODEL), lambda i, j: (_mod_row_index(i, tm), 0, 0)),
                pl.BlockSpec((1, D_MODEL), lambda i, j: (0, 0)),
                w_spec]
    out_specs = [pl.BlockSpec((tm, tn), lambda i, j: (i, j))]
    out_shape = [jax.ShapeDtypeStruct((N_TOK, n), out_dtype)]
    args = [x, mod_l, g_row, w]
    if with_gates:
        in_specs.append(pl.BlockSpec(w_gate.shape, lambda i, j: (0, 0)))
        out_specs.append(pl.BlockSpec((tm, LANES), lambda i, j: (i, 0)))
        out_shape.append(jax.ShapeDtypeStruct((N_TOK, LANES), f32))
        args.append(w_gate)
    return pl.pallas_call(
        functools.partial(_proj_kernel, rows=rows, with_gates=with_gates, w_transposed=w_transposed),
        grid=(N_TOK // tm, n // tn),
        in_specs=in_specs, out_specs=out_specs, out_shape=out_shape,
        scratch_shapes=[pltpu.VMEM((tm, D_MODEL), bf16)],
        compiler_params=pltpu.CompilerParams(dimension_semantics=("parallel", "arbitrary"),
                                             vmem_limit_bytes=VMEM_LIMIT),
        name="in_proj",
    )(*args)


def _chunk_masks():
    r = lax.broadcasted_iota(jnp.int32, (CHUNK, CHUNK), 0)
    c = lax.broadcasted_iota(jnp.int32, (CHUNK, CHUNK), 1)
    return r >= c, r <= c, r > c, r < c


def _lane_col(x, lane, j):
    return jnp.sum(jnp.where(lane == j, x, 0.0), axis=1, keepdims=True)


def _head_lane_select(base, pair):
    r = lax.broadcasted_iota(jnp.int32, (SUBLANES, LANES), 0)
    ln = lax.broadcasted_iota(jnp.int32, (SUBLANES, LANES), 1)
    return jnp.where((ln == base + 2 * pair + r) & (r < 2), 1.0, 0.0)


def _split3_bf16(x):
    hi = x.astype(bf16)
    r = x - hi.astype(f32)
    mid = r.astype(bf16)
    return hi, mid, (r - mid.astype(f32)).astype(bf16)


def _thrice(a16):
    return jnp.concatenate([a16] * 3, axis=1)


def _mask_matmul_f32(mask16x3, x):
    return jnp.dot(mask16x3, jnp.concatenate(_split3_bf16(x), axis=0), preferred_element_type=f32)


def _select_rows_f32(sel16x3, x):
    return lax.dot_general(sel16x3, jnp.concatenate(_split3_bf16(x), axis=1), NT_DIMS, preferred_element_type=f32)


def _chunks_per_trip(nc):
    return min(MAX_CHUNKS_PER_TRIP, nc)


def _chunk_start(trip, u, d, nc):
    step = trip * _chunks_per_trip(nc) + u
    c = step if d == 0 else nc - 1 - step
    return pl.multiple_of(c * CHUNK, CHUNK)


def _mlstm_kernel(*refs, seq, zero_init):
    if zero_init:
        qp_ref, kp_ref, v0_ref, v1_ref, g_ref, par_ref, h_ref, cf_ref, nf_ref, mf_ref = refs
    else:
        (qp_ref, kp_ref, v0_ref, v1_ref, g_ref, par_ref, c0_ref, n0_ref, m0_ref,
         h_ref, cf_ref, nf_ref, mf_ref) = refs
    nc = seq // CHUNK
    pair = pl.program_id(1)
    lane = lax.broadcasted_iota(jnp.int32, (1, LANES), 1)
    tril, triu, _, _ = _chunk_masks()
    masks = (tril, triu)
    masks16 = (_thrice(tril.astype(bf16)), _thrice(triu.astype(bf16)))
    sel = [_thrice(_head_lane_select(LANE_F + d * H_A, pair).astype(bf16)) for d in range(2)]
    bi_row = par_ref[0:1, :]
    bf_row = par_ref[1:2, :]
    ones_col = jnp.where(lane == 0, 1.0, 0.0) + jnp.zeros((CHUNK, LANES), f32)
    v_refs = (v0_ref, v1_ref)
    streams = [(hh, d) for hh in range(2) for d in range(2)]

    h_ref[...] = jnp.zeros_like(h_ref)

    init = []
    for hh, d in streams:
        if zero_init:
            init.append((jnp.zeros((DQK_A, 2 * LANES), f32), jnp.zeros((1, 1), f32)))
        else:
            n_aug = jnp.where(lane == 0, n0_ref[d, hh], 0.0)
            init.append((jnp.concatenate([c0_ref[d, hh], n_aug], axis=1), m0_ref[d, hh]))

    def body(trip, carry):
        subs = range(_chunks_per_trip(nc))
        shared = {}
        for u in subs:
            for d in range(2):
                r0 = _chunk_start(trip, u, d, nc)
                gates = g_ref[pl.ds(r0, CHUNK), :]
                gi = gates + bi_row
                gf = _log_sigmoid(gates + bf_row)
                cum = _mask_matmul_f32(masks16[d], gf)
                stack = jnp.concatenate([cum, pltpu.roll(gi, LANE_F - LANE_I, axis=1)], axis=0)
                rows = _select_rows_f32(sel[d], stack)
                total = cum[CHUNK - 1:CHUNK, :] if d == 0 else cum[0:1, :]
                shared[u, d] = (r0, gi, cum, rows, total)
        chains = [(u, hh, d) for u in subs for hh, d in streams]
        st = {ch: {} for ch in chains}
        for ch in chains:
            u, hh, d = ch
            r0 = shared[u, d][0]
            s = st[ch]
            s["q"] = (qp_ref[pl.ds(r0, CHUNK), hh * DQK_A:(hh + 1) * DQK_A] * DQK_A ** -0.5).astype(bf16)
            s["k"] = kp_ref[pl.ds(r0, CHUNK), hh * DQK_A:(hh + 1) * DQK_A].astype(bf16)
            s["v_aug"] = jnp.concatenate([v_refs[hh][pl.ds(r0, CHUNK), :].astype(f32), ones_col], axis=1)
            s["qk"] = lax.dot_general(s["q"], s["k"], NT_DIMS, preferred_element_type=f32)
        for ch in chains:
            u, hh, d = ch
            _, gi, cum, rows, total = shared[u, d]
            s = st[ch]
            head = 2 * pair + hh
            jf = LANE_F + d * H_A + head
            ji = LANE_I + d * H_A + head
            bcol = _lane_col(cum, lane, jf)
            icol = _lane_col(gi, lane, ji)
            dmat = jnp.where(masks[d], bcol - rows[hh:hh + 1, :CHUNK] + rows[hh:hh + 1, CHUNK:], -jnp.inf)
            m_loc = jnp.max(dmat, axis=1, keepdims=True)
            s["sw"] = (s["qk"] * jnp.exp(dmat - m_loc)).astype(bf16)
            blast = _lane_col(total, lane, jf)
            gs = blast - bcol + icol
            ms_loc = jnp.max(gs, axis=0, keepdims=True)
            s["wv"] = (jnp.exp(gs - ms_loc) * s["v_aug"]).astype(bf16)
            s.update(bcol=bcol, m_loc=m_loc, blast=blast, ms_loc=ms_loc)
        for ch in chains:
            s = st[ch]
            s["num"] = jnp.dot(s["sw"], s["v_aug"].astype(bf16), preferred_element_type=f32)
            s["kv"] = lax.dot_general(s["k"], s["wv"], TN_DIMS, preferred_element_type=f32)
        state = list(carry)
        for u in subs:
            qc = [jnp.dot(st[u, hh, d]["q"], state[i][0].astype(bf16), preferred_element_type=f32)
                  for i, (hh, d) in enumerate(streams)]
            for i, (hh, d) in enumerate(streams):
                s = st[u, hh, d]
                c_aug, m = state[i]
                m_inter = s["bcol"] + m
                m_t = jnp.maximum(m_inter, s["m_loc"])
                nd = jnp.exp(s["m_loc"] - m_t) * s["num"] + jnp.exp(m_inter - m_t) * qc[i]
                den = nd[:, DV_A:DV_A + 1]
                hval = nd[:, :DV_A] / jnp.maximum(jnp.abs(den), jnp.exp(-m_t))
                r0 = shared[u, d][0]
                h_ref[pl.ds(r0, CHUNK), hh * DV_A:(hh + 1) * DV_A] += hval
                m_new = jnp.maximum(s["blast"] + m, s["ms_loc"])
                c_new = jnp.exp(s["blast"] + m - m_new) * c_aug + jnp.exp(s["ms_loc"] - m_new) * s["kv"]
                state[i] = (c_new, m_new)
        return tuple(state)

    final = lax.fori_loop(0, nc // _chunks_per_trip(nc), body, tuple(init))
    for i, (hh, d) in enumerate(streams):
        c_aug, m = final[i]
        cf_ref[d, hh] = c_aug[:, :DV_A]
        nf_ref[d, hh] = c_aug[:, DV_A:DV_A + 1]
        mf_ref[d, hh] = m


def _mlstm_call(proj, gates, par, nb, seq, off, state):
    zero_init = state is None
    blk = lambda col: pl.BlockSpec((seq, LANES), col)
    in_specs = [blk(lambda b, p: (off + b, p)),
                blk(lambda b, p: (off + b, 2 + p)),
                blk(lambda b, p: (off + b, 4 + 2 * p)),
                blk(lambda b, p: (off + b, 5 + 2 * p)),
                blk(lambda b, p: (off + b, 0)),
                pl.BlockSpec((SUBLANES, LANES), lambda b, p: (0, 0))]
    st_specs = [pl.BlockSpec((None, 2, 2, DQK_A, DV_A), lambda b, p: (b, 0, p, 0, 0)),
                pl.BlockSpec((None, 2, 2, DQK_A, 1), lambda b, p: (b, 0, p, 0, 0)),
                pl.BlockSpec((None, 2, 2, 1, 1), lambda b, p: (b, 0, p, 0, 0))]
    args = [proj, proj, proj, proj, gates, par]
    if not zero_init:
        in_specs += st_specs
        args += list(state)
    return pl.pallas_call(
        functools.partial(_mlstm_kernel, seq=seq, zero_init=zero_init),
        grid=(nb, H_A // 2),
        in_specs=in_specs,
        out_specs=[pl.BlockSpec((seq, 2 * DV_A), lambda b, p: (b, p))] + st_specs,
        out_shape=[jax.ShapeDtypeStruct((nb * seq, H_A * DV_A), f32),
                   jax.ShapeDtypeStruct((nb, 2, H_A, DQK_A, DV_A), f32),
                   jax.ShapeDtypeStruct((nb, 2, H_A, DQK_A, 1), f32),
                   jax.ShapeDtypeStruct((nb, 2, H_A, 1, 1), f32)],
        compiler_params=pltpu.CompilerParams(vmem_limit_bytes=VMEM_LIMIT),
        name="mlstm",
    )(*args)


CONV_PAD = SUBLANES


def _short_conv_silu(x, w_ref, pad_scr, seq):
    pad_scr[:CONV_PAD, :] = jnp.zeros((CONV_PAD, LANES), f32)
    pad_scr[CONV_PAD + seq:, :] = jnp.zeros((CONV_PAD, LANES), f32)
    pad_scr[CONV_PAD:CONV_PAD + seq, :] = x
    acc = x * w_ref[CONV_K // 2:CONV_K // 2 + 1, :]
    for tap in range(CONV_K):
        delta = tap - CONV_K // 2
        if delta != 0:
            acc = acc + pad_scr[CONV_PAD + delta:CONV_PAD + delta + seq, :] * w_ref[tap:tap + 1, :]
    return _silu(acc)


def _l2_unit(y):
    return y * lax.rsqrt(jnp.sum(y * y, axis=-1, keepdims=True) + EPS)


def _split_bf16(x):
    hi = x.astype(bf16)
    return hi, (x - hi.astype(f32)).astype(bf16)


def _matmul_3pass(m, x):
    mh, ml = _split_bf16(m)
    xh, xl = _split_bf16(x)
    return jnp.dot(jnp.concatenate([mh, ml, mh], axis=1), jnp.concatenate([xh, xh, xl], axis=0),
                   preferred_element_type=f32)


def _unit_triangular_solves(ns, xs):
    levels = CHUNK.bit_length() - 1
    r = lax.broadcasted_iota(jnp.int32, (CHUNK, CHUNK), 0)
    c = lax.broadcasted_iota(jnp.int32, (CHUNK, CHUNK), 1)
    eye = jnp.where(r == c, 1.0, 0.0)
    ts = [eye + n for n in ns]
    ms = [_matmul_3pass(n, n) for n in ns]
    for lvl in range(1, levels):
        for i in range(len(ns)):
            if lvl < levels - 1:
                prod = _matmul_3pass(ms[i], jnp.concatenate([ts[i], ms[i]], axis=1))
                ts[i] = ts[i] + prod[:, :CHUNK]
                ms[i] = prod[:, CHUNK:]
            else:
                ts[i] = ts[i] + _matmul_3pass(ms[i], ts[i])
    return [_matmul_3pass(t, x) for t, x in zip(ts, xs)]


def _delta_kernel(*refs, seq, zero_init):
    if zero_init:
        (q0_ref, q1_ref, k0_ref, k1_ref, v0_ref, v1_ref, wq0_ref, wq1_ref, wk0_ref, wk1_ref, wv0_ref, wv1_ref,
         g_ref, par_ref, o_ref, sf_ref, q_scr, k_scr, v_scr, pad_scr) = refs
    else:
        (q0_ref, q1_ref, k0_ref, k1_ref, v0_ref, v1_ref, wq0_ref, wq1_ref, wk0_ref, wk1_ref, wv0_ref, wv1_ref,
         g_ref, par_ref, s0_ref, o_ref, sf_ref, q_scr, k_scr, v_scr, pad_scr) = refs
    nc = seq // CHUNK
    pair = pl.program_id(1)
    lane = lax.broadcasted_iota(jnp.int32, (1, LANES), 1)
    tril, triu, stril, striu = _chunk_masks()
    masks, smasks = (tril, triu), (stril, striu)
    masks16 = (_thrice(tril.astype(bf16)), _thrice(triu.astype(bf16)))
    sel = [_thrice(_head_lane_select(LANE_A + d * H_B, pair).astype(bf16)) for d in range(2)]
    neg_a_row = -jnp.exp(par_ref[0:1, :])
    dt_row = par_ref[1:2, :]
    streams = [(hh, d) for hh in range(2) for d in range(2)]

    for hh, (q_ref, k_ref, v_ref, wq_ref, wk_ref, wv_ref) in enumerate(
            ((q0_ref, k0_ref, v0_ref, wq0_ref, wk0_ref, wv0_ref), (q1_ref, k1_ref, v1_ref, wq1_ref, wk1_ref, wv1_ref))):
        q_scr[hh] = _l2_unit(_short_conv_silu(q_ref[...].astype(f32), wq_ref, pad_scr, seq)) * DK_B ** -0.5
        k_scr[hh] = _l2_unit(_short_conv_silu(k_ref[...].astype(f32), wk_ref, pad_scr, seq))
        v_scr[hh] = _short_conv_silu(v_ref[...].astype(f32), wv_ref, pad_scr, seq)

    o_ref[...] = jnp.zeros_like(o_ref)
    init = tuple(jnp.zeros((DK_B, DV_B), f32) if zero_init else s0_ref[d, hh] for hh, d in streams)

    def body(trip, carry):
        subs = range(_chunks_per_trip(nc))
        shared = {}
        for u in subs:
            for d in range(2):
                r0 = _chunk_start(trip, u, d, nc)
                gates = g_ref[pl.ds(r0, CHUNK), :]
                beta_all = jax.nn.sigmoid(gates)
                glog = neg_a_row * _softplus(gates + dt_row)
                cum = _mask_matmul_f32(masks16[d], glog)
                rows = _select_rows_f32(sel[d], cum)
                shared[u, d] = (r0, beta_all, cum, rows)
        chains = [(u, hh, d) for u in subs for hh, d in streams]
        st = {ch: {} for ch in chains}
        for ch in chains:
            u, hh, d = ch
            r0, beta_all, cum, rows = shared[u, d]
            s = st[ch]
            head = 2 * pair + hh
            beta = _lane_col(beta_all, lane, LANE_BETA + d * H_B + head)
            gcol = _lane_col(cum, lane, LANE_A + d * H_B + head)
            glast = gcol[CHUNK - 1:CHUNK, :] if d == 0 else gcol[0:1, :]
            q = q_scr[hh, pl.ds(r0, CHUNK), :]
            k = k_scr[hh, pl.ds(r0, CHUNK), :]
            v = v_scr[hh, pl.ds(r0, CHUNK), :]
            k16 = k.astype(bf16)
            kbeta = k * beta
            eg = jnp.exp(gcol)
            decay = jnp.exp(jnp.where(masks[d], gcol - rows[hh:hh + 1, :], -jnp.inf))
            kk = lax.dot_general(kbeta.astype(bf16), k16, NT_DIMS, preferred_element_type=f32)
            qk = lax.dot_general(q.astype(bf16), k16, NT_DIMS, preferred_element_type=f32)
            s["n"] = -jnp.where(smasks[d], kk * decay, 0.0)
            s["x"] = jnp.concatenate([v * beta, kbeta * eg], axis=1)
            s["qk"] = (qk * decay).astype(bf16)
            s["qg"] = (q * eg).astype(bf16)
            s["kd_t"] = (k * jnp.exp(glast - gcol)).T.astype(bf16)
            s["gl"] = jnp.exp(glast)
        solved = _unit_triangular_solves([st[ch]["n"] for ch in chains], [st[ch]["x"] for ch in chains])
        for ch, uw in zip(chains, solved):
            s = st[ch]
            s["u"] = uw[:, :DV_B]
            s["w_qg"] = jnp.concatenate([uw[:, DV_B:].astype(bf16), s["qg"]], axis=0)
        state = list(carry)
        for u in subs:
            ws = [jnp.dot(st[u, hh, d]["w_qg"], state[i].astype(bf16), preferred_element_type=f32)
                  for i, (hh, d) in enumerate(streams)]
            v_new = [(st[u, hh, d]["u"] - ws[i][:CHUNK]).astype(bf16) for i, (hh, d) in enumerate(streams)]
            for i, (hh, d) in enumerate(streams):
                s = st[u, hh, d]
                o = ws[i][CHUNK:] + jnp.dot(s["qk"], v_new[i], preferred_element_type=f32)
                r0 = shared[u, d][0]
                o_ref[pl.ds(r0, CHUNK), hh * DV_B:(hh + 1) * DV_B] += o
                state[i] = state[i] * s["gl"] + jnp.dot(s["kd_t"], v_new[i], preferred_element_type=f32)
        return tuple(state)

    final = lax.fori_loop(0, nc // _chunks_per_trip(nc), body, init)
    for i, (hh, d) in enumerate(streams):
        sf_ref[d, hh] = final[i]


def _delta_call(proj, conv_w8, gates, par, nb, seq, off, state):
    zero_init = state is None
    col0 = (2 * H_A * DQK_A + 2 * H_A * DV_A) // LANES
    blk = lambda col: pl.BlockSpec((seq, LANES), col)
    wblk = lambda col: pl.BlockSpec((SUBLANES, LANES), col)
    in_specs = [blk(lambda b, p: (off + b, col0 + 2 * p)), blk(lambda b, p: (off + b, col0 + 2 * p + 1)),
                blk(lambda b, p: (off + b, col0 + H_B + 2 * p)), blk(lambda b, p: (off + b, col0 + H_B + 2 * p + 1)),
                blk(lambda b, p: (off + b, col0 + 2 * H_B + 2 * p)), blk(lambda b, p: (off + b, col0 + 2 * H_B + 2 * p + 1)),
                wblk(lambda b, p: (0, 2 * p)), wblk(lambda b, p: (0, 2 * p + 1)),
                wblk(lambda b, p: (0, H_B + 2 * p)), wblk(lambda b, p: (0, H_B + 2 * p + 1)),
                wblk(lambda b, p: (0, 2 * H_B + 2 * p)), wblk(lambda b, p: (0, 2 * H_B + 2 * p + 1)),
                blk(lambda b, p: (off + b, 0)),
                pl.BlockSpec((SUBLANES, LANES), lambda b, p: (0, 0))]
    st_spec = pl.BlockSpec((None, 2, 2, DK_B, DV_B), lambda b, p: (b, 0, p, 0, 0))
    args = [proj] * 6 + [conv_w8] * 6 + [gates, par]
    if not zero_init:
        in_specs.append(st_spec)
        args.append(state)
    return pl.pallas_call(
        functools.partial(_delta_kernel, seq=seq, zero_init=zero_init),
        grid=(nb, H_B // 2),
        in_specs=in_specs,
        out_specs=[pl.BlockSpec((seq, 2 * DV_B), lambda b, p: (b, p)), st_spec],
        out_shape=[jax.ShapeDtypeStruct((nb * seq, H_B * DV_B), f32),
                   jax.ShapeDtypeStruct((nb, 2, H_B, DK_B, DV_B), f32)],
        scratch_shapes=[pltpu.VMEM((2, seq, LANES), f32)] * 3 + [pltpu.VMEM((seq + 2 * CONV_PAD, LANES), f32)],
        compiler_params=pltpu.CompilerParams(vmem_limit_bytes=VMEM_LIMIT),
        name="delta",
    )(*args)


def _head_rms(x):
    return x * lax.rsqrt(jnp.mean(x * x, axis=-1, keepdims=True) + EPS)


def _ab_out_kernel(x_ref, hp_ref, hs_ref, op_ref, os_ref, om_ref, zd_ref, gm_ref, gd_ref, w_ref, mod_ref, o_ref,
                   *, gate_row, tm):
    is_prompt = pl.program_id(0) < N_PROMPT // tm
    w = w_ref[...].astype(bf16)
    for r in range(tm // PROJ_ROW_BLOCK):
        rs = slice(r * PROJ_ROW_BLOCK, (r + 1) * PROJ_ROW_BLOCK)
        hsum = jnp.where(is_prompt, hp_ref[rs, :], hs_ref[rs, :])
        osum = jnp.where(is_prompt, op_ref[rs, :], os_ref[rs, :])
        parts = []
        for h in range(H_A):
            sl = slice(h * DV_A, (h + 1) * DV_A)
            parts.append(_head_rms(hsum[:, sl]) * gm_ref[:, sl] * jax.nn.sigmoid(om_ref[rs, sl].astype(f32)))
        for h in range(H_B):
            sl = slice(h * DV_B, (h + 1) * DV_B)
            parts.append(_head_rms(osum[:, sl]) * gd_ref[:, sl] * _silu(zd_ref[rs, sl].astype(f32)))
        cat = jnp.concatenate(parts, axis=1).astype(bf16)
        y = jnp.dot(cat, w, preferred_element_type=f32)
        o_ref[rs, :] = x_ref[rs, :] + mod_ref[gate_row:gate_row + 1, :] * y


def _ab_out_call(x, hp, hs, op, os_, proj, gm_row, gd_row, w_out, mod_l, gate_row, tm=1024):
    wide = H_A * DV_A
    om_blk = (2 * H_A * DQK_A) // wide + 1
    zd_blk = AB_MAIN // wide - 1
    return pl.pallas_call(
        functools.partial(_ab_out_kernel, gate_row=gate_row, tm=tm),
        grid=(N_TOK // tm,),
        in_specs=[pl.BlockSpec((tm, D_MODEL), lambda i: (i, 0))] + _split_specs(tm, wide) + _split_specs(tm, wide) + [
            pl.BlockSpec((tm, wide), lambda i: (i, om_blk)),
            pl.BlockSpec((tm, wide), lambda i: (i, zd_blk)),
            pl.BlockSpec((1, wide), lambda i: (0, 0)),
            pl.BlockSpec((1, wide), lambda i: (0, 0)),
            pl.BlockSpec((2 * wide, D_MODEL), lambda i: (0, 0)),
            pl.BlockSpec((None, N_ADA, D_MODEL), lambda i: (_mod_row_index(i, tm), 0, 0))],
        out_specs=pl.BlockSpec((tm, D_MODEL), lambda i: (i, 0)),
        out_shape=jax.ShapeDtypeStruct((N_TOK, D_MODEL), f32),
        compiler_params=pltpu.CompilerParams(vmem_limit_bytes=VMEM_LIMIT),
        name="ab_out",
    )(x, hp, hs, op, os_, proj, proj, gm_row, gd_row, w_out, mod_l)


def _na_out_kernel(x_ref, ap_ref, as_ref, w_ref, mod_ref, o_ref, *, gate_row, tm):
    a = _prompt_or_sample(ap_ref, as_ref, tm).astype(bf16)
    y = jnp.dot(a, w_ref[...].astype(bf16), preferred_element_type=f32)
    o_ref[...] = x_ref[...] + mod_ref[gate_row:gate_row + 1, :] * y


def _na_out_call(x, attn_p, attn_s, w_out, mod_l, gate_row, tm=1024):
    return pl.pallas_call(
        functools.partial(_na_out_kernel, gate_row=gate_row, tm=tm),
        grid=(N_TOK // tm,),
        in_specs=[pl.BlockSpec((tm, D_MODEL), lambda i: (i, 0))] + _split_specs(tm, NA_W) + [
            pl.BlockSpec((NA_W, D_MODEL), lambda i: (0, 0)),
            pl.BlockSpec((None, N_ADA, D_MODEL), lambda i: (_mod_row_index(i, tm), 0, 0))],
        out_specs=pl.BlockSpec((tm, D_MODEL), lambda i: (i, 0)),
        out_shape=jax.ShapeDtypeStruct((N_TOK, D_MODEL), f32),
        compiler_params=pltpu.CompilerParams(vmem_limit_bytes=VMEM_LIMIT),
        name="na_out",
    )(x, attn_p, attn_s, w_out, mod_l)


CTX_HEADS = 8


def _ctx_attn_kernel(q_ref, k_ref, v_ref, o_ref, nk_ref, nv_ref):
    low = lax.broadcasted_iota(jnp.int32, (1, LANES), 1) < DH_C
    def pair_scores(pair):
        sl = slice(pair * LANES, (pair + 1) * LANES)
        q = q_ref[:, sl] * DH_C ** -0.5
        k_t = k_ref[:, sl].T
        v_t = v_ref[:, sl].T
        for hh in range(2):
            nk_ref[2 * pair + hh] = k_t[hh * DH_C:(hh + 1) * DH_C, :]
            nv_ref[2 * pair + hh] = v_t[hh * DH_C:(hh + 1) * DH_C, :]
        k = k_ref[:, sl].astype(bf16)
        scores = [lax.dot_general(jnp.where(low if hh == 0 else jnp.logical_not(low), q, 0.0).astype(bf16), k,
                                  NT_DIMS, preferred_element_type=f32) for hh in range(2)]
        return sl, scores

    def pair_finish(sl, scores):
        v = v_ref[:, sl].astype(bf16)
        probs = [jnp.exp(s - jnp.max(s, axis=1, keepdims=True)) for s in scores]
        outs = [jnp.dot(p.astype(bf16), v, preferred_element_type=f32) / jnp.sum(p, axis=1, keepdims=True)
                for p in probs]
        o_ref[:, sl] = jnp.where(low, outs[0], outs[1]).astype(o_ref.dtype)

    n_pairs = CTX_HEADS // 2
    pending = pair_scores(0)
    for pair in range(n_pairs):
        upcoming = pair_scores(pair + 1) if pair + 1 < n_pairs else None
        pair_finish(*pending)
        pending = upcoming


def _ctx_attn_call(proj):
    w = CTX_HEADS * DH_C
    nblk = NA_W // w
    return pl.pallas_call(
        _ctx_attn_kernel,
        grid=(BATCH, nblk),
        in_specs=[pl.BlockSpec((SEQ, w), lambda b, j: (b, j)),
                  pl.BlockSpec((SEQ, w), lambda b, j: (b, nblk + j)),
                  pl.BlockSpec((SEQ, w), lambda b, j: (b, 2 * nblk + j))],
        out_specs=[pl.BlockSpec((SEQ, w), lambda b, j: (b, j)),
                   pl.BlockSpec((None, CTX_HEADS, DH_C, SEQ), lambda b, j: (b, j, 0, 0)),
                   pl.BlockSpec((None, CTX_HEADS, DH_C, SEQ), lambda b, j: (b, j, 0, 0))],
        out_shape=[jax.ShapeDtypeStruct((N_PROMPT, NA_W), bf16),
                   jax.ShapeDtypeStruct((BATCH, H_C, DH_C, SEQ), f32),
                   jax.ShapeDtypeStruct((BATCH, H_C, DH_C, SEQ), f32)],
        compiler_params=pltpu.CompilerParams(vmem_limit_bytes=VMEM_LIMIT),
        name="ctx_attn",
    )(proj, proj, proj)


QROWS = 4
QBLK = QROWS * GRID_W
KROWS_MID = QROWS + WIN_R - 1
N_RIDX = 2 * WIN_R - 1
N_CIDX = 2 * WIN_C - 1
N_QBLK = GRID_ROWS // QROWS


def _nattn_bias_tables(rb_ref, tb_scr, bmid_scr, btop_scr, bbot_scr):
    qc = lax.broadcasted_iota(jnp.int32, (GRID_W, GRID_W), 0)
    kc = lax.broadcasted_iota(jnp.int32, (GRID_W, GRID_W), 1)
    cs = jnp.clip(qc - WIN_C // 2, 0, GRID_W - WIN_C)
    valid = (kc >= cs) & (kc < cs + WIN_C)
    neg = jnp.full((GRID_W, GRID_W), -jnp.inf, f32)
    for hh in range(2):
        for ri in range(N_RIDX):
            row = jnp.broadcast_to(rb_ref[hh, ri:ri + 1, :], (GRID_W, LANES))
            tile = pltpu.roll(row, LANES - (WIN_C - 1), 1, stride=1, stride_axis=0)[:, :GRID_W]
            tb_scr[ri] = jnp.where(valid, tile, -jnp.inf)
        for i in range(QROWS):
            rs = slice(i * GRID_W, (i + 1) * GRID_W)
            for jj in range(KROWS_MID):
                inside = 0 <= jj - i < WIN_R
                bmid_scr[hh, rs, jj * GRID_W:(jj + 1) * GRID_W] = tb_scr[jj - i + WIN_R // 2 - 1] if inside else neg
            for jj in range(WIN_R):
                btop_scr[hh, rs, jj * GRID_W:(jj + 1) * GRID_W] = tb_scr[jj - i + WIN_R - 1]
                bbot_scr[hh, rs, jj * GRID_W:(jj + 1) * GRID_W] = tb_scr[jj - i + WIN_R // 2 - 1]


def _nattn_kernel(rb_ref, q_ref, k_ref, v_ref, ck_ref, cv_ref, o_ref, tb_scr, bmid_scr, btop_scr, bbot_scr):
    scale = DH_C ** -0.5

    @pl.when(pl.program_id(1) == 0)
    def _():
        _nattn_bias_tables(rb_ref, tb_scr, bmid_scr, btop_scr, bbot_scr)

    low = lax.broadcasted_iota(jnp.int32, (1, LANES), 1) < DH_C
    kctx_t = jnp.concatenate([ck_ref[0], ck_ref[1]], axis=0).astype(bf16)
    vctx = jnp.concatenate([cv_ref[0], cv_ref[1]], axis=0).T.astype(bf16)

    def block_scores(q_start, k_start, k_rows, bias_scr):
        nk = k_rows * GRID_W
        q = q_ref[pl.ds(q_start, QBLK), :] * scale
        ku = k_ref[pl.ds(k_start, nk), :].astype(bf16)
        vu = v_ref[pl.ds(k_start, nk), :].astype(bf16)
        scores = []
        for hh in range(2):
            qh = jnp.where(low if hh == 0 else jnp.logical_not(low), q, 0.0).astype(bf16)
            scores.append((lax.dot_general(qh, ku, NT_DIMS, preferred_element_type=f32) + bias_scr[hh],
                           jnp.dot(qh, kctx_t, preferred_element_type=f32)))
        return q_start, vu, scores

    def block_finish(q_start, vu, scores):
        probs = []
        for s_loc, s_ctx in scores:
            m = jnp.maximum(jnp.max(s_loc, axis=1, keepdims=True), jnp.max(s_ctx, axis=1, keepdims=True))
            p_loc = jnp.exp(s_loc - m)
            p_ctx = jnp.exp(s_ctx - m)
            denom = jnp.sum(p_loc, axis=1, keepdims=True) + jnp.sum(p_ctx, axis=1, keepdims=True)
            probs.append((p_loc.astype(bf16), p_ctx.astype(bf16), denom))
        outs = [(jnp.dot(p_loc, vu, preferred_element_type=f32) + jnp.dot(p_ctx, vctx, preferred_element_type=f32)) / denom
                for p_loc, p_ctx, denom in probs]
        o_ref[pl.ds(q_start, QBLK), :] = jnp.where(low, outs[0], outs[1]).astype(o_ref.dtype)

    blocks = ([(0, 0, WIN_R, btop_scr)]
              + [(b * QBLK, b * QBLK - (WIN_R // 2) * GRID_W, KROWS_MID, bmid_scr) for b in range(1, N_QBLK - 1)]
              + [((N_QBLK - 1) * QBLK, (GRID_ROWS - WIN_R) * GRID_W, WIN_R, bbot_scr)])
    pending = block_scores(*blocks[0])
    for nxt in blocks[1:] + [None]:
        upcoming = block_scores(*nxt) if nxt is not None else None
        block_finish(*pending)
        pending = upcoming


def _nattn_call(proj, cache_k, cache_v, layer, rel_bias, off):
    npair = H_C // 2
    return pl.pallas_call(
        _nattn_kernel,
        grid=(npair, DEC_BATCH),
        in_specs=[pl.BlockSpec((2, 2 * WIN_R, LANES), lambda p, b: (p, 0, 0)),
                  pl.BlockSpec((DEC_SEQ, LANES), lambda p, b: (off + b, p)),
                  pl.BlockSpec((DEC_SEQ, LANES), lambda p, b: (off + b, npair + p)),
                  pl.BlockSpec((DEC_SEQ, LANES), lambda p, b: (off + b, 2 * npair + p)),
                  pl.BlockSpec((None, None, 2, DH_C, PAST_LEN), lambda p, b: (b, layer, p, 0, 0)),
                  pl.BlockSpec((None, None, 2, DH_C, PAST_LEN), lambda p, b: (b, layer, p, 0, 0))],
        out_specs=pl.BlockSpec((DEC_SEQ, LANES), lambda p, b: (b, p)),
        out_shape=jax.ShapeDtypeStruct((N_SAMPLE, NA_W), bf16),
        scratch_shapes=[pltpu.VMEM((N_RIDX, GRID_W, GRID_W), f32),
                        pltpu.VMEM((2, QBLK, KROWS_MID * GRID_W), f32),
                        pltpu.VMEM((2, QBLK, WIN_R * GRID_W), f32),
                        pltpu.VMEM((2, QBLK, WIN_R * GRID_W), f32)],
        compiler_params=pltpu.CompilerParams(dimension_semantics=("arbitrary", "arbitrary"),
                                             vmem_limit_bytes=VMEM_LIMIT),
        name="nattn",
    )(jnp.pad(rel_bias, ((0, 0), (0, 2 * WIN_R - N_RIDX), (0, LANES - N_CIDX))), proj, proj, proj, cache_k, cache_v)


def _lane_row(pieces):
    row = jnp.zeros((LANES,), f32)
    for off, vals in pieces:
        row = row.at[off:off + vals.shape[0]].set(vals.astype(f32))
    return row


def _param_rows(rows):
    out = jnp.zeros((SUBLANES, LANES), f32)
    for r, row in enumerate(rows):
        out = out.at[r].set(row)
    return out


def kernel(x_prompt, x_sample, c, state_mlstm_C, state_mlstm_n, state_mlstm_m, state_delta_S, cache_na_k, cache_na_v, c_ctx, ada_w, ada_b, norm_g, ffn_wg, ffn_wu, ffn_wd, ab_w_in, ab_w_out, mlstm_b_i, mlstm_b_f, mlstm_norm_g, delta_conv_w, delta_a_log, delta_dt_bias, delta_norm_g, na_w_in, na_w_out, na_rel_bias, final_norm_g):
    xs = (x_prompt.reshape(N_PROMPT, D_MODEL), x_sample.reshape(N_SAMPLE, D_MODEL))
    mods = _ada_call(jnp.concatenate([c_ctx[None, :], c], axis=0), ada_w, ada_b)
    mods = mods.reshape(DEPTH, 3, N_ADA, D_MODEL)
    gf_row = final_norm_g.reshape(1, D_MODEL)
    s_off = N_PROMPT // DEC_SEQ
    new_c, new_n, new_m, new_s, new_k, new_v = [], [], [], [], [], []

    for l in range(DEPTH):
        mod_l = mods[l]
        a = l // 2
        x = _ffn_call(xs if l == 0 else (x,), mod_l, norm_g[l, 0].reshape(1, D_MODEL), ffn_wg, ffn_wu, ffn_wd, l, 0,
                      gf_row, rows=(0, 1, 2), final=False, split_out=False, tm=FFN_TM)[0]
        g_mix = norm_g[l, 1].reshape(1, D_MODEL)
        if l % 2 == 0:
            w_main, w_gate = _ab_weight_call(ab_w_in, a)
            proj, gates = _proj_call(x, mod_l, g_mix, w_main, w_gate, rows=(3, 4), out_dtype=bf16, w_transposed=True)

            par_m = _param_rows([_lane_row([(LANE_I, mlstm_b_i[a].reshape(-1))]),
                                 _lane_row([(LANE_F, mlstm_b_f[a].reshape(-1))])])
            par_d = _param_rows([_lane_row([(LANE_A, delta_a_log[a].reshape(-1))]),
                                 _lane_row([(LANE_A, delta_dt_bias[a].reshape(-1))])])
            conv_w8 = jnp.concatenate([delta_conv_w[a], jnp.zeros((SUBLANES - CONV_K, 3 * H_B * DK_B), f32)], axis=0)

            hp, cn, nn_, mn = _mlstm_call(proj, gates, par_m, BATCH, SEQ, 0, None)
            st = (state_mlstm_C[:, a], state_mlstm_n[:, a][..., None], state_mlstm_m[:, a][..., None, None])
            hs, _, _, _ = _mlstm_call(proj, gates, par_m, DEC_BATCH, DEC_SEQ, s_off, st)
            op, sn = _delta_call(proj, conv_w8, gates, par_d, BATCH, SEQ, 0, None)
            os_, _ = _delta_call(proj, conv_w8, gates, par_d, DEC_BATCH, DEC_SEQ, s_off, state_delta_S[:, a])
            new_c.append(cn)
            new_n.append(nn_[..., 0])
            new_m.append(mn[..., 0, 0])
            new_s.append(sn)
            x = _ab_out_call(x, hp, hs, op, os_, proj, mlstm_norm_g[a].reshape(1, -1),
                             jnp.tile(delta_norm_g[a], H_B).reshape(1, -1), ab_w_out[a], mod_l, gate_row=5)
        else:
            proj = _proj_call(x, mod_l, g_mix, na_w_in[a], None, rows=(3, 4), out_dtype=f32, w_transposed=False)[0]
            attn_p, kp_t, vp_t = _ctx_attn_call(proj)
            attn_s = _nattn_call(proj, jnp.swapaxes(cache_na_k, -1, -2), jnp.swapaxes(cache_na_v, -1, -2), a,
                                 na_rel_bias[a], s_off)
            new_k.append(jnp.swapaxes(kp_t, -1, -2))
            new_v.append(jnp.swapaxes(vp_t, -1, -2))
            x = _na_out_call(x, attn_p, attn_s, na_w_out[a], mod_l, gate_row=5)
        last = l == DEPTH - 1
        outs = _ffn_call((x,), mod_l, norm_g[l, 2].reshape(1, D_MODEL), ffn_wg, ffn_wu, ffn_wd, l, 1,
                         gf_row, rows=(6, 7, 8), final=last, split_out=last, tm=FFN_TM)
        x = outs[0]

    y_prompt = outs[0].reshape(BATCH, SEQ, D_MODEL)
    y_sample = outs[1].reshape(DEC_BATCH, DEC_SEQ, D_MODEL)
    return (y_prompt, y_sample, jnp.stack(new_c, axis=1), jnp.stack(new_n, axis=1), jnp.stack(new_m, axis=1),
            jnp.stack(new_s, axis=1), jnp.stack(new_k, axis=1), jnp.stack(new_v, axis=1))
```
